```python
import math
import jax, jax.numpy as jnp
from jax import lax
import numpy as np

D_MODEL = 1024
BATCH = 4
SEQ = 4096
DEPTH = 2

GRID_W = 64
CTX_LEN = 256
EPS = 1e-6
NEG_INF = -1e30
N_BRANCH = 3
BRANCH_WIDTH = D_MODEL // 2
HEAD_DIM = 64
N_HEADS = BRANCH_WIDTH // HEAD_DIM
N_KV_HEADS = 2
WINDOW = 128
BLOCK = 128
ROPE_THETA = 10000.0
ATTN_WIDTH = N_HEADS * HEAD_DIM
KV_WIDTH = N_KV_HEADS * HEAD_DIM
CONV_K = 3
POOL_SIZES = (2, 4, 8, 16)
N_POOL_GROUPS = 4
POOL_GROUP = BRANCH_WIDTH // N_POOL_GROUPS
Q_END = ATTN_WIDTH
K_END = Q_END + KV_WIDTH
V_END = K_END + KV_WIDTH
CX_END = V_END + BRANCH_WIDTH
CB_END = CX_END + BRANCH_WIDTH
CC_END = CB_END + BRANCH_WIDTH
POOL_END = CC_END + BRANCH_WIDTH
IN_WIDTH = POOL_END + N_BRANCH * D_MODEL
D_FF = 2816
N_EXPERTS = 8
TOP_K = 2
D_FF_EXPERT = 3584
N_DENSE = (DEPTH + 1) // 2
N_MOE = DEPTH // 2

kernel_name = 'hybrid_gated_dit_block'


def rmsnorm(x, g):
    xf = x.astype(jnp.float32)
    y = xf * lax.rsqrt(jnp.mean(xf * xf, axis=-1, keepdims=True) + EPS)
    return (y * g.astype(jnp.float32)).astype(x.dtype)


def modulate(h, shift, scale):
    return h * (1 + scale) + shift


def rope_2d(t, row, col):
    n_freq = HEAD_DIM // 4
    inv = ROPE_THETA ** (-jnp.arange(n_freq, dtype=jnp.float32) / n_freq)
    def rot(u, pos):
        ang = pos.astype(jnp.float32)[:, None] * inv[None, :]
        cos = jnp.cos(ang)[None, :, None, :].astype(u.dtype)
        sin = jnp.sin(ang)[None, :, None, :].astype(u.dtype)
        u1, u2 = u[..., :n_freq], u[..., n_freq:]
        return jnp.concatenate([u1 * cos - u2 * sin, u1 * sin + u2 * cos], axis=-1)
    half = HEAD_DIM // 2
    return jnp.concatenate([rot(t[..., :half], row), rot(t[..., half:], col)], axis=-1)


def split_heads(z):
    b, l, _ = z.shape
    q = z[..., :Q_END].reshape(b, l, N_HEADS, HEAD_DIM)
    k = z[..., Q_END:K_END].reshape(b, l, N_KV_HEADS, HEAD_DIM)
    v = z[..., K_END:V_END].reshape(b, l, N_KV_HEADS, HEAD_DIM)
    return q, k, v


def windowed_attention(q, k, v, kc, vc, sink):
    b, l, h, dh = q.shape
    nb = l // BLOCK
    grp = h // N_KV_HEADS
    scale = dh ** -0.5
    qb = q.reshape(b, nb, BLOCK, N_KV_HEADS, grp, dh)
    def band(t):
        tp = jnp.pad(t, ((0, 0), (BLOCK, BLOCK), (0, 0), (0, 0))).reshape(b, nb + 2, BLOCK, N_KV_HEADS, dh)
        return jnp.concatenate([tp[:, :-2], tp[:, 1:-1], tp[:, 2:]], axis=2)
    kw, vw = band(k), band(v)
    s_loc = jnp.einsum('bnqkgd,bnjkd->bnkgqj', qb, kw, preferred_element_type=jnp.float32) * scale
    blk = jnp.arange(nb)[:, None, None] * BLOCK
    qpos = blk + jnp.arange(BLOCK)[None, :, None]
    kpos = blk - BLOCK + jnp.arange(3 * BLOCK)[None, None, :]
    valid = (kpos >= 0) & (kpos < l) & (jnp.abs(qpos - kpos) <= WINDOW)
    s_loc = jnp.where(valid[None, :, None, None], s_loc, NEG_INF)
    s_ctx = jnp.einsum('bnqkgd,bckd->bnkgqc', qb, kc, preferred_element_type=jnp.float32) * scale
    s_sink = jnp.broadcast_to(sink.astype(jnp.float32).reshape(N_KV_HEADS, grp)[None, None, :, :, None, None],
                              s_ctx.shape[:-1] + (1,))
    p = jax.nn.softmax(jnp.concatenate([s_loc, s_ctx, s_sink], axis=-1), axis=-1).astype(v.dtype)
    lc = kc.shape[1]
    o = (jnp.einsum('bnkgqj,bnjkd->bnqkgd', p[..., :3 * BLOCK], vw)
         + jnp.einsum('bnkgqc,bckd->bnqkgd', p[..., 3 * BLOCK:3 * BLOCK + lc], vc))
    return o.reshape(b, l, h * dh)


def context_attention(qc, kc, vc, sink):
    b, lc, h, dh = qc.shape
    grp = h // N_KV_HEADS
    qg = qc.reshape(b, lc, N_KV_HEADS, grp, dh)
    s = jnp.einsum('bqkgd,bckd->bkgqc', qg, kc, preferred_element_type=jnp.float32) * (dh ** -0.5)
    s_sink = jnp.broadcast_to(sink.astype(jnp.float32).reshape(N_KV_HEADS, grp)[None, :, :, None, None],
                              s.shape[:-1] + (1,))
    p = jax.nn.softmax(jnp.concatenate([s, s_sink], axis=-1), axis=-1).astype(vc.dtype)
    o = jnp.einsum('bkgqc,bckd->bqkgd', p[..., :lc], vc)
    return o.reshape(b, lc, h * dh)


def short_conv(u, w):
    up = jnp.pad(u, ((0, 0), (1, 1), (0, 0)))
    return up[:, :-2] * w[0] + up[:, 1:-1] * w[1] + up[:, 2:] * w[2]


def pool_mixer(u, pool_w, pool_scale):
    b, l, ch = u.shape
    uf = u.astype(jnp.float32)
    csum = jnp.concatenate([jnp.zeros((b, 1, ch), jnp.float32), jnp.cumsum(uf, axis=1)], axis=1)
    t = jnp.arange(l)
    means = []
    for gi, w in enumerate(POOL_SIZES):
        lo = jnp.clip(t - w // 2, 0, l)
        hi = jnp.clip(t + w // 2, 0, l)
        cs = csum[..., gi * POOL_GROUP:(gi + 1) * POOL_GROUP]
        means.append((cs[:, hi] - cs[:, lo]) / (hi - lo).astype(jnp.float32)[None, :, None])
    d = (jnp.concatenate(means, axis=-1) - uf).astype(u.dtype).reshape(b, l, N_POOL_GROUPS, POOL_GROUP)
    y = jnp.einsum('blgc,gcd->blgd', d, pool_w).reshape(b, l, ch)
    return y * pool_scale


def local_mixers(z, conv_w, pool_w, pool_scale):
    cx, cb, cc = z[..., V_END:CX_END], z[..., CX_END:CB_END], z[..., CB_END:CC_END]
    conv_out = cb * short_conv(cc * cx, conv_w)
    pool_out = pool_mixer(z[..., CC_END:POOL_END], pool_w, pool_scale)
    return conv_out, pool_out


def merge_branches(attn_out, conv_out, pool_out, gate_logits, w_branch, w_out):
    g = jax.nn.sigmoid(gate_logits.astype(jnp.float32)).astype(attn_out.dtype)
    y = (g[..., :D_MODEL] * (attn_out @ w_branch[0])
         + g[..., D_MODEL:2 * D_MODEL] * (conv_out @ w_branch[1])
         + g[..., 2 * D_MODEL:] * (pool_out @ w_branch[2]))
    return y @ w_out


def swiglu(h, w_gu, w_down, d_ff):
    gu = h @ w_gu
    return (jax.nn.silu(gu[..., :d_ff]) * gu[..., d_ff:]) @ w_down


def moe_ffn(h, router_w, w_gu, w_down):
    b, l, d = h.shape
    t = h.reshape(b * l, d)
    logits = (t @ router_w).astype(jnp.float32)
    top_v, top_i = lax.top_k(logits, TOP_K)
    wts = jax.nn.softmax(top_v, axis=-1)
    combine = jnp.sum(jax.nn.one_hot(top_i, N_EXPERTS, dtype=jnp.float32) * wts[..., None], axis=1)
    combine = combine.astype(h.dtype)
    out = jnp.zeros_like(t)
    for e in range(N_EXPERTS):
        out = out + combine[:, e:e + 1] * swiglu(t, w_gu[e], w_down[e], D_FF_EXPERT)
    return out.reshape(b, l, d)


def setup_inputs(seed: int = 0) -> dict:
    key = jax.random.key(seed)
    ks = jax.random.split(key, 21)
    f32 = jnp.float32
    def nrm(k, shape, scale):
        return jax.random.normal(k, shape, f32) * scale
    D = D_MODEL
    return {
        'x': nrm(ks[0], (BATCH, SEQ, D), 1.0),
        'c': nrm(ks[1], (BATCH, D), 1.0),
        'ctx': nrm(ks[2], (BATCH, CTX_LEN, D), 1.0),
        'c_ctx': nrm(ks[3], (D,), 1.0),
        'norm1_g': 1.0 + nrm(ks[4], (DEPTH, D), 0.1),
        'norm2_g': 1.0 + nrm(ks[5], (DEPTH, D), 0.1),
        'final_g': 1.0 + nrm(ks[6], (D,), 0.1),
        'w_mod': nrm(ks[7], (DEPTH, D, 6 * D), 0.5 * D ** -0.5),
        'b_mod': nrm(ks[8], (DEPTH, 6 * D), 0.02),
        'w_in': nrm(ks[9], (DEPTH, D, IN_WIDTH), D ** -0.5),
        'conv_w': nrm(ks[10], (DEPTH, CONV_K, BRANCH_WIDTH), CONV_K ** -0.5),
        'sink': nrm(ks[11], (DEPTH, N_HEADS), 1.0),
        'pool_w': nrm(ks[12], (DEPTH, N_POOL_GROUPS, POOL_GROUP, POOL_GROUP), POOL_GROUP ** -0.5),
        'pool_scale': 1.0 + nrm(ks[13], (DEPTH, BRANCH_WIDTH), 0.1),
        'w_branch': nrm(ks[14], (DEPTH, N_BRANCH, BRANCH_WIDTH, D), BRANCH_WIDTH ** -0.5),
        'w_out': nrm(ks[15], (DEPTH, D, D), D ** -0.5),
        'ffn_w_gu': nrm(ks[16], (N_DENSE, D, 2 * D_FF), D ** -0.5),
        'ffn_w_down': nrm(ks[17], (N_DENSE, D_FF, D), D_FF ** -0.5),
        'router_w': nrm(ks[18], (N_MOE, D, N_EXPERTS), D ** -0.5),
        'moe_w_gu': nrm(ks[19], (N_MOE, N_EXPERTS, D, 2 * D_FF_EXPERT), D ** -0.5),
        'moe_w_down': nrm(ks[20], (N_MOE, N_EXPERTS, D_FF_EXPERT, D), D_FF_EXPERT ** -0.5),
    }


def reference(x, c, ctx, c_ctx, norm1_g, norm2_g, final_g, w_mod, b_mod, w_in, conv_w, sink,
              pool_w, pool_scale, w_branch, w_out, ffn_w_gu, ffn_w_down, router_w, moe_w_gu, moe_w_down):
    L = x.shape[1]
    ROWS = L // GRID_W
    row = jnp.repeat(jnp.arange(ROWS), GRID_W)
    col = jnp.tile(jnp.arange(GRID_W), ROWS)
    s_lat = jax.nn.silu(c)[:, None, :]
    s_ctx = jax.nn.silu(c_ctx)[None, None, :]
    xc = ctx
    for l in range(DEPTH):
        last = l == DEPTH - 1
        sh1, sc1, g1, sh2, sc2, g2 = jnp.split(s_lat @ w_mod[l] + b_mod[l], 6, axis=-1)
        csh1, csc1, cg1, csh2, csc2, cg2 = jnp.split(s_ctx @ w_mod[l] + b_mod[l], 6, axis=-1)
        if l % 2 == 0:
            ffn = functools_partial_dense(ffn_w_gu[l // 2], ffn_w_down[l // 2])
        else:
            ffn = functools_partial_moe(router_w[l // 2], moe_w_gu[l // 2], moe_w_down[l // 2])
        hc = modulate(rmsnorm(xc, norm1_g[l]), csh1, csc1)
        if last:
            zkv = hc @ w_in[l][:, Q_END:V_END]
            kc = zkv[..., :KV_WIDTH].reshape(xc.shape[0], xc.shape[1], N_KV_HEADS, HEAD_DIM)
            vc = zkv[..., KV_WIDTH:].reshape(xc.shape[0], xc.shape[1], N_KV_HEADS, HEAD_DIM)
        else:
            zc = hc @ w_in[l]
            qc, kc, vc = split_heads(zc)
            attn_c = context_attention(qc, kc, vc, sink[l])
            conv_c, pool_c = local_mixers(zc, conv_w[l], pool_w[l], pool_scale[l])
            xc_mid = xc + cg1 * merge_branches(attn_c, conv_c, pool_c, zc[..., POOL_END:], w_branch[l], w_out[l])
        h = modulate(rmsnorm(x, norm1_g[l]), sh1, sc1)
        z = h @ w_in[l]
        q, k, v = split_heads(z)
        q, k = rope_2d(q, row, col), rope_2d(k, row, col)
        attn = windowed_attention(q, k, v, kc, vc, sink[l])
        conv_o, pool_o = local_mixers(z, conv_w[l], pool_w[l], pool_scale[l])
        x = x + g1 * merge_branches(attn, conv_o, pool_o, z[..., POOL_END:], w_branch[l], w_out[l])
        x = x + g2 * ffn(modulate(rmsnorm(x, norm2_g[l]), sh2, sc2))
        if not last:
            xc = xc_mid + cg2 * ffn(modulate(rmsnorm(xc_mid, norm2_g[l]), csh2, csc2))
    return rmsnorm(x, final_g)


def functools_partial_dense(w_gu, w_down):
    def f(h):
        return swiglu(h, w_gu, w_down, D_FF)
    return f


def functools_partial_moe(r_w, w_gu, w_down):
    def f(h):
        return moe_ffn(h, r_w, w_gu, w_down)
    return f
```

```python
import functools

import jax
import jax.numpy as jnp
from jax import lax
from jax.experimental import pallas as pl
from jax.experimental.pallas import tpu as pltpu

F32 = jnp.float32
BF16 = jnp.bfloat16

GRID_W = 64
EPS = 1e-6
NEG_INF = -1e30
HEAD_DIM = 64
N_HEADS = 8
N_KV_HEADS = 2
GROUP = N_HEADS // N_KV_HEADS
WINDOW = 128
BLOCK = 128
ROPE_THETA = 10000.0
BRANCH = 512
POOL_SIZES = (2, 4, 8, 16)
POOL_GROUP = 128
Q_END = 512
V_END = 768
MIX_W = 4 * BRANCH
POOL_END = V_END + MIX_W

LANE = 128
BF16_SUBLANE_TILE = 16
HALO = BF16_SUBLANE_TILE
MIB = 1024 * 1024


def _cparams(sem, vmem_mib):
    return pltpu.CompilerParams(dimension_semantics=sem, vmem_limit_bytes=vmem_mib * MIB)


def _pick_chunk(n, cap):
    best = None
    for c in range(LANE, min(n, cap) + 1, LANE):
        if n % c == 0:
            best = c
    assert best is not None, (n, cap)
    return best


def _resident(shape):
    nd = len(shape)
    return pl.BlockSpec(shape, lambda *_: (0,) * nd, pipeline_mode=pl.Buffered(1))


def _norm_mod(x, g, shift, scale):
    y = x * lax.rsqrt(jnp.mean(x * x, axis=-1, keepdims=True) + EPS) * g
    return y * (1.0 + scale) + shift


def _modvec_kernel(c_ref, w_ref, b_ref, o_ref):
    c = c_ref[...]
    s = c * jax.nn.sigmoid(c)
    o_ref[...] = jnp.dot(s, w_ref[...], preferred_element_type=F32) + b_ref[...]


def _modvec(cvec, w_mod, b_mod):
    depth, d, n = w_mod.shape
    nc = _pick_chunk(n, 1536)
    return pl.pallas_call(
        _modvec_kernel,
        grid=(depth, n // nc),
        in_specs=[
            pl.BlockSpec((8, d), lambda l, j: (0, 0)),
            pl.BlockSpec((None, d, nc), lambda l, j: (l, 0, j)),
            pl.BlockSpec((None, 1, nc), lambda l, j: (l, 0, j)),
        ],
        out_specs=pl.BlockSpec((None, 8, nc), lambda l, j: (l, 0, j)),
        out_shape=jax.ShapeDtypeStruct((depth, 8, n), F32),
        compiler_params=_cparams(("arbitrary", "arbitrary"), 32),
        name="modvec",
    )(cvec, w_mod, b_mod.reshape(depth, 1, n))


def _rope_tables(seq_len):
    n_freq = HEAD_DIM // 4
    inv = ROPE_THETA ** (-jnp.arange(n_freq, dtype=F32) / n_freq)
    pos = jnp.arange(seq_len)
    row = (pos // GRID_W).astype(F32)[:, None] * inv[None, :]
    col = (pos % GRID_W).astype(F32)[:, None] * inv[None, :]
    zero = jnp.zeros_like(row)
    cos = jnp.concatenate([jnp.cos(row)] * 2 + [jnp.cos(col)] * 2, axis=-1)
    s_lo = jnp.concatenate([-jnp.sin(row), zero, -jnp.sin(col), zero], axis=-1)
    s_hi = jnp.concatenate([zero, jnp.sin(row), zero, jnp.sin(col)], axis=-1)
    return tuple(jnp.tile(t, (1, LANE // HEAD_DIM)) for t in (cos, s_lo, s_hi))


def _inproj_kernel(*refs, rope, kv_only, d):
    x_ref, g_ref, mod_ref, w_ref = refs[:4]
    refs = refs[4:]
    if rope:
        cos_ref, slo_ref, shi_ref = refs[:3]
        refs = refs[3:]

        def rot(z):
            return (z * cos_ref[...] + pltpu.roll(z, LANE - 16, 1) * slo_ref[...]
                    + pltpu.roll(z, 16, 1) * shi_ref[...])
    else:
        def rot(z):
            return z

    mod = mod_ref[...]
    h = _norm_mod(x_ref[...], g_ref[...], mod[:, 0:d], mod[:, d:2 * d]).astype(BF16)

    def proj(c0, c1):
        return jnp.dot(h, w_ref[:, c0:c1], preferred_element_type=F32)

    if kv_only:
        (kv_ref,) = refs
        z = proj(0, 2 * LANE)
        kv_ref[:, 0:LANE] = rot(z[:, 0:LANE]).astype(BF16)
        kv_ref[:, LANE:] = z[:, LANE:].astype(BF16)
        return

    q_ref, kv_ref, mix_ref, gate_ref = refs
    z = proj(0, Q_END)
    for j in range(Q_END // LANE):
        q_ref[:, j * LANE:(j + 1) * LANE] = (rot(z[:, j * LANE:(j + 1) * LANE]) * HEAD_DIM ** -0.5).astype(BF16)
    z = proj(Q_END, V_END)
    kv_ref[:, 0:LANE] = rot(z[:, 0:LANE]).astype(BF16)
    kv_ref[:, LANE:] = z[:, LANE:].astype(BF16)
    cw = 512
    for c in range(MIX_W // cw):
        mix_ref[:, c * cw:(c + 1) * cw] = proj(V_END + c * cw, V_END + (c + 1) * cw).astype(BF16)
    n_gate = gate_ref.shape[1]
    for c in range(n_gate // cw):
        zg = proj(POOL_END + c * cw, POOL_END + (c + 1) * cw)
        gate_ref[:, c * cw:(c + 1) * cw] = jax.nn.sigmoid(zg).astype(BF16)


def _inproj(x, norm_g, mod, w_in, *, seq_len, mod_row_of_tile, rope_tabs=None, kv_only=False):
    m, d = x.shape
    tm = min(512, seq_len)
    assert m % tm == 0 and seq_len % tm == 0
    tiles_per_seq = seq_len // tm
    rope = rope_tabs is not None
    in_w = w_in.shape[1]
    in_specs = [
        pl.BlockSpec((tm, d), lambda i: (i, 0)),
        pl.BlockSpec((1, d), lambda i: (0, 0)),
        pl.BlockSpec((None, 1, 6 * d), lambda i: (mod_row_of_tile(i, tm), 0, 0)),
        pl.BlockSpec((d, 2 * LANE), lambda i: (0, Q_END // (2 * LANE))) if kv_only else _resident((d, in_w)),
    ]
    args = [x, norm_g.reshape(1, d), mod, w_in]
    if rope:
        in_specs += [pl.BlockSpec((tm, LANE), lambda i: (i % tiles_per_seq, 0))] * 3
        args += list(rope_tabs)
    if kv_only:
        out_specs = pl.BlockSpec((tm, 2 * LANE), lambda i: (i, 0))
        out_shape = jax.ShapeDtypeStruct((m, 2 * LANE), BF16)
    else:
        widths = (Q_END, 2 * LANE, MIX_W, in_w - POOL_END)
        out_specs = [pl.BlockSpec((tm, w), lambda i: (i, 0)) for w in widths]
        out_shape = [jax.ShapeDtypeStruct((m, w), BF16) for w in widths]
    return pl.pallas_call(
        functools.partial(_inproj_kernel, rope=rope, kv_only=kv_only, d=d),
        grid=(m // tm,),
        in_specs=in_specs,
        out_specs=out_specs,
        out_shape=out_shape,
        compiler_params=_cparams(("parallel",), 48),
        name="inproj_kv" if kv_only else "inproj",
    )(*args)


def _attn_kernel(sink_ref, q_ref, *refs, band, seq_len):
    if band:
        kvp_ref, kvm_ref, kvn_ref, kvc_ref, o_ref = refs
    else:
        kvc_ref, o_ref = refs
    q = q_ref[0]
    kvc = kvc_ref[0]
    rows = GROUP * BLOCK
    nt = (((1,), (1,)), ((), ()))
    if band:
        n = pl.program_id(1)
        kvl = jnp.concatenate([kvp_ref[0], kvm_ref[0], kvn_ref[0]], axis=0)
        qi = lax.broadcasted_iota(jnp.int32, (rows, 3 * BLOCK), 0) & (BLOCK - 1)
        kj = lax.broadcasted_iota(jnp.int32, (rows, 3 * BLOCK), 1)
        kpos = (n - 1) * BLOCK + kj
        valid = (jnp.abs(kj - BLOCK - qi) <= WINDOW) & (kpos >= 0) & (kpos < seq_len)
    outs = []
    for k in range(N_KV_HEADS):
        heads = [GROUP * k + g for g in range(GROUP)]
        qk = jnp.concatenate([q[:, h * HEAD_DIM:(h + 1) * HEAD_DIM] for h in heads], axis=0)
        sink = jnp.concatenate([jnp.full((BLOCK, 1), sink_ref[h], F32) for h in heads], axis=0)
        ks = slice(k * HEAD_DIM, (k + 1) * HEAD_DIM)
        vs = slice(LANE + k * HEAD_DIM, LANE + (k + 1) * HEAD_DIM)
        s_ctx = lax.dot_general(qk, kvc[:, ks], nt, preferred_element_type=F32)
        m = jnp.maximum(jnp.max(s_ctx, axis=-1, keepdims=True), sink)
        if band:
            s_loc = lax.dot_general(qk, kvl[:, ks], nt, preferred_element_type=F32)
            s_loc = jnp.where(valid, s_loc, NEG_INF)
            m = jnp.maximum(m, jnp.max(s_loc, axis=-1, keepdims=True))
        p_ctx = jnp.exp(s_ctx - m)
        den = jnp.sum(p_ctx, axis=-1, keepdims=True) + jnp.exp(sink - m)
        o = jnp.dot(p_ctx.astype(BF16), kvc[:, vs], preferred_element_type=F32)
        if band:
            p_loc = jnp.exp(s_loc - m)
            den = den + jnp.sum(p_loc, axis=-1, keepdims=True)
            o = o + jnp.dot(p_loc.astype(BF16), kvl[:, vs], preferred_element_type=F32)
        o = o / den
        outs.append(jnp.concatenate([o[g * BLOCK:(g + 1) * BLOCK] for g in range(GROUP)], axis=1))
    o_ref[0] = jnp.concatenate(outs, axis=1).astype(BF16)


def _attention(q, kv, kvc, sink, *, band):
    b, l, _ = q.shape
    lc = kvc.shape[1]
    nb = l // BLOCK
    in_specs = [
        pl.BlockSpec(memory_space=pltpu.SMEM),
        pl.BlockSpec((1, BLOCK, N_HEADS * HEAD_DIM), lambda bi, n: (bi, n, 0)),
    ]
    args = [sink, q]
    if band:
        in_specs += [
            pl.BlockSpec((1, BLOCK, 2 * LANE), lambda bi, n: (bi, jnp.maximum(n - 1, 0), 0)),
            pl.BlockSpec((1, BLOCK, 2 * LANE), lambda bi, n: (bi, n, 0)),
            pl.BlockSpec((1, BLOCK, 2 * LANE), lambda bi, n: (bi, jnp.minimum(n + 1, nb - 1), 0)),
        ]
        args += [kv, kv, kv]
    in_specs.append(pl.BlockSpec((1, lc, 2 * LANE), lambda bi, n: (bi, 0, 0)))
    args.append(kvc)
    return pl.pallas_call(
        functools.partial(_attn_kernel, band=band, seq_len=l),
        grid=(b, nb),
        in_specs=in_specs,
        out_specs=pl.BlockSpec((1, BLOCK, N_HEADS * HEAD_DIM), lambda bi, n: (bi, n, 0)),
        out_shape=jax.ShapeDtypeStruct((b, l, N_HEADS * HEAD_DIM), BF16),
        compiler_params=_cparams(("parallel", "parallel"), 32),
        name="attn_band" if band else "attn_ctx",
    )(*args)


def _merge_kernel(attn_ref, mix_ref, prev_ref, next_ref, gate_ref, x_ref, mod_ref, n2g_ref, convw_ref,
                  poolw_ref, pscale_ref, wb_ref, wo_ref, xo_ref, h2_ref, *, tm, seq_len, d):
    tile = pl.program_id(0) % (seq_len // tm)
    keep_prev = (tile != 0).astype(F32)
    keep_next = (tile != seq_len // tm - 1).astype(F32)
    mixm = mix_ref[...]
    prev = prev_ref[...].astype(F32) * keep_prev
    nxt = next_ref[...].astype(F32) * keep_next
    b = BRANCH
    cx, cb, cc = (mixm[:, j * b:(j + 1) * b].astype(F32) for j in range(3))

    p = cc * cx
    p_prev = prev[HALO - 1:HALO, 2 * b:3 * b] * prev[HALO - 1:HALO, 0:b]
    p_next = nxt[0:1, 2 * b:3 * b] * nxt[0:1, 0:b]
    ridx = lax.broadcasted_iota(jnp.int32, (tm, b), 0)
    p_dn = jnp.where(ridx == 0, p_prev, pltpu.roll(p, 1, 0))
    p_up = jnp.where(ridx == tm - 1, p_next, pltpu.roll(p, tm - 1, 0))
    cw = convw_ref[...]
    conv_out = (cb * (p_dn * cw[0:1] + p * cw[1:2] + p_up * cw[2:3])).astype(BF16)

    u_main = mixm[:, 3 * b:4 * b].astype(F32)
    u_ext = jnp.concatenate([prev[:, 3 * b:4 * b], u_main, nxt[:, 3 * b:4 * b]], axis=0)
    ext = tm + 2 * HALO

    def shift(a, s):
        return pltpu.roll(a, s % ext, 0)

    tpos = tile * tm + lax.broadcasted_iota(jnp.int32, (tm, 1), 0)
    pooled = []
    for gi, w in enumerate(POOL_SIZES):
        gs = slice(gi * POOL_GROUP, (gi + 1) * POOL_GROUP)
        ug = u_ext[:, gs]
        a = ug + shift(ug, 1)
        ww = 2
        while ww < w:
            a = shift(a, ww // 2) + shift(a, -(ww // 2))
            ww *= 2
        cnt = jnp.minimum(tpos + w // 2, seq_len) - jnp.maximum(tpos - w // 2, 0)
        dlt = a[HALO:HALO + tm] / cnt.astype(F32) - u_main[:, gs]
        pooled.append(jnp.dot(dlt.astype(BF16), poolw_ref[gi], preferred_element_type=F32))
    pool_out = (jnp.concatenate(pooled, axis=1) * pscale_ref[...]).astype(BF16)

    y = (gate_ref[:, 0:d].astype(F32) * jnp.dot(attn_ref[...], wb_ref[0], preferred_element_type=F32)
         + gate_ref[:, d:2 * d].astype(F32) * jnp.dot(conv_out, wb_ref[1], preferred_element_type=F32)
         + gate_ref[:, 2 * d:3 * d].astype(F32) * jnp.dot(pool_out, wb_ref[2], preferred_element_type=F32))
    o = jnp.dot(y.astype(BF16), wo_ref[...], preferred_element_type=F32)
    mod = mod_ref[...]
    xn = x_ref[...] + mod[:, 2 * d:3 * d] * o
    xo_ref[...] = xn
    h2_ref[...] = _norm_mod(xn, n2g_ref[...], mod[:, 3 * d:4 * d], mod[:, 4 * d:5 * d]).astype(BF16)


def _merge(attn, mix, gate, x, mod, norm2_g, conv_w, pool_w, pool_scale, w_branch, w_out, *, seq_len,
           mod_row_of_tile):
    m, d = x.shape
    tm = min(256, seq_len)
    assert m % tm == 0 and seq_len % tm == 0 and tm % HALO == 0
    hb = tm // HALO
    n_halo = m // HALO
    row = lambda w: pl.BlockSpec((tm, w), lambda i: (i, 0))
    in_specs = [
        row(BRANCH),
        row(MIX_W),
        pl.BlockSpec((HALO, MIX_W), lambda i: (jnp.maximum(i * hb - 1, 0), 0)),
        pl.BlockSpec((HALO, MIX_W), lambda i: (jnp.minimum((i + 1) * hb, n_halo - 1), 0)),
        row(3 * d),
        row(d),
        pl.BlockSpec((None, 1, 6 * d), lambda i: (mod_row_of_tile(i, tm), 0, 0)),
        pl.BlockSpec((1, d), lambda i: (0, 0)),
        _resident(conv_w.shape),
        _resident(pool_w.shape),
        pl.BlockSpec((1, BRANCH), lambda i: (0, 0)),
        _resident(w_branch.shape),
        _resident(w_out.shape),
    ]
    return pl.pallas_call(
        functools.partial(_merge_kernel, tm=tm, seq_len=seq_len, d=d),
        grid=(m // tm,),
        in_specs=in_specs,
        out_specs=[row(d), row(d)],
        out_shape=[jax.ShapeDtypeStruct((m, d), F32), jax.ShapeDtypeStruct((m, d), BF16)],
        compiler_params=_cparams(("parallel",), 48),
        name="merge",
    )(attn, mix, mix, mix, gate, x, mod, norm2_g.reshape(1, d), conv_w, pool_w,
      pool_scale.reshape(1, BRANCH), w_branch, w_out)


def _ffn_kernel(h_ref, x_ref, mod_ref, wgu_ref, wd_ref, o_ref, *, d, d_ff, fc):
    h = h_ref[...]
    acc = None
    for f in range(0, d_ff, fc):
        g = jnp.dot(h, wgu_ref[:, f:f + fc], preferred_element_type=F32)
        u = jnp.dot(h, wgu_ref[:, d_ff + f:d_ff + f + fc], preferred_element_type=F32)
        a = (g * jax.nn.sigmoid(g) * u).astype(BF16)
        y = jnp.dot(a, wd_ref[f:f + fc, :], preferred_element_type=F32)
        acc = y if acc is None else acc + y
    o_ref[...] = x_ref[...] + mod_ref[...][:, 5 * d:6 * d] * acc


def _ffn_dense(h2, x, mod, w_gu, w_down, *, seq_len, mod_row_of_tile):
    m, d = x.shape
    d_ff = w_down.shape[0]
    tm = min(512, seq_len)
    assert m % tm == 0 and seq_len % tm == 0
    fc = _pick_chunk(d_ff, 1536)
    return pl.pallas_call(
        functools.partial(_ffn_kernel, d=d, d_ff=d_ff, fc=fc),
        grid=(m // tm,),
        in_specs=[
            pl.BlockSpec((tm, d), lambda i: (i, 0)),
            pl.BlockSpec((tm, d), lambda i: (i, 0)),
            pl.BlockSpec((None, 1, 6 * d), lambda i: (mod_row_of_tile(i, tm), 0, 0)),
            _resident(w_gu.shape),
            _resident(w_down.shape),
        ],
        out_specs=pl.BlockSpec((tm, d), lambda i: (i, 0)),
        out_shape=jax.ShapeDtypeStruct((m, d), F32),
        compiler_params=_cparams(("parallel",), 56),
        name="ffn_dense",
    )(h2, x, mod, w_gu, w_down)


def _router_kernel(x_ref, mod_ref, n2g_ref, rw_ref, c_ref, *, d, n_experts):
    mod = mod_ref[...]
    h = _norm_mod(x_ref[...], n2g_ref[...], mod[:, 3 * d:4 * d], mod[:, 4 * d:5 * d])
    logits = jnp.dot(h, rw_ref[...], preferred_element_type=F32, precision=lax.Precision.HIGHEST)
    lane = lax.broadcasted_iota(jnp.int32, logits.shape, 1).astype(F32)
    logits = jnp.where(lane < n_experts, logits, -jnp.inf)
    m1 = jnp.max(logits, axis=-1, keepdims=True)
    i1 = jnp.min(jnp.where(logits == m1, lane, float(LANE)), axis=-1, keepdims=True)
    rest = jnp.where(lane == i1, -jnp.inf, logits)
    m2 = jnp.max(rest, axis=-1, keepdims=True)
    i2 = jnp.min(jnp.where(rest == m2, lane, float(LANE)), axis=-1, keepdims=True)
    e2 = jnp.exp(m2 - m1)
    den = 1.0 + e2
    c_ref[...] = jnp.where(lane == i1, 1.0 / den, 0.0) + jnp.where(lane == i2, e2 / den, 0.0)


def _router(x, mod, norm2_g, router_w, *, seq_len, mod_row_of_tile):
    m, d = x.shape
    n_experts = router_w.shape[1]
    rw = jnp.pad(router_w, ((0, 0), (0, LANE - n_experts)))
    tm = min(512, seq_len)
    return pl.pallas_call(
        functools.partial(_router_kernel, d=d, n_experts=n_experts),
        grid=(m // tm,),
        in_specs=[
            pl.BlockSpec((tm, d), lambda i: (i, 0)),
            pl.BlockSpec((None, 1, 6 * d), lambda i: (mod_row_of_tile(i, tm), 0, 0)),
            pl.BlockSpec((1, d), lambda i: (0, 0)),
            pl.BlockSpec((d, LANE), lambda i: (0, 0)),
        ],
        out_specs=pl.BlockSpec((tm, LANE), lambda i: (i, 0)),
        out_shape=jax.ShapeDtypeStruct((m, LANE), F32),
        compiler_params=_cparams(("parallel",), 32),
        name="router",
    )(x, mod, norm2_g.reshape(1, d), rw)


def _moe_kernel(h_ref, x_ref, c_ref, mod_ref, fg_ref, wg_ref, wu_ref, wd_ref, o_ref, acc_ref, *, d, final_norm):
    e = pl.program_id(1)
    f = pl.program_id(2)

    @pl.when((e == 0) & (f == 0))
    def _():
        acc_ref[...] = jnp.zeros_like(acc_ref)

    h = h_ref[...]
    g = jnp.dot(h, wg_ref[...], preferred_element_type=F32)
    u = jnp.dot(h, wu_ref[...], preferred_element_type=F32)
    a = (g * jax.nn.sigmoid(g) * u).astype(BF16)
    y = jnp.dot(a, wd_ref[...], preferred_element_type=F32)
    c = c_ref[...]
    lane = lax.broadcasted_iota(jnp.int32, c.shape, 1)
    ce = jnp.sum(jnp.where(lane == e, c, 0.0), axis=-1, keepdims=True)
    acc_ref[...] += ce * y

    @pl.when((e == pl.num_programs(1) - 1) & (f == pl.num_programs(2) - 1))
    def _():
        xn = x_ref[...] + mod_ref[...][:, 5 * d:6 * d] * acc_ref[...]
        if final_norm:
            xn = xn * lax.rsqrt(jnp.mean(xn * xn, axis=-1, keepdims=True) + EPS) * fg_ref[...]
        o_ref[...] = xn


def _moe_dense(h2, x, combine, mod, final_g, w_gu, w_down, *, seq_len, mod_row_of_tile, final_norm):
    m, d = x.shape
    n_experts, d_ff, _ = w_down.shape
    tm = min(1024, seq_len)
    fc = _pick_chunk(d_ff, 512)
    nf = d_ff // fc
    return pl.pallas_call(
        functools.partial(_moe_kernel, d=d, final_norm=final_norm),
        grid=(m // tm, n_experts, nf),
        in_specs=[
            pl.BlockSpec((tm, d), lambda i, e, f: (i, 0)),
            pl.BlockSpec((tm, d), lambda i, e, f: (i, 0)),
            pl.BlockSpec((tm, LANE), lambda i, e, f: (i, 0)),
            pl.BlockSpec((None, 1, 6 * d), lambda i, e, f: (mod_row_of_tile(i, tm), 0, 0)),
            pl.BlockSpec((1, d), lambda i, e, f: (0, 0)),
            pl.BlockSpec((None, d, fc), lambda i, e, f: (e, 0, f)),
            pl.BlockSpec((None, d, fc), lambda i, e, f: (e, 0, nf + f)),
            pl.BlockSpec((None, fc, d), lambda i, e, f: (e, f, 0)),
        ],
        out_specs=pl.BlockSpec((tm, d), lambda i, e, f: (i, 0)),
        out_shape=jax.ShapeDtypeStruct((m, d), F32),
        scratch_shapes=[pltpu.VMEM((tm, d), F32)],
        compiler_params=_cparams(("parallel", "arbitrary", "arbitrary"), 48),
        name="moe_dense",
    )(h2, x, combine, mod, final_g.reshape(1, d), w_gu, w_gu, w_down)


def kernel(x, c, ctx, c_ctx, norm1_g, norm2_g, final_g, w_mod, b_mod, w_in, conv_w, sink, pool_w, pool_scale,
           w_branch, w_out, ffn_w_gu, ffn_w_down, router_w, moe_w_gu, moe_w_down):
    bsz, seq, d = x.shape
    lc = ctx.shape[1]
    depth = w_in.shape[0]
    assert bsz + 1 <= 8 and seq % BLOCK == 0 and lc % BLOCK == 0 and seq % GRID_W == 0

    cvec = jnp.zeros((8, d), F32).at[:bsz].set(c).at[bsz].set(c_ctx)
    mods = _modvec(cvec, w_mod, b_mod)
    rope_tabs = _rope_tables(seq)

    lat_row = lambda i, tm: (i * tm) // seq
    ctx_row = lambda i, tm: bsz

    xl = x.reshape(bsz * seq, d)
    xc = ctx.reshape(bsz * lc, d)
    for l in range(depth):
        last = l == depth - 1
        mod = mods[l].reshape(8, 1, 6 * d)
        w_in_l = w_in[l].astype(BF16)
        wb_l, wo_l, pw_l = w_branch[l].astype(BF16), w_out[l].astype(BF16), pool_w[l].astype(BF16)
        mixer = functools.partial(_merge, mod=mod, norm2_g=norm2_g[l], conv_w=conv_w[l], pool_w=pw_l,
                                  pool_scale=pool_scale[l], w_branch=wb_l, w_out=wo_l)
        if l % 2 == 0:
            wgu, wd = ffn_w_gu[l // 2].astype(BF16), ffn_w_down[l // 2].astype(BF16)
        else:
            wgu, wd = moe_w_gu[l // 2].astype(BF16), moe_w_down[l // 2].astype(BF16)

        def channel_mix(h2, xm, *, seq_len, row_fn, final_norm):
            if l % 2 == 0:
                out = _ffn_dense(h2, xm, mod, wgu, wd, seq_len=seq_len, mod_row_of_tile=row_fn)
                assert not final_norm
                return out
            combine = _router(xm, mod, norm2_g[l], router_w[l // 2], seq_len=seq_len, mod_row_of_tile=row_fn)
            return _moe_dense(h2, xm, combine, mod, final_g, wgu, wd, seq_len=seq_len, mod_row_of_tile=row_fn,
                              final_norm=final_norm)

        if last:
            kvc = _inproj(xc, norm1_g[l], mod, w_in_l, seq_len=lc, mod_row_of_tile=ctx_row, kv_only=True)
        else:
            qc, kvc, mixc, gatec = _inproj(xc, norm1_g[l], mod, w_in_l, seq_len=lc, mod_row_of_tile=ctx_row)
            attn_c = _attention(qc.reshape(bsz, lc, -1), None, kvc.reshape(bsz, lc, -1), sink[l], band=False)
            xc_mid, h2c = mixer(attn_c.reshape(bsz * lc, -1), mixc, gatec, xc, seq_len=lc, mod_row_of_tile=ctx_row)
            xc_next = channel_mix(h2c, xc_mid, seq_len=lc, row_fn=ctx_row, final_norm=False)
        q, kv, mix, gate = _inproj(xl, norm1_g[l], mod, w_in_l, seq_len=seq, mod_row_of_tile=lat_row,
                                   rope_tabs=rope_tabs)
        attn = _attention(q.reshape(bsz, seq, -1), kv.reshape(bsz, seq, -1), kvc.reshape(bsz, lc, -1), sink[l],
                          band=True)
        x_mid, h2 = mixer(attn.reshape(bsz * seq, -1), mix, gate, xl, seq_len=seq, mod_row_of_tile=lat_row)
        xl = channel_mix(h2, x_mid, seq_len=seq, row_fn=lat_row, final_norm=last and l % 2 == 1)
        if not last:
            xc = xc_next
    if depth % 2 == 1:
        raise NotImplementedError("final norm is fused into the expert layer; depth must be even")
    return xl.reshape(bsz, seq, d)
```

```python
import functools

import jax
import jax.numpy as jnp
from jax import lax
from jax.experimental import pallas as pl
from jax.experimental.pallas import tpu as pltpu

F32 = jnp.float32
BF16 = jnp.bfloat16

GRID_W = 64
EPS = 1e-6
NEG_INF = -1e30
HEAD_DIM = 64
N_HEADS = 8
N_KV_HEADS = 2
GROUP = N_HEADS // N_KV_HEADS
WINDOW = 128
BLOCK = 128
ROPE_THETA = 10000.0
BRANCH = 512
POOL_SIZES = (2, 4, 8, 16)
POOL_GROUP = 128
Q_END = 512
V_END = 768
MIX_W = 4 * BRANCH
POOL_END = V_END + MIX_W

LANE = 128
BF16_SUBLANE_TILE = 16
HALO = BF16_SUBLANE_TILE
MIB = 1024 * 1024


def _cparams(sem, vmem_mib):
    return pltpu.CompilerParams(dimension_semantics=sem, vmem_limit_bytes=vmem_mib * MIB)


def _pick_chunk(n, cap):
    best = None
    for c in range(LANE, min(n, cap) + 1, LANE):
        if n % c == 0:
            best = c
    assert best is not None, (n, cap)
    return best


def _resident(shape):
    nd = len(shape)
    return pl.BlockSpec(shape, lambda *_: (0,) * nd, pipeline_mode=pl.Buffered(1))


def _norm_mod(x, g, shift, scale):
    y = x * lax.rsqrt(jnp.mean(x * x, axis=-1, keepdims=True) + EPS) * g
    return y * (1.0 + scale) + shift


def _modvec_kernel(c_ref, w_ref, b_ref, o_ref):
    c = c_ref[...]
    s = c * jax.nn.sigmoid(c)
    o_ref[...] = jnp.dot(s, w_ref[...], preferred_element_type=F32) + b_ref[...]


def _modvec(cvec, w_mod, b_mod):
    depth, d, n = w_mod.shape
    nc = _pick_chunk(n, 1536)
    return pl.pallas_call(
        _modvec_kernel,
        grid=(depth, n // nc),
        in_specs=[
            pl.BlockSpec((8, d), lambda l, j: (0, 0)),
            pl.BlockSpec((None, d, nc), lambda l, j: (l, 0, j)),
            pl.BlockSpec((None, 1, nc), lambda l, j: (l, 0, j)),
        ],
        out_specs=pl.BlockSpec((None, 8, nc), lambda l, j: (l, 0, j)),
        out_shape=jax.ShapeDtypeStruct((depth, 8, n), F32),
        compiler_params=_cparams(("arbitrary", "arbitrary"), 32),
        name="modvec",
    )(cvec, w_mod, b_mod.reshape(depth, 1, n))


def _rope_tables(seq_len):
    n_freq = HEAD_DIM // 4
    inv = ROPE_THETA ** (-jnp.arange(n_freq, dtype=F32) / n_freq)
    pos = jnp.arange(seq_len)
    row = (pos // GRID_W).astype(F32)[:, None] * inv[None, :]
    col = (pos % GRID_W).astype(F32)[:, None] * inv[None, :]
    zero = jnp.zeros_like(row)
    cos = jnp.concatenate([jnp.cos(row)] * 2 + [jnp.cos(col)] * 2, axis=-1)
    s_lo = jnp.concatenate([-jnp.sin(row), zero, -jnp.sin(col), zero], axis=-1)
    s_hi = jnp.concatenate([zero, jnp.sin(row), zero, jnp.sin(col)], axis=-1)
    return tuple(jnp.tile(t, (1, LANE // HEAD_DIM)) for t in (cos, s_lo, s_hi))


def _inproj_kernel(*refs, rope, kv_only, d):
    x_ref, g_ref, mod_ref, w_ref = refs[:4]
    refs = refs[4:]
    if rope:
        cos_ref, slo_ref, shi_ref = refs[:3]
        refs = refs[3:]

        def rot(z):
            return (z * cos_ref[...] + pltpu.roll(z, LANE - 16, 1) * slo_ref[...]
                    + pltpu.roll(z, 16, 1) * shi_ref[...])
    else:
        def rot(z):
            return z

    mod = mod_ref[...]
    h = _norm_mod(x_ref[...], g_ref[...], mod[:, 0:d], mod[:, d:2 * d]).astype(BF16)

    def proj(c0, c1):
        return jnp.dot(h, w_ref[:, c0:c1], preferred_element_type=F32)

    if kv_only:
        (kv_ref,) = refs
        z = proj(0, 2 * LANE)
        kv_ref[:, 0:LANE] = rot(z[:, 0:LANE]).astype(BF16)
        kv_ref[:, LANE:] = z[:, LANE:].astype(BF16)
        return

    q_ref, kv_ref, mix_ref, gate_ref = refs
    z = proj(0, Q_END)
    for j in range(Q_END // LANE):
        q_ref[:, j * LANE:(j + 1) * LANE] = (rot(z[:, j * LANE:(j + 1) * LANE]) * HEAD_DIM ** -0.5).astype(BF16)
    z = proj(Q_END, V_END)
    kv_ref[:, 0:LANE] = rot(z[:, 0:LANE]).astype(BF16)
    kv_ref[:, LANE:] = z[:, LANE:].astype(BF16)
    cw = 512
    for c in range(MIX_W // cw):
        mix_ref[:, c * cw:(c + 1) * cw] = proj(V_END + c * cw, V_END + (c + 1) * cw).astype(BF16)
    n_gate = gate_ref.shape[1]
    for c in range(n_gate // cw):
        zg = proj(POOL_END + c * cw, POOL_END + (c + 1) * cw)
        gate_ref[:, c * cw:(c + 1) * cw] = jax.nn.sigmoid(zg).astype(BF16)


def _inproj(x, norm_g, mod, w_in, *, seq_len, mod_row_of_tile, rope_tabs=None, kv_only=False):
    m, d = x.shape
    tm = min(512, seq_len)
    assert m % tm == 0 and seq_len % tm == 0
    tiles_per_seq = seq_len // tm
    rope = rope_tabs is not None
    in_w = w_in.shape[1]
    in_specs = [
        pl.BlockSpec((tm, d), lambda i: (i, 0)),
        pl.BlockSpec((1, d), lambda i: (0, 0)),
        pl.BlockSpec((None, 1, 6 * d), lambda i: (mod_row_of_tile(i, tm), 0, 0)),
        pl.BlockSpec((d, 2 * LANE), lambda i: (0, Q_END // (2 * LANE))) if kv_only else _resident((d, in_w)),
    ]
    args = [x, norm_g.reshape(1, d), mod, w_in]
    if rope:
        in_specs += [pl.BlockSpec((tm, LANE), lambda i: (i % tiles_per_seq, 0))] * 3
        args += list(rope_tabs)
    if kv_only:
        out_specs = pl.BlockSpec((tm, 2 * LANE), lambda i: (i, 0))
        out_shape = jax.ShapeDtypeStruct((m, 2 * LANE), BF16)
    else:
        widths = (Q_END, 2 * LANE, MIX_W, in_w - POOL_END)
        out_specs = [pl.BlockSpec((tm, w), lambda i: (i, 0)) for w in widths]
        out_shape = [jax.ShapeDtypeStruct((m, w), BF16) for w in widths]
    return pl.pallas_call(
        functools.partial(_inproj_kernel, rope=rope, kv_only=kv_only, d=d),
        grid=(m // tm,),
        in_specs=in_specs,
        out_specs=out_specs,
        out_shape=out_shape,
        compiler_params=_cparams(("parallel",), 48),
        name="inproj_kv" if kv_only else "inproj",
    )(*args)


def _attn_kernel(sink_ref, q_ref, *refs, band, seq_len):
    if band:
        kvp_ref, kvm_ref, kvn_ref, kvc_ref, o_ref = refs
    else:
        kvc_ref, o_ref = refs
    q = q_ref[0]
    kvc = kvc_ref[0]
    rows = GROUP * BLOCK
    nt = (((1,), (1,)), ((), ()))
    if band:
        n = pl.program_id(1)
        kvl = jnp.concatenate([kvp_ref[0], kvm_ref[0], kvn_ref[0]], axis=0)
        qi = lax.broadcasted_iota(jnp.int32, (rows, 3 * BLOCK), 0) & (BLOCK - 1)
        kj = lax.broadcasted_iota(jnp.int32, (rows, 3 * BLOCK), 1)
        kpos = (n - 1) * BLOCK + kj
        valid = (jnp.abs(kj - BLOCK - qi) <= WINDOW) & (kpos >= 0) & (kpos < seq_len)
    outs = []
    for k in range(N_KV_HEADS):
        heads = [GROUP * k + g for g in range(GROUP)]
        qk = jnp.concatenate([q[:, h * HEAD_DIM:(h + 1) * HEAD_DIM] for h in heads], axis=0)
        sink = jnp.concatenate([jnp.full((BLOCK, 1), sink_ref[h], F32) for h in heads], axis=0)
        ks = slice(k * HEAD_DIM, (k + 1) * HEAD_DIM)
        vs = slice(LANE + k * HEAD_DIM, LANE + (k + 1) * HEAD_DIM)
        s_ctx = lax.dot_general(qk, kvc[:, ks], nt, preferred_element_type=F32)
        m = jnp.maximum(jnp.max(s_ctx, axis=-1, keepdims=True), sink)
        if band:
            s_loc = lax.dot_general(qk, kvl[:, ks], nt, preferred_element_type=F32)
            s_loc = jnp.where(valid, s_loc, NEG_INF)
            m = jnp.maximum(m, jnp.max(s_loc, axis=-1, keepdims=True))
        p_ctx = jnp.exp(s_ctx - m)
        den = jnp.sum(p_ctx, axis=-1, keepdims=True) + jnp.exp(sink - m)
        o = jnp.dot(p_ctx.astype(BF16), kvc[:, vs], preferred_element_type=F32)
        if band:
            p_loc = jnp.exp(s_loc - m)
            den = den + jnp.sum(p_loc, axis=-1, keepdims=True)
            o = o + jnp.dot(p_loc.astype(BF16), kvl[:, vs], preferred_element_type=F32)
        o = o / den
        outs.append(jnp.concatenate([o[g * BLOCK:(g + 1) * BLOCK] for g in range(GROUP)], axis=1))
    o_ref[0] = jnp.concatenate(outs, axis=1).astype(BF16)


def _attention(q, kv, kvc, sink, *, band):
    b, l, _ = q.shape
    lc = kvc.shape[1]
    nb = l // BLOCK
    in_specs = [
        pl.BlockSpec(memory_space=pltpu.SMEM),
        pl.BlockSpec((1, BLOCK, N_HEADS * HEAD_DIM), lambda bi, n: (bi, n, 0)),
    ]
    args = [sink, q]
    if band:
        in_specs += [
            pl.BlockSpec((1, BLOCK, 2 * LANE), lambda bi, n: (bi, jnp.maximum(n - 1, 0), 0)),
            pl.BlockSpec((1, BLOCK, 2 * LANE), lambda bi, n: (bi, n, 0)),
            pl.BlockSpec((1, BLOCK, 2 * LANE), lambda bi, n: (bi, jnp.minimum(n + 1, nb - 1), 0)),
        ]
        args += [kv, kv, kv]
    in_specs.append(pl.BlockSpec((1, lc, 2 * LANE), lambda bi, n: (bi, 0, 0)))
    args.append(kvc)
    return pl.pallas_call(
        functools.partial(_attn_kernel, band=band, seq_len=l),
        grid=(b, nb),
        in_specs=in_specs,
        out_specs=pl.BlockSpec((1, BLOCK, N_HEADS * HEAD_DIM), lambda bi, n: (bi, n, 0)),
        out_shape=jax.ShapeDtypeStruct((b, l, N_HEADS * HEAD_DIM), BF16),
        compiler_params=_cparams(("parallel", "parallel"), 32),
        name="attn_band" if band else "attn_ctx",
    )(*args)


def _merge_kernel(attn_ref, mix_ref, prev_ref, next_ref, gate_ref, x_ref, mod_ref, n2g_ref, convw_ref,
                  poolw_ref, pscale_ref, wb_ref, wo_ref, xo_ref, h2_ref, *, tm, seq_len, d):
    tile = pl.program_id(0) % (seq_len // tm)
    keep_prev = (tile != 0).astype(F32)
    keep_next = (tile != seq_len // tm - 1).astype(F32)
    mixm = mix_ref[...]
    prev = prev_ref[...].astype(F32) * keep_prev
    nxt = next_ref[...].astype(F32) * keep_next
    b = BRANCH
    cx, cb, cc = (mixm[:, j * b:(j + 1) * b].astype(F32) for j in range(3))

    p = cc * cx
    p_prev = prev[HALO - 1:HALO, 2 * b:3 * b] * prev[HALO - 1:HALO, 0:b]
    p_next = nxt[0:1, 2 * b:3 * b] * nxt[0:1, 0:b]
    ridx = lax.broadcasted_iota(jnp.int32, (tm, b), 0)
    p_dn = jnp.where(ridx == 0, p_prev, pltpu.roll(p, 1, 0))
    p_up = jnp.where(ridx == tm - 1, p_next, pltpu.roll(p, tm - 1, 0))
    cw = convw_ref[...]
    conv_out = (cb * (p_dn * cw[0:1] + p * cw[1:2] + p_up * cw[2:3])).astype(BF16)

    u_main = mixm[:, 3 * b:4 * b].astype(F32)
    u_ext = jnp.concatenate([prev[:, 3 * b:4 * b], u_main, nxt[:, 3 * b:4 * b]], axis=0)
    ext = tm + 2 * HALO

    def shift(a, s):
        return pltpu.roll(a, s % ext, 0)

    tpos = tile * tm + lax.broadcasted_iota(jnp.int32, (tm, 1), 0)
    pooled = []
    for gi, w in enumerate(POOL_SIZES):
        gs = slice(gi * POOL_GROUP, (gi + 1) * POOL_GROUP)
        ug = u_ext[:, gs]
        a = ug + shift(ug, 1)
        ww = 2
        while ww < w:
            a = shift(a, ww // 2) + shift(a, -(ww // 2))
            ww *= 2
        cnt = jnp.minimum(tpos + w // 2, seq_len) - jnp.maximum(tpos - w // 2, 0)
        dlt = a[HALO:HALO + tm] / cnt.astype(F32) - u_main[:, gs]
        pooled.append(jnp.dot(dlt.astype(BF16), poolw_ref[gi], preferred_element_type=F32))
    pool_out = (jnp.concatenate(pooled, axis=1) * pscale_ref[...]).astype(BF16)

    y = (gate_ref[:, 0:d].astype(F32) * jnp.dot(attn_ref[...], wb_ref[0], preferred_element_type=F32)
         + gate_ref[:, d:2 * d].astype(F32) * jnp.dot(conv_out, wb_ref[1], preferred_element_type=F32)
         + gate_ref[:, 2 * d:3 * d].astype(F32) * jnp.dot(pool_out, wb_ref[2], preferred_element_type=F32))
    o = jnp.dot(y.astype(BF16), wo_ref[...], preferred_element_type=F32)
    mod = mod_ref[...]
    xn = x_ref[...] + mod[:, 2 * d:3 * d] * o
    xo_ref[...] = xn
    h2_ref[...] = _norm_mod(xn, n2g_ref[...], mod[:, 3 * d:4 * d], mod[:, 4 * d:5 * d]).astype(h2_ref.dtype)


def _merge(attn, mix, gate, x, mod, norm2_g, conv_w, pool_w, pool_scale, w_branch, w_out, *, seq_len,
           mod_row_of_tile, h2_dtype):
    m, d = x.shape
    tm = min(256, seq_len)
    assert m % tm == 0 and seq_len % tm == 0 and tm % HALO == 0
    hb = tm // HALO
    n_halo = m // HALO
    row = lambda w: pl.BlockSpec((tm, w), lambda i: (i, 0))
    in_specs = [
        row(BRANCH),
        row(MIX_W),
        pl.BlockSpec((HALO, MIX_W), lambda i: (jnp.maximum(i * hb - 1, 0), 0)),
        pl.BlockSpec((HALO, MIX_W), lambda i: (jnp.minimum((i + 1) * hb, n_halo - 1), 0)),
        row(3 * d),
        row(d),
        pl.BlockSpec((None, 1, 6 * d), lambda i: (mod_row_of_tile(i, tm), 0, 0)),
        pl.BlockSpec((1, d), lambda i: (0, 0)),
        _resident(conv_w.shape),
        _resident(pool_w.shape),
        pl.BlockSpec((1, BRANCH), lambda i: (0, 0)),
        _resident(w_branch.shape),
        _resident(w_out.shape),
    ]
    return pl.pallas_call(
        functools.partial(_merge_kernel, tm=tm, seq_len=seq_len, d=d),
        grid=(m // tm,),
        in_specs=in_specs,
        out_specs=[row(d), row(d)],
        out_shape=[jax.ShapeDtypeStruct((m, d), F32), jax.ShapeDtypeStruct((m, d), h2_dtype)],
        compiler_params=_cparams(("parallel",), 48),
        name="merge",
    )(attn, mix, mix, mix, gate, x, mod, norm2_g.reshape(1, d), conv_w, pool_w,
      pool_scale.reshape(1, BRANCH), w_branch, w_out)


def _ffn_kernel(h_ref, x_ref, mod_ref, wgu_ref, wd_ref, o_ref, *, d, d_ff, fc):
    h = h_ref[...]
    acc = None
    for f in range(0, d_ff, fc):
        g = jnp.dot(h, wgu_ref[:, f:f + fc], preferred_element_type=F32)
        u = jnp.dot(h, wgu_ref[:, d_ff + f:d_ff + f + fc], preferred_element_type=F32)
        a = (g * jax.nn.sigmoid(g) * u).astype(BF16)
        y = jnp.dot(a, wd_ref[f:f + fc, :], preferred_element_type=F32)
        acc = y if acc is None else acc + y
    o_ref[...] = x_ref[...] + mod_ref[...][:, 5 * d:6 * d] * acc


def _ffn_dense(h2, x, mod, w_gu, w_down, *, seq_len, mod_row_of_tile):
    m, d = x.shape
    d_ff = w_down.shape[0]
    tm = min(512, seq_len)
    assert m % tm == 0 and seq_len % tm == 0
    fc = _pick_chunk(d_ff, 1536)
    return pl.pallas_call(
        functools.partial(_ffn_kernel, d=d, d_ff=d_ff, fc=fc),
        grid=(m // tm,),
        in_specs=[
            pl.BlockSpec((tm, d), lambda i: (i, 0)),
            pl.BlockSpec((tm, d), lambda i: (i, 0)),
            pl.BlockSpec((None, 1, 6 * d), lambda i: (mod_row_of_tile(i, tm), 0, 0)),
            _resident(w_gu.shape),
            _resident(w_down.shape),
        ],
        out_specs=pl.BlockSpec((tm, d), lambda i: (i, 0)),
        out_shape=jax.ShapeDtypeStruct((m, d), F32),
        compiler_params=_cparams(("parallel",), 56),
        name="ffn_dense",
    )(h2, x, mod, w_gu, w_down)


ROUTE_ROWS = 8


def _router_kernel(h_ref, rw_ref, tri_ref, wts_ref, pairs_ref, cnt_ref, carry_ref, *, n_experts):
    @pl.when(pl.program_id(0) == 0)
    def _():
        carry_ref[...] = jnp.zeros_like(carry_ref)

    logits = jnp.dot(h_ref[...], rw_ref[...], preferred_element_type=F32, precision=lax.Precision.HIGHEST)
    lane = lax.broadcasted_iota(jnp.int32, logits.shape, 1).astype(F32)
    logits = jnp.where(lane < n_experts, logits, -jnp.inf)
    m1 = jnp.max(logits, axis=-1, keepdims=True)
    i1 = jnp.min(jnp.where(logits == m1, lane, float(LANE)), axis=-1, keepdims=True)
    rest = jnp.where(lane == i1, -jnp.inf, logits)
    m2 = jnp.max(rest, axis=-1, keepdims=True)
    i2 = jnp.min(jnp.where(rest == m2, lane, float(LANE)), axis=-1, keepdims=True)
    e2 = jnp.exp(m2 - m1)
    den = 1.0 + e2
    wts_ref[...] = jnp.where(lane == 0, 1.0 / den, jnp.where(lane == 1, e2 / den, 0.0))

    m1t = (lane == i1).astype(F32).T[0:ROUTE_ROWS]
    m2t = (lane == i2).astype(F32).T[0:ROUTE_ROWS]
    mem = m1t + m2t
    before = jnp.dot(mem.astype(BF16), tri_ref[...], preferred_element_type=F32) + carry_ref[:, 0:1]
    eid = lax.broadcasted_iota(jnp.int32, mem.shape, 0).astype(F32)
    rows = [jnp.sum(m1t * eid, axis=0, keepdims=True), jnp.sum(m2t * eid, axis=0, keepdims=True),
            jnp.sum(m1t * before, axis=0, keepdims=True), jnp.sum(m2t * before, axis=0, keepdims=True)]
    rows += [jnp.zeros_like(rows[0])] * (ROUTE_ROWS - len(rows))
    pairs_ref[...] = jnp.concatenate(rows, axis=0)
    carry_ref[...] = carry_ref[...] + jnp.sum(mem, axis=1, keepdims=True)
    cnt_ref[...] = carry_ref[...]


def _router(h2, router_w):
    m, d = h2.shape
    n_experts = router_w.shape[1]
    assert n_experts <= ROUTE_ROWS
    rw = jnp.pad(router_w, ((0, 0), (0, LANE - n_experts)))
    tm = min(512, m)
    tri = jnp.triu(jnp.ones((tm, tm), BF16), k=1)
    return pl.pallas_call(
        functools.partial(_router_kernel, n_experts=n_experts),
        grid=(m // tm,),
        in_specs=[
            pl.BlockSpec((tm, d), lambda i: (i, 0)),
            pl.BlockSpec((d, LANE), lambda i: (0, 0)),
            pl.BlockSpec((tm, tm), lambda i: (0, 0)),
        ],
        out_specs=[
            pl.BlockSpec((tm, LANE), lambda i: (i, 0)),
            pl.BlockSpec((ROUTE_ROWS, tm), lambda i: (0, i)),
            pl.BlockSpec((ROUTE_ROWS, LANE), lambda i: (0, 0)),
        ],
        out_shape=[jax.ShapeDtypeStruct((m, LANE), F32), jax.ShapeDtypeStruct((ROUTE_ROWS, m), F32),
                   jax.ShapeDtypeStruct((ROUTE_ROWS, LANE), F32)],
        scratch_shapes=[pltpu.VMEM((ROUTE_ROWS, LANE), F32)],
        compiler_params=_cparams(("arbitrary",), 32),
        name="router",
    )(h2, rw, tri)


def _plan_kernel(pairs_ref, cnt_ref, dest_ref, meta_ref, *, tile_rows):
    cnt = cnt_ref[...]
    padded = jnp.floor((cnt + (tile_rows - 1)) * (1.0 / tile_rows)) * tile_rows
    sub = lax.broadcasted_iota(jnp.int32, cnt.shape, 0)
    lane = lax.broadcasted_iota(jnp.int32, cnt.shape, 1)
    end_row = jnp.sum(jnp.where(sub <= lane, padded, 0.0), axis=0, keepdims=True)
    end_col = jnp.sum(jnp.where(lane == sub, end_row, 0.0), axis=1, keepdims=True)
    start_col = end_col - padded[:, 0:1]
    p = pairs_ref[...]
    eid = lax.broadcasted_iota(jnp.int32, p.shape, 0).astype(F32)
    d1 = jnp.sum(jnp.where(eid == p[0:1], start_col, 0.0), axis=0, keepdims=True) + p[2:3]
    d2 = jnp.sum(jnp.where(eid == p[1:2], start_col, 0.0), axis=0, keepdims=True) + p[3:4]
    dest_ref[...] = jnp.concatenate([d1, d2] + [jnp.zeros_like(d1)] * (ROUTE_ROWS - 2), axis=0).astype(jnp.int32)
    tile_start = (lane * tile_rows).astype(F32)
    tile_e = jnp.sum((end_col <= tile_start).astype(F32), axis=0, keepdims=True)
    n_tiles = jnp.max(end_col, axis=0, keepdims=True) * (1.0 / tile_rows)
    meta = jnp.concatenate([tile_e, jnp.broadcast_to(n_tiles, tile_e.shape)]
                           + [jnp.zeros_like(tile_e)] * (ROUTE_ROWS - 2), axis=0)
    meta_ref[...] = meta.astype(jnp.int32)


def _plan(pairs, counts, *, tile_rows):
    m = pairs.shape[1]
    tp = min(2048, m)
    return pl.pallas_call(
        functools.partial(_plan_kernel, tile_rows=tile_rows),
        grid=(m // tp,),
        in_specs=[pl.BlockSpec((ROUTE_ROWS, tp), lambda i: (0, i)), pl.BlockSpec((ROUTE_ROWS, LANE), lambda i: (0, 0))],
        out_specs=[pl.BlockSpec((ROUTE_ROWS, tp), lambda i: (0, i)), pl.BlockSpec((ROUTE_ROWS, LANE), lambda i: (0, 0))],
        out_shape=[jax.ShapeDtypeStruct((ROUTE_ROWS, m), jnp.int32), jax.ShapeDtypeStruct((ROUTE_ROWS, LANE), jnp.int32)],
        compiler_params=_cparams(("arbitrary",), 32),
        name="plan",
    )(pairs, counts)


def _row_copy(src, src_row, dst, dst_row, sem):
    return pltpu.make_async_copy(src.at[pl.ds(src_row, 1), :], dst.at[pl.ds(dst_row, 1), :], sem)


def _dispatch_kernel(dest_ref, h_ref, xs_in, xs_ref, sem, *, tm, n_tok):
    del xs_in
    base = pl.program_id(0) * tm

    def issue(r, carry):
        _row_copy(h_ref, r, xs_ref, dest_ref[base + r], sem).start()
        _row_copy(h_ref, r, xs_ref, dest_ref[n_tok + base + r], sem).start()
        return carry

    lax.fori_loop(0, tm, issue, 0)
    for _ in range(2):
        pltpu.make_async_copy(h_ref, xs_ref.at[pl.ds(0, tm), :], sem).wait()


def _dispatch(h2, dest, n_rows):
    m, d = h2.shape
    tm = min(512, m)
    return pl.pallas_call(
        functools.partial(_dispatch_kernel, tm=tm, n_tok=m),
        grid_spec=pltpu.PrefetchScalarGridSpec(
            num_scalar_prefetch=1,
            grid=(m // tm,),
            in_specs=[pl.BlockSpec((tm, d), lambda i, dr: (i, 0)), pl.BlockSpec(memory_space=pl.ANY)],
            out_specs=pl.BlockSpec(memory_space=pl.ANY),
            scratch_shapes=[pltpu.SemaphoreType.DMA],
        ),
        out_shape=jax.ShapeDtypeStruct((n_rows, d), F32),
        input_output_aliases={2: 0},
        compiler_params=_cparams(("arbitrary",), 32),
        name="dispatch",
    )(dest, h2, jnp.zeros((n_rows, d), F32))


def _gmm_kernel(meta_ref, xs_ref, wg_ref, wu_ref, wd_ref, o_ref, xb_ref, acc_ref):
    f = pl.program_id(1)
    nf = pl.num_programs(1)

    @pl.when(pl.program_id(0) < meta_ref[LANE])
    def _():
        @pl.when(f == 0)
        def _():
            xb_ref[...] = xs_ref[...].astype(BF16)

        xb = xb_ref[...]
        g = jnp.dot(xb, wg_ref[...], preferred_element_type=F32)
        u = jnp.dot(xb, wu_ref[...], preferred_element_type=F32)
        a = (g * jax.nn.sigmoid(g) * u).astype(BF16)
        y = jnp.dot(a, wd_ref[...], preferred_element_type=F32)

        @pl.when(f == 0)
        def _():
            acc_ref[...] = y

        @pl.when((f > 0) & (f < nf - 1))
        def _():
            acc_ref[...] += y

        @pl.when(f == nf - 1)
        def _():
            o_ref[...] = acc_ref[...] + y

    @pl.when((pl.program_id(0) >= meta_ref[LANE]) & (f == nf - 1))
    def _():
        o_ref[...] = jnp.zeros_like(o_ref)


def _grouped_swiglu(xs, meta, w_gu, w_down, *, tm):
    n_rows, d = xs.shape
    n_experts, d_ff, _ = w_down.shape
    fc = _pick_chunk(d_ff, 512)
    nf = d_ff // fc
    assert nf >= 2 and n_rows % tm == 0 and n_rows // tm <= LANE

    def tile(i, mt):
        return jnp.minimum(i, mt[LANE] - 1)

    def expert(i, mt):
        return jnp.minimum(mt[tile(i, mt)], n_experts - 1)

    def chunk(i, f, mt):
        return jnp.where(i < mt[LANE], f, nf - 1)

    return pl.pallas_call(
        _gmm_kernel,
        grid_spec=pltpu.PrefetchScalarGridSpec(
            num_scalar_prefetch=1,
            grid=(n_rows // tm, nf),
            in_specs=[
                pl.BlockSpec((tm, d), lambda i, f, mt: (tile(i, mt), 0)),
                pl.BlockSpec((None, d, fc), lambda i, f, mt: (expert(i, mt), 0, chunk(i, f, mt))),
                pl.BlockSpec((None, d, fc), lambda i, f, mt: (expert(i, mt), 0, nf + chunk(i, f, mt))),
                pl.BlockSpec((None, fc, d), lambda i, f, mt: (expert(i, mt), chunk(i, f, mt), 0)),
            ],
            out_specs=pl.BlockSpec((tm, d), lambda i, f, mt: (i, 0)),
            scratch_shapes=[pltpu.VMEM((tm, d), BF16), pltpu.VMEM((tm, d), F32)],
        ),
        out_shape=jax.ShapeDtypeStruct((n_rows, d), F32),
        compiler_params=_cparams(("arbitrary", "arbitrary"), 48),
        name="grouped_swiglu",
    )(meta, xs, w_gu, w_gu, w_down)


def _combine_kernel(dest_ref, x_ref, w_ref, mod_ref, fg_ref, ys_ref, o_ref, y1_ref, y2_ref, sem, *, tm, n_tok, d,
                    final_norm):
    base = pl.program_id(0) * tm

    def issue(r, carry):
        _row_copy(ys_ref, dest_ref[base + r], y1_ref, r, sem).start()
        _row_copy(ys_ref, dest_ref[n_tok + base + r], y2_ref, r, sem).start()
        return carry

    lax.fori_loop(0, tm, issue, 0)
    for buf in (y1_ref, y2_ref):
        pltpu.make_async_copy(ys_ref.at[pl.ds(0, tm), :], buf, sem).wait()
    w = w_ref[...]
    moe = w[:, 0:1] * y1_ref[...] + w[:, 1:2] * y2_ref[...]
    xn = x_ref[...] + mod_ref[...][:, 5 * d:6 * d] * moe
    if final_norm:
        xn = xn * lax.rsqrt(jnp.mean(xn * xn, axis=-1, keepdims=True) + EPS) * fg_ref[...]
    o_ref[...] = xn


def _combine(x, wts, dest, ys, mod, final_g, *, seq_len, mod_row_of_tile, final_norm):
    m, d = x.shape
    tm = min(512, seq_len)
    return pl.pallas_call(
        functools.partial(_combine_kernel, tm=tm, n_tok=m, d=d, final_norm=final_norm),
        grid_spec=pltpu.PrefetchScalarGridSpec(
            num_scalar_prefetch=1,
            grid=(m // tm,),
            in_specs=[
                pl.BlockSpec((tm, d), lambda i, dr: (i, 0)),
                pl.BlockSpec((tm, LANE), lambda i, dr: (i, 0)),
                pl.BlockSpec((None, 1, 6 * d), lambda i, dr: (mod_row_of_tile(i, tm), 0, 0)),
                pl.BlockSpec((1, d), lambda i, dr: (0, 0)),
                pl.BlockSpec(memory_space=pl.ANY),
            ],
            out_specs=pl.BlockSpec((tm, d), lambda i, dr: (i, 0)),
            scratch_shapes=[pltpu.VMEM((tm, d), F32), pltpu.VMEM((tm, d), F32), pltpu.SemaphoreType.DMA],
        ),
        out_shape=jax.ShapeDtypeStruct((m, d), F32),
        compiler_params=_cparams(("arbitrary",), 32),
        name="combine",
    )(dest, x, wts, mod, final_g.reshape(1, d), ys)


def _moe(h2, x, mod, final_g, router_w, w_gu, w_down, *, seq_len, mod_row_of_tile, final_norm):
    m, d = x.shape
    n_experts = router_w.shape[1]
    tm = 512
    n_rows = 2 * m + n_experts * tm
    wts, pairs, counts = _router(h2, router_w)
    dest2d, meta2d = _plan(pairs, counts, tile_rows=tm)
    dest = dest2d[0:2].reshape(2 * m)
    meta = meta2d[0:2].reshape(2 * LANE)
    xs = _dispatch(h2, dest, n_rows)
    ys = _grouped_swiglu(xs, meta, w_gu, w_down, tm=tm)
    return _combine(x, wts, dest, ys, mod, final_g, seq_len=seq_len, mod_row_of_tile=mod_row_of_tile,
                    final_norm=final_norm)


def kernel(x, c, ctx, c_ctx, norm1_g, norm2_g, final_g, w_mod, b_mod, w_in, conv_w, sink, pool_w, pool_scale,
           w_branch, w_out, ffn_w_gu, ffn_w_down, router_w, moe_w_gu, moe_w_down):
    bsz, seq, d = x.shape
    lc = ctx.shape[1]
    depth = w_in.shape[0]
    assert bsz + 1 <= 8 and seq % BLOCK == 0 and lc % BLOCK == 0 and seq % GRID_W == 0

    cvec = jnp.zeros((8, d), F32).at[:bsz].set(c).at[bsz].set(c_ctx)
    mods = _modvec(cvec, w_mod, b_mod)
    rope_tabs = _rope_tables(seq)

    lat_row = lambda i, tm: (i * tm) // seq
    ctx_row = lambda i, tm: bsz

    xl = x.reshape(bsz * seq, d)
    xc = ctx.reshape(bsz * lc, d)
    for l in range(depth):
        last = l == depth - 1
        mod = mods[l].reshape(8, 1, 6 * d)
        w_in_l = w_in[l].astype(BF16)
        wb_l, wo_l, pw_l = w_branch[l].astype(BF16), w_out[l].astype(BF16), pool_w[l].astype(BF16)
        routed = l % 2 == 1
        mixer = functools.partial(_merge, mod=mod, norm2_g=norm2_g[l], conv_w=conv_w[l], pool_w=pw_l,
                                  pool_scale=pool_scale[l], w_branch=wb_l, w_out=wo_l,
                                  h2_dtype=F32 if routed else BF16)
        if routed:
            wgu, wd = moe_w_gu[l // 2].astype(BF16), moe_w_down[l // 2].astype(BF16)
        else:
            wgu, wd = ffn_w_gu[l // 2].astype(BF16), ffn_w_down[l // 2].astype(BF16)

        def channel_mix(h2, xm, *, seq_len, row_fn, final_norm):
            if routed:
                return _moe(h2, xm, mod, final_g, router_w[l // 2], wgu, wd, seq_len=seq_len,
                            mod_row_of_tile=row_fn, final_norm=final_norm)
            assert not final_norm
            return _ffn_dense(h2, xm, mod, wgu, wd, seq_len=seq_len, mod_row_of_tile=row_fn)

        if last:
            kvc = _inproj(xc, norm1_g[l], mod, w_in_l, seq_len=lc, mod_row_of_tile=ctx_row, kv_only=True)
        else:
            qc, kvc, mixc, gatec = _inproj(xc, norm1_g[l], mod, w_in_l, seq_len=lc, mod_row_of_tile=ctx_row)
            attn_c = _attention(qc.reshape(bsz, lc, -1), None, kvc.reshape(bsz, lc, -1), sink[l], band=False)
            xc_mid, h2c = mixer(attn_c.reshape(bsz * lc, -1), mixc, gatec, xc, seq_len=lc, mod_row_of_tile=ctx_row)
            xc_next = channel_mix(h2c, xc_mid, seq_len=lc, row_fn=ctx_row, final_norm=False)
        q, kv, mix, gate = _inproj(xl, norm1_g[l], mod, w_in_l, seq_len=seq, mod_row_of_tile=lat_row,
                                   rope_tabs=rope_tabs)
        attn = _attention(q.reshape(bsz, seq, -1), kv.reshape(bsz, seq, -1), kvc.reshape(bsz, lc, -1), sink[l],
                          band=True)
        x_mid, h2 = mixer(attn.reshape(bsz * seq, -1), mix, gate, xl, seq_len=seq, mod_row_of_tile=lat_row)
        xl = channel_mix(h2, x_mid, seq_len=seq, row_fn=lat_row, final_norm=last and l % 2 == 1)
        if not last:
            xc = xc_next
    if depth % 2 == 1:
        raise NotImplementedError("final norm is fused into the expert layer; depth must be even")
    return xl.reshape(bsz, seq, d)
```

```python
import functools

import jax
import jax.numpy as jnp
from jax import lax
from jax.experimental import pallas as pl
from jax.experimental.pallas import tpu as pltpu

F32 = jnp.float32
BF16 = jnp.bfloat16

GRID_W = 64
EPS = 1e-6
NEG_INF = -1e30
HEAD_DIM = 64
N_HEADS = 8
N_KV_HEADS = 2
GROUP = N_HEADS // N_KV_HEADS
WINDOW = 128
BLOCK = 128
ROPE_THETA = 10000.0
BRANCH = 512
POOL_SIZES = (2, 4, 8, 16)
POOL_GROUP = 128
Q_END = 512
V_END = 768
MIX_W = 4 * BRANCH
POOL_END = V_END + MIX_W

LANE = 128
BF16_SUBLANE_TILE = 16
HALO = BF16_SUBLANE_TILE
MIB = 1024 * 1024


def _cparams(sem, vmem_mib):
    return pltpu.CompilerParams(dimension_semantics=sem, vmem_limit_bytes=vmem_mib * MIB)


def _pick_chunk(n, cap):
    best = None
    for c in range(LANE, min(n, cap) + 1, LANE):
        if n % c == 0:
            best = c
    assert best is not None, (n, cap)
    return best


def _resident(shape):
    nd = len(shape)
    return pl.BlockSpec(shape, lambda *_: (0,) * nd, pipeline_mode=pl.Buffered(1))


def _norm_mod(x, g, shift, scale):
    y = x * lax.rsqrt(jnp.mean(x * x, axis=-1, keepdims=True) + EPS) * g
    return y * (1.0 + scale) + shift


def _modvec_kernel(c_ref, w_ref, b_ref, o_ref):
    c = c_ref[...]
    s = c * jax.nn.sigmoid(c)
    o_ref[...] = jnp.dot(s, w_ref[...], preferred_element_type=F32) + b_ref[...]


def _modvec(cvec, w_mod, b_mod):
    depth, d, n = w_mod.shape
    nc = _pick_chunk(n, 1536)
    return pl.pallas_call(
        _modvec_kernel,
        grid=(depth, n // nc),
        in_specs=[
            pl.BlockSpec((8, d), lambda l, j: (0, 0)),
            pl.BlockSpec((None, d, nc), lambda l, j: (l, 0, j)),
            pl.BlockSpec((None, 1, nc), lambda l, j: (l, 0, j)),
        ],
        out_specs=pl.BlockSpec((None, 8, nc), lambda l, j: (l, 0, j)),
        out_shape=jax.ShapeDtypeStruct((depth, 8, n), F32),
        compiler_params=_cparams(("arbitrary", "arbitrary"), 32),
        name="modvec",
    )(cvec, w_mod, b_mod.reshape(depth, 1, n))


def _rope_tables(seq_len):
    n_freq = HEAD_DIM // 4
    inv = ROPE_THETA ** (-jnp.arange(n_freq, dtype=F32) / n_freq)
    pos = jnp.arange(seq_len)
    row = (pos // GRID_W).astype(F32)[:, None] * inv[None, :]
    col = (pos % GRID_W).astype(F32)[:, None] * inv[None, :]
    zero = jnp.zeros_like(row)
    cos = jnp.concatenate([jnp.cos(row)] * 2 + [jnp.cos(col)] * 2, axis=-1)
    s_lo = jnp.concatenate([-jnp.sin(row), zero, -jnp.sin(col), zero], axis=-1)
    s_hi = jnp.concatenate([zero, jnp.sin(row), zero, jnp.sin(col)], axis=-1)
    return tuple(jnp.tile(t, (1, LANE // HEAD_DIM)) for t in (cos, s_lo, s_hi))


def _inproj_kernel(*refs, rope, kv_only, d):
    x_ref, g_ref, mod_ref, w_ref = refs[:4]
    refs = refs[4:]
    if rope:
        cos_ref, slo_ref, shi_ref = refs[:3]
        refs = refs[3:]

        def rot(z):
            return (z * cos_ref[...] + pltpu.roll(z, LANE - 16, 1) * slo_ref[...]
                    + pltpu.roll(z, 16, 1) * shi_ref[...])
    else:
        def rot(z):
            return z

    mod = mod_ref[...]
    h = _norm_mod(x_ref[...], g_ref[...], mod[:, 0:d], mod[:, d:2 * d]).astype(BF16)

    def proj(c0, c1):
        return jnp.dot(h, w_ref[:, c0:c1], preferred_element_type=F32)

    if kv_only:
        (kv_ref,) = refs
        z = proj(0, 2 * LANE)
        kv_ref[:, 0:LANE] = rot(z[:, 0:LANE]).astype(BF16)
        kv_ref[:, LANE:] = z[:, LANE:].astype(BF16)
        return

    q_ref, kv_ref, mix_ref, gate_ref = refs
    z = proj(0, Q_END)
    for j in range(Q_END // LANE):
        q_ref[:, j * LANE:(j + 1) * LANE] = (rot(z[:, j * LANE:(j + 1) * LANE]) * HEAD_DIM ** -0.5).astype(BF16)
    z = proj(Q_END, V_END)
    kv_ref[:, 0:LANE] = rot(z[:, 0:LANE]).astype(BF16)
    kv_ref[:, LANE:] = z[:, LANE:].astype(BF16)
    cw = 512
    for c in range(MIX_W // cw):
        mix_ref[:, c * cw:(c + 1) * cw] = proj(V_END + c * cw, V_END + (c + 1) * cw).astype(BF16)
    n_gate = gate_ref.shape[1]
    for c in range(n_gate // cw):
        zg = proj(POOL_END + c * cw, POOL_END + (c + 1) * cw)
        gate_ref[:, c * cw:(c + 1) * cw] = jax.nn.sigmoid(zg).astype(BF16)


def _inproj(x, norm_g, mod, w_in, *, seq_len, mod_row_of_tile, rope_tabs=None, kv_only=False):
    m, d = x.shape
    tm = min(512, seq_len)
    assert m % tm == 0 and seq_len % tm == 0
    tiles_per_seq = seq_len // tm
    rope = rope_tabs is not None
    in_w = w_in.shape[1]
    in_specs = [
        pl.BlockSpec((tm, d), lambda i: (i, 0)),
        pl.BlockSpec((1, d), lambda i: (0, 0)),
        pl.BlockSpec((None, 1, 6 * d), lambda i: (mod_row_of_tile(i, tm), 0, 0)),
        pl.BlockSpec((d, 2 * LANE), lambda i: (0, Q_END // (2 * LANE))) if kv_only else _resident((d, in_w)),
    ]
    args = [x, norm_g.reshape(1, d), mod, w_in]
    if rope:
        in_specs += [pl.BlockSpec((tm, LANE), lambda i: (i % tiles_per_seq, 0))] * 3
        args += list(rope_tabs)
    if kv_only:
        out_specs = pl.BlockSpec((tm, 2 * LANE), lambda i: (i, 0))
        out_shape = jax.ShapeDtypeStruct((m, 2 * LANE), BF16)
    else:
        widths = (Q_END, 2 * LANE, MIX_W, in_w - POOL_END)
        out_specs = [pl.BlockSpec((tm, w), lambda i: (i, 0)) for w in widths]
        out_shape = [jax.ShapeDtypeStruct((m, w), BF16) for w in widths]
    return pl.pallas_call(
        functools.partial(_inproj_kernel, rope=rope, kv_only=kv_only, d=d),
        grid=(m // tm,),
        in_specs=in_specs,
        out_specs=out_specs,
        out_shape=out_shape,
        compiler_params=_cparams(("parallel",), 48),
        name="inproj_kv" if kv_only else "inproj",
    )(*args)


ATTN_STRIP = 32


def _attn_kernel(sink_ref, q_ref, *refs, band):
    if band:
        kvp_ref, kvm_ref, kvn_ref, kvc_ref, bias_ref, o_ref, s_ref, p_ref = refs
    else:
        kvc_ref, o_ref, s_ref, p_ref = refs
    q = q_ref[0]
    kvc = kvc_ref[0]
    lc = kvc.shape[0]
    nloc = 3 * BLOCK if band else 0
    nt = (((1,), (1,)), ((), ()))
    ones = jnp.ones((lc, LANE), BF16)
    w_ctx = jnp.concatenate([kvc[:, LANE:], ones], axis=1)
    if band:
        n = pl.program_id(1)
        kvl = jnp.concatenate([kvp_ref[0], kvm_ref[0], kvn_ref[0]], axis=0)
        w_loc = jnp.concatenate([kvl[:, LANE:], jnp.ones((nloc, LANE), BF16)], axis=1)
        col = lax.broadcasted_iota(jnp.int32, (1, nloc), 1)
        edge = jnp.where(col < BLOCK, jnp.where(n == 0, NEG_INF, 0.0),
                         jnp.where(col >= 2 * BLOCK, jnp.where(n == pl.num_programs(1) - 1, NEG_INF, 0.0), 0.0))
        bias = bias_ref[...] + edge
    outs = []
    for h in range(N_HEADS):
        k = h // GROUP
        slot = h % 2
        qh = q[:, h * HEAD_DIM:(h + 1) * HEAD_DIM]
        ks = slice(k * HEAD_DIM, (k + 1) * HEAD_DIM)
        if band:
            s_ref[slot, :, 0:nloc] = lax.dot_general(qh, kvl[:, ks], nt, preferred_element_type=F32) + bias
        s_ref[slot, :, nloc:] = lax.dot_general(qh, kvc[:, ks], nt, preferred_element_type=F32)
        sink = sink_ref[h]
        esink = []
        for r in range(0, BLOCK, ATTN_STRIP):
            s = s_ref[slot, r:r + ATTN_STRIP, :]
            m = jnp.maximum(jnp.max(s, axis=-1, keepdims=True), sink)
            p_ref[slot, r:r + ATTN_STRIP, :] = jnp.exp(s - m).astype(BF16)
            esink.append(jnp.exp(sink - m))
        o2 = jnp.dot(p_ref[slot, :, nloc:], w_ctx, preferred_element_type=F32)
        if band:
            o2 = o2 + jnp.dot(p_ref[slot, :, 0:nloc], w_loc, preferred_element_type=F32)
        den = o2[:, LANE:LANE + HEAD_DIM] + jnp.concatenate(esink, axis=0)
        outs.append(o2[:, ks] / den)
    o_ref[0] = jnp.concatenate(outs, axis=1).astype(BF16)


def _attention(q, kv, kvc, sink, *, band):
    b, l, _ = q.shape
    lc = kvc.shape[1]
    nb = l // BLOCK
    in_specs = [
        pl.BlockSpec(memory_space=pltpu.SMEM),
        pl.BlockSpec((1, BLOCK, N_HEADS * HEAD_DIM), lambda bi, n: (bi, n, 0)),
    ]
    args = [sink, q]
    if band:
        in_specs += [
            pl.BlockSpec((1, BLOCK, 2 * LANE), lambda bi, n: (bi, jnp.maximum(n - 1, 0), 0)),
            pl.BlockSpec((1, BLOCK, 2 * LANE), lambda bi, n: (bi, n, 0)),
            pl.BlockSpec((1, BLOCK, 2 * LANE), lambda bi, n: (bi, jnp.minimum(n + 1, nb - 1), 0)),
        ]
        args += [kv, kv, kv]
    in_specs.append(pl.BlockSpec((1, lc, 2 * LANE), lambda bi, n: (bi, 0, 0)))
    args.append(kvc)
    nkeys = lc
    if band:
        rel = jnp.arange(3 * BLOCK)[None, :] - BLOCK - jnp.arange(BLOCK)[:, None]
        in_specs.append(pl.BlockSpec((BLOCK, 3 * BLOCK), lambda bi, n: (0, 0)))
        args.append(jnp.where(jnp.abs(rel) <= WINDOW, 0.0, NEG_INF).astype(F32))
        nkeys += 3 * BLOCK
    return pl.pallas_call(
        functools.partial(_attn_kernel, band=band),
        grid=(b, nb),
        in_specs=in_specs,
        out_specs=pl.BlockSpec((1, BLOCK, N_HEADS * HEAD_DIM), lambda bi, n: (bi, n, 0)),
        out_shape=jax.ShapeDtypeStruct((b, l, N_HEADS * HEAD_DIM), BF16),
        scratch_shapes=[pltpu.VMEM((2, BLOCK, nkeys), F32), pltpu.VMEM((2, BLOCK, nkeys), BF16)],
        compiler_params=_cparams(("parallel", "parallel"), 32),
        name="attn_band" if band else "attn_ctx",
    )(*args)


def _merge_kernel(attn_ref, mix_ref, prev_ref, next_ref, gate_ref, x_ref, mod_ref, n2g_ref, convw_ref,
                  poolw_ref, pscale_ref, wb_ref, wo_ref, xo_ref, h2_ref, *, tm, seq_len, d):
    tile = pl.program_id(0) % (seq_len // tm)
    keep_prev = (tile != 0).astype(F32)
    keep_next = (tile != seq_len // tm - 1).astype(F32)
    mixm = mix_ref[...]
    prev = prev_ref[...].astype(F32) * keep_prev
    nxt = next_ref[...].astype(F32) * keep_next
    b = BRANCH
    cx, cb, cc = (mixm[:, j * b:(j + 1) * b].astype(F32) for j in range(3))

    p = cc * cx
    p_prev = prev[HALO - 1:HALO, 2 * b:3 * b] * prev[HALO - 1:HALO, 0:b]
    p_next = nxt[0:1, 2 * b:3 * b] * nxt[0:1, 0:b]
    ridx = lax.broadcasted_iota(jnp.int32, (tm, b), 0)
    p_dn = jnp.where(ridx == 0, p_prev, pltpu.roll(p, 1, 0))
    p_up = jnp.where(ridx == tm - 1, p_next, pltpu.roll(p, tm - 1, 0))
    cw = convw_ref[...]
    conv_out = (cb * (p_dn * cw[0:1] + p * cw[1:2] + p_up * cw[2:3])).astype(BF16)

    u_main = mixm[:, 3 * b:4 * b].astype(F32)
    u_ext = jnp.concatenate([prev[:, 3 * b:4 * b], u_main, nxt[:, 3 * b:4 * b]], axis=0)
    ext = tm + 2 * HALO

    def shift(a, s):
        return pltpu.roll(a, s % ext, 0)

    tpos = tile * tm + lax.broadcasted_iota(jnp.int32, (tm, 1), 0)
    pooled = []
    for gi, w in enumerate(POOL_SIZES):
        gs = slice(gi * POOL_GROUP, (gi + 1) * POOL_GROUP)
        ug = u_ext[:, gs]
        a = ug + shift(ug, 1)
        ww = 2
        while ww < w:
            a = shift(a, ww // 2) + shift(a, -(ww // 2))
            ww *= 2
        cnt = jnp.minimum(tpos + w // 2, seq_len) - jnp.maximum(tpos - w // 2, 0)
        dlt = a[HALO:HALO + tm] / cnt.astype(F32) - u_main[:, gs]
        pooled.append(jnp.dot(dlt.astype(BF16), poolw_ref[gi], preferred_element_type=F32))
    pool_out = (jnp.concatenate(pooled, axis=1) * pscale_ref[...]).astype(BF16)

    y = (gate_ref[:, 0:d].astype(F32) * jnp.dot(attn_ref[...], wb_ref[0], preferred_element_type=F32)
         + gate_ref[:, d:2 * d].astype(F32) * jnp.dot(conv_out, wb_ref[1], preferred_element_type=F32)
         + gate_ref[:, 2 * d:3 * d].astype(F32) * jnp.dot(pool_out, wb_ref[2], preferred_element_type=F32))
    o = jnp.dot(y.astype(BF16), wo_ref[...], preferred_element_type=F32)
    mod = mod_ref[...]
    xn = x_ref[...] + mod[:, 2 * d:3 * d] * o
    xo_ref[...] = xn
    h2_ref[...] = _norm_mod(xn, n2g_ref[...], mod[:, 3 * d:4 * d], mod[:, 4 * d:5 * d]).astype(h2_ref.dtype)


def _merge(attn, mix, gate, x, mod, norm2_g, conv_w, pool_w, pool_scale, w_branch, w_out, *, seq_len,
           mod_row_of_tile, h2_dtype):
    m, d = x.shape
    tm = min(256, seq_len)
    assert m % tm == 0 and seq_len % tm == 0 and tm % HALO == 0
    hb = tm // HALO
    n_halo = m // HALO
    row = lambda w: pl.BlockSpec((tm, w), lambda i: (i, 0))
    in_specs = [
        row(BRANCH),
        row(MIX_W),
        pl.BlockSpec((HALO, MIX_W), lambda i: (jnp.maximum(i * hb - 1, 0), 0)),
        pl.BlockSpec((HALO, MIX_W), lambda i: (jnp.minimum((i + 1) * hb, n_halo - 1), 0)),
        row(3 * d),
        row(d),
        pl.BlockSpec((None, 1, 6 * d), lambda i: (mod_row_of_tile(i, tm), 0, 0)),
        pl.BlockSpec((1, d), lambda i: (0, 0)),
        _resident(conv_w.shape),
        _resident(pool_w.shape),
        pl.BlockSpec((1, BRANCH), lambda i: (0, 0)),
        _resident(w_branch.shape),
        _resident(w_out.shape),
    ]
    return pl.pallas_call(
        functools.partial(_merge_kernel, tm=tm, seq_len=seq_len, d=d),
        grid=(m // tm,),
        in_specs=in_specs,
        out_specs=[row(d), row(d)],
        out_shape=[jax.ShapeDtypeStruct((m, d), F32), jax.ShapeDtypeStruct((m, d), h2_dtype)],
        compiler_params=_cparams(("parallel",), 48),
        name="merge",
    )(attn, mix, mix, mix, gate, x, mod, norm2_g.reshape(1, d), conv_w, pool_w,
      pool_scale.reshape(1, BRANCH), w_branch, w_out)


def _ffn_kernel(h_ref, x_ref, mod_ref, wgu_ref, wd_ref, o_ref, *, d, d_ff, fc):
    h = h_ref[...]
    acc = None
    for f in range(0, d_ff, fc):
        g = jnp.dot(h, wgu_ref[:, f:f + fc], preferred_element_type=F32)
        u = jnp.dot(h, wgu_ref[:, d_ff + f:d_ff + f + fc], preferred_element_type=F32)
        a = (g * jax.nn.sigmoid(g) * u).astype(BF16)
        y = jnp.dot(a, wd_ref[f:f + fc, :], preferred_element_type=F32)
        acc = y if acc is None else acc + y
    o_ref[...] = x_ref[...] + mod_ref[...][:, 5 * d:6 * d] * acc


def _ffn_dense(h2, x, mod, w_gu, w_down, *, seq_len, mod_row_of_tile):
    m, d = x.shape
    d_ff = w_down.shape[0]
    tm = min(512, seq_len)
    assert m % tm == 0 and seq_len % tm == 0
    fc = _pick_chunk(d_ff, 1536)
    return pl.pallas_call(
        functools.partial(_ffn_kernel, d=d, d_ff=d_ff, fc=fc),
        grid=(m // tm,),
        in_specs=[
            pl.BlockSpec((tm, d), lambda i: (i, 0)),
            pl.BlockSpec((tm, d), lambda i: (i, 0)),
            pl.BlockSpec((None, 1, 6 * d), lambda i: (mod_row_of_tile(i, tm), 0, 0)),
            _resident(w_gu.shape),
            _resident(w_down.shape),
        ],
        out_specs=pl.BlockSpec((tm, d), lambda i: (i, 0)),
        out_shape=jax.ShapeDtypeStruct((m, d), F32),
        compiler_params=_cparams(("parallel",), 56),
        name="ffn_dense",
    )(h2, x, mod, w_gu, w_down)


ROUTE_ROWS = 8


def _router_kernel(h_ref, rw_ref, tri_ref, wts_ref, pairs_ref, cnt_ref, carry_ref, *, n_experts):
    @pl.when(pl.program_id(0) == 0)
    def _():
        carry_ref[...] = jnp.zeros_like(carry_ref)

    h, w = h_ref[...], rw_ref[...]
    h_hi, w_hi = h.astype(BF16), w.astype(BF16)
    h_lo, w_lo = (h - h_hi.astype(F32)).astype(BF16), (w - w_hi.astype(F32)).astype(BF16)
    logits = (jnp.dot(h_hi, w_hi, preferred_element_type=F32) + jnp.dot(h_lo, w_hi, preferred_element_type=F32)
              + jnp.dot(h_hi, w_lo, preferred_element_type=F32))
    lane = lax.broadcasted_iota(jnp.int32, logits.shape, 1).astype(F32)
    logits = jnp.where(lane < n_experts, logits, -jnp.inf)
    m1 = jnp.max(logits, axis=-1, keepdims=True)
    i1 = jnp.min(jnp.where(logits == m1, lane, float(LANE)), axis=-1, keepdims=True)
    rest = jnp.where(lane == i1, -jnp.inf, logits)
    m2 = jnp.max(rest, axis=-1, keepdims=True)
    i2 = jnp.min(jnp.where(rest == m2, lane, float(LANE)), axis=-1, keepdims=True)
    e2 = jnp.exp(m2 - m1)
    den = 1.0 + e2
    wts_ref[...] = jnp.where(lane == 0, 1.0 / den, jnp.where(lane == 1, e2 / den, 0.0))

    m1t = (lane == i1).astype(F32).T[0:ROUTE_ROWS]
    m2t = (lane == i2).astype(F32).T[0:ROUTE_ROWS]
    mem = m1t + m2t
    before = jnp.dot(mem.astype(BF16), tri_ref[...], preferred_element_type=F32) + carry_ref[:, 0:1]
    eid = lax.broadcasted_iota(jnp.int32, mem.shape, 0).astype(F32)
    rows = [jnp.sum(m1t * eid, axis=0, keepdims=True), jnp.sum(m2t * eid, axis=0, keepdims=True),
            jnp.sum(m1t * before, axis=0, keepdims=True), jnp.sum(m2t * before, axis=0, keepdims=True)]
    rows += [jnp.zeros_like(rows[0])] * (ROUTE_ROWS - len(rows))
    pairs_ref[...] = jnp.concatenate(rows, axis=0)
    carry_ref[...] = carry_ref[...] + jnp.sum(mem, axis=1, keepdims=True)
    cnt_ref[...] = carry_ref[...]


def _router(h2, router_w):
    m, d = h2.shape
    n_experts = router_w.shape[1]
    assert n_experts <= ROUTE_ROWS
    rw = jnp.pad(router_w, ((0, 0), (0, LANE - n_experts)))
    tm = min(512, m)
    tri = jnp.triu(jnp.ones((tm, tm), BF16), k=1)
    return pl.pallas_call(
        functools.partial(_router_kernel, n_experts=n_experts),
        grid=(m // tm,),
        in_specs=[
            pl.BlockSpec((tm, d), lambda i: (i, 0)),
            pl.BlockSpec((d, LANE), lambda i: (0, 0)),
            pl.BlockSpec((tm, tm), lambda i: (0, 0)),
        ],
        out_specs=[
            pl.BlockSpec((tm, LANE), lambda i: (i, 0)),
            pl.BlockSpec((ROUTE_ROWS, tm), lambda i: (0, i)),
            pl.BlockSpec((ROUTE_ROWS, LANE), lambda i: (0, 0)),
        ],
        out_shape=[jax.ShapeDtypeStruct((m, LANE), F32), jax.ShapeDtypeStruct((ROUTE_ROWS, m), F32),
                   jax.ShapeDtypeStruct((ROUTE_ROWS, LANE), F32)],
        scratch_shapes=[pltpu.VMEM((ROUTE_ROWS, LANE), F32)],
        compiler_params=_cparams(("arbitrary",), 32),
        name="router",
    )(h2, rw, tri)


def _plan_kernel(pairs_ref, cnt_ref, dest_ref, meta_ref, *, tile_rows):
    cnt = cnt_ref[...]
    padded = jnp.floor((cnt + (tile_rows - 1)) * (1.0 / tile_rows)) * tile_rows
    sub = lax.broadcasted_iota(jnp.int32, cnt.shape, 0)
    lane = lax.broadcasted_iota(jnp.int32, cnt.shape, 1)
    end_row = jnp.sum(jnp.where(sub <= lane, padded, 0.0), axis=0, keepdims=True)
    end_col = jnp.sum(jnp.where(lane == sub, end_row, 0.0), axis=1, keepdims=True)
    start_col = end_col - padded[:, 0:1]
    p = pairs_ref[...]
    eid = lax.broadcasted_iota(jnp.int32, p.shape, 0).astype(F32)
    d1 = jnp.sum(jnp.where(eid == p[0:1], start_col, 0.0), axis=0, keepdims=True) + p[2:3]
    d2 = jnp.sum(jnp.where(eid == p[1:2], start_col, 0.0), axis=0, keepdims=True) + p[3:4]
    dest_ref[...] = jnp.concatenate([d1, d2] + [jnp.zeros_like(d1)] * (ROUTE_ROWS - 2), axis=0).astype(jnp.int32)
    tile_start = (lane * tile_rows).astype(F32)
    tile_e = jnp.sum((end_col <= tile_start).astype(F32), axis=0, keepdims=True)
    n_tiles = jnp.max(end_col, axis=0, keepdims=True) * (1.0 / tile_rows)
    meta = jnp.concatenate([tile_e, jnp.broadcast_to(n_tiles, tile_e.shape)]
                           + [jnp.zeros_like(tile_e)] * (ROUTE_ROWS - 2), axis=0)
    meta_ref[...] = meta.astype(jnp.int32)


def _plan(pairs, counts, *, tile_rows):
    m = pairs.shape[1]
    tp = min(2048, m)
    return pl.pallas_call(
        functools.partial(_plan_kernel, tile_rows=tile_rows),
        grid=(m // tp,),
        in_specs=[pl.BlockSpec((ROUTE_ROWS, tp), lambda i: (0, i)), pl.BlockSpec((ROUTE_ROWS, LANE), lambda i: (0, 0))],
        out_specs=[pl.BlockSpec((ROUTE_ROWS, tp), lambda i: (0, i)), pl.BlockSpec((ROUTE_ROWS, LANE), lambda i: (0, 0))],
        out_shape=[jax.ShapeDtypeStruct((ROUTE_ROWS, m), jnp.int32), jax.ShapeDtypeStruct((ROUTE_ROWS, LANE), jnp.int32)],
        compiler_params=_cparams(("arbitrary",), 32),
        name="plan",
    )(pairs, counts)


ROW_DMA_UNROLL = 8


def _row_copy(src, src_row, dst, dst_row, sem):
    return pltpu.make_async_copy(src.at[pl.ds(src_row, 1), :], dst.at[pl.ds(dst_row, 1), :], sem)


def _dispatch_kernel(dest_ref, h_ref, xs_in, xs_ref, sem, *, tm, n_tok):
    del xs_in
    base = pl.program_id(0) * tm

    def issue(blk, carry):
        for j in range(ROW_DMA_UNROLL):
            r = blk * ROW_DMA_UNROLL + j
            _row_copy(h_ref, r, xs_ref, dest_ref[base + r], sem).start()
            _row_copy(h_ref, r, xs_ref, dest_ref[n_tok + base + r], sem).start()
        return carry

    lax.fori_loop(0, tm // ROW_DMA_UNROLL, issue, 0)
    for _ in range(2):
        pltpu.make_async_copy(h_ref, xs_ref.at[pl.ds(0, tm), :], sem).wait()


def _dispatch(h2, dest, n_rows):
    m, d = h2.shape
    tm = min(512, m)
    return pl.pallas_call(
        functools.partial(_dispatch_kernel, tm=tm, n_tok=m),
        grid_spec=pltpu.PrefetchScalarGridSpec(
            num_scalar_prefetch=1,
            grid=(m // tm,),
            in_specs=[pl.BlockSpec((tm, d), lambda i, dr: (i, 0)), pl.BlockSpec(memory_space=pl.ANY)],
            out_specs=pl.BlockSpec(memory_space=pl.ANY),
            scratch_shapes=[pltpu.SemaphoreType.DMA],
        ),
        out_shape=jax.ShapeDtypeStruct((n_rows, d), F32),
        input_output_aliases={2: 0},
        compiler_params=_cparams(("arbitrary",), 32),
        name="dispatch",
    )(dest, h2, jnp.zeros((n_rows, d), F32))


def _gmm_kernel(meta_ref, xs_ref, wg_ref, wu_ref, wd_ref, o_ref, xb_ref, acc_ref):
    f = pl.program_id(1)
    nf = pl.num_programs(1)

    @pl.when(pl.program_id(0) < meta_ref[LANE])
    def _():
        @pl.when(f == 0)
        def _():
            xb_ref[...] = xs_ref[...].astype(BF16)

        xb = xb_ref[...]
        g = jnp.dot(xb, wg_ref[...], preferred_element_type=F32)
        u = jnp.dot(xb, wu_ref[...], preferred_element_type=F32)
        a = (g * jax.nn.sigmoid(g) * u).astype(BF16)
        y = jnp.dot(a, wd_ref[...], preferred_element_type=F32)

        @pl.when(f == 0)
        def _():
            acc_ref[...] = y

        @pl.when((f > 0) & (f < nf - 1))
        def _():
            acc_ref[...] += y

        @pl.when(f == nf - 1)
        def _():
            o_ref[...] = acc_ref[...] + y

    @pl.when((pl.program_id(0) >= meta_ref[LANE]) & (f == nf - 1))
    def _():
        o_ref[...] = jnp.zeros_like(o_ref)


def _grouped_swiglu(xs, meta, w_gu, w_down, *, tm):
    n_rows, d = xs.shape
    n_experts, d_ff, _ = w_down.shape
    fc = _pick_chunk(d_ff, d_ff // 2)
    nf = d_ff // fc
    assert nf >= 2 and n_rows % tm == 0 and n_rows // tm <= LANE

    def tile(i, mt):
        return jnp.minimum(i, mt[LANE] - 1)

    def expert(i, mt):
        return jnp.minimum(mt[tile(i, mt)], n_experts - 1)

    def chunk(i, f, mt):
        return jnp.where(i < mt[LANE], f, nf - 1)

    return pl.pallas_call(
        _gmm_kernel,
        grid_spec=pltpu.PrefetchScalarGridSpec(
            num_scalar_prefetch=1,
            grid=(n_rows // tm, nf),
            in_specs=[
                pl.BlockSpec((tm, d), lambda i, f, mt: (tile(i, mt), 0)),
                pl.BlockSpec((None, d, fc), lambda i, f, mt: (expert(i, mt), 0, chunk(i, f, mt))),
                pl.BlockSpec((None, d, fc), lambda i, f, mt: (expert(i, mt), 0, nf + chunk(i, f, mt))),
                pl.BlockSpec((None, fc, d), lambda i, f, mt: (expert(i, mt), chunk(i, f, mt), 0)),
            ],
            out_specs=pl.BlockSpec((tm, d), lambda i, f, mt: (i, 0)),
            scratch_shapes=[pltpu.VMEM((tm, d), BF16), pltpu.VMEM((tm, d), F32)],
        ),
        out_shape=jax.ShapeDtypeStruct((n_rows, d), F32),
        compiler_params=_cparams(("arbitrary", "arbitrary"), 56),
        name="grouped_swiglu",
    )(meta, xs, w_gu, w_gu, w_down)


def _combine_kernel(dest_ref, x_ref, w_ref, mod_ref, fg_ref, ys_ref, o_ref, y1_ref, y2_ref, sem, *, tm, n_tok, d,
                    final_norm):
    base = pl.program_id(0) * tm

    def issue(blk, carry):
        for j in range(ROW_DMA_UNROLL):
            r = blk * ROW_DMA_UNROLL + j
            _row_copy(ys_ref, dest_ref[base + r], y1_ref, r, sem).start()
            _row_copy(ys_ref, dest_ref[n_tok + base + r], y2_ref, r, sem).start()
        return carry

    lax.fori_loop(0, tm // ROW_DMA_UNROLL, issue, 0)
    for buf in (y1_ref, y2_ref):
        pltpu.make_async_copy(ys_ref.at[pl.ds(0, tm), :], buf, sem).wait()
    w = w_ref[...]
    moe = w[:, 0:1] * y1_ref[...] + w[:, 1:2] * y2_ref[...]
    xn = x_ref[...] + mod_ref[...][:, 5 * d:6 * d] * moe
    if final_norm:
        xn = xn * lax.rsqrt(jnp.mean(xn * xn, axis=-1, keepdims=True) + EPS) * fg_ref[...]
    o_ref[...] = xn


def _combine(x, wts, dest, ys, mod, final_g, *, seq_len, mod_row_of_tile, final_norm):
    m, d = x.shape
    tm = min(512, seq_len)
    return pl.pallas_call(
        functools.partial(_combine_kernel, tm=tm, n_tok=m, d=d, final_norm=final_norm),
        grid_spec=pltpu.PrefetchScalarGridSpec(
            num_scalar_prefetch=1,
            grid=(m // tm,),
            in_specs=[
                pl.BlockSpec((tm, d), lambda i, dr: (i, 0)),
                pl.BlockSpec((tm, LANE), lambda i, dr: (i, 0)),
                pl.BlockSpec((None, 1, 6 * d), lambda i, dr: (mod_row_of_tile(i, tm), 0, 0)),
                pl.BlockSpec((1, d), lambda i, dr: (0, 0)),
                pl.BlockSpec(memory_space=pl.ANY),
            ],
            out_specs=pl.BlockSpec((tm, d), lambda i, dr: (i, 0)),
            scratch_shapes=[pltpu.VMEM((tm, d), F32), pltpu.VMEM((tm, d), F32), pltpu.SemaphoreType.DMA],
        ),
        out_shape=jax.ShapeDtypeStruct((m, d), F32),
        compiler_params=_cparams(("arbitrary",), 32),
        name="combine",
    )(dest, x, wts, mod, final_g.reshape(1, d), ys)


def _moe(h2, x, mod, final_g, router_w, w_gu, w_down, *, seq_len, mod_row_of_tile, final_norm):
    m, d = x.shape
    n_experts = router_w.shape[1]
    tm = 512
    n_rows = 2 * m + n_experts * tm
    wts, pairs, counts = _router(h2, router_w)
    dest2d, meta2d = _plan(pairs, counts, tile_rows=tm)
    dest = dest2d[0:2].reshape(2 * m)
    meta = meta2d[0:2].reshape(2 * LANE)
    xs = _dispatch(h2, dest, n_rows)
    ys = _grouped_swiglu(xs, meta, w_gu, w_down, tm=tm)
    return _combine(x, wts, dest, ys, mod, final_g, seq_len=seq_len, mod_row_of_tile=mod_row_of_tile,
                    final_norm=final_norm)


def kernel(x, c, ctx, c_ctx, norm1_g, norm2_g, final_g, w_mod, b_mod, w_in, conv_w, sink, pool_w, pool_scale,
           w_branch, w_out, ffn_w_gu, ffn_w_down, router_w, moe_w_gu, moe_w_down):
    bsz, seq, d = x.shape
    lc = ctx.shape[1]
    depth = w_in.shape[0]
    assert bsz + 1 <= 8 and seq % BLOCK == 0 and lc % BLOCK == 0 and seq % GRID_W == 0

    cvec = jnp.zeros((8, d), F32).at[:bsz].set(c).at[bsz].set(c_ctx)
    mods = _modvec(cvec, w_mod, b_mod)
    rope_tabs = _rope_tables(seq)

    lat_row = lambda i, tm: (i * tm) // seq
    ctx_row = lambda i, tm: bsz

    xl = x.reshape(bsz * seq, d)
    xc = ctx.reshape(bsz * lc, d)
    for l in range(depth):
        last = l == depth - 1
        mod = mods[l].reshape(8, 1, 6 * d)
        w_in_l = w_in[l].astype(BF16)
        wb_l, wo_l, pw_l = w_branch[l].astype(BF16), w_out[l].astype(BF16), pool_w[l].astype(BF16)
        routed = l % 2 == 1
        mixer = functools.partial(_merge, mod=mod, norm2_g=norm2_g[l], conv_w=conv_w[l], pool_w=pw_l,
                                  pool_scale=pool_scale[l], w_branch=wb_l, w_out=wo_l,
                                  h2_dtype=F32 if routed else BF16)
        if routed:
            wgu, wd = moe_w_gu[l // 2].astype(BF16), moe_w_down[l // 2].astype(BF16)
        else:
            wgu, wd = ffn_w_gu[l // 2].astype(BF16), ffn_w_down[l // 2].astype(BF16)

        def channel_mix(h2, xm, *, seq_len, row_fn, final_norm):
            if routed:
                return _moe(h2, xm, mod, final_g, router_w[l // 2], wgu, wd, seq_len=seq_len,
                            mod_row_of_tile=row_fn, final_norm=final_norm)
            assert not final_norm
            return _ffn_dense(h2, xm, mod, wgu, wd, seq_len=seq_len, mod_row_of_tile=row_fn)

        if last:
            kvc = _inproj(xc, norm1_g[l], mod, w_in_l, seq_len=lc, mod_row_of_tile=ctx_row, kv_only=True)
        else:
            qc, kvc, mixc, gatec = _inproj(xc, norm1_g[l], mod, w_in_l, seq_len=lc, mod_row_of_tile=ctx_row)
            attn_c = _attention(qc.reshape(bsz, lc, -1), None, kvc.reshape(bsz, lc, -1), sink[l], band=False)
            xc_mid, h2c = mixer(attn_c.reshape(bsz * lc, -1), mixc, gatec, xc, seq_len=lc, mod_row_of_tile=ctx_row)
            xc_next = channel_mix(h2c, xc_mid, seq_len=lc, row_fn=ctx_row, final_norm=False)
        q, kv, mix, gate = _inproj(xl, norm1_g[l], mod, w_in_l, seq_len=seq, mod_row_of_tile=lat_row,
                                   rope_tabs=rope_tabs)
        attn = _attention(q.reshape(bsz, seq, -1), kv.reshape(bsz, seq, -1), kvc.reshape(bsz, lc, -1), sink[l],
                          band=True)
        x_mid, h2 = mixer(attn.reshape(bsz * seq, -1), mix, gate, xl, seq_len=seq, mod_row_of_tile=lat_row)
        xl = channel_mix(h2, x_mid, seq_len=seq, row_fn=lat_row, final_norm=last and l % 2 == 1)
        if not last:
            xc = xc_next
    if depth % 2 == 1:
        raise NotImplementedError("final norm is fused into the expert layer; depth must be even")
    return xl.reshape(bsz, seq, d)
```

```python
import functools

import jax
import jax.numpy as jnp
from jax import lax
from jax.experimental import pallas as pl
from jax.experimental.pallas import tpu as pltpu

F32 = jnp.float32
BF16 = jnp.bfloat16

GRID_W = 64
EPS = 1e-6
NEG_INF = -1e30
HEAD_DIM = 64
N_HEADS = 8
N_KV_HEADS = 2
GROUP = N_HEADS // N_KV_HEADS
WINDOW = 128
BLOCK = 128
ROPE_THETA = 10000.0
BRANCH = 512
POOL_SIZES = (2, 4, 8, 16)
POOL_GROUP = 128
Q_END = 512
V_END = 768
MIX_W = 4 * BRANCH
POOL_END = V_END + MIX_W

LANE = 128
BF16_SUBLANE_TILE = 16
HALO = BF16_SUBLANE_TILE
MIB = 1024 * 1024


def _cparams(sem, vmem_mib):
    return pltpu.CompilerParams(dimension_semantics=sem, vmem_limit_bytes=vmem_mib * MIB)


def _pick_chunk(n, cap):
    best = None
    for c in range(LANE, min(n, cap) + 1, LANE):
        if n % c == 0:
            best = c
    assert best is not None, (n, cap)
    return best


def _resident(shape):
    nd = len(shape)
    return pl.BlockSpec(shape, lambda *_: (0,) * nd, pipeline_mode=pl.Buffered(1))


def _norm_mod(x, g, shift, scale):
    y = x * lax.rsqrt(jnp.mean(x * x, axis=-1, keepdims=True) + EPS) * g
    return y * (1.0 + scale) + shift


def _modvec_kernel(c_ref, w_ref, b_ref, o_ref):
    c = c_ref[...]
    s = c * jax.nn.sigmoid(c)
    o_ref[...] = jnp.dot(s, w_ref[...], preferred_element_type=F32) + b_ref[...]


def _modvec(cvec, w_mod, b_mod):
    depth, d, n = w_mod.shape
    nc = _pick_chunk(n, 1536)
    return pl.pallas_call(
        _modvec_kernel,
        grid=(depth, n // nc),
        in_specs=[
            pl.BlockSpec((8, d), lambda l, j: (0, 0)),
            pl.BlockSpec((None, d, nc), lambda l, j: (l, 0, j)),
            pl.BlockSpec((None, 1, nc), lambda l, j: (l, 0, j)),
        ],
        out_specs=pl.BlockSpec((None, 8, nc), lambda l, j: (l, 0, j)),
        out_shape=jax.ShapeDtypeStruct((depth, 8, n), F32),
        compiler_params=_cparams(("arbitrary", "arbitrary"), 32),
        name="modvec",
    )(cvec, w_mod, b_mod.reshape(depth, 1, n))


def _rope_tables(seq_len):
    n_freq = HEAD_DIM // 4
    inv = ROPE_THETA ** (-jnp.arange(n_freq, dtype=F32) / n_freq)
    pos = jnp.arange(seq_len)
    row = (pos // GRID_W).astype(F32)[:, None] * inv[None, :]
    col = (pos % GRID_W).astype(F32)[:, None] * inv[None, :]
    zero = jnp.zeros_like(row)
    cos = jnp.concatenate([jnp.cos(row)] * 2 + [jnp.cos(col)] * 2, axis=-1)
    s_lo = jnp.concatenate([-jnp.sin(row), zero, -jnp.sin(col), zero], axis=-1)
    s_hi = jnp.concatenate([zero, jnp.sin(row), zero, jnp.sin(col)], axis=-1)
    return tuple(jnp.tile(t, (1, LANE // HEAD_DIM)) for t in (cos, s_lo, s_hi))


def _inproj_kernel(*refs, rope, kv_only, d):
    x_ref, g_ref, mod_ref, w_ref = refs[:4]
    refs = refs[4:]
    if rope:
        cos_ref, slo_ref, shi_ref = refs[:3]
        refs = refs[3:]

        def rot(z):
            return (z * cos_ref[...] + pltpu.roll(z, LANE - 16, 1) * slo_ref[...]
                    + pltpu.roll(z, 16, 1) * shi_ref[...])
    else:
        def rot(z):
            return z

    mod = mod_ref[...]
    h = _norm_mod(x_ref[...], g_ref[...], mod[:, 0:d], mod[:, d:2 * d]).astype(BF16)

    def proj(c0, c1):
        return jnp.dot(h, w_ref[:, c0:c1], preferred_element_type=F32)

    if kv_only:
        (kv_ref,) = refs
        z = proj(0, 2 * LANE)
        kv_ref[:, 0:LANE] = rot(z[:, 0:LANE]).astype(BF16)
        kv_ref[:, LANE:] = z[:, LANE:].astype(BF16)
        return

    q_ref, kv_ref, mix_ref, gate_ref = refs
    z = proj(0, Q_END)
    for j in range(Q_END // LANE):
        q_ref[:, j * LANE:(j + 1) * LANE] = (rot(z[:, j * LANE:(j + 1) * LANE]) * HEAD_DIM ** -0.5).astype(BF16)
    z = proj(Q_END, V_END)
    kv_ref[:, 0:LANE] = rot(z[:, 0:LANE]).astype(BF16)
    kv_ref[:, LANE:] = z[:, LANE:].astype(BF16)
    cw = 512
    for c in range(MIX_W // cw):
        mix_ref[:, c * cw:(c + 1) * cw] = proj(V_END + c * cw, V_END + (c + 1) * cw).astype(BF16)
    n_gate = gate_ref.shape[1]
    for c in range(n_gate // cw):
        zg = proj(POOL_END + c * cw, POOL_END + (c + 1) * cw)
        gate_ref[:, c * cw:(c + 1) * cw] = jax.nn.sigmoid(zg).astype(BF16)


def _inproj(x, norm_g, mod, w_in, *, seq_len, mod_row_of_tile, rope_tabs=None, kv_only=False):
    m, d = x.shape
    tm = min(512, seq_len)
    assert m % tm == 0 and seq_len % tm == 0
    tiles_per_seq = seq_len // tm
    rope = rope_tabs is not None
    in_w = w_in.shape[1]
    in_specs = [
        pl.BlockSpec((tm, d), lambda i: (i, 0)),
        pl.BlockSpec((1, d), lambda i: (0, 0)),
        pl.BlockSpec((None, 1, 6 * d), lambda i: (mod_row_of_tile(i, tm), 0, 0)),
        pl.BlockSpec((d, 2 * LANE), lambda i: (0, Q_END // (2 * LANE))) if kv_only else _resident((d, in_w)),
    ]
    args = [x, norm_g.reshape(1, d), mod, w_in]
    if rope:
        in_specs += [pl.BlockSpec((tm, LANE), lambda i: (i % tiles_per_seq, 0))] * 3
        args += list(rope_tabs)
    if kv_only:
        out_specs = pl.BlockSpec((tm, 2 * LANE), lambda i: (i, 0))
        out_shape = jax.ShapeDtypeStruct((m, 2 * LANE), BF16)
    else:
        widths = (Q_END, 2 * LANE, MIX_W, in_w - POOL_END)
        out_specs = [pl.BlockSpec((tm, w), lambda i: (i, 0)) for w in widths]
        out_shape = [jax.ShapeDtypeStruct((m, w), BF16) for w in widths]
    return pl.pallas_call(
        functools.partial(_inproj_kernel, rope=rope, kv_only=kv_only, d=d),
        grid=(m // tm,),
        in_specs=in_specs,
        out_specs=out_specs,
        out_shape=out_shape,
        compiler_params=_cparams(("parallel",), 48),
        name="inproj_kv" if kv_only else "inproj",
    )(*args)


ATTN_STRIP = 32
ATTN_QBLOCKS = 2
ATTN_AHEAD = 2


def _attn_kernel(sink_ref, q_ref, *refs, band, carry_cast):
    refs, (s_ref, p_ref) = list(refs[:-2]), refs[-2:]
    if carry_cast:
        cast_out = refs.pop()
        cast_in = refs.pop(-2)
        cast_out[...] = cast_in[...].astype(BF16)
    if band:
        kvp_ref, kvm_ref, kvn_ref, kvc_ref, bias_ref, o_ref = refs
    else:
        kvc_ref, o_ref = refs
    kvc = kvc_ref[0]
    lc = kvc.shape[0]
    nloc = 3 * BLOCK if band else 0
    nt = (((1,), (1,)), ((), ()))
    w_ctx = jnp.concatenate([kvc[:, LANE:], jnp.ones((lc, LANE), BF16)], axis=1)
    if band:
        n = pl.program_id(1)
        kv4 = jnp.concatenate([kvp_ref[0], kvm_ref[0], kvn_ref[0]], axis=0)
        w4 = jnp.concatenate([kv4[:, LANE:], jnp.ones((4 * BLOCK, LANE), BF16)], axis=1)
        col = lax.broadcasted_iota(jnp.int32, (1, nloc), 1)
        head_edge = jnp.where((col < BLOCK) & (n == 0), NEG_INF, 0.0)
        tail_edge = jnp.where((col >= 2 * BLOCK) & (n == pl.num_programs(1) - 1), NEG_INF, 0.0)
        biases = [bias_ref[...] + head_edge, bias_ref[...] + tail_edge]
    units = [(sb, h) for sb in range(ATTN_QBLOCKS) for h in range(N_HEADS)]
    def scores(u):
        sb, h = units[u]
        qh = q_ref[0, sb * BLOCK:(sb + 1) * BLOCK, h * HEAD_DIM:(h + 1) * HEAD_DIM]
        ks = slice(h // GROUP * HEAD_DIM, (h // GROUP + 1) * HEAD_DIM)
        if band:
            kl = kv4[sb * BLOCK:sb * BLOCK + nloc, ks]
            s_ref[u, :, 0:nloc] = lax.dot_general(qh, kl, nt, preferred_element_type=F32) + biases[sb]
        s_ref[u, :, nloc:] = lax.dot_general(qh, kvc[:, ks], nt, preferred_element_type=F32)

    def probs(u):
        sink = sink_ref[units[u][1]]
        esink = []
        for r in range(0, BLOCK, ATTN_STRIP):
            s = s_ref[u, r:r + ATTN_STRIP, :]
            m = jnp.maximum(jnp.max(s, axis=-1, keepdims=True), sink)
            p_ref[u, r:r + ATTN_STRIP, :] = jnp.exp(s - m).astype(BF16)
            esink.append(jnp.exp(sink - m))
        return jnp.concatenate(esink, axis=0)

    def weighted_values(u, esink):
        sb, h = units[u]
        ks = slice(h // GROUP * HEAD_DIM, (h // GROUP + 1) * HEAD_DIM)
        o2 = jnp.dot(p_ref[u, :, nloc:], w_ctx, preferred_element_type=F32)
        if band:
            o2 = o2 + jnp.dot(p_ref[u, :, 0:nloc], w4[sb * BLOCK:sb * BLOCK + nloc], preferred_element_type=F32)
        return o2[:, ks] / (o2[:, LANE:LANE + HEAD_DIM] + esink)

    outs, esinks = [], []
    for u in range(min(ATTN_AHEAD, len(units))):
        scores(u)
    for u in range(len(units)):
        if u + ATTN_AHEAD < len(units):
            scores(u + ATTN_AHEAD)
        esinks.append(probs(u))
        if u >= 1:
            outs.append(weighted_values(u - 1, esinks[u - 1]))
    outs.append(weighted_values(len(units) - 1, esinks[-1]))
    for sb in range(ATTN_QBLOCKS):
        o_ref[0, sb * BLOCK:(sb + 1) * BLOCK, :] = jnp.concatenate(
            outs[sb * N_HEADS:(sb + 1) * N_HEADS], axis=1).astype(BF16)


def _cast_rider(w, n_steps, index_map):
    rows, cols = w.shape
    assert rows % (n_steps * BF16_SUBLANE_TILE) == 0, (w.shape, n_steps)
    return pl.BlockSpec((rows // n_steps, cols), index_map), jax.ShapeDtypeStruct((rows, cols), BF16)


def _attention(q, kv, kvc, sink, *, band, cast=None):
    b, l, _ = q.shape
    lc = kvc.shape[1]
    tq = ATTN_QBLOCKS * BLOCK
    assert l % tq == 0
    nb = l // BLOCK
    in_specs = [
        pl.BlockSpec(memory_space=pltpu.SMEM),
        pl.BlockSpec((1, tq, N_HEADS * HEAD_DIM), lambda bi, n: (bi, n, 0)),
    ]
    args = [sink, q]
    if band:
        in_specs += [
            pl.BlockSpec((1, BLOCK, 2 * LANE), lambda bi, n: (bi, jnp.maximum(ATTN_QBLOCKS * n - 1, 0), 0)),
            pl.BlockSpec((1, tq, 2 * LANE), lambda bi, n: (bi, n, 0)),
            pl.BlockSpec((1, BLOCK, 2 * LANE), lambda bi, n: (bi, jnp.minimum(ATTN_QBLOCKS * (n + 1), nb - 1), 0)),
        ]
        args += [kv, kv, kv]
    in_specs.append(pl.BlockSpec((1, lc, 2 * LANE), lambda bi, n: (bi, 0, 0)))
    args.append(kvc)
    nkeys = lc
    if band:
        rel = jnp.arange(3 * BLOCK)[None, :] - BLOCK - jnp.arange(BLOCK)[:, None]
        in_specs.append(pl.BlockSpec((BLOCK, 3 * BLOCK), lambda bi, n: (0, 0)))
        args.append(jnp.where(jnp.abs(rel) <= WINDOW, 0.0, NEG_INF).astype(F32))
        nkeys += 3 * BLOCK
    nq = l // tq
    out_specs = [pl.BlockSpec((1, tq, N_HEADS * HEAD_DIM), lambda bi, n: (bi, n, 0))]
    out_shape = [jax.ShapeDtypeStruct((b, l, N_HEADS * HEAD_DIM), BF16)]
    if cast is not None:
        spec, shape = _cast_rider(cast, b * nq, lambda bi, n: (bi * nq + n, 0))
        in_specs.append(spec)
        args.append(cast)
        out_specs.append(spec)
        out_shape.append(shape)
    outs = pl.pallas_call(
        functools.partial(_attn_kernel, band=band, carry_cast=cast is not None),
        grid=(b, nq),
        in_specs=in_specs,
        out_specs=out_specs,
        out_shape=out_shape,
        scratch_shapes=[pltpu.VMEM((ATTN_QBLOCKS * N_HEADS, BLOCK, nkeys), F32),
                        pltpu.VMEM((ATTN_QBLOCKS * N_HEADS, BLOCK, nkeys), BF16)],
        compiler_params=_cparams(("parallel", "parallel"), 48),
        name="attn_band" if band else "attn_ctx",
    )(*args)
    return outs[0] if cast is None else outs


def _merge_kernel(attn_ref, mix_ref, prev_ref, next_ref, gate_ref, x_ref, mod_ref, n2g_ref, convw_ref,
                  poolw_ref, pscale_ref, wb_ref, wo_ref, *refs, tm, seq_len, d):
    if len(refs) == 4:
        cast_in, xo_ref, h2_ref, cast_out = refs
        cast_out[...] = cast_in[...].astype(BF16)
    else:
        xo_ref, h2_ref = refs
    tile = pl.program_id(0) % (seq_len // tm)
    keep_prev = (tile != 0).astype(F32)
    keep_next = (tile != seq_len // tm - 1).astype(F32)
    mixm = mix_ref[...]
    prev = prev_ref[...].astype(F32) * keep_prev
    nxt = next_ref[...].astype(F32) * keep_next
    b = BRANCH
    cx, cb, cc = (mixm[:, j * b:(j + 1) * b].astype(F32) for j in range(3))

    p = cc * cx
    p_prev = prev[HALO - 1:HALO, 2 * b:3 * b] * prev[HALO - 1:HALO, 0:b]
    p_next = nxt[0:1, 2 * b:3 * b] * nxt[0:1, 0:b]
    ridx = lax.broadcasted_iota(jnp.int32, (tm, b), 0)
    p_dn = jnp.where(ridx == 0, p_prev, pltpu.roll(p, 1, 0))
    p_up = jnp.where(ridx == tm - 1, p_next, pltpu.roll(p, tm - 1, 0))
    cw = convw_ref[...]
    conv_out = (cb * (p_dn * cw[0:1] + p * cw[1:2] + p_up * cw[2:3])).astype(BF16)

    u_main = mixm[:, 3 * b:4 * b].astype(F32)
    u_ext = jnp.concatenate([prev[:, 3 * b:4 * b], u_main, nxt[:, 3 * b:4 * b]], axis=0)
    ext = tm + 2 * HALO

    def shift(a, s):
        return pltpu.roll(a, s % ext, 0)

    tpos = tile * tm + lax.broadcasted_iota(jnp.int32, (tm, 1), 0)
    pooled = []
    for gi, w in enumerate(POOL_SIZES):
        gs = slice(gi * POOL_GROUP, (gi + 1) * POOL_GROUP)
        ug = u_ext[:, gs]
        a = ug + shift(ug, 1)
        ww = 2
        while ww < w:
            a = shift(a, ww // 2) + shift(a, -(ww // 2))
            ww *= 2
        cnt = jnp.minimum(tpos + w // 2, seq_len) - jnp.maximum(tpos - w // 2, 0)
        dlt = a[HALO:HALO + tm] / cnt.astype(F32) - u_main[:, gs]
        pooled.append(jnp.dot(dlt.astype(BF16), poolw_ref[gi], preferred_element_type=F32))
    pool_out = (jnp.concatenate(pooled, axis=1) * pscale_ref[...]).astype(BF16)

    y = (gate_ref[:, 0:d].astype(F32) * jnp.dot(attn_ref[...], wb_ref[0], preferred_element_type=F32)
         + gate_ref[:, d:2 * d].astype(F32) * jnp.dot(conv_out, wb_ref[1], preferred_element_type=F32)
         + gate_ref[:, 2 * d:3 * d].astype(F32) * jnp.dot(pool_out, wb_ref[2], preferred_element_type=F32))
    o = jnp.dot(y.astype(BF16), wo_ref[...], preferred_element_type=F32)
    mod = mod_ref[...]
    xn = x_ref[...] + mod[:, 2 * d:3 * d] * o
    xo_ref[...] = xn
    h2_ref[...] = _norm_mod(xn, n2g_ref[...], mod[:, 3 * d:4 * d], mod[:, 4 * d:5 * d]).astype(h2_ref.dtype)


def _merge(attn, mix, gate, x, mod, norm2_g, conv_w, pool_w, pool_scale, w_branch, w_out, *, seq_len,
           mod_row_of_tile, h2_dtype, cast=None):
    m, d = x.shape
    tm = min(256, seq_len)
    assert m % tm == 0 and seq_len % tm == 0 and tm % HALO == 0
    hb = tm // HALO
    n_halo = m // HALO
    row = lambda w: pl.BlockSpec((tm, w), lambda i: (i, 0))
    in_specs = [
        row(BRANCH),
        row(MIX_W),
        pl.BlockSpec((HALO, MIX_W), lambda i: (jnp.maximum(i * hb - 1, 0), 0)),
        pl.BlockSpec((HALO, MIX_W), lambda i: (jnp.minimum((i + 1) * hb, n_halo - 1), 0)),
        row(3 * d),
        row(d),
        pl.BlockSpec((None, 1, 6 * d), lambda i: (mod_row_of_tile(i, tm), 0, 0)),
        pl.BlockSpec((1, d), lambda i: (0, 0)),
        _resident(conv_w.shape),
        _resident(pool_w.shape),
        pl.BlockSpec((1, BRANCH), lambda i: (0, 0)),
        _resident(w_branch.shape),
        _resident(w_out.shape),
    ]
    args = [attn, mix, mix, mix, gate, x, mod, norm2_g.reshape(1, d), conv_w, pool_w,
            pool_scale.reshape(1, BRANCH), w_branch, w_out]
    out_specs = [row(d), row(d)]
    out_shape = [jax.ShapeDtypeStruct((m, d), F32), jax.ShapeDtypeStruct((m, d), h2_dtype)]
    if cast is not None:
        spec, shape = _cast_rider(cast, m // tm, lambda i: (i, 0))
        in_specs.append(spec)
        args.append(cast)
        out_specs.append(spec)
        out_shape.append(shape)
    return pl.pallas_call(
        functools.partial(_merge_kernel, tm=tm, seq_len=seq_len, d=d),
        grid=(m // tm,),
        in_specs=in_specs,
        out_specs=out_specs,
        out_shape=out_shape,
        compiler_params=_cparams(("parallel",), 48),
        name="merge",
    )(*args)


def _ffn_kernel(h_ref, x_ref, mod_ref, wgu_ref, wd_ref, o_ref, *, d, d_ff, fc):
    h = h_ref[...]
    acc = None
    for f in range(0, d_ff, fc):
        g = jnp.dot(h, wgu_ref[:, f:f + fc], preferred_element_type=F32)
        u = jnp.dot(h, wgu_ref[:, d_ff + f:d_ff + f + fc], preferred_element_type=F32)
        a = (g * jax.nn.sigmoid(g) * u).astype(BF16)
        y = jnp.dot(a, wd_ref[f:f + fc, :], preferred_element_type=F32)
        acc = y if acc is None else acc + y
    o_ref[...] = x_ref[...] + mod_ref[...][:, 5 * d:6 * d] * acc


def _ffn_dense(h2, x, mod, w_gu, w_down, *, seq_len, mod_row_of_tile):
    m, d = x.shape
    d_ff = w_down.shape[0]
    tm = min(512, seq_len)
    assert m % tm == 0 and seq_len % tm == 0
    fc = _pick_chunk(d_ff, 1536)
    return pl.pallas_call(
        functools.partial(_ffn_kernel, d=d, d_ff=d_ff, fc=fc),
        grid=(m // tm,),
        in_specs=[
            pl.BlockSpec((tm, d), lambda i: (i, 0)),
            pl.BlockSpec((tm, d), lambda i: (i, 0)),
            pl.BlockSpec((None, 1, 6 * d), lambda i: (mod_row_of_tile(i, tm), 0, 0)),
            _resident(w_gu.shape),
            _resident(w_down.shape),
        ],
        out_specs=pl.BlockSpec((tm, d), lambda i: (i, 0)),
        out_shape=jax.ShapeDtypeStruct((m, d), F32),
        compiler_params=_cparams(("parallel",), 56),
        name="ffn_dense",
    )(h2, x, mod, w_gu, w_down)


ROUTE_ROWS = 8


def _router_kernel(h_ref, rw_ref, tri_ref, wts_ref, pairs_ref, cnt_ref, carry_ref, *, n_experts):
    @pl.when(pl.program_id(0) == 0)
    def _():
        carry_ref[...] = jnp.zeros_like(carry_ref)

    h, w = h_ref[...], rw_ref[...]
    h_hi, w_hi = h.astype(BF16), w.astype(BF16)
    h_lo, w_lo = (h - h_hi.astype(F32)).astype(BF16), (w - w_hi.astype(F32)).astype(BF16)
    logits = (jnp.dot(h_hi, w_hi, preferred_element_type=F32) + jnp.dot(h_lo, w_hi, preferred_element_type=F32)
              + jnp.dot(h_hi, w_lo, preferred_element_type=F32))
    lane = lax.broadcasted_iota(jnp.int32, logits.shape, 1).astype(F32)
    logits = jnp.where(lane < n_experts, logits, -jnp.inf)
    m1 = jnp.max(logits, axis=-1, keepdims=True)
    i1 = jnp.min(jnp.where(logits == m1, lane, float(LANE)), axis=-1, keepdims=True)
    rest = jnp.where(lane == i1, -jnp.inf, logits)
    m2 = jnp.max(rest, axis=-1, keepdims=True)
    i2 = jnp.min(jnp.where(rest == m2, lane, float(LANE)), axis=-1, keepdims=True)
    e2 = jnp.exp(m2 - m1)
    den = 1.0 + e2
    wts_ref[...] = jnp.where(lane == 0, 1.0 / den, jnp.where(lane == 1, e2 / den, 0.0))

    m1t = (lane == i1).astype(F32).T[0:ROUTE_ROWS]
    m2t = (lane == i2).astype(F32).T[0:ROUTE_ROWS]
    mem = m1t + m2t
    before = jnp.dot(mem.astype(BF16), tri_ref[...], preferred_element_type=F32) + carry_ref[:, 0:1]
    eid = lax.broadcasted_iota(jnp.int32, mem.shape, 0).astype(F32)
    rows = [jnp.sum(m1t * eid, axis=0, keepdims=True), jnp.sum(m2t * eid, axis=0, keepdims=True),
            jnp.sum(m1t * before, axis=0, keepdims=True), jnp.sum(m2t * before, axis=0, keepdims=True)]
    rows += [jnp.zeros_like(rows[0])] * (ROUTE_ROWS - len(rows))
    pairs_ref[...] = jnp.concatenate(rows, axis=0)
    carry_ref[...] = carry_ref[...] + jnp.sum(mem, axis=1, keepdims=True)
    cnt_ref[...] = carry_ref[...]


def _router(h2, router_w):
    m, d = h2.shape
    n_experts = router_w.shape[1]
    assert n_experts <= ROUTE_ROWS
    rw = jnp.pad(router_w, ((0, 0), (0, LANE - n_experts)))
    tm = min(512, m)
    tri = jnp.triu(jnp.ones((tm, tm), BF16), k=1)
    return pl.pallas_call(
        functools.partial(_router_kernel, n_experts=n_experts),
        grid=(m // tm,),
        in_specs=[
            pl.BlockSpec((tm, d), lambda i: (i, 0)),
            pl.BlockSpec((d, LANE), lambda i: (0, 0)),
            pl.BlockSpec((tm, tm), lambda i: (0, 0)),
        ],
        out_specs=[
            pl.BlockSpec((tm, LANE), lambda i: (i, 0)),
            pl.BlockSpec((ROUTE_ROWS, tm), lambda i: (0, i)),
            pl.BlockSpec((ROUTE_ROWS, LANE), lambda i: (0, 0)),
        ],
        out_shape=[jax.ShapeDtypeStruct((m, LANE), F32), jax.ShapeDtypeStruct((ROUTE_ROWS, m), F32),
                   jax.ShapeDtypeStruct((ROUTE_ROWS, LANE), F32)],
        scratch_shapes=[pltpu.VMEM((ROUTE_ROWS, LANE), F32)],
        compiler_params=_cparams(("arbitrary",), 32),
        name="router",
    )(h2, rw, tri)


def _plan_kernel(pairs_ref, cnt_ref, dest_ref, meta_ref, *, tile_rows):
    cnt = cnt_ref[...]
    padded = jnp.floor((cnt + (tile_rows - 1)) * (1.0 / tile_rows)) * tile_rows
    sub = lax.broadcasted_iota(jnp.int32, cnt.shape, 0)
    lane = lax.broadcasted_iota(jnp.int32, cnt.shape, 1)
    end_row = jnp.sum(jnp.where(sub <= lane, padded, 0.0), axis=0, keepdims=True)
    end_col = jnp.sum(jnp.where(lane == sub, end_row, 0.0), axis=1, keepdims=True)
    start_col = end_col - padded[:, 0:1]
    p = pairs_ref[...]
    eid = lax.broadcasted_iota(jnp.int32, p.shape, 0).astype(F32)
    d1 = jnp.sum(jnp.where(eid == p[0:1], start_col, 0.0), axis=0, keepdims=True) + p[2:3]
    d2 = jnp.sum(jnp.where(eid == p[1:2], start_col, 0.0), axis=0, keepdims=True) + p[3:4]
    dest_ref[...] = jnp.concatenate([d1, d2] + [jnp.zeros_like(d1)] * (ROUTE_ROWS - 2), axis=0).astype(jnp.int32)
    tile_start = (lane * tile_rows).astype(F32)
    tile_e = jnp.sum((end_col <= tile_start).astype(F32), axis=0, keepdims=True)
    n_tiles = jnp.max(end_col, axis=0, keepdims=True) * (1.0 / tile_rows)
    meta = jnp.concatenate([tile_e, jnp.broadcast_to(n_tiles, tile_e.shape)]
                           + [jnp.zeros_like(tile_e)] * (ROUTE_ROWS - 2), axis=0)
    meta_ref[...] = meta.astype(jnp.int32)


def _plan(pairs, counts, *, tile_rows):
    m = pairs.shape[1]
    tp = min(2048, m)
    return pl.pallas_call(
        functools.partial(_plan_kernel, tile_rows=tile_rows),
        grid=(m // tp,),
        in_specs=[pl.BlockSpec((ROUTE_ROWS, tp), lambda i: (0, i)), pl.BlockSpec((ROUTE_ROWS, LANE), lambda i: (0, 0))],
        out_specs=[pl.BlockSpec((ROUTE_ROWS, tp), lambda i: (0, i)), pl.BlockSpec((ROUTE_ROWS, LANE), lambda i: (0, 0))],
        out_shape=[jax.ShapeDtypeStruct((ROUTE_ROWS, m), jnp.int32), jax.ShapeDtypeStruct((ROUTE_ROWS, LANE), jnp.int32)],
        compiler_params=_cparams(("arbitrary",), 32),
        name="plan",
    )(pairs, counts)


ROW_DMA_UNROLL = 8


def _row_copy(src, src_row, dst, dst_row, sem):
    return pltpu.make_async_copy(src.at[pl.ds(src_row, 1), :], dst.at[pl.ds(dst_row, 1), :], sem)


def _dispatch_kernel(dest_ref, h_ref, xs_in, xs_ref, sem, *, tm, n_tok):
    del xs_in
    base = pl.program_id(0) * tm

    def issue(blk, carry):
        for j in range(ROW_DMA_UNROLL):
            r = blk * ROW_DMA_UNROLL + j
            _row_copy(h_ref, r, xs_ref, dest_ref[base + r], sem).start()
            _row_copy(h_ref, r, xs_ref, dest_ref[n_tok + base + r], sem).start()
        return carry

    lax.fori_loop(0, tm // ROW_DMA_UNROLL, issue, 0)
    for _ in range(2):
        pltpu.make_async_copy(h_ref, xs_ref.at[pl.ds(0, tm), :], sem).wait()


def _dispatch(h2, dest, n_rows):
    m, d = h2.shape
    tm = min(512, m)
    return pl.pallas_call(
        functools.partial(_dispatch_kernel, tm=tm, n_tok=m),
        grid_spec=pltpu.PrefetchScalarGridSpec(
            num_scalar_prefetch=1,
            grid=(m // tm,),
            in_specs=[pl.BlockSpec((tm, d), lambda i, dr: (i, 0)), pl.BlockSpec(memory_space=pl.ANY)],
            out_specs=pl.BlockSpec(memory_space=pl.ANY),
            scratch_shapes=[pltpu.SemaphoreType.DMA],
        ),
        out_shape=jax.ShapeDtypeStruct((n_rows, d), F32),
        input_output_aliases={2: 0},
        compiler_params=_cparams(("arbitrary",), 32),
        name="dispatch",
    )(dest, h2, jnp.zeros((n_rows, d), F32))


def _gmm_kernel(meta_ref, xs_ref, wg_ref, wu_ref, wd_ref, o_ref, xb_ref, acc_ref):
    f = pl.program_id(1)
    nf = pl.num_programs(1)

    @pl.when(pl.program_id(0) < meta_ref[LANE])
    def _():
        @pl.when(f == 0)
        def _():
            xb_ref[...] = xs_ref[...].astype(BF16)

        xb = xb_ref[...]
        g = jnp.dot(xb, wg_ref[...], preferred_element_type=F32)
        u = jnp.dot(xb, wu_ref[...], preferred_element_type=F32)
        a = (g * jax.nn.sigmoid(g) * u).astype(BF16)
        y = jnp.dot(a, wd_ref[...], preferred_element_type=F32)

        @pl.when(f == 0)
        def _():
            acc_ref[...] = y

        @pl.when((f > 0) & (f < nf - 1))
        def _():
            acc_ref[...] += y

        @pl.when(f == nf - 1)
        def _():
            o_ref[...] = acc_ref[...] + y

    @pl.when((pl.program_id(0) >= meta_ref[LANE]) & (f == nf - 1))
    def _():
        o_ref[...] = jnp.zeros_like(o_ref)


def _grouped_swiglu(xs, meta, w_gu, w_down, *, tm):
    n_rows, d = xs.shape
    n_experts, d_ff, _ = w_down.shape
    fc = _pick_chunk(d_ff, d_ff // 2)
    nf = d_ff // fc
    assert nf >= 2 and n_rows % tm == 0 and n_rows // tm <= LANE

    def tile(i, mt):
        return jnp.maximum(jnp.minimum(i, mt[LANE] - 1), 0)

    def expert(i, mt):
        return jnp.minimum(mt[tile(i, mt)], n_experts - 1)

    def chunk(i, f, mt):
        return jnp.where(i < mt[LANE], f, nf - 1)

    return pl.pallas_call(
        _gmm_kernel,
        grid_spec=pltpu.PrefetchScalarGridSpec(
            num_scalar_prefetch=1,
            grid=(n_rows // tm, nf),
            in_specs=[
                pl.BlockSpec((tm, d), lambda i, f, mt: (tile(i, mt), 0)),
                pl.BlockSpec((None, d, fc), lambda i, f, mt: (expert(i, mt), 0, chunk(i, f, mt))),
                pl.BlockSpec((None, d, fc), lambda i, f, mt: (expert(i, mt), 0, nf + chunk(i, f, mt))),
                pl.BlockSpec((None, fc, d), lambda i, f, mt: (expert(i, mt), chunk(i, f, mt), 0)),
            ],
            out_specs=pl.BlockSpec((tm, d), lambda i, f, mt: (i, 0)),
            scratch_shapes=[pltpu.VMEM((tm, d), BF16), pltpu.VMEM((tm, d), F32)],
        ),
        out_shape=jax.ShapeDtypeStruct((n_rows, d), F32),
        compiler_params=_cparams(("arbitrary", "arbitrary"), 56),
        name="grouped_swiglu",
    )(meta, xs, w_gu, w_gu, w_down)


def _combine_kernel(dest_ref, x_ref, w_ref, mod_ref, fg_ref, ys_ref, o_ref, y1_ref, y2_ref, sem, *, tm, n_tok, d,
                    final_norm):
    base = pl.program_id(0) * tm

    def issue(blk, carry):
        for j in range(ROW_DMA_UNROLL):
            r = blk * ROW_DMA_UNROLL + j
            _row_copy(ys_ref, dest_ref[base + r], y1_ref, r, sem).start()
            _row_copy(ys_ref, dest_ref[n_tok + base + r], y2_ref, r, sem).start()
        return carry

    lax.fori_loop(0, tm // ROW_DMA_UNROLL, issue, 0)
    for buf in (y1_ref, y2_ref):
        pltpu.make_async_copy(ys_ref.at[pl.ds(0, tm), :], buf, sem).wait()
    w = w_ref[...]
    moe = w[:, 0:1] * y1_ref[...] + w[:, 1:2] * y2_ref[...]
    xn = x_ref[...] + mod_ref[...][:, 5 * d:6 * d] * moe
    if final_norm:
        xn = xn * lax.rsqrt(jnp.mean(xn * xn, axis=-1, keepdims=True) + EPS) * fg_ref[...]
    o_ref[...] = xn


def _combine(x, wts, dest, ys, mod, final_g, *, seq_len, mod_row_of_tile, final_norm):
    m, d = x.shape
    tm = min(512, seq_len)
    return pl.pallas_call(
        functools.partial(_combine_kernel, tm=tm, n_tok=m, d=d, final_norm=final_norm),
        grid_spec=pltpu.PrefetchScalarGridSpec(
            num_scalar_prefetch=1,
            grid=(m // tm,),
            in_specs=[
                pl.BlockSpec((tm, d), lambda i, dr: (i, 0)),
                pl.BlockSpec((tm, LANE), lambda i, dr: (i, 0)),
                pl.BlockSpec((None, 1, 6 * d), lambda i, dr: (mod_row_of_tile(i, tm), 0, 0)),
                pl.BlockSpec((1, d), lambda i, dr: (0, 0)),
                pl.BlockSpec(memory_space=pl.ANY),
            ],
            out_specs=pl.BlockSpec((tm, d), lambda i, dr: (i, 0)),
            scratch_shapes=[pltpu.VMEM((tm, d), F32), pltpu.VMEM((tm, d), F32), pltpu.SemaphoreType.DMA],
        ),
        out_shape=jax.ShapeDtypeStruct((m, d), F32),
        compiler_params=_cparams(("arbitrary",), 32),
        name="combine",
    )(dest, x, wts, mod, final_g.reshape(1, d), ys)


def _moe(h2, x, mod, final_g, router_w, w_gu, w_down, *, seq_len, mod_row_of_tile, final_norm):
    m, d = x.shape
    n_experts = router_w.shape[1]
    tm = 512
    n_rows = 2 * m + n_experts * tm
    wts, pairs, counts = _router(h2, router_w)
    dest2d, meta2d = _plan(pairs, counts, tile_rows=tm)
    dest = dest2d[0:2].reshape(2 * m)
    meta = meta2d[0:2].reshape(2 * LANE)
    xs = _dispatch(h2, dest, n_rows)
    ys = _grouped_swiglu(xs, meta, w_gu, w_down, tm=tm)
    return _combine(x, wts, dest, ys, mod, final_g, seq_len=seq_len, mod_row_of_tile=mod_row_of_tile,
                    final_norm=final_norm)


def kernel(x, c, ctx, c_ctx, norm1_g, norm2_g, final_g, w_mod, b_mod, w_in, conv_w, sink, pool_w, pool_scale,
           w_branch, w_out, ffn_w_gu, ffn_w_down, router_w, moe_w_gu, moe_w_down):
    bsz, seq, d = x.shape
    lc = ctx.shape[1]
    depth = w_in.shape[0]
    assert bsz + 1 <= 8 and seq % BLOCK == 0 and lc % BLOCK == 0 and seq % GRID_W == 0

    cvec = jnp.zeros((8, d), F32).at[:bsz].set(c).at[bsz].set(c_ctx)
    mods = _modvec(cvec, w_mod, b_mod)
    rope_tabs = _rope_tables(seq)

    lat_row = lambda i, tm: (i * tm) // seq
    ctx_row = lambda i, tm: bsz

    xl = x.reshape(bsz * seq, d)
    xc = ctx.reshape(bsz * lc, d)
    expert_w = None
    for l in range(depth):
        last = l == depth - 1
        mod = mods[l].reshape(8, 1, 6 * d)
        w_in_l = w_in[l].astype(BF16)
        wb_l, wo_l, pw_l = w_branch[l].astype(BF16), w_out[l].astype(BF16), pool_w[l].astype(BF16)
        routed = l % 2 == 1
        mixer = functools.partial(_merge, mod=mod, norm2_g=norm2_g[l], conv_w=conv_w[l], pool_w=pw_l,
                                  pool_scale=pool_scale[l], w_branch=wb_l, w_out=wo_l,
                                  h2_dtype=F32 if routed else BF16)
        if routed:
            wgu, wd = expert_w if expert_w is not None else (moe_w_gu[l // 2].astype(BF16),
                                                             moe_w_down[l // 2].astype(BF16))
        else:
            wgu, wd = ffn_w_gu[l // 2].astype(BF16), ffn_w_down[l // 2].astype(BF16)
        ride = not routed and not last
        ride_gu = moe_w_gu[(l + 1) // 2].reshape(-1, moe_w_gu.shape[-1]) if ride else None
        ride_d = moe_w_down[(l + 1) // 2].reshape(-1, d) if ride else None

        def channel_mix(h2, xm, *, seq_len, row_fn, final_norm):
            if routed:
                return _moe(h2, xm, mod, final_g, router_w[l // 2], wgu, wd, seq_len=seq_len,
                            mod_row_of_tile=row_fn, final_norm=final_norm)
            assert not final_norm
            return _ffn_dense(h2, xm, mod, wgu, wd, seq_len=seq_len, mod_row_of_tile=row_fn)

        if last:
            kvc = _inproj(xc, norm1_g[l], mod, w_in_l, seq_len=lc, mod_row_of_tile=ctx_row, kv_only=True)
        else:
            qc, kvc, mixc, gatec = _inproj(xc, norm1_g[l], mod, w_in_l, seq_len=lc, mod_row_of_tile=ctx_row)
            attn_c = _attention(qc.reshape(bsz, lc, -1), None, kvc.reshape(bsz, lc, -1), sink[l], band=False)
            xc_mid, h2c = mixer(attn_c.reshape(bsz * lc, -1), mixc, gatec, xc, seq_len=lc, mod_row_of_tile=ctx_row)
            xc_next = channel_mix(h2c, xc_mid, seq_len=lc, row_fn=ctx_row, final_norm=False)
        q, kv, mix, gate = _inproj(xl, norm1_g[l], mod, w_in_l, seq_len=seq, mod_row_of_tile=lat_row,
                                   rope_tabs=rope_tabs)
        attn = _attention(q.reshape(bsz, seq, -1), kv.reshape(bsz, seq, -1), kvc.reshape(bsz, lc, -1), sink[l],
                          band=True, cast=ride_gu)
        merged = mixer((attn[0] if ride else attn).reshape(bsz * seq, -1), mix, gate, xl, seq_len=seq,
                       mod_row_of_tile=lat_row, cast=ride_d)
        x_mid, h2 = merged[0], merged[1]
        xl = channel_mix(h2, x_mid, seq_len=seq, row_fn=lat_row, final_norm=last and l % 2 == 1)
        expert_w = (attn[1].reshape(moe_w_gu.shape[1:]), merged[2].reshape(moe_w_down.shape[1:])) if ride else None
        if not last:
            xc = xc_next
    if depth % 2 == 1:
        raise NotImplementedError("final norm is fused into the expert layer; depth must be even")
    return xl.reshape(bsz, seq, d)
```

```python
import functools

import jax
import jax.numpy as jnp
from jax import lax
from jax.experimental import pallas as pl
from jax.experimental.pallas import tpu as pltpu

F32 = jnp.float32
BF16 = jnp.bfloat16

GRID_W = 64
EPS = 1e-6
NEG_INF = -1e30
HEAD_DIM = 64
N_HEADS = 8
N_KV_HEADS = 2
GROUP = N_HEADS // N_KV_HEADS
WINDOW = 128
BLOCK = 128
ROPE_THETA = 10000.0
BRANCH = 512
POOL_SIZES = (2, 4, 8, 16)
POOL_GROUP = 128
Q_END = 512
V_END = 768
MIX_W = 4 * BRANCH
POOL_END = V_END + MIX_W

LANE = 128
BF16_SUBLANE_TILE = 16
HALO = BF16_SUBLANE_TILE
MIB = 1024 * 1024


def _cparams(sem, vmem_mib):
    return pltpu.CompilerParams(dimension_semantics=sem, vmem_limit_bytes=vmem_mib * MIB)


def _pick_chunk(n, cap):
    best = None
    for c in range(LANE, min(n, cap) + 1, LANE):
        if n % c == 0:
            best = c
    assert best is not None, (n, cap)
    return best


def _resident(shape):
    nd = len(shape)
    return pl.BlockSpec(shape, lambda *_: (0,) * nd, pipeline_mode=pl.Buffered(1))


def _norm_mod(x, g, shift, scale):
    y = x * lax.rsqrt(jnp.mean(x * x, axis=-1, keepdims=True) + EPS) * g
    return y * (1.0 + scale) + shift


def _modvec_kernel(c_ref, w_ref, b_ref, o_ref):
    c = c_ref[...]
    s = c * jax.nn.sigmoid(c)
    o_ref[...] = jnp.dot(s, w_ref[...], preferred_element_type=F32) + b_ref[...]


def _modvec(cvec, w_mod, b_mod):
    depth, d, n = w_mod.shape
    nc = _pick_chunk(n, 1536)
    return pl.pallas_call(
        _modvec_kernel,
        grid=(depth, n // nc),
        in_specs=[
            pl.BlockSpec((8, d), lambda l, j: (0, 0)),
            pl.BlockSpec((None, d, nc), lambda l, j: (l, 0, j)),
            pl.BlockSpec((None, 1, nc), lambda l, j: (l, 0, j)),
        ],
        out_specs=pl.BlockSpec((None, 8, nc), lambda l, j: (l, 0, j)),
        out_shape=jax.ShapeDtypeStruct((depth, 8, n), F32),
        compiler_params=_cparams(("arbitrary", "arbitrary"), 32),
        name="modvec",
    )(cvec, w_mod, b_mod.reshape(depth, 1, n))


def _rope_tables(seq_len):
    n_freq = HEAD_DIM // 4
    inv = ROPE_THETA ** (-jnp.arange(n_freq, dtype=F32) / n_freq)
    pos = jnp.arange(seq_len)
    row = (pos // GRID_W).astype(F32)[:, None] * inv[None, :]
    col = (pos % GRID_W).astype(F32)[:, None] * inv[None, :]
    zero = jnp.zeros_like(row)
    cos = jnp.concatenate([jnp.cos(row)] * 2 + [jnp.cos(col)] * 2, axis=-1)
    s_lo = jnp.concatenate([-jnp.sin(row), zero, -jnp.sin(col), zero], axis=-1)
    s_hi = jnp.concatenate([zero, jnp.sin(row), zero, jnp.sin(col)], axis=-1)
    return tuple(jnp.tile(t, (1, LANE // HEAD_DIM)) for t in (cos, s_lo, s_hi))


def _inproj_kernel(*refs, rope, kv_only, d):
    x_ref, g_ref, mod_ref, w_ref = refs[:4]
    refs = refs[4:]
    if rope:
        cos_ref, slo_ref, shi_ref = refs[:3]
        refs = refs[3:]

        def rot(z):
            return (z * cos_ref[...] + pltpu.roll(z, LANE - 16, 1) * slo_ref[...]
                    + pltpu.roll(z, 16, 1) * shi_ref[...])
    else:
        def rot(z):
            return z

    mod = mod_ref[...]
    h = _norm_mod(x_ref[...], g_ref[...], mod[:, 0:d], mod[:, d:2 * d]).astype(BF16)

    def proj(c0, c1):
        return jnp.dot(h, w_ref[:, c0:c1], preferred_element_type=F32)

    if kv_only:
        (kv_ref,) = refs
        z = proj(0, 2 * LANE)
        kv_ref[:, 0:LANE] = rot(z[:, 0:LANE]).astype(BF16)
        kv_ref[:, LANE:] = z[:, LANE:].astype(BF16)
        return

    q_ref, kv_ref, mix_ref, gate_ref = refs
    z = proj(0, Q_END)
    for j in range(Q_END // LANE):
        q_ref[:, j * LANE:(j + 1) * LANE] = (rot(z[:, j * LANE:(j + 1) * LANE]) * HEAD_DIM ** -0.5).astype(BF16)
    z = proj(Q_END, V_END)
    kv_ref[:, 0:LANE] = rot(z[:, 0:LANE]).astype(BF16)
    kv_ref[:, LANE:] = z[:, LANE:].astype(BF16)
    cw = 512
    for c in range(MIX_W // cw):
        mix_ref[:, c * cw:(c + 1) * cw] = proj(V_END + c * cw, V_END + (c + 1) * cw).astype(BF16)
    n_gate = gate_ref.shape[1]
    for c in range(n_gate // cw):
        zg = proj(POOL_END + c * cw, POOL_END + (c + 1) * cw)
        gate_ref[:, c * cw:(c + 1) * cw] = jax.nn.sigmoid(zg).astype(BF16)


def _inproj(x, norm_g, mod, w_in, *, seq_len, mod_row_of_tile, rope_tabs=None, kv_only=False):
    m, d = x.shape
    tm = min(512, seq_len)
    assert m % tm == 0 and seq_len % tm == 0
    tiles_per_seq = seq_len // tm
    rope = rope_tabs is not None
    in_w = w_in.shape[1]
    in_specs = [
        pl.BlockSpec((tm, d), lambda i: (i, 0)),
        pl.BlockSpec((1, d), lambda i: (0, 0)),
        pl.BlockSpec((None, 1, 6 * d), lambda i: (mod_row_of_tile(i, tm), 0, 0)),
        pl.BlockSpec((d, 2 * LANE), lambda i: (0, Q_END // (2 * LANE))) if kv_only else _resident((d, in_w)),
    ]
    args = [x, norm_g.reshape(1, d), mod, w_in]
    if rope:
        in_specs += [pl.BlockSpec((tm, LANE), lambda i: (i % tiles_per_seq, 0))] * 3
        args += list(rope_tabs)
    if kv_only:
        out_specs = pl.BlockSpec((tm, 2 * LANE), lambda i: (i, 0))
        out_shape = jax.ShapeDtypeStruct((m, 2 * LANE), BF16)
    else:
        widths = (Q_END, 2 * LANE, MIX_W, in_w - POOL_END)
        out_specs = [pl.BlockSpec((tm, w), lambda i: (i, 0)) for w in widths]
        out_shape = [jax.ShapeDtypeStruct((m, w), BF16) for w in widths]
    return pl.pallas_call(
        functools.partial(_inproj_kernel, rope=rope, kv_only=kv_only, d=d),
        grid=(m // tm,),
        in_specs=in_specs,
        out_specs=out_specs,
        out_shape=out_shape,
        compiler_params=_cparams(("parallel",), 48),
        name="inproj_kv" if kv_only else "inproj",
    )(*args)


ATTN_STRIP = 32
ATTN_QBLOCKS = 2
ATTN_AHEAD = 2


def _attn_kernel(sink_ref, q_ref, *refs, band, carry_cast):
    refs, (s_ref, p_ref) = list(refs[:-2]), refs[-2:]
    if carry_cast:
        cast_out = refs.pop()
        cast_in = refs.pop(-2)
        cast_out[...] = cast_in[...].astype(BF16)
    if band:
        kvp_ref, kvm_ref, kvn_ref, kvc_ref, bias_ref, o_ref = refs
    else:
        kvc_ref, o_ref = refs
    kvc = kvc_ref[0]
    lc = kvc.shape[0]
    nloc = 3 * BLOCK if band else 0
    nt = (((1,), (1,)), ((), ()))
    w_ctx = jnp.concatenate([kvc[:, LANE:], jnp.ones((lc, LANE), BF16)], axis=1)
    if band:
        n = pl.program_id(1)
        kv4 = jnp.concatenate([kvp_ref[0], kvm_ref[0], kvn_ref[0]], axis=0)
        w4 = jnp.concatenate([kv4[:, LANE:], jnp.ones((4 * BLOCK, LANE), BF16)], axis=1)
        col = lax.broadcasted_iota(jnp.int32, (1, nloc), 1)
        head_edge = jnp.where((col < BLOCK) & (n == 0), NEG_INF, 0.0)
        tail_edge = jnp.where((col >= 2 * BLOCK) & (n == pl.num_programs(1) - 1), NEG_INF, 0.0)
        biases = [bias_ref[...] + head_edge, bias_ref[...] + tail_edge]
    units = [(sb, h) for sb in range(ATTN_QBLOCKS) for h in range(N_HEADS)]
    def scores(u):
        sb, h = units[u]
        qh = q_ref[0, sb * BLOCK:(sb + 1) * BLOCK, h * HEAD_DIM:(h + 1) * HEAD_DIM]
        ks = slice(h // GROUP * HEAD_DIM, (h // GROUP + 1) * HEAD_DIM)
        if band:
            kl = kv4[sb * BLOCK:sb * BLOCK + nloc, ks]
            s_ref[u, :, 0:nloc] = lax.dot_general(qh, kl, nt, preferred_element_type=F32) + biases[sb]
        s_ref[u, :, nloc:] = lax.dot_general(qh, kvc[:, ks], nt, preferred_element_type=F32)

    def probs(u):
        sink = sink_ref[units[u][1]]
        esink = []
        for r in range(0, BLOCK, ATTN_STRIP):
            s = s_ref[u, r:r + ATTN_STRIP, :]
            m = jnp.maximum(jnp.max(s, axis=-1, keepdims=True), sink)
            p_ref[u, r:r + ATTN_STRIP, :] = jnp.exp(s - m).astype(BF16)
            esink.append(jnp.exp(sink - m))
        return jnp.concatenate(esink, axis=0)

    def weighted_values(u, esink):
        sb, h = units[u]
        ks = slice(h // GROUP * HEAD_DIM, (h // GROUP + 1) * HEAD_DIM)
        o2 = jnp.dot(p_ref[u, :, nloc:], w_ctx, preferred_element_type=F32)
        if band:
            o2 = o2 + jnp.dot(p_ref[u, :, 0:nloc], w4[sb * BLOCK:sb * BLOCK + nloc], preferred_element_type=F32)
        return o2[:, ks] / (o2[:, LANE:LANE + HEAD_DIM] + esink)

    outs, esinks = [], []
    for u in range(min(ATTN_AHEAD, len(units))):
        scores(u)
    for u in range(len(units)):
        if u + ATTN_AHEAD < len(units):
            scores(u + ATTN_AHEAD)
        esinks.append(probs(u))
        if u >= 1:
            outs.append(weighted_values(u - 1, esinks[u - 1]))
    outs.append(weighted_values(len(units) - 1, esinks[-1]))
    for sb in range(ATTN_QBLOCKS):
        o_ref[0, sb * BLOCK:(sb + 1) * BLOCK, :] = jnp.concatenate(
            outs[sb * N_HEADS:(sb + 1) * N_HEADS], axis=1).astype(BF16)


def _cast_rider(w, n_steps, index_map):
    rows, cols = w.shape
    assert rows % (n_steps * BF16_SUBLANE_TILE) == 0, (w.shape, n_steps)
    return pl.BlockSpec((rows // n_steps, cols), index_map), jax.ShapeDtypeStruct((rows, cols), BF16)


def _attention(q, kv, kvc, sink, *, band, cast=None):
    b, l, _ = q.shape
    lc = kvc.shape[1]
    tq = ATTN_QBLOCKS * BLOCK
    assert l % tq == 0
    nb = l // BLOCK
    in_specs = [
        pl.BlockSpec(memory_space=pltpu.SMEM),
        pl.BlockSpec((1, tq, N_HEADS * HEAD_DIM), lambda bi, n: (bi, n, 0)),
    ]
    args = [sink, q]
    if band:
        in_specs += [
            pl.BlockSpec((1, BLOCK, 2 * LANE), lambda bi, n: (bi, jnp.maximum(ATTN_QBLOCKS * n - 1, 0), 0)),
            pl.BlockSpec((1, tq, 2 * LANE), lambda bi, n: (bi, n, 0)),
            pl.BlockSpec((1, BLOCK, 2 * LANE), lambda bi, n: (bi, jnp.minimum(ATTN_QBLOCKS * (n + 1), nb - 1), 0)),
        ]
        args += [kv, kv, kv]
    in_specs.append(pl.BlockSpec((1, lc, 2 * LANE), lambda bi, n: (bi, 0, 0)))
    args.append(kvc)
    nkeys = lc
    if band:
        rel = jnp.arange(3 * BLOCK)[None, :] - BLOCK - jnp.arange(BLOCK)[:, None]
        in_specs.append(pl.BlockSpec((BLOCK, 3 * BLOCK), lambda bi, n: (0, 0)))
        args.append(jnp.where(jnp.abs(rel) <= WINDOW, 0.0, NEG_INF).astype(F32))
        nkeys += 3 * BLOCK
    nq = l // tq
    out_specs = [pl.BlockSpec((1, tq, N_HEADS * HEAD_DIM), lambda bi, n: (bi, n, 0))]
    out_shape = [jax.ShapeDtypeStruct((b, l, N_HEADS * HEAD_DIM), BF16)]
    if cast is not None:
        spec, shape = _cast_rider(cast, b * nq, lambda bi, n: (bi * nq + n, 0))
        in_specs.append(spec)
        args.append(cast)
        out_specs.append(spec)
        out_shape.append(shape)
    outs = pl.pallas_call(
        functools.partial(_attn_kernel, band=band, carry_cast=cast is not None),
        grid=(b, nq),
        in_specs=in_specs,
        out_specs=out_specs,
        out_shape=out_shape,
        scratch_shapes=[pltpu.VMEM((ATTN_QBLOCKS * N_HEADS, BLOCK, nkeys), F32),
                        pltpu.VMEM((ATTN_QBLOCKS * N_HEADS, BLOCK, nkeys), BF16)],
        compiler_params=_cparams(("parallel", "parallel"), 48),
        name="attn_band" if band else "attn_ctx",
    )(*args)
    return outs[0] if cast is None else outs


def _merge_kernel(attn_ref, mix_ref, prev_ref, next_ref, gate_ref, x_ref, mod_ref, n2g_ref, convw_ref,
                  poolw_ref, pscale_ref, wb_ref, wo_ref, *refs, tm, seq_len, d):
    if len(refs) == 4:
        cast_in, xo_ref, h2_ref, cast_out = refs
        cast_out[...] = cast_in[...].astype(BF16)
    else:
        xo_ref, h2_ref = refs
    tile = pl.program_id(0) % (seq_len // tm)
    keep_prev = (tile != 0).astype(F32)
    keep_next = (tile != seq_len // tm - 1).astype(F32)
    mixm = mix_ref[...]
    prev = prev_ref[...].astype(F32) * keep_prev
    nxt = next_ref[...].astype(F32) * keep_next
    b = BRANCH
    cx, cb, cc = (mixm[:, j * b:(j + 1) * b].astype(F32) for j in range(3))

    p = cc * cx
    p_prev = prev[HALO - 1:HALO, 2 * b:3 * b] * prev[HALO - 1:HALO, 0:b]
    p_next = nxt[0:1, 2 * b:3 * b] * nxt[0:1, 0:b]
    ridx = lax.broadcasted_iota(jnp.int32, (tm, b), 0)
    p_dn = jnp.where(ridx == 0, p_prev, pltpu.roll(p, 1, 0))
    p_up = jnp.where(ridx == tm - 1, p_next, pltpu.roll(p, tm - 1, 0))
    cw = convw_ref[...]
    conv_out = (cb * (p_dn * cw[0:1] + p * cw[1:2] + p_up * cw[2:3])).astype(BF16)

    u_main = mixm[:, 3 * b:4 * b].astype(F32)
    u_ext = jnp.concatenate([prev[:, 3 * b:4 * b], u_main, nxt[:, 3 * b:4 * b]], axis=0)
    ext = tm + 2 * HALO

    def shift(a, s):
        return pltpu.roll(a, s % ext, 0)

    tpos = tile * tm + lax.broadcasted_iota(jnp.int32, (tm, 1), 0)
    pooled = []
    for gi, w in enumerate(POOL_SIZES):
        gs = slice(gi * POOL_GROUP, (gi + 1) * POOL_GROUP)
        ug = u_ext[:, gs]
        a = ug + shift(ug, 1)
        ww = 2
        while ww < w:
            a = shift(a, ww // 2) + shift(a, -(ww // 2))
            ww *= 2
        cnt = jnp.minimum(tpos + w // 2, seq_len) - jnp.maximum(tpos - w // 2, 0)
        dlt = a[HALO:HALO + tm] / cnt.astype(F32) - u_main[:, gs]
        pooled.append(jnp.dot(dlt.astype(BF16), poolw_ref[gi], preferred_element_type=F32))
    pool_out = (jnp.concatenate(pooled, axis=1) * pscale_ref[...]).astype(BF16)

    y = (gate_ref[:, 0:d].astype(F32) * jnp.dot(attn_ref[...], wb_ref[0], preferred_element_type=F32)
         + gate_ref[:, d:2 * d].astype(F32) * jnp.dot(conv_out, wb_ref[1], preferred_element_type=F32)
         + gate_ref[:, 2 * d:3 * d].astype(F32) * jnp.dot(pool_out, wb_ref[2], preferred_element_type=F32))
    o = jnp.dot(y.astype(BF16), wo_ref[...], preferred_element_type=F32)
    mod = mod_ref[...]
    xn = x_ref[...] + mod[:, 2 * d:3 * d] * o
    xo_ref[...] = xn
    h2_ref[...] = _norm_mod(xn, n2g_ref[...], mod[:, 3 * d:4 * d], mod[:, 4 * d:5 * d]).astype(h2_ref.dtype)


def _merge(attn, mix, gate, x, mod, norm2_g, conv_w, pool_w, pool_scale, w_branch, w_out, *, seq_len,
           mod_row_of_tile, h2_dtype, cast=None):
    m, d = x.shape
    tm = min(512, seq_len)
    assert m % tm == 0 and seq_len % tm == 0 and tm % HALO == 0
    hb = tm // HALO
    n_halo = m // HALO
    row = lambda w: pl.BlockSpec((tm, w), lambda i: (i, 0))
    in_specs = [
        row(BRANCH),
        row(MIX_W),
        pl.BlockSpec((HALO, MIX_W), lambda i: (jnp.maximum(i * hb - 1, 0), 0)),
        pl.BlockSpec((HALO, MIX_W), lambda i: (jnp.minimum((i + 1) * hb, n_halo - 1), 0)),
        row(3 * d),
        row(d),
        pl.BlockSpec((None, 1, 6 * d), lambda i: (mod_row_of_tile(i, tm), 0, 0)),
        pl.BlockSpec((1, d), lambda i: (0, 0)),
        _resident(conv_w.shape),
        _resident(pool_w.shape),
        pl.BlockSpec((1, BRANCH), lambda i: (0, 0)),
        _resident(w_branch.shape),
        _resident(w_out.shape),
    ]
    args = [attn, mix, mix, mix, gate, x, mod, norm2_g.reshape(1, d), conv_w, pool_w,
            pool_scale.reshape(1, BRANCH), w_branch, w_out]
    out_specs = [row(d), row(d)]
    out_shape = [jax.ShapeDtypeStruct((m, d), F32), jax.ShapeDtypeStruct((m, d), h2_dtype)]
    if cast is not None:
        spec, shape = _cast_rider(cast, m // tm, lambda i: (i, 0))
        in_specs.append(spec)
        args.append(cast)
        out_specs.append(spec)
        out_shape.append(shape)
    return pl.pallas_call(
        functools.partial(_merge_kernel, tm=tm, seq_len=seq_len, d=d),
        grid=(m // tm,),
        in_specs=in_specs,
        out_specs=out_specs,
        out_shape=out_shape,
        compiler_params=_cparams(("parallel",), 48),
        name="merge",
    )(*args)


def _ffn_kernel(h_ref, x_ref, mod_ref, wgu_ref, wd_ref, o_ref, *, d, d_ff, fc):
    h = h_ref[...]
    acc = None
    for f in range(0, d_ff, fc):
        g = jnp.dot(h, wgu_ref[:, f:f + fc], preferred_element_type=F32)
        u = jnp.dot(h, wgu_ref[:, d_ff + f:d_ff + f + fc], preferred_element_type=F32)
        a = (g * jax.nn.sigmoid(g) * u).astype(BF16)
        y = jnp.dot(a, wd_ref[f:f + fc, :], preferred_element_type=F32)
        acc = y if acc is None else acc + y
    o_ref[...] = x_ref[...] + mod_ref[...][:, 5 * d:6 * d] * acc


def _ffn_dense(h2, x, mod, w_gu, w_down, *, seq_len, mod_row_of_tile):
    m, d = x.shape
    d_ff = w_down.shape[0]
    tm = min(512, seq_len)
    assert m % tm == 0 and seq_len % tm == 0
    fc = _pick_chunk(d_ff, 1536)
    return pl.pallas_call(
        functools.partial(_ffn_kernel, d=d, d_ff=d_ff, fc=fc),
        grid=(m // tm,),
        in_specs=[
            pl.BlockSpec((tm, d), lambda i: (i, 0)),
            pl.BlockSpec((tm, d), lambda i: (i, 0)),
            pl.BlockSpec((None, 1, 6 * d), lambda i: (mod_row_of_tile(i, tm), 0, 0)),
            _resident(w_gu.shape),
            _resident(w_down.shape),
        ],
        out_specs=pl.BlockSpec((tm, d), lambda i: (i, 0)),
        out_shape=jax.ShapeDtypeStruct((m, d), F32),
        compiler_params=_cparams(("parallel",), 56),
        name="ffn_dense",
    )(h2, x, mod, w_gu, w_down)


ROUTE_ROWS = 8
META_ROWS = 4
META_TILES, META_PAD_START, META_PAD_LEN = LANE, 2 * LANE, 3 * LANE


def _router_kernel(h_ref, rw_ref, tri_ref, wts_ref, pairs_ref, cnt_ref, carry_ref, *, n_experts):
    @pl.when(pl.program_id(0) == 0)
    def _():
        carry_ref[...] = jnp.zeros_like(carry_ref)

    h, w = h_ref[...], rw_ref[...]
    h_hi, w_hi = h.astype(BF16), w.astype(BF16)
    h_lo, w_lo = (h - h_hi.astype(F32)).astype(BF16), (w - w_hi.astype(F32)).astype(BF16)
    logits = (jnp.dot(h_hi, w_hi, preferred_element_type=F32) + jnp.dot(h_lo, w_hi, preferred_element_type=F32)
              + jnp.dot(h_hi, w_lo, preferred_element_type=F32))
    lane = lax.broadcasted_iota(jnp.int32, logits.shape, 1).astype(F32)
    logits = jnp.where(lane < n_experts, logits, -jnp.inf)
    m1 = jnp.max(logits, axis=-1, keepdims=True)
    i1 = jnp.min(jnp.where(logits == m1, lane, float(LANE)), axis=-1, keepdims=True)
    rest = jnp.where(lane == i1, -jnp.inf, logits)
    m2 = jnp.max(rest, axis=-1, keepdims=True)
    i2 = jnp.min(jnp.where(rest == m2, lane, float(LANE)), axis=-1, keepdims=True)
    e2 = jnp.exp(m2 - m1)
    den = 1.0 + e2
    wts_ref[...] = jnp.where(lane == 0, 1.0 / den, jnp.where(lane == 1, e2 / den, 0.0))

    m1t = (lane == i1).astype(F32).T[0:ROUTE_ROWS]
    m2t = (lane == i2).astype(F32).T[0:ROUTE_ROWS]
    mem = m1t + m2t
    before = jnp.dot(mem.astype(BF16), tri_ref[...], preferred_element_type=F32) + carry_ref[:, 0:1]
    eid = lax.broadcasted_iota(jnp.int32, mem.shape, 0).astype(F32)
    rows = [jnp.sum(m1t * eid, axis=0, keepdims=True), jnp.sum(m2t * eid, axis=0, keepdims=True),
            jnp.sum(m1t * before, axis=0, keepdims=True), jnp.sum(m2t * before, axis=0, keepdims=True)]
    rows += [jnp.zeros_like(rows[0])] * (ROUTE_ROWS - len(rows))
    pairs_ref[...] = jnp.concatenate(rows, axis=0)
    carry_ref[...] = carry_ref[...] + jnp.sum(mem, axis=1, keepdims=True)
    cnt_ref[...] = carry_ref[...]


def _router(h2, router_w):
    m, d = h2.shape
    n_experts = router_w.shape[1]
    assert n_experts <= ROUTE_ROWS
    rw = jnp.pad(router_w, ((0, 0), (0, LANE - n_experts)))
    tm = min(512, m)
    tri = jnp.triu(jnp.ones((tm, tm), BF16), k=1)
    return pl.pallas_call(
        functools.partial(_router_kernel, n_experts=n_experts),
        grid=(m // tm,),
        in_specs=[
            pl.BlockSpec((tm, d), lambda i: (i, 0)),
            pl.BlockSpec((d, LANE), lambda i: (0, 0)),
            pl.BlockSpec((tm, tm), lambda i: (0, 0)),
        ],
        out_specs=[
            pl.BlockSpec((tm, LANE), lambda i: (i, 0)),
            pl.BlockSpec((ROUTE_ROWS, tm), lambda i: (0, i)),
            pl.BlockSpec((ROUTE_ROWS, LANE), lambda i: (0, 0)),
        ],
        out_shape=[jax.ShapeDtypeStruct((m, LANE), F32), jax.ShapeDtypeStruct((ROUTE_ROWS, m), F32),
                   jax.ShapeDtypeStruct((ROUTE_ROWS, LANE), F32)],
        scratch_shapes=[pltpu.VMEM((ROUTE_ROWS, LANE), F32)],
        compiler_params=_cparams(("arbitrary",), 32),
        name="router",
    )(h2, rw, tri)


def _plan_kernel(pairs_ref, cnt_ref, dest_ref, meta_ref, *, tile_rows):
    cnt = cnt_ref[...]
    padded = jnp.floor((cnt + (tile_rows - 1)) * (1.0 / tile_rows)) * tile_rows
    sub = lax.broadcasted_iota(jnp.int32, cnt.shape, 0)
    lane = lax.broadcasted_iota(jnp.int32, cnt.shape, 1)
    end_row = jnp.sum(jnp.where(sub <= lane, padded, 0.0), axis=0, keepdims=True)
    end_col = jnp.sum(jnp.where(lane == sub, end_row, 0.0), axis=1, keepdims=True)
    start_col = end_col - padded[:, 0:1]
    p = pairs_ref[...]
    eid = lax.broadcasted_iota(jnp.int32, p.shape, 0).astype(F32)
    d1 = jnp.sum(jnp.where(eid == p[0:1], start_col, 0.0), axis=0, keepdims=True) + p[2:3]
    d2 = jnp.sum(jnp.where(eid == p[1:2], start_col, 0.0), axis=0, keepdims=True) + p[3:4]
    dest_ref[...] = jnp.concatenate([d1, d2] + [jnp.zeros_like(d1)] * (ROUTE_ROWS - 2), axis=0).astype(jnp.int32)
    tile_start = (lane * tile_rows).astype(F32)
    tile_e = jnp.sum((end_col <= tile_start).astype(F32), axis=0, keepdims=True)
    n_tiles = jnp.max(end_col, axis=0, keepdims=True) * (1.0 / tile_rows)
    pad_start = jnp.sum(jnp.where(lane == sub, start_col + cnt, 0.0), axis=0, keepdims=True)
    pad_len = jnp.sum(jnp.where(lane == sub, padded - cnt, 0.0), axis=0, keepdims=True)
    meta = jnp.concatenate([tile_e, jnp.broadcast_to(n_tiles, tile_e.shape), pad_start, pad_len]
                           + [jnp.zeros_like(tile_e)] * (ROUTE_ROWS - META_ROWS), axis=0)
    meta_ref[...] = meta.astype(jnp.int32)


def _plan(pairs, counts, *, tile_rows):
    m = pairs.shape[1]
    tp = min(2048, m)
    return pl.pallas_call(
        functools.partial(_plan_kernel, tile_rows=tile_rows),
        grid=(m // tp,),
        in_specs=[pl.BlockSpec((ROUTE_ROWS, tp), lambda i: (0, i)), pl.BlockSpec((ROUTE_ROWS, LANE), lambda i: (0, 0))],
        out_specs=[pl.BlockSpec((ROUTE_ROWS, tp), lambda i: (0, i)), pl.BlockSpec((ROUTE_ROWS, LANE), lambda i: (0, 0))],
        out_shape=[jax.ShapeDtypeStruct((ROUTE_ROWS, m), jnp.int32), jax.ShapeDtypeStruct((ROUTE_ROWS, LANE), jnp.int32)],
        compiler_params=_cparams(("arbitrary",), 32),
        name="plan",
    )(pairs, counts)


ROW_DMA_UNROLL = 8


def _row_copy(src, src_row, dst, dst_row, sem):
    return pltpu.make_async_copy(src.at[pl.ds(src_row, 1), :], dst.at[pl.ds(dst_row, 1), :], sem)


def _dispatch_kernel(dest_ref, meta_ref, h_ref, xs_ref, zero_ref, sem, zsem, *, tm, n_tok, n_experts, n_tiles_max):
    tile_rows = zero_ref.shape[0]
    i = pl.program_id(0)
    last = pl.num_programs(0) - 1
    base = i * tm

    def wait_rows():
        for _ in range(2):
            pltpu.make_async_copy(h_ref, xs_ref.at[pl.ds(0, tm), :], sem).wait()

    def issue(blk, carry):
        for j in range(ROW_DMA_UNROLL):
            r = blk * ROW_DMA_UNROLL + j
            _row_copy(h_ref, r, xs_ref, dest_ref[base + r], sem).start()
            _row_copy(h_ref, r, xs_ref, dest_ref[n_tok + base + r], sem).start()
        return carry

    lax.fori_loop(0, tm // ROW_DMA_UNROLL, issue, 0)

    @pl.when(i < last)
    def _():
        wait_rows()

    @pl.when(i == last)
    def _():
        zero_ref[...] = jnp.zeros_like(zero_ref)

        def zero_row(e):
            def body(r, carry):
                _row_copy(zero_ref, 0, xs_ref, meta_ref[META_PAD_START + e] + r, zsem).start()
                return carry
            return body

        def tail_copy(k):
            return pltpu.make_async_copy(zero_ref, xs_ref.at[pl.ds(k * tile_rows, tile_rows), :], zsem)

        def zero_tile(k, carry):
            tail_copy(k).start()
            return carry

        def wait_tile(k, carry):
            tail_copy(k).wait()
            return carry

        for e in range(n_experts):
            lax.fori_loop(0, meta_ref[META_PAD_LEN + e], zero_row(e), 0)
        lax.fori_loop(meta_ref[META_TILES], n_tiles_max, zero_tile, 0)
        def wait_row(r, carry):
            _row_copy(zero_ref, 0, xs_ref, 0, zsem).wait()
            return carry

        for e in range(n_experts):
            lax.fori_loop(0, meta_ref[META_PAD_LEN + e], wait_row, 0)

        lax.fori_loop(meta_ref[META_TILES], n_tiles_max, wait_tile, 0)
        wait_rows()


def _dispatch(h2, dest, meta, n_rows, *, tm, n_experts):
    m, d = h2.shape
    tt = min(512, m)
    return pl.pallas_call(
        functools.partial(_dispatch_kernel, tm=tt, n_tok=m, n_experts=n_experts, n_tiles_max=n_rows // tm),
        grid_spec=pltpu.PrefetchScalarGridSpec(
            num_scalar_prefetch=2,
            grid=(m // tt,),
            in_specs=[pl.BlockSpec((tt, d), lambda i, dr, mt: (i, 0))],
            out_specs=pl.BlockSpec(memory_space=pl.ANY),
            scratch_shapes=[pltpu.VMEM((tm, d), F32), pltpu.SemaphoreType.DMA, pltpu.SemaphoreType.DMA],
        ),
        out_shape=jax.ShapeDtypeStruct((n_rows, d), F32),
        compiler_params=_cparams(("arbitrary",), 32),
        name="dispatch",
    )(dest, meta, h2)


def _gmm_kernel(meta_ref, xs_ref, wg_ref, wu_ref, wd_ref, o_ref, xb_ref, acc_ref):
    f = pl.program_id(1)
    nf = pl.num_programs(1)

    @pl.when(pl.program_id(0) < meta_ref[LANE])
    def _():
        @pl.when(f == 0)
        def _():
            xb_ref[...] = xs_ref[...].astype(BF16)

        xb = xb_ref[...]
        g = jnp.dot(xb, wg_ref[...], preferred_element_type=F32)
        u = jnp.dot(xb, wu_ref[...], preferred_element_type=F32)
        a = (g * jax.nn.sigmoid(g) * u).astype(BF16)
        y = jnp.dot(a, wd_ref[...], preferred_element_type=F32)

        @pl.when(f == 0)
        def _():
            acc_ref[...] = y

        @pl.when((f > 0) & (f < nf - 1))
        def _():
            acc_ref[...] += y

        @pl.when(f == nf - 1)
        def _():
            o_ref[...] = acc_ref[...] + y

    @pl.when((pl.program_id(0) >= meta_ref[LANE]) & (f == nf - 1))
    def _():
        o_ref[...] = jnp.zeros_like(o_ref)


def _grouped_swiglu(xs, meta, w_gu, w_down, *, tm):
    n_rows, d = xs.shape
    n_experts, d_ff, _ = w_down.shape
    fc = _pick_chunk(d_ff, d_ff // 2)
    nf = d_ff // fc
    assert nf >= 2 and n_rows % tm == 0 and n_rows // tm <= LANE

    def tile(i, mt):
        return jnp.maximum(jnp.minimum(i, mt[LANE] - 1), 0)

    def expert(i, mt):
        return jnp.minimum(mt[tile(i, mt)], n_experts - 1)

    def chunk(i, f, mt):
        return jnp.where(i < mt[LANE], f, nf - 1)

    return pl.pallas_call(
        _gmm_kernel,
        grid_spec=pltpu.PrefetchScalarGridSpec(
            num_scalar_prefetch=1,
            grid=(n_rows // tm, nf),
            in_specs=[
                pl.BlockSpec((tm, d), lambda i, f, mt: (tile(i, mt), 0)),
                pl.BlockSpec((None, d, fc), lambda i, f, mt: (expert(i, mt), 0, chunk(i, f, mt))),
                pl.BlockSpec((None, d, fc), lambda i, f, mt: (expert(i, mt), 0, nf + chunk(i, f, mt))),
                pl.BlockSpec((None, fc, d), lambda i, f, mt: (expert(i, mt), chunk(i, f, mt), 0)),
            ],
            out_specs=pl.BlockSpec((tm, d), lambda i, f, mt: (i, 0)),
            scratch_shapes=[pltpu.VMEM((tm, d), BF16), pltpu.VMEM((tm, d), F32)],
        ),
        out_shape=jax.ShapeDtypeStruct((n_rows, d), F32),
        compiler_params=_cparams(("arbitrary", "arbitrary"), 56),
        name="grouped_swiglu",
    )(meta, xs, w_gu, w_gu, w_down)


def _combine_kernel(dest_ref, x_ref, w_ref, mod_ref, fg_ref, ys_ref, o_ref, y_ref, sems, *, tm, n_tok, d,
                    final_norm):
    i = pl.program_id(0)
    slot = i % 2

    def gather(tile, to_slot):
        def issue(blk, carry):
            for j in range(ROW_DMA_UNROLL):
                r = blk * ROW_DMA_UNROLL + j
                for c in range(2):
                    _row_copy(ys_ref, dest_ref[c * n_tok + tile * tm + r], y_ref.at[to_slot, c], r,
                              sems.at[to_slot]).start()
            return carry

        lax.fori_loop(0, tm // ROW_DMA_UNROLL, issue, 0)

    @pl.when(i == 0)
    def _():
        gather(0, 0)

    @pl.when(i + 1 < pl.num_programs(0))
    def _():
        gather(i + 1, 1 - slot)

    for c in range(2):
        pltpu.make_async_copy(ys_ref.at[pl.ds(0, tm), :], y_ref.at[slot, c], sems.at[slot]).wait()
    w = w_ref[...]
    moe = w[:, 0:1] * y_ref[slot, 0] + w[:, 1:2] * y_ref[slot, 1]
    xn = x_ref[...] + mod_ref[...][:, 5 * d:6 * d] * moe
    if final_norm:
        xn = xn * lax.rsqrt(jnp.mean(xn * xn, axis=-1, keepdims=True) + EPS) * fg_ref[...]
    o_ref[...] = xn


def _combine(x, wts, dest, ys, mod, final_g, *, seq_len, mod_row_of_tile, final_norm):
    m, d = x.shape
    tm = min(512, seq_len)
    return pl.pallas_call(
        functools.partial(_combine_kernel, tm=tm, n_tok=m, d=d, final_norm=final_norm),
        grid_spec=pltpu.PrefetchScalarGridSpec(
            num_scalar_prefetch=1,
            grid=(m // tm,),
            in_specs=[
                pl.BlockSpec((tm, d), lambda i, dr: (i, 0)),
                pl.BlockSpec((tm, LANE), lambda i, dr: (i, 0)),
                pl.BlockSpec((None, 1, 6 * d), lambda i, dr: (mod_row_of_tile(i, tm), 0, 0)),
                pl.BlockSpec((1, d), lambda i, dr: (0, 0)),
                pl.BlockSpec(memory_space=pl.ANY),
            ],
            out_specs=pl.BlockSpec((tm, d), lambda i, dr: (i, 0)),
            scratch_shapes=[pltpu.VMEM((2, 2, tm, d), F32), pltpu.SemaphoreType.DMA((2,))],
        ),
        out_shape=jax.ShapeDtypeStruct((m, d), F32),
        compiler_params=_cparams(("arbitrary",), 40),
        name="combine",
    )(dest, x, wts, mod, final_g.reshape(1, d), ys)


def _moe(h2, x, mod, final_g, router_w, w_gu, w_down, *, seq_len, mod_row_of_tile, final_norm):
    m, d = x.shape
    n_experts = router_w.shape[1]
    tm = 512
    n_rows = 2 * m + n_experts * tm
    wts, pairs, counts = _router(h2, router_w)
    dest2d, meta2d = _plan(pairs, counts, tile_rows=tm)
    dest = dest2d[0:2].reshape(2 * m)
    meta = meta2d[0:META_ROWS].reshape(META_ROWS * LANE)
    xs = _dispatch(h2, dest, meta, n_rows, tm=tm, n_experts=n_experts)
    ys = _grouped_swiglu(xs, meta, w_gu, w_down, tm=tm)
    return _combine(x, wts, dest, ys, mod, final_g, seq_len=seq_len, mod_row_of_tile=mod_row_of_tile,
                    final_norm=final_norm)


def kernel(x, c, ctx, c_ctx, norm1_g, norm2_g, final_g, w_mod, b_mod, w_in, conv_w, sink, pool_w, pool_scale,
           w_branch, w_out, ffn_w_gu, ffn_w_down, router_w, moe_w_gu, moe_w_down):
    bsz, seq, d = x.shape
    lc = ctx.shape[1]
    depth = w_in.shape[0]
    assert bsz + 1 <= 8 and seq % BLOCK == 0 and lc % BLOCK == 0 and seq % GRID_W == 0

    cvec = jnp.zeros((8, d), F32).at[:bsz].set(c).at[bsz].set(c_ctx)
    mods = _modvec(cvec, w_mod, b_mod)
    rope_tabs = _rope_tables(seq)

    lat_row = lambda i, tm: (i * tm) // seq
    ctx_row = lambda i, tm: bsz

    xl = x.reshape(bsz * seq, d)
    xc = ctx.reshape(bsz * lc, d)
    expert_w = None
    for l in range(depth):
        last = l == depth - 1
        mod = mods[l].reshape(8, 1, 6 * d)
        w_in_l = w_in[l].astype(BF16)
        wb_l, wo_l, pw_l = w_branch[l].astype(BF16), w_out[l].astype(BF16), pool_w[l].astype(BF16)
        routed = l % 2 == 1
        mixer = functools.partial(_merge, mod=mod, norm2_g=norm2_g[l], conv_w=conv_w[l], pool_w=pw_l,
                                  pool_scale=pool_scale[l], w_branch=wb_l, w_out=wo_l,
                                  h2_dtype=F32 if routed else BF16)
        if routed:
            wgu, wd = expert_w if expert_w is not None else (moe_w_gu[l // 2].astype(BF16),
                                                             moe_w_down[l // 2].astype(BF16))
        else:
            wgu, wd = ffn_w_gu[l // 2].astype(BF16), ffn_w_down[l // 2].astype(BF16)
        ride = not routed and not last
        ride_gu = moe_w_gu[(l + 1) // 2].reshape(-1, moe_w_gu.shape[-1]) if ride else None
        ride_d = moe_w_down[(l + 1) // 2].reshape(-1, d) if ride else None

        def channel_mix(h2, xm, *, seq_len, row_fn, final_norm):
            if routed:
                return _moe(h2, xm, mod, final_g, router_w[l // 2], wgu, wd, seq_len=seq_len,
                            mod_row_of_tile=row_fn, final_norm=final_norm)
            assert not final_norm
            return _ffn_dense(h2, xm, mod, wgu, wd, seq_len=seq_len, mod_row_of_tile=row_fn)

        if last:
            kvc = _inproj(xc, norm1_g[l], mod, w_in_l, seq_len=lc, mod_row_of_tile=ctx_row, kv_only=True)
        else:
            qc, kvc, mixc, gatec = _inproj(xc, norm1_g[l], mod, w_in_l, seq_len=lc, mod_row_of_tile=ctx_row)
            attn_c = _attention(qc.reshape(bsz, lc, -1), None, kvc.reshape(bsz, lc, -1), sink[l], band=False)
            xc_mid, h2c = mixer(attn_c.reshape(bsz * lc, -1), mixc, gatec, xc, seq_len=lc, mod_row_of_tile=ctx_row)
            xc_next = channel_mix(h2c, xc_mid, seq_len=lc, row_fn=ctx_row, final_norm=False)
        q, kv, mix, gate = _inproj(xl, norm1_g[l], mod, w_in_l, seq_len=seq, mod_row_of_tile=lat_row,
                                   rope_tabs=rope_tabs)
        attn = _attention(q.reshape(bsz, seq, -1), kv.reshape(bsz, seq, -1), kvc.reshape(bsz, lc, -1), sink[l],
                          band=True, cast=ride_gu)
        merged = mixer((attn[0] if ride else attn).reshape(bsz * seq, -1), mix, gate, xl, seq_len=seq,
                       mod_row_of_tile=lat_row, cast=ride_d)
        x_mid, h2 = merged[0], merged[1]
        xl = channel_mix(h2, x_mid, seq_len=seq, row_fn=lat_row, final_norm=last and l % 2 == 1)
        expert_w = (attn[1].reshape(moe_w_gu.shape[1:]), merged[2].reshape(moe_w_down.shape[1:])) if ride else None
        if not last:
            xc = xc_next
    if depth % 2 == 1:
        raise NotImplementedError("final norm is fused into the expert layer; depth must be even")
    return xl.reshape(bsz, seq, d)
```

```python
import functools

import jax
import jax.numpy as jnp
from jax import lax
from jax.experimental import pallas as pl
from jax.experimental.pallas import tpu as pltpu

F32 = jnp.float32
BF16 = jnp.bfloat16

GRID_W = 64
EPS = 1e-6
NEG_INF = -1e30
HEAD_DIM = 64
N_HEADS = 8
N_KV_HEADS = 2
GROUP = N_HEADS // N_KV_HEADS
WINDOW = 128
BLOCK = 128
ROPE_THETA = 10000.0
BRANCH = 512
POOL_SIZES = (2, 4, 8, 16)
POOL_GROUP = 128
Q_END = 512
V_END = 768
MIX_W = 4 * BRANCH
POOL_END = V_END + MIX_W

LANE = 128
MXU_DIM = 256
BF16_SUBLANE_TILE = 16
HALO = BF16_SUBLANE_TILE
MIB = 1024 * 1024


def _cparams(sem, vmem_mib):
    return pltpu.CompilerParams(dimension_semantics=sem, vmem_limit_bytes=vmem_mib * MIB)


def _pick_chunk(n, cap):
    for unit in (MXU_DIM, LANE):
        fits = [c for c in range(unit, min(n, cap) + 1, unit) if n % c == 0]
        if fits:
            return fits[-1]
    raise ValueError((n, cap))


def _resident(shape):
    nd = len(shape)
    return pl.BlockSpec(shape, lambda *_: (0,) * nd, pipeline_mode=pl.Buffered(1))


def _norm_mod(x, g, shift, scale):
    y = x * lax.rsqrt(jnp.mean(x * x, axis=-1, keepdims=True) + EPS) * g
    return y * (1.0 + scale) + shift


def _modvec_kernel(c_ref, w_ref, b_ref, o_ref):
    c = c_ref[...]
    s = c * jax.nn.sigmoid(c)
    o_ref[...] = jnp.dot(s, w_ref[...], preferred_element_type=F32) + b_ref[...]


def _modvec(cvec, w_mod, b_mod):
    depth, d, n = w_mod.shape
    nc = _pick_chunk(n, 1536)
    return pl.pallas_call(
        _modvec_kernel,
        grid=(depth, n // nc),
        in_specs=[
            pl.BlockSpec((8, d), lambda l, j: (0, 0)),
            pl.BlockSpec((None, d, nc), lambda l, j: (l, 0, j)),
            pl.BlockSpec((None, 1, nc), lambda l, j: (l, 0, j)),
        ],
        out_specs=pl.BlockSpec((None, 8, nc), lambda l, j: (l, 0, j)),
        out_shape=jax.ShapeDtypeStruct((depth, 8, n), F32),
        compiler_params=_cparams(("arbitrary", "arbitrary"), 32),
        name="modvec",
    )(cvec, w_mod, b_mod.reshape(depth, 1, n))


def _rope_tables(seq_len):
    n_freq = HEAD_DIM // 4
    inv = ROPE_THETA ** (-jnp.arange(n_freq, dtype=F32) / n_freq)
    pos = jnp.arange(seq_len)
    row = (pos // GRID_W).astype(F32)[:, None] * inv[None, :]
    col = (pos % GRID_W).astype(F32)[:, None] * inv[None, :]
    zero = jnp.zeros_like(row)
    cos = jnp.concatenate([jnp.cos(row)] * 2 + [jnp.cos(col)] * 2, axis=-1)
    s_lo = jnp.concatenate([-jnp.sin(row), zero, -jnp.sin(col), zero], axis=-1)
    s_hi = jnp.concatenate([zero, jnp.sin(row), zero, jnp.sin(col)], axis=-1)
    return tuple(jnp.tile(t, (1, LANE // HEAD_DIM)) for t in (cos, s_lo, s_hi))


def _inproj_kernel(*refs, rope, kv_only, d):
    x_ref, g_ref, mod_ref, w_ref = refs[:4]
    refs = refs[4:]
    if rope:
        cos_ref, slo_ref, shi_ref = refs[:3]
        refs = refs[3:]

        def rot(z):
            return (z * cos_ref[...] + pltpu.roll(z, LANE - 16, 1) * slo_ref[...]
                    + pltpu.roll(z, 16, 1) * shi_ref[...])
    else:
        def rot(z):
            return z

    mod = mod_ref[...]
    h = _norm_mod(x_ref[...], g_ref[...], mod[:, 0:d], mod[:, d:2 * d]).astype(BF16)

    def proj(c0, c1):
        return jnp.dot(h, w_ref[:, c0:c1], preferred_element_type=F32)

    if kv_only:
        (kv_ref,) = refs
        z = proj(0, 2 * LANE)
        kv_ref[:, 0:LANE] = rot(z[:, 0:LANE]).astype(BF16)
        kv_ref[:, LANE:] = z[:, LANE:].astype(BF16)
        return

    q_ref, kv_ref, mix_ref, gate_ref = refs
    cw = 512
    n_gate = gate_ref.shape[1]
    for c in range(n_gate // cw):
        zg = proj(POOL_END + c * cw, POOL_END + (c + 1) * cw)
        gate_ref[:, c * cw:(c + 1) * cw] = jax.nn.sigmoid(zg).astype(BF16)
    z = proj(0, Q_END)
    for j in range(Q_END // LANE):
        q_ref[:, j * LANE:(j + 1) * LANE] = (rot(z[:, j * LANE:(j + 1) * LANE]) * HEAD_DIM ** -0.5).astype(BF16)
    z = proj(Q_END, V_END)
    kv_ref[:, 0:LANE] = rot(z[:, 0:LANE]).astype(BF16)
    kv_ref[:, LANE:] = z[:, LANE:].astype(BF16)
    for c in range(MIX_W // cw):
        mix_ref[:, c * cw:(c + 1) * cw] = proj(V_END + c * cw, V_END + (c + 1) * cw).astype(BF16)


def _inproj(x, norm_g, mod, w_in, *, seq_len, mod_row_of_tile, rope_tabs=None, kv_only=False):
    m, d = x.shape
    tm = min(512, seq_len)
    assert m % tm == 0 and seq_len % tm == 0
    tiles_per_seq = seq_len // tm
    rope = rope_tabs is not None
    in_w = w_in.shape[1]
    in_specs = [
        pl.BlockSpec((tm, d), lambda i: (i, 0)),
        pl.BlockSpec((1, d), lambda i: (0, 0)),
        pl.BlockSpec((None, 1, 6 * d), lambda i: (mod_row_of_tile(i, tm), 0, 0)),
        pl.BlockSpec((d, 2 * LANE), lambda i: (0, Q_END // (2 * LANE))) if kv_only else _resident((d, in_w)),
    ]
    args = [x, norm_g.reshape(1, d), mod, w_in]
    if rope:
        in_specs += [pl.BlockSpec((tm, LANE), lambda i: (i % tiles_per_seq, 0))] * 3
        args += list(rope_tabs)
    if kv_only:
        out_specs = pl.BlockSpec((tm, 2 * LANE), lambda i: (i, 0))
        out_shape = jax.ShapeDtypeStruct((m, 2 * LANE), BF16)
    else:
        widths = (Q_END, 2 * LANE, MIX_W, in_w - POOL_END)
        out_specs = [pl.BlockSpec((tm, w), lambda i: (i, 0)) for w in widths]
        out_shape = [jax.ShapeDtypeStruct((m, w), BF16) for w in widths]
    return pl.pallas_call(
        functools.partial(_inproj_kernel, rope=rope, kv_only=kv_only, d=d),
        grid=(m // tm,),
        in_specs=in_specs,
        out_specs=out_specs,
        out_shape=out_shape,
        compiler_params=_cparams(("parallel",), 48),
        name="inproj_kv" if kv_only else "inproj",
    )(*args)


ATTN_STRIP = 32
ATTN_QBLOCKS = 2
ATTN_AHEAD = 2


def _attn_kernel(sink_ref, q_ref, *refs, band, carry_cast):
    refs, (s_ref, p_ref) = list(refs[:-2]), refs[-2:]
    if carry_cast:
        cast_out = refs.pop()
        cast_in = refs.pop(-2)
        cast_out[...] = cast_in[...].astype(BF16)
    if band:
        kvp_ref, kvm_ref, kvn_ref, kvc_ref, bias_ref, o_ref = refs
    else:
        kvc_ref, o_ref = refs
    kvc = kvc_ref[0]
    lc = kvc.shape[0]
    nloc = 3 * BLOCK if band else 0
    nt = (((1,), (1,)), ((), ()))
    w_ctx = jnp.concatenate([kvc[:, LANE:], jnp.ones((lc, LANE), BF16)], axis=1)
    if band:
        n = pl.program_id(1)
        kv4 = jnp.concatenate([kvp_ref[0], kvm_ref[0], kvn_ref[0]], axis=0)
        w4 = jnp.concatenate([kv4[:, LANE:], jnp.ones((4 * BLOCK, LANE), BF16)], axis=1)
        col = lax.broadcasted_iota(jnp.int32, (1, nloc), 1)
        head_edge = jnp.where((col < BLOCK) & (n == 0), NEG_INF, 0.0)
        tail_edge = jnp.where((col >= 2 * BLOCK) & (n == pl.num_programs(1) - 1), NEG_INF, 0.0)
        biases = [bias_ref[...] + head_edge, bias_ref[...] + tail_edge]
    units = [(sb, h) for sb in range(ATTN_QBLOCKS) for h in range(N_HEADS)]
    def scores(u):
        sb, h = units[u]
        qh = q_ref[0, sb * BLOCK:(sb + 1) * BLOCK, h * HEAD_DIM:(h + 1) * HEAD_DIM]
        ks = slice(h // GROUP * HEAD_DIM, (h // GROUP + 1) * HEAD_DIM)
        if band:
            kl = kv4[sb * BLOCK:sb * BLOCK + nloc, ks]
            s_ref[u, :, 0:nloc] = lax.dot_general(qh, kl, nt, preferred_element_type=F32) + biases[sb]
        s_ref[u, :, nloc:] = lax.dot_general(qh, kvc[:, ks], nt, preferred_element_type=F32)

    def probs(u):
        sink = sink_ref[units[u][1]]
        esink = []
        for r in range(0, BLOCK, ATTN_STRIP):
            s = s_ref[u, r:r + ATTN_STRIP, :]
            m = jnp.maximum(jnp.max(s, axis=-1, keepdims=True), sink)
            p_ref[u, r:r + ATTN_STRIP, :] = jnp.exp(s - m).astype(BF16)
            esink.append(jnp.exp(sink - m))
        return jnp.concatenate(esink, axis=0)

    def weighted_values(u, esink):
        sb, h = units[u]
        ks = slice(h // GROUP * HEAD_DIM, (h // GROUP + 1) * HEAD_DIM)
        o2 = jnp.dot(p_ref[u, :, nloc:], w_ctx, preferred_element_type=F32)
        if band:
            o2 = o2 + jnp.dot(p_ref[u, :, 0:nloc], w4[sb * BLOCK:sb * BLOCK + nloc], preferred_element_type=F32)
        return o2[:, ks] / (o2[:, LANE:LANE + HEAD_DIM] + esink)

    outs, esinks = [], []
    for u in range(min(ATTN_AHEAD, len(units))):
        scores(u)
    for u in range(len(units)):
        if u + ATTN_AHEAD < len(units):
            scores(u + ATTN_AHEAD)
        esinks.append(probs(u))
        if u >= 1:
            outs.append(weighted_values(u - 1, esinks[u - 1]))
    outs.append(weighted_values(len(units) - 1, esinks[-1]))
    for sb in range(ATTN_QBLOCKS):
        o_ref[0, sb * BLOCK:(sb + 1) * BLOCK, :] = jnp.concatenate(
            outs[sb * N_HEADS:(sb + 1) * N_HEADS], axis=1).astype(BF16)


def _cast_rider(w, n_steps, index_map):
    rows, cols = w.shape
    assert rows % (n_steps * BF16_SUBLANE_TILE) == 0, (w.shape, n_steps)
    return pl.BlockSpec((rows // n_steps, cols), index_map), jax.ShapeDtypeStruct((rows, cols), BF16)


def _attention(q, kv, kvc, sink, *, band, cast=None):
    b, l, _ = q.shape
    lc = kvc.shape[1]
    tq = ATTN_QBLOCKS * BLOCK
    assert l % tq == 0
    nb = l // BLOCK
    in_specs = [
        pl.BlockSpec(memory_space=pltpu.SMEM),
        pl.BlockSpec((1, tq, N_HEADS * HEAD_DIM), lambda bi, n: (bi, n, 0)),
    ]
    args = [sink, q]
    if band:
        in_specs += [
            pl.BlockSpec((1, BLOCK, 2 * LANE), lambda bi, n: (bi, jnp.maximum(ATTN_QBLOCKS * n - 1, 0), 0)),
            pl.BlockSpec((1, tq, 2 * LANE), lambda bi, n: (bi, n, 0)),
            pl.BlockSpec((1, BLOCK, 2 * LANE), lambda bi, n: (bi, jnp.minimum(ATTN_QBLOCKS * (n + 1), nb - 1), 0)),
        ]
        args += [kv, kv, kv]
    in_specs.append(pl.BlockSpec((1, lc, 2 * LANE), lambda bi, n: (bi, 0, 0)))
    args.append(kvc)
    nkeys = lc
    if band:
        rel = jnp.arange(3 * BLOCK)[None, :] - BLOCK - jnp.arange(BLOCK)[:, None]
        in_specs.append(pl.BlockSpec((BLOCK, 3 * BLOCK), lambda bi, n: (0, 0)))
        args.append(jnp.where(jnp.abs(rel) <= WINDOW, 0.0, NEG_INF).astype(F32))
        nkeys += 3 * BLOCK
    nq = l // tq
    out_specs = [pl.BlockSpec((1, tq, N_HEADS * HEAD_DIM), lambda bi, n: (bi, n, 0))]
    out_shape = [jax.ShapeDtypeStruct((b, l, N_HEADS * HEAD_DIM), BF16)]
    if cast is not None:
        spec, shape = _cast_rider(cast, b * nq, lambda bi, n: (bi * nq + n, 0))
        in_specs.append(spec)
        args.append(cast)
        out_specs.append(spec)
        out_shape.append(shape)
    outs = pl.pallas_call(
        functools.partial(_attn_kernel, band=band, carry_cast=cast is not None),
        grid=(b, nq),
        in_specs=in_specs,
        out_specs=out_specs,
        out_shape=out_shape,
        scratch_shapes=[pltpu.VMEM((ATTN_QBLOCKS * N_HEADS, BLOCK, nkeys), F32),
                        pltpu.VMEM((ATTN_QBLOCKS * N_HEADS, BLOCK, nkeys), BF16)],
        compiler_params=_cparams(("parallel", "parallel"), 48),
        name="attn_band" if band else "attn_ctx",
    )(*args)
    return outs[0] if cast is None else outs


def _merge_kernel(attn_ref, mix_ref, prev_ref, next_ref, gate_ref, x_ref, mod_ref, n2g_ref, convw_ref,
                  poolw_ref, pscale_ref, wb_ref, wo_ref, *refs, tm, seq_len, d):
    if len(refs) == 4:
        cast_in, xo_ref, h2_ref, cast_out = refs
        cast_out[...] = cast_in[...].astype(BF16)
    else:
        xo_ref, h2_ref = refs
    tile = pl.program_id(0) % (seq_len // tm)
    keep_prev = (tile != 0).astype(F32)
    keep_next = (tile != seq_len // tm - 1).astype(F32)
    mixm = mix_ref[...]
    prev = prev_ref[...].astype(F32) * keep_prev
    nxt = next_ref[...].astype(F32) * keep_next
    b = BRANCH
    cx, cb, cc = (mixm[:, j * b:(j + 1) * b].astype(F32) for j in range(3))

    p = cc * cx
    p_prev = prev[HALO - 1:HALO, 2 * b:3 * b] * prev[HALO - 1:HALO, 0:b]
    p_next = nxt[0:1, 2 * b:3 * b] * nxt[0:1, 0:b]
    ridx = lax.broadcasted_iota(jnp.int32, (tm, b), 0)
    p_dn = jnp.where(ridx == 0, p_prev, pltpu.roll(p, 1, 0))
    p_up = jnp.where(ridx == tm - 1, p_next, pltpu.roll(p, tm - 1, 0))
    cw = convw_ref[...]
    conv_out = (cb * (p_dn * cw[0:1] + p * cw[1:2] + p_up * cw[2:3])).astype(BF16)

    u_main = mixm[:, 3 * b:4 * b].astype(F32)
    u_ext = jnp.concatenate([prev[:, 3 * b:4 * b], u_main, nxt[:, 3 * b:4 * b]], axis=0)
    ext = tm + 2 * HALO

    def shift(a, s):
        return pltpu.roll(a, s % ext, 0)

    tpos = tile * tm + lax.broadcasted_iota(jnp.int32, (tm, 1), 0)
    pooled = []
    for gi, w in enumerate(POOL_SIZES):
        gs = slice(gi * POOL_GROUP, (gi + 1) * POOL_GROUP)
        ug = u_ext[:, gs]
        a = ug + shift(ug, 1)
        ww = 2
        while ww < w:
            a = shift(a, ww // 2) + shift(a, -(ww // 2))
            ww *= 2
        cnt = jnp.minimum(tpos + w // 2, seq_len) - jnp.maximum(tpos - w // 2, 0)
        dlt = a[HALO:HALO + tm] / cnt.astype(F32) - u_main[:, gs]
        pooled.append(jnp.dot(dlt.astype(BF16), poolw_ref[gi], preferred_element_type=F32))
    pool_out = (jnp.concatenate(pooled, axis=1) * pscale_ref[...]).astype(BF16)

    y = (gate_ref[:, 0:d].astype(F32) * jnp.dot(attn_ref[...], wb_ref[0], preferred_element_type=F32)
         + gate_ref[:, d:2 * d].astype(F32) * jnp.dot(conv_out, wb_ref[1], preferred_element_type=F32)
         + gate_ref[:, 2 * d:3 * d].astype(F32) * jnp.dot(pool_out, wb_ref[2], preferred_element_type=F32))
    o = jnp.dot(y.astype(BF16), wo_ref[...], preferred_element_type=F32)
    mod = mod_ref[...]
    xn = x_ref[...] + mod[:, 2 * d:3 * d] * o
    xo_ref[...] = xn
    h2_ref[...] = _norm_mod(xn, n2g_ref[...], mod[:, 3 * d:4 * d], mod[:, 4 * d:5 * d]).astype(h2_ref.dtype)


def _merge(attn, mix, gate, x, mod, norm2_g, conv_w, pool_w, pool_scale, w_branch, w_out, *, seq_len,
           mod_row_of_tile, h2_dtype, cast=None):
    m, d = x.shape
    tm = min(512, seq_len)
    assert m % tm == 0 and seq_len % tm == 0 and tm % HALO == 0
    hb = tm // HALO
    n_halo = m // HALO
    row = lambda w: pl.BlockSpec((tm, w), lambda i: (i, 0))
    in_specs = [
        row(BRANCH),
        row(MIX_W),
        pl.BlockSpec((HALO, MIX_W), lambda i: (jnp.maximum(i * hb - 1, 0), 0)),
        pl.BlockSpec((HALO, MIX_W), lambda i: (jnp.minimum((i + 1) * hb, n_halo - 1), 0)),
        row(3 * d),
        row(d),
        pl.BlockSpec((None, 1, 6 * d), lambda i: (mod_row_of_tile(i, tm), 0, 0)),
        pl.BlockSpec((1, d), lambda i: (0, 0)),
        _resident(conv_w.shape),
        _resident(pool_w.shape),
        pl.BlockSpec((1, BRANCH), lambda i: (0, 0)),
        _resident(w_branch.shape),
        _resident(w_out.shape),
    ]
    args = [attn, mix, mix, mix, gate, x, mod, norm2_g.reshape(1, d), conv_w, pool_w,
            pool_scale.reshape(1, BRANCH), w_branch, w_out]
    out_specs = [row(d), row(d)]
    out_shape = [jax.ShapeDtypeStruct((m, d), F32), jax.ShapeDtypeStruct((m, d), h2_dtype)]
    if cast is not None:
        spec, shape = _cast_rider(cast, m // tm, lambda i: (i, 0))
        in_specs.append(spec)
        args.append(cast)
        out_specs.append(spec)
        out_shape.append(shape)
    return pl.pallas_call(
        functools.partial(_merge_kernel, tm=tm, seq_len=seq_len, d=d),
        grid=(m // tm,),
        in_specs=in_specs,
        out_specs=out_specs,
        out_shape=out_shape,
        compiler_params=_cparams(("parallel",), 48),
        name="merge",
    )(*args)


def _ffn_kernel(h_ref, x_ref, mod_ref, wgu_ref, wd_ref, o_ref, *, d, d_ff, fc):
    h = h_ref[...]
    acc = None
    for f in range(0, d_ff, fc):
        g = jnp.dot(h, wgu_ref[:, f:f + fc], preferred_element_type=F32)
        u = jnp.dot(h, wgu_ref[:, d_ff + f:d_ff + f + fc], preferred_element_type=F32)
        a = (g * jax.nn.sigmoid(g) * u).astype(BF16)
        y = jnp.dot(a, wd_ref[f:f + fc, :], preferred_element_type=F32)
        acc = y if acc is None else acc + y
    o_ref[...] = x_ref[...] + mod_ref[...][:, 5 * d:6 * d] * acc


def _ffn_dense(h2, x, mod, w_gu, w_down, *, seq_len, mod_row_of_tile):
    m, d = x.shape
    d_ff = w_down.shape[0]
    tm = min(512, seq_len)
    assert m % tm == 0 and seq_len % tm == 0
    fc = _pick_chunk(d_ff, 3072)
    return pl.pallas_call(
        functools.partial(_ffn_kernel, d=d, d_ff=d_ff, fc=fc),
        grid=(m // tm,),
        in_specs=[
            pl.BlockSpec((tm, d), lambda i: (i, 0)),
            pl.BlockSpec((tm, d), lambda i: (i, 0)),
            pl.BlockSpec((None, 1, 6 * d), lambda i: (mod_row_of_tile(i, tm), 0, 0)),
            _resident(w_gu.shape),
            _resident(w_down.shape),
        ],
        out_specs=pl.BlockSpec((tm, d), lambda i: (i, 0)),
        out_shape=jax.ShapeDtypeStruct((m, d), F32),
        compiler_params=_cparams(("parallel",), 56),
        name="ffn_dense",
    )(h2, x, mod, w_gu, w_down)


ROUTE_ROWS = 8
META_ROWS = 4
META_TILES, META_PAD_START, META_PAD_LEN = LANE, 2 * LANE, 3 * LANE


def _router_kernel(h_ref, rw_ref, tri_ref, wts_ref, pairs_ref, cnt_ref, carry_ref, *, n_experts):
    @pl.when(pl.program_id(0) == 0)
    def _():
        carry_ref[...] = jnp.zeros_like(carry_ref)

    h, w = h_ref[...], rw_ref[...]
    h_hi, w_hi = h.astype(BF16), w.astype(BF16)
    h_lo, w_lo = (h - h_hi.astype(F32)).astype(BF16), (w - w_hi.astype(F32)).astype(BF16)
    logits = (jnp.dot(h_hi, w_hi, preferred_element_type=F32) + jnp.dot(h_lo, w_hi, preferred_element_type=F32)
              + jnp.dot(h_hi, w_lo, preferred_element_type=F32))
    lane = lax.broadcasted_iota(jnp.int32, logits.shape, 1).astype(F32)
    logits = jnp.where(lane < n_experts, logits, -jnp.inf)
    m1 = jnp.max(logits, axis=-1, keepdims=True)
    i1 = jnp.min(jnp.where(logits == m1, lane, float(LANE)), axis=-1, keepdims=True)
    rest = jnp.where(lane == i1, -jnp.inf, logits)
    m2 = jnp.max(rest, axis=-1, keepdims=True)
    i2 = jnp.min(jnp.where(rest == m2, lane, float(LANE)), axis=-1, keepdims=True)
    e2 = jnp.exp(m2 - m1)
    den = 1.0 + e2
    wts_ref[...] = jnp.where(lane == 0, 1.0 / den, jnp.where(lane == 1, e2 / den, 0.0))

    m1t = (lane == i1).astype(F32).T[0:ROUTE_ROWS]
    m2t = (lane == i2).astype(F32).T[0:ROUTE_ROWS]
    mem = m1t + m2t
    before = jnp.dot(mem.astype(BF16), tri_ref[...], preferred_element_type=F32) + carry_ref[:, 0:1]
    eid = lax.broadcasted_iota(jnp.int32, mem.shape, 0).astype(F32)
    rows = [jnp.sum(m1t * eid, axis=0, keepdims=True), jnp.sum(m2t * eid, axis=0, keepdims=True),
            jnp.sum(m1t * before, axis=0, keepdims=True), jnp.sum(m2t * before, axis=0, keepdims=True)]
    rows += [jnp.zeros_like(rows[0])] * (ROUTE_ROWS - len(rows))
    pairs_ref[...] = jnp.concatenate(rows, axis=0)
    carry_ref[...] = carry_ref[...] + jnp.sum(mem, axis=1, keepdims=True)
    cnt_ref[...] = carry_ref[...]


def _router(h2, router_w):
    m, d = h2.shape
    n_experts = router_w.shape[1]
    assert n_experts <= ROUTE_ROWS
    rw = jnp.pad(router_w, ((0, 0), (0, LANE - n_experts)))
    tm = min(512, m)
    tri = jnp.triu(jnp.ones((tm, tm), BF16), k=1)
    return pl.pallas_call(
        functools.partial(_router_kernel, n_experts=n_experts),
        grid=(m // tm,),
        in_specs=[
            pl.BlockSpec((tm, d), lambda i: (i, 0)),
            pl.BlockSpec((d, LANE), lambda i: (0, 0)),
            pl.BlockSpec((tm, tm), lambda i: (0, 0)),
        ],
        out_specs=[
            pl.BlockSpec((tm, LANE), lambda i: (i, 0)),
            pl.BlockSpec((ROUTE_ROWS, tm), lambda i: (0, i)),
            pl.BlockSpec((ROUTE_ROWS, LANE), lambda i: (0, 0)),
        ],
        out_shape=[jax.ShapeDtypeStruct((m, LANE), F32), jax.ShapeDtypeStruct((ROUTE_ROWS, m), F32),
                   jax.ShapeDtypeStruct((ROUTE_ROWS, LANE), F32)],
        scratch_shapes=[pltpu.VMEM((ROUTE_ROWS, LANE), F32)],
        compiler_params=_cparams(("arbitrary",), 32),
        name="router",
    )(h2, rw, tri)


def _plan_kernel(pairs_ref, cnt_ref, dest_ref, meta_ref, *, tile_rows):
    cnt = cnt_ref[...]
    padded = jnp.floor((cnt + (tile_rows - 1)) * (1.0 / tile_rows)) * tile_rows
    sub = lax.broadcasted_iota(jnp.int32, cnt.shape, 0)
    lane = lax.broadcasted_iota(jnp.int32, cnt.shape, 1)
    end_row = jnp.sum(jnp.where(sub <= lane, padded, 0.0), axis=0, keepdims=True)
    end_col = jnp.sum(jnp.where(lane == sub, end_row, 0.0), axis=1, keepdims=True)
    start_col = end_col - padded[:, 0:1]
    p = pairs_ref[...]
    eid = lax.broadcasted_iota(jnp.int32, p.shape, 0).astype(F32)
    d1 = jnp.sum(jnp.where(eid == p[0:1], start_col, 0.0), axis=0, keepdims=True) + p[2:3]
    d2 = jnp.sum(jnp.where(eid == p[1:2], start_col, 0.0), axis=0, keepdims=True) + p[3:4]
    dest_ref[...] = jnp.concatenate([d1, d2] + [jnp.zeros_like(d1)] * (ROUTE_ROWS - 2), axis=0).astype(jnp.int32)
    tile_start = (lane * tile_rows).astype(F32)
    tile_e = jnp.sum((end_col <= tile_start).astype(F32), axis=0, keepdims=True)
    n_tiles = jnp.max(end_col, axis=0, keepdims=True) * (1.0 / tile_rows)
    pad_start = jnp.sum(jnp.where(lane == sub, start_col + cnt, 0.0), axis=0, keepdims=True)
    pad_len = jnp.sum(jnp.where(lane == sub, padded - cnt, 0.0), axis=0, keepdims=True)
    meta = jnp.concatenate([tile_e, jnp.broadcast_to(n_tiles, tile_e.shape), pad_start, pad_len]
                           + [jnp.zeros_like(tile_e)] * (ROUTE_ROWS - META_ROWS), axis=0)
    meta_ref[...] = meta.astype(jnp.int32)


def _plan(pairs, counts, *, tile_rows):
    m = pairs.shape[1]
    tp = min(2048, m)
    return pl.pallas_call(
        functools.partial(_plan_kernel, tile_rows=tile_rows),
        grid=(m // tp,),
        in_specs=[pl.BlockSpec((ROUTE_ROWS, tp), lambda i: (0, i)), pl.BlockSpec((ROUTE_ROWS, LANE), lambda i: (0, 0))],
        out_specs=[pl.BlockSpec((ROUTE_ROWS, tp), lambda i: (0, i)), pl.BlockSpec((ROUTE_ROWS, LANE), lambda i: (0, 0))],
        out_shape=[jax.ShapeDtypeStruct((ROUTE_ROWS, m), jnp.int32), jax.ShapeDtypeStruct((ROUTE_ROWS, LANE), jnp.int32)],
        compiler_params=_cparams(("arbitrary",), 32),
        name="plan",
    )(pairs, counts)


ROW_DMA_UNROLL = 8


def _row_copy(src, src_row, dst, dst_row, sem):
    return pltpu.make_async_copy(src.at[pl.ds(src_row, 1), :], dst.at[pl.ds(dst_row, 1), :], sem)


def _dispatch_kernel(dest_ref, meta_ref, h_ref, xs_ref, zero_ref, sem, zsem, *, tm, n_tok, n_experts, n_tiles_max):
    tile_rows = zero_ref.shape[0]
    i = pl.program_id(0)
    last = pl.num_programs(0) - 1
    base = i * tm

    def wait_rows():
        for _ in range(2):
            pltpu.make_async_copy(h_ref, xs_ref.at[pl.ds(0, tm), :], sem).wait()

    def issue(blk, carry):
        for j in range(ROW_DMA_UNROLL):
            r = blk * ROW_DMA_UNROLL + j
            _row_copy(h_ref, r, xs_ref, dest_ref[base + r], sem).start()
            _row_copy(h_ref, r, xs_ref, dest_ref[n_tok + base + r], sem).start()
        return carry

    lax.fori_loop(0, tm // ROW_DMA_UNROLL, issue, 0)

    @pl.when(i < last)
    def _():
        wait_rows()

    @pl.when(i == last)
    def _():
        zero_ref[...] = jnp.zeros_like(zero_ref)

        def zero_row(e):
            def body(r, carry):
                _row_copy(zero_ref, 0, xs_ref, meta_ref[META_PAD_START + e] + r, zsem).start()
                return carry
            return body

        def tail_copy(k):
            return pltpu.make_async_copy(zero_ref, xs_ref.at[pl.ds(k * tile_rows, tile_rows), :], zsem)

        def zero_tile(k, carry):
            tail_copy(k).start()
            return carry

        def wait_tile(k, carry):
            tail_copy(k).wait()
            return carry

        for e in range(n_experts):
            lax.fori_loop(0, meta_ref[META_PAD_LEN + e], zero_row(e), 0)
        lax.fori_loop(meta_ref[META_TILES], n_tiles_max, zero_tile, 0)
        def wait_row(r, carry):
            _row_copy(zero_ref, 0, xs_ref, 0, zsem).wait()
            return carry

        for e in range(n_experts):
            lax.fori_loop(0, meta_ref[META_PAD_LEN + e], wait_row, 0)

        lax.fori_loop(meta_ref[META_TILES], n_tiles_max, wait_tile, 0)
        wait_rows()


def _dispatch(h2, dest, meta, n_rows, *, tm, n_experts):
    m, d = h2.shape
    tt = min(512, m)
    return pl.pallas_call(
        functools.partial(_dispatch_kernel, tm=tt, n_tok=m, n_experts=n_experts, n_tiles_max=n_rows // tm),
        grid_spec=pltpu.PrefetchScalarGridSpec(
            num_scalar_prefetch=2,
            grid=(m // tt,),
            in_specs=[pl.BlockSpec((tt, d), lambda i, dr, mt: (i, 0))],
            out_specs=pl.BlockSpec(memory_space=pl.ANY),
            scratch_shapes=[pltpu.VMEM((tm, d), F32), pltpu.SemaphoreType.DMA, pltpu.SemaphoreType.DMA],
        ),
        out_shape=jax.ShapeDtypeStruct((n_rows, d), F32),
        compiler_params=_cparams(("arbitrary",), 32),
        name="dispatch",
    )(dest, meta, h2)


def _gmm_kernel(meta_ref, xs_ref, wg_ref, wu_ref, wd_ref, o_ref, acc_ref):
    f = pl.program_id(1)
    nf = pl.num_programs(1)

    @pl.when((pl.program_id(0) == 0) & (f == 0))
    def _():
        acc_ref[...] = jnp.zeros_like(acc_ref)

    @pl.when(pl.program_id(0) < meta_ref[LANE])
    def _():
        xb = xs_ref[...].astype(BF16)
        g = jnp.dot(xb, wg_ref[...], preferred_element_type=F32)
        u = jnp.dot(xb, wu_ref[...], preferred_element_type=F32)
        a = (g * jax.nn.sigmoid(g) * u).astype(BF16)
        y = jnp.dot(a, wd_ref[...], preferred_element_type=F32)
        total = y + jnp.where(f > 0, acc_ref[...], 0.0)
        acc_ref[...] = total
        o_ref[...] = total

    @pl.when((pl.program_id(0) >= meta_ref[LANE]) & (f == nf - 1))
    def _():
        o_ref[...] = jnp.zeros_like(o_ref)


def _grouped_swiglu(xs, meta, w_gu, w_down, *, tm):
    n_rows, d = xs.shape
    n_experts, d_ff, _ = w_down.shape
    fc = _pick_chunk(d_ff, d_ff // 2)
    nf = d_ff // fc
    assert nf >= 2 and n_rows % tm == 0 and n_rows // tm <= LANE

    def tile(i, mt):
        return jnp.maximum(jnp.minimum(i, mt[LANE] - 1), 0)

    def expert(i, mt):
        return jnp.minimum(mt[tile(i, mt)], n_experts - 1)

    def chunk(i, f, mt):
        return jnp.where(i < mt[LANE], f, nf - 1)

    return pl.pallas_call(
        _gmm_kernel,
        grid_spec=pltpu.PrefetchScalarGridSpec(
            num_scalar_prefetch=1,
            grid=(n_rows // tm, nf),
            in_specs=[
                pl.BlockSpec((tm, d), lambda i, f, mt: (tile(i, mt), 0)),
                pl.BlockSpec((None, d, fc), lambda i, f, mt: (expert(i, mt), 0, chunk(i, f, mt))),
                pl.BlockSpec((None, d, fc), lambda i, f, mt: (expert(i, mt), 0, nf + chunk(i, f, mt))),
                pl.BlockSpec((None, fc, d), lambda i, f, mt: (expert(i, mt), chunk(i, f, mt), 0)),
            ],
            out_specs=pl.BlockSpec((tm, d), lambda i, f, mt: (i, 0)),
            scratch_shapes=[pltpu.VMEM((tm, d), F32)],
        ),
        out_shape=jax.ShapeDtypeStruct((n_rows, d), F32),
        compiler_params=_cparams(("arbitrary", "arbitrary"), 56),
        name="grouped_swiglu",
    )(meta, xs, w_gu, w_gu, w_down)


def _combine_kernel(dest_ref, x_ref, w_ref, mod_ref, fg_ref, ys_ref, o_ref, y_ref, sems, *, tm, n_tok, d,
                    final_norm):
    i = pl.program_id(0)
    slot = i % 2

    def gather(tile, to_slot):
        def issue(blk, carry):
            for j in range(ROW_DMA_UNROLL):
                r = blk * ROW_DMA_UNROLL + j
                for c in range(2):
                    _row_copy(ys_ref, dest_ref[c * n_tok + tile * tm + r], y_ref.at[to_slot, c], r,
                              sems.at[to_slot]).start()
            return carry

        lax.fori_loop(0, tm // ROW_DMA_UNROLL, issue, 0)

    @pl.when(i == 0)
    def _():
        gather(0, 0)

    @pl.when(i + 1 < pl.num_programs(0))
    def _():
        gather(i + 1, 1 - slot)

    for c in range(2):
        pltpu.make_async_copy(ys_ref.at[pl.ds(0, tm), :], y_ref.at[slot, c], sems.at[slot]).wait()
    w = w_ref[...]
    moe = w[:, 0:1] * y_ref[slot, 0] + w[:, 1:2] * y_ref[slot, 1]
    xn = x_ref[...] + mod_ref[...][:, 5 * d:6 * d] * moe
    if final_norm:
        xn = xn * lax.rsqrt(jnp.mean(xn * xn, axis=-1, keepdims=True) + EPS) * fg_ref[...]
    o_ref[...] = xn


def _combine(x, wts, dest, ys, mod, final_g, *, seq_len, mod_row_of_tile, final_norm):
    m, d = x.shape
    tm = min(512, seq_len)
    return pl.pallas_call(
        functools.partial(_combine_kernel, tm=tm, n_tok=m, d=d, final_norm=final_norm),
        grid_spec=pltpu.PrefetchScalarGridSpec(
            num_scalar_prefetch=1,
            grid=(m // tm,),
            in_specs=[
                pl.BlockSpec((tm, d), lambda i, dr: (i, 0)),
                pl.BlockSpec((tm, LANE), lambda i, dr: (i, 0)),
                pl.BlockSpec((None, 1, 6 * d), lambda i, dr: (mod_row_of_tile(i, tm), 0, 0)),
                pl.BlockSpec((1, d), lambda i, dr: (0, 0)),
                pl.BlockSpec(memory_space=pl.ANY),
            ],
            out_specs=pl.BlockSpec((tm, d), lambda i, dr: (i, 0)),
            scratch_shapes=[pltpu.VMEM((2, 2, tm, d), F32), pltpu.SemaphoreType.DMA((2,))],
        ),
        out_shape=jax.ShapeDtypeStruct((m, d), F32),
        compiler_params=_cparams(("arbitrary",), 40),
        name="combine",
    )(dest, x, wts, mod, final_g.reshape(1, d), ys)


def _moe(h2, x, mod, final_g, router_w, w_gu, w_down, *, seq_len, mod_row_of_tile, final_norm):
    m, d = x.shape
    n_experts = router_w.shape[1]
    tm = 512
    n_rows = 2 * m + n_experts * tm
    wts, pairs, counts = _router(h2, router_w)
    dest2d, meta2d = _plan(pairs, counts, tile_rows=tm)
    dest = dest2d[0:2].reshape(2 * m)
    meta = meta2d[0:META_ROWS].reshape(META_ROWS * LANE)
    xs = _dispatch(h2, dest, meta, n_rows, tm=tm, n_experts=n_experts)
    ys = _grouped_swiglu(xs, meta, w_gu, w_down, tm=tm)
    return _combine(x, wts, dest, ys, mod, final_g, seq_len=seq_len, mod_row_of_tile=mod_row_of_tile,
                    final_norm=final_norm)


def kernel(x, c, ctx, c_ctx, norm1_g, norm2_g, final_g, w_mod, b_mod, w_in, conv_w, sink, pool_w, pool_scale,
           w_branch, w_out, ffn_w_gu, ffn_w_down, router_w, moe_w_gu, moe_w_down):
    bsz, seq, d = x.shape
    lc = ctx.shape[1]
    depth = w_in.shape[0]
    assert bsz + 1 <= 8 and seq % BLOCK == 0 and lc % BLOCK == 0 and seq % GRID_W == 0

    cvec = jnp.zeros((8, d), F32).at[:bsz].set(c).at[bsz].set(c_ctx)
    mods = _modvec(cvec, w_mod, b_mod)
    rope_tabs = _rope_tables(seq)

    lat_row = lambda i, tm: (i * tm) // seq
    ctx_row = lambda i, tm: bsz

    xl = x.reshape(bsz * seq, d)
    xc = ctx.reshape(bsz * lc, d)
    expert_w = None
    for l in range(depth):
        last = l == depth - 1
        mod = mods[l].reshape(8, 1, 6 * d)
        w_in_l = w_in[l].astype(BF16)
        wb_l, wo_l, pw_l = w_branch[l].astype(BF16), w_out[l].astype(BF16), pool_w[l].astype(BF16)
        routed = l % 2 == 1
        mixer = functools.partial(_merge, mod=mod, norm2_g=norm2_g[l], conv_w=conv_w[l], pool_w=pw_l,
                                  pool_scale=pool_scale[l], w_branch=wb_l, w_out=wo_l,
                                  h2_dtype=F32 if routed else BF16)
        if routed:
            wgu, wd = expert_w if expert_w is not None else (moe_w_gu[l // 2].astype(BF16),
                                                             moe_w_down[l // 2].astype(BF16))
        else:
            wgu, wd = ffn_w_gu[l // 2].astype(BF16), ffn_w_down[l // 2].astype(BF16)
        ride = not routed and not last
        ride_gu = moe_w_gu[(l + 1) // 2].reshape(-1, moe_w_gu.shape[-1]) if ride else None
        ride_d = moe_w_down[(l + 1) // 2].reshape(-1, d) if ride else None

        def channel_mix(h2, xm, *, seq_len, row_fn, final_norm):
            if routed:
                return _moe(h2, xm, mod, final_g, router_w[l // 2], wgu, wd, seq_len=seq_len,
                            mod_row_of_tile=row_fn, final_norm=final_norm)
            assert not final_norm
            return _ffn_dense(h2, xm, mod, wgu, wd, seq_len=seq_len, mod_row_of_tile=row_fn)

        if last:
            kvc = _inproj(xc, norm1_g[l], mod, w_in_l, seq_len=lc, mod_row_of_tile=ctx_row, kv_only=True)
        else:
            qc, kvc, mixc, gatec = _inproj(xc, norm1_g[l], mod, w_in_l, seq_len=lc, mod_row_of_tile=ctx_row)
            attn_c = _attention(qc.reshape(bsz, lc, -1), None, kvc.reshape(bsz, lc, -1), sink[l], band=False)
            xc_mid, h2c = mixer(attn_c.reshape(bsz * lc, -1), mixc, gatec, xc, seq_len=lc, mod_row_of_tile=ctx_row)
            xc_next = channel_mix(h2c, xc_mid, seq_len=lc, row_fn=ctx_row, final_norm=False)
        q, kv, mix, gate = _inproj(xl, norm1_g[l], mod, w_in_l, seq_len=seq, mod_row_of_tile=lat_row,
                                   rope_tabs=rope_tabs)
        attn = _attention(q.reshape(bsz, seq, -1), kv.reshape(bsz, seq, -1), kvc.reshape(bsz, lc, -1), sink[l],
                          band=True, cast=ride_gu)
        merged = mixer((attn[0] if ride else attn).reshape(bsz * seq, -1), mix, gate, xl, seq_len=seq,
                       mod_row_of_tile=lat_row, cast=ride_d)
        x_mid, h2 = merged[0], merged[1]
        xl = channel_mix(h2, x_mid, seq_len=seq, row_fn=lat_row, final_norm=last and l % 2 == 1)
        expert_w = (attn[1].reshape(moe_w_gu.shape[1:]), merged[2].reshape(moe_w_down.shape[1:])) if ride else None
        if not last:
            xc = xc_next
    if depth % 2 == 1:
        raise NotImplementedError("final norm is fused into the expert layer; depth must be even")
    return xl.reshape(bsz, seq, d)
```

```python
import functools

import jax
import jax.numpy as jnp
from jax import lax
from jax.experimental import pallas as pl
from jax.experimental.pallas import tpu as pltpu

F32 = jnp.float32
BF16 = jnp.bfloat16

GRID_W = 64
EPS = 1e-6
NEG_INF = -1e30
HEAD_DIM = 64
N_HEADS = 8
N_KV_HEADS = 2
GROUP = N_HEADS // N_KV_HEADS
WINDOW = 128
BLOCK = 128
ROPE_THETA = 10000.0
BRANCH = 512
POOL_SIZES = (2, 4, 8, 16)
POOL_GROUP = 128
Q_END = 512
V_END = 768
MIX_W = 4 * BRANCH
POOL_END = V_END + MIX_W

LANE = 128
SUBLANE = 8
MXU_DIM = 256
BF16_SUBLANE_TILE = 16
HALO = BF16_SUBLANE_TILE
MIB = 1024 * 1024


def _cparams(sem, vmem_mib):
    return pltpu.CompilerParams(dimension_semantics=sem, vmem_limit_bytes=vmem_mib * MIB)


def _pick_chunk(n, cap):
    for unit in (MXU_DIM, LANE):
        fits = [c for c in range(unit, min(n, cap) + 1, unit) if n % c == 0]
        if fits:
            return fits[-1]
    raise ValueError((n, cap))


def _resident(shape):
    nd = len(shape)
    return pl.BlockSpec(shape, lambda *_: (0,) * nd, pipeline_mode=pl.Buffered(1))


def _norm_mod(x, g, shift, scale):
    y = x * lax.rsqrt(jnp.mean(x * x, axis=-1, keepdims=True) + EPS) * g
    return y * (1.0 + scale) + shift


def _modvec_kernel(c_ref, w_ref, b_ref, o_ref):
    c = c_ref[...]
    s = c * jax.nn.sigmoid(c)
    o_ref[...] = jnp.dot(s, w_ref[...], preferred_element_type=F32) + b_ref[...]


def _modvec(cvec, w_mod, b_mod):
    depth, d, n = w_mod.shape
    nc = _pick_chunk(n, 1536)
    return pl.pallas_call(
        _modvec_kernel,
        grid=(depth, n // nc),
        in_specs=[
            pl.BlockSpec((8, d), lambda l, j: (0, 0)),
            pl.BlockSpec((None, d, nc), lambda l, j: (l, 0, j)),
            pl.BlockSpec((None, 1, nc), lambda l, j: (l, 0, j)),
        ],
        out_specs=pl.BlockSpec((None, 8, nc), lambda l, j: (l, 0, j)),
        out_shape=jax.ShapeDtypeStruct((depth, 8, n), F32),
        compiler_params=_cparams(("arbitrary", "arbitrary"), 32),
        name="modvec",
    )(cvec, w_mod, b_mod.reshape(depth, 1, n))


def _rope_tables(seq_len):
    n_freq = HEAD_DIM // 4
    inv = ROPE_THETA ** (-jnp.arange(n_freq, dtype=F32) / n_freq)
    pos = jnp.arange(seq_len)
    row = (pos // GRID_W).astype(F32)[:, None] * inv[None, :]
    col = (pos % GRID_W).astype(F32)[:, None] * inv[None, :]
    zero = jnp.zeros_like(row)
    cos = jnp.concatenate([jnp.cos(row)] * 2 + [jnp.cos(col)] * 2, axis=-1)
    s_lo = jnp.concatenate([-jnp.sin(row), zero, -jnp.sin(col), zero], axis=-1)
    s_hi = jnp.concatenate([zero, jnp.sin(row), zero, jnp.sin(col)], axis=-1)
    return tuple(jnp.tile(t, (1, LANE // HEAD_DIM)) for t in (cos, s_lo, s_hi))


def _inproj_kernel(*refs, rope, kv_only, d):
    x_ref, g_ref, mod_ref, w_ref = refs[:4]
    refs = refs[4:]
    if rope:
        cos_ref, slo_ref, shi_ref = refs[:3]
        refs = refs[3:]

        def rot(z):
            return (z * cos_ref[...] + pltpu.roll(z, LANE - 16, 1) * slo_ref[...]
                    + pltpu.roll(z, 16, 1) * shi_ref[...])
    else:
        def rot(z):
            return z

    mod = mod_ref[...]
    h = _norm_mod(x_ref[...], g_ref[...], mod[:, 0:d], mod[:, d:2 * d]).astype(BF16)

    def proj(c0, c1):
        return jnp.dot(h, w_ref[:, c0:c1], preferred_element_type=F32)

    if kv_only:
        (kv_ref,) = refs
        z = proj(0, 2 * LANE)
        kv_ref[:, 0:LANE] = rot(z[:, 0:LANE]).astype(BF16)
        kv_ref[:, LANE:] = z[:, LANE:].astype(BF16)
        return

    q_ref, kv_ref, mix_ref, gate_ref = refs
    cw = 512
    n_gate = gate_ref.shape[1]
    for c in range(n_gate // cw):
        zg = proj(POOL_END + c * cw, POOL_END + (c + 1) * cw)
        gate_ref[:, c * cw:(c + 1) * cw] = jax.nn.sigmoid(zg).astype(BF16)
    z = proj(0, Q_END)
    for j in range(Q_END // LANE):
        q_ref[:, j * LANE:(j + 1) * LANE] = (rot(z[:, j * LANE:(j + 1) * LANE]) * HEAD_DIM ** -0.5).astype(BF16)
    z = proj(Q_END, V_END)
    kv_ref[:, 0:LANE] = rot(z[:, 0:LANE]).astype(BF16)
    kv_ref[:, LANE:] = z[:, LANE:].astype(BF16)
    for c in range(MIX_W // cw):
        mix_ref[:, c * cw:(c + 1) * cw] = proj(V_END + c * cw, V_END + (c + 1) * cw).astype(BF16)


def _inproj(x, norm_g, mod, w_in, *, seq_len, mod_row_of_tile, rope_tabs=None, kv_only=False):
    m, d = x.shape
    tm = min(1024, seq_len)
    assert m % tm == 0 and seq_len % tm == 0
    tiles_per_seq = seq_len // tm
    rope = rope_tabs is not None
    in_w = w_in.shape[1]
    in_specs = [
        pl.BlockSpec((tm, d), lambda i: (i, 0)),
        pl.BlockSpec((1, d), lambda i: (0, 0)),
        pl.BlockSpec((None, 1, 6 * d), lambda i: (mod_row_of_tile(i, tm), 0, 0)),
        pl.BlockSpec((d, 2 * LANE), lambda i: (0, Q_END // (2 * LANE))) if kv_only else _resident((d, in_w)),
    ]
    args = [x, norm_g.reshape(1, d), mod, w_in]
    if rope:
        in_specs += [pl.BlockSpec((tm, LANE), lambda i: (i % tiles_per_seq, 0))] * 3
        args += list(rope_tabs)
    if kv_only:
        out_specs = pl.BlockSpec((tm, 2 * LANE), lambda i: (i, 0))
        out_shape = jax.ShapeDtypeStruct((m, 2 * LANE), BF16)
    else:
        widths = (Q_END, 2 * LANE, MIX_W, in_w - POOL_END)
        out_specs = [pl.BlockSpec((tm, w), lambda i: (i, 0)) for w in widths]
        out_shape = [jax.ShapeDtypeStruct((m, w), BF16) for w in widths]
    return pl.pallas_call(
        functools.partial(_inproj_kernel, rope=rope, kv_only=kv_only, d=d),
        grid=(m // tm,),
        in_specs=in_specs,
        out_specs=out_specs,
        out_shape=out_shape,
        compiler_params=_cparams(("parallel",), 56),
        name="inproj_kv" if kv_only else "inproj",
    )(*args)


ATTN_STRIP = 32
ATTN_QBLOCKS = 2
ATTN_AHEAD = 2


def _attn_kernel(sink_ref, q_ref, *refs, band, carry_cast):
    refs, (s_ref, p_ref) = list(refs[:-2]), refs[-2:]
    qblocks = q_ref.shape[1] // BLOCK
    if carry_cast:
        cast_out = refs.pop()
        cast_in = refs.pop(-2)
        cast_out[...] = cast_in[...].astype(BF16)
    if band:
        kvp_ref, kvm_ref, kvn_ref, kvc_ref, bias_ref, o_ref = refs
    else:
        kvc_ref, o_ref = refs
    kvc = kvc_ref[0]
    lc = kvc.shape[0]
    nloc = 3 * BLOCK if band else 0
    nt = (((1,), (1,)), ((), ()))
    w_ctx = jnp.concatenate([kvc[:, LANE:], jnp.ones((lc, LANE), BF16)], axis=1)
    if band:
        n = pl.program_id(1)
        kv4 = jnp.concatenate([kvp_ref[0], kvm_ref[0], kvn_ref[0]], axis=0)
        w4 = jnp.concatenate([kv4[:, LANE:], jnp.ones((kv4.shape[0], LANE), BF16)], axis=1)
        col = lax.broadcasted_iota(jnp.int32, (1, nloc), 1)
        head_edge = jnp.where((col < BLOCK) & (n == 0), NEG_INF, 0.0)
        tail_edge = jnp.where((col >= 2 * BLOCK) & (n == pl.num_programs(1) - 1), NEG_INF, 0.0)
        biases = ([bias_ref[...] + head_edge] + [bias_ref[...]] * (qblocks - 2)
                  + [bias_ref[...] + tail_edge])
    units = [(sb, h) for sb in range(qblocks) for h in range(N_HEADS)]
    def scores(u):
        sb, h = units[u]
        qh = q_ref[0, sb * BLOCK:(sb + 1) * BLOCK, h * HEAD_DIM:(h + 1) * HEAD_DIM]
        ks = slice(h // GROUP * HEAD_DIM, (h // GROUP + 1) * HEAD_DIM)
        if band:
            kl = kv4[sb * BLOCK:sb * BLOCK + nloc, ks]
            s_ref[u, :, 0:nloc] = lax.dot_general(qh, kl, nt, preferred_element_type=F32) + biases[sb]
        s_ref[u, :, nloc:] = lax.dot_general(qh, kvc[:, ks], nt, preferred_element_type=F32)

    def probs(u):
        sink = sink_ref[units[u][1]]
        esink = []
        for r in range(0, BLOCK, ATTN_STRIP):
            s = s_ref[u, r:r + ATTN_STRIP, :]
            m = jnp.maximum(jnp.max(s, axis=-1, keepdims=True), sink)
            p_ref[u, r:r + ATTN_STRIP, :] = jnp.exp(s - m).astype(BF16)
            esink.append(jnp.exp(sink - m))
        return jnp.concatenate(esink, axis=0)

    def weighted_values(u, esink):
        sb, h = units[u]
        ks = slice(h // GROUP * HEAD_DIM, (h // GROUP + 1) * HEAD_DIM)
        o2 = jnp.dot(p_ref[u, :, nloc:], w_ctx, preferred_element_type=F32)
        if band:
            o2 = o2 + jnp.dot(p_ref[u, :, 0:nloc], w4[sb * BLOCK:sb * BLOCK + nloc], preferred_element_type=F32)
        return o2[:, ks] / (o2[:, LANE:LANE + HEAD_DIM] + esink)

    outs, esinks = [], []
    for u in range(min(ATTN_AHEAD, len(units))):
        scores(u)
    for u in range(len(units)):
        if u + ATTN_AHEAD < len(units):
            scores(u + ATTN_AHEAD)
        esinks.append(probs(u))
        if u >= 1:
            outs.append(weighted_values(u - 1, esinks[u - 1]))
    outs.append(weighted_values(len(units) - 1, esinks[-1]))
    for sb in range(qblocks):
        o_ref[0, sb * BLOCK:(sb + 1) * BLOCK, :] = jnp.concatenate(
            outs[sb * N_HEADS:(sb + 1) * N_HEADS], axis=1).astype(BF16)


def _cast_rider(w, n_steps, index_map):
    rows, cols = w.shape
    assert rows % (n_steps * BF16_SUBLANE_TILE) == 0, (w.shape, n_steps)
    return pl.BlockSpec((rows // n_steps, cols), index_map), jax.ShapeDtypeStruct((rows, cols), BF16)


def _attention(q, kv, kvc, sink, *, band, cast=None):
    b, l, _ = q.shape
    lc = kvc.shape[1]
    nb = l // BLOCK
    qb = min(ATTN_QBLOCKS, nb)
    tq = qb * BLOCK
    assert l % tq == 0 and qb >= 2
    in_specs = [
        pl.BlockSpec(memory_space=pltpu.SMEM),
        pl.BlockSpec((1, tq, N_HEADS * HEAD_DIM), lambda bi, n: (bi, n, 0)),
    ]
    args = [sink, q]
    if band:
        in_specs += [
            pl.BlockSpec((1, BLOCK, 2 * LANE), lambda bi, n: (bi, jnp.maximum(qb * n - 1, 0), 0)),
            pl.BlockSpec((1, tq, 2 * LANE), lambda bi, n: (bi, n, 0)),
            pl.BlockSpec((1, BLOCK, 2 * LANE), lambda bi, n: (bi, jnp.minimum(qb * (n + 1), nb - 1), 0)),
        ]
        args += [kv, kv, kv]
    in_specs.append(pl.BlockSpec((1, lc, 2 * LANE), lambda bi, n: (bi, 0, 0)))
    args.append(kvc)
    nkeys = lc
    if band:
        rel = jnp.arange(3 * BLOCK)[None, :] - BLOCK - jnp.arange(BLOCK)[:, None]
        in_specs.append(pl.BlockSpec((BLOCK, 3 * BLOCK), lambda bi, n: (0, 0)))
        args.append(jnp.where(jnp.abs(rel) <= WINDOW, 0.0, NEG_INF).astype(F32))
        nkeys += 3 * BLOCK
    nq = l // tq
    out_specs = [pl.BlockSpec((1, tq, N_HEADS * HEAD_DIM), lambda bi, n: (bi, n, 0))]
    out_shape = [jax.ShapeDtypeStruct((b, l, N_HEADS * HEAD_DIM), BF16)]
    if cast is not None:
        spec, shape = _cast_rider(cast, b * nq, lambda bi, n: (bi * nq + n, 0))
        in_specs.append(spec)
        args.append(cast)
        out_specs.append(spec)
        out_shape.append(shape)
    outs = pl.pallas_call(
        functools.partial(_attn_kernel, band=band, carry_cast=cast is not None),
        grid=(b, nq),
        in_specs=in_specs,
        out_specs=out_specs,
        out_shape=out_shape,
        scratch_shapes=[pltpu.VMEM((qb * N_HEADS, BLOCK, nkeys), F32),
                        pltpu.VMEM((qb * N_HEADS, BLOCK, nkeys), BF16)],
        compiler_params=_cparams(("parallel", "parallel"), 56),
        name="attn_band" if band else "attn_ctx",
    )(*args)
    return outs[0] if cast is None else outs


def _merge_kernel(attn_ref, mix_ref, prev_ref, next_ref, gate_ref, x_ref, mod_ref, n2g_ref, convw_ref,
                  poolw_ref, pscale_ref, wb_ref, wo_ref, *refs, tm, seq_len, d):
    if len(refs) == 4:
        cast_in, xo_ref, h2_ref, cast_out = refs
        cast_out[...] = cast_in[...].astype(BF16)
    else:
        xo_ref, h2_ref = refs
    tile = pl.program_id(0) % (seq_len // tm)
    keep_prev = (tile != 0).astype(F32)
    keep_next = (tile != seq_len // tm - 1).astype(F32)
    mixm = mix_ref[...]
    prev = prev_ref[...].astype(F32) * keep_prev
    nxt = next_ref[...].astype(F32) * keep_next
    b = BRANCH
    cx, cb, cc = (mixm[:, j * b:(j + 1) * b].astype(F32) for j in range(3))

    p = cc * cx
    p_prev = prev[HALO - 1:HALO, 2 * b:3 * b] * prev[HALO - 1:HALO, 0:b]
    p_next = nxt[0:1, 2 * b:3 * b] * nxt[0:1, 0:b]
    ridx = lax.broadcasted_iota(jnp.int32, (tm, b), 0)
    p_dn = jnp.where(ridx == 0, p_prev, pltpu.roll(p, 1, 0))
    p_up = jnp.where(ridx == tm - 1, p_next, pltpu.roll(p, tm - 1, 0))
    cw = convw_ref[...]
    conv_out = (cb * (p_dn * cw[0:1] + p * cw[1:2] + p_up * cw[2:3])).astype(BF16)

    u_main = mixm[:, 3 * b:4 * b].astype(F32)
    u_ext = jnp.concatenate([prev[:, 3 * b:4 * b], u_main, nxt[:, 3 * b:4 * b]], axis=0)
    ext = tm + 2 * HALO

    def shift(a, s):
        return pltpu.roll(a, s % ext, 0)

    tpos = tile * tm + lax.broadcasted_iota(jnp.int32, (tm, 1), 0)
    pooled = []
    for gi, w in enumerate(POOL_SIZES):
        gs = slice(gi * POOL_GROUP, (gi + 1) * POOL_GROUP)
        ug = u_ext[:, gs]
        a = ug + shift(ug, 1)
        ww = 2
        while ww < w:
            a = shift(a, ww // 2) + shift(a, -(ww // 2))
            ww *= 2
        cnt = jnp.minimum(tpos + w // 2, seq_len) - jnp.maximum(tpos - w // 2, 0)
        dlt = a[HALO:HALO + tm] / cnt.astype(F32) - u_main[:, gs]
        pooled.append(jnp.dot(dlt.astype(BF16), poolw_ref[gi], preferred_element_type=F32))
    pool_out = (jnp.concatenate(pooled, axis=1) * pscale_ref[...]).astype(BF16)

    y = (gate_ref[:, 0:d].astype(F32) * jnp.dot(attn_ref[...], wb_ref[0], preferred_element_type=F32)
         + gate_ref[:, d:2 * d].astype(F32) * jnp.dot(conv_out, wb_ref[1], preferred_element_type=F32)
         + gate_ref[:, 2 * d:3 * d].astype(F32) * jnp.dot(pool_out, wb_ref[2], preferred_element_type=F32))
    o = jnp.dot(y.astype(BF16), wo_ref[...], preferred_element_type=F32)
    mod = mod_ref[...]
    xn = x_ref[...] + mod[:, 2 * d:3 * d] * o
    xo_ref[...] = xn
    h2_ref[...] = _norm_mod(xn, n2g_ref[...], mod[:, 3 * d:4 * d], mod[:, 4 * d:5 * d]).astype(h2_ref.dtype)


def _merge(attn, mix, gate, x, mod, norm2_g, conv_w, pool_w, pool_scale, w_branch, w_out, *, seq_len,
           mod_row_of_tile, h2_dtype, cast=None):
    m, d = x.shape
    tm = min(512, seq_len)
    assert m % tm == 0 and seq_len % tm == 0 and tm % HALO == 0
    hb = tm // HALO
    n_halo = m // HALO
    row = lambda w: pl.BlockSpec((tm, w), lambda i: (i, 0))
    in_specs = [
        row(BRANCH),
        row(MIX_W),
        pl.BlockSpec((HALO, MIX_W), lambda i: (jnp.maximum(i * hb - 1, 0), 0)),
        pl.BlockSpec((HALO, MIX_W), lambda i: (jnp.minimum((i + 1) * hb, n_halo - 1), 0)),
        row(3 * d),
        row(d),
        pl.BlockSpec((None, 1, 6 * d), lambda i: (mod_row_of_tile(i, tm), 0, 0)),
        pl.BlockSpec((1, d), lambda i: (0, 0)),
        _resident(conv_w.shape),
        _resident(pool_w.shape),
        pl.BlockSpec((1, BRANCH), lambda i: (0, 0)),
        _resident(w_branch.shape),
        _resident(w_out.shape),
    ]
    args = [attn, mix, mix, mix, gate, x, mod, norm2_g.reshape(1, d), conv_w, pool_w,
            pool_scale.reshape(1, BRANCH), w_branch, w_out]
    out_specs = [row(d), row(d)]
    out_shape = [jax.ShapeDtypeStruct((m, d), F32), jax.ShapeDtypeStruct((m, d), h2_dtype)]
    if cast is not None:
        spec, shape = _cast_rider(cast, m // tm, lambda i: (i, 0))
        in_specs.append(spec)
        args.append(cast)
        out_specs.append(spec)
        out_shape.append(shape)
    return pl.pallas_call(
        functools.partial(_merge_kernel, tm=tm, seq_len=seq_len, d=d),
        grid=(m // tm,),
        in_specs=in_specs,
        out_specs=out_specs,
        out_shape=out_shape,
        compiler_params=_cparams(("parallel",), 48),
        name="merge",
    )(*args)


def _ffn_kernel(h_ref, x_ref, mod_ref, wgu_ref, wd_ref, o_ref, *, d, d_ff, fc):
    h = h_ref[...]
    acc = None
    for f in range(0, d_ff, fc):
        g = jnp.dot(h, wgu_ref[:, f:f + fc], preferred_element_type=F32)
        u = jnp.dot(h, wgu_ref[:, d_ff + f:d_ff + f + fc], preferred_element_type=F32)
        a = (g * jax.nn.sigmoid(g) * u).astype(BF16)
        y = jnp.dot(a, wd_ref[f:f + fc, :], preferred_element_type=F32)
        acc = y if acc is None else acc + y
    o_ref[...] = x_ref[...] + mod_ref[...][:, 5 * d:6 * d] * acc


def _ffn_dense(h2, x, mod, w_gu, w_down, *, seq_len, mod_row_of_tile):
    m, d = x.shape
    d_ff = w_down.shape[0]
    tm = min(512, seq_len)
    assert m % tm == 0 and seq_len % tm == 0
    fc = _pick_chunk(d_ff, 3072)
    return pl.pallas_call(
        functools.partial(_ffn_kernel, d=d, d_ff=d_ff, fc=fc),
        grid=(m // tm,),
        in_specs=[
            pl.BlockSpec((tm, d), lambda i: (i, 0)),
            pl.BlockSpec((tm, d), lambda i: (i, 0)),
            pl.BlockSpec((None, 1, 6 * d), lambda i: (mod_row_of_tile(i, tm), 0, 0)),
            _resident(w_gu.shape),
            _resident(w_down.shape),
        ],
        out_specs=pl.BlockSpec((tm, d), lambda i: (i, 0)),
        out_shape=jax.ShapeDtypeStruct((m, d), F32),
        compiler_params=_cparams(("parallel",), 56),
        name="ffn_dense",
    )(h2, x, mod, w_gu, w_down)


ROUTE_ROWS = 8
META_ROWS = 4
META_TILES, META_PAD_START, META_PAD_LEN = LANE, 2 * LANE, 3 * LANE


def _router_kernel(h_ref, rw_ref, tri_ref, wts_ref, pairs_ref, cnt_ref, carry_ref, *, n_experts):
    @pl.when(pl.program_id(0) == 0)
    def _():
        carry_ref[...] = jnp.zeros_like(carry_ref)

    h, w = h_ref[...], rw_ref[...]
    h_hi, w_hi = h.astype(BF16), w.astype(BF16)
    h_lo, w_lo = (h - h_hi.astype(F32)).astype(BF16), (w - w_hi.astype(F32)).astype(BF16)
    logits = (jnp.dot(h_hi, w_hi, preferred_element_type=F32) + jnp.dot(h_lo, w_hi, preferred_element_type=F32)
              + jnp.dot(h_hi, w_lo, preferred_element_type=F32))
    lane = lax.broadcasted_iota(jnp.int32, logits.shape, 1).astype(F32)
    logits = jnp.where(lane < n_experts, logits, -jnp.inf)
    m1 = jnp.max(logits, axis=-1, keepdims=True)
    i1 = jnp.min(jnp.where(logits == m1, lane, float(LANE)), axis=-1, keepdims=True)
    rest = jnp.where(lane == i1, -jnp.inf, logits)
    m2 = jnp.max(rest, axis=-1, keepdims=True)
    i2 = jnp.min(jnp.where(rest == m2, lane, float(LANE)), axis=-1, keepdims=True)
    e2 = jnp.exp(m2 - m1)
    den = 1.0 + e2
    wts_ref[...] = jnp.where(lane == 0, 1.0 / den, jnp.where(lane == 1, e2 / den, 0.0))

    m1t = (lane == i1).astype(F32).T[0:ROUTE_ROWS]
    m2t = (lane == i2).astype(F32).T[0:ROUTE_ROWS]
    mem = m1t + m2t
    before = jnp.dot(mem.astype(BF16), tri_ref[...], preferred_element_type=F32) + carry_ref[:, 0:1]
    eid = lax.broadcasted_iota(jnp.int32, mem.shape, 0).astype(F32)
    rows = [jnp.sum(m1t * eid, axis=0, keepdims=True), jnp.sum(m2t * eid, axis=0, keepdims=True),
            jnp.sum(m1t * before, axis=0, keepdims=True), jnp.sum(m2t * before, axis=0, keepdims=True)]
    rows += [jnp.zeros_like(rows[0])] * (ROUTE_ROWS - len(rows))
    pairs_ref[...] = jnp.concatenate(rows, axis=0)
    carry_ref[...] = carry_ref[...] + jnp.sum(mem, axis=1, keepdims=True)
    cnt_ref[...] = carry_ref[...]


def _router(h2, router_w):
    m, d = h2.shape
    n_experts = router_w.shape[1]
    assert n_experts <= ROUTE_ROWS
    rw = jnp.pad(router_w, ((0, 0), (0, LANE - n_experts)))
    tm = min(512, m)
    tri = jnp.triu(jnp.ones((tm, tm), BF16), k=1)
    return pl.pallas_call(
        functools.partial(_router_kernel, n_experts=n_experts),
        grid=(m // tm,),
        in_specs=[
            pl.BlockSpec((tm, d), lambda i: (i, 0)),
            pl.BlockSpec((d, LANE), lambda i: (0, 0)),
            pl.BlockSpec((tm, tm), lambda i: (0, 0)),
        ],
        out_specs=[
            pl.BlockSpec((tm, LANE), lambda i: (i, 0)),
            pl.BlockSpec((ROUTE_ROWS, tm), lambda i: (0, i)),
            pl.BlockSpec((ROUTE_ROWS, LANE), lambda i: (0, 0)),
        ],
        out_shape=[jax.ShapeDtypeStruct((m, LANE), F32), jax.ShapeDtypeStruct((ROUTE_ROWS, m), F32),
                   jax.ShapeDtypeStruct((ROUTE_ROWS, LANE), F32)],
        scratch_shapes=[pltpu.VMEM((ROUTE_ROWS, LANE), F32)],
        compiler_params=_cparams(("arbitrary",), 32),
        name="router",
    )(h2, rw, tri)


def _plan_kernel(pairs_ref, cnt_ref, dest_ref, meta_ref, *, tile_rows):
    cnt = cnt_ref[...]
    padded = jnp.floor((cnt + (tile_rows - 1)) * (1.0 / tile_rows)) * tile_rows
    sub = lax.broadcasted_iota(jnp.int32, cnt.shape, 0)
    lane = lax.broadcasted_iota(jnp.int32, cnt.shape, 1)
    end_row = jnp.sum(jnp.where(sub <= lane, padded, 0.0), axis=0, keepdims=True)
    end_col = jnp.sum(jnp.where(lane == sub, end_row, 0.0), axis=1, keepdims=True)
    start_col = end_col - padded[:, 0:1]
    p = pairs_ref[...]
    eid = lax.broadcasted_iota(jnp.int32, p.shape, 0).astype(F32)
    d1 = jnp.sum(jnp.where(eid == p[0:1], start_col, 0.0), axis=0, keepdims=True) + p[2:3]
    d2 = jnp.sum(jnp.where(eid == p[1:2], start_col, 0.0), axis=0, keepdims=True) + p[3:4]
    dest_ref[...] = jnp.concatenate([d1, d2] + [jnp.zeros_like(d1)] * (ROUTE_ROWS - 2), axis=0).astype(jnp.int32)
    tile_start = (lane * tile_rows).astype(F32)
    tile_e = jnp.sum((end_col <= tile_start).astype(F32), axis=0, keepdims=True)
    n_tiles = jnp.max(end_col, axis=0, keepdims=True) * (1.0 / tile_rows)
    pad_start = jnp.sum(jnp.where(lane == sub, start_col + cnt, 0.0), axis=0, keepdims=True)
    pad_len = jnp.sum(jnp.where(lane == sub, padded - cnt, 0.0), axis=0, keepdims=True)
    meta = jnp.concatenate([tile_e, jnp.broadcast_to(n_tiles, tile_e.shape), pad_start, pad_len]
                           + [jnp.zeros_like(tile_e)] * (ROUTE_ROWS - META_ROWS), axis=0)
    meta_ref[...] = meta.astype(jnp.int32)


def _plan(pairs, counts, *, tile_rows):
    m = pairs.shape[1]
    tp = min(2048, m)
    return pl.pallas_call(
        functools.partial(_plan_kernel, tile_rows=tile_rows),
        grid=(m // tp,),
        in_specs=[pl.BlockSpec((ROUTE_ROWS, tp), lambda i: (0, i)), pl.BlockSpec((ROUTE_ROWS, LANE), lambda i: (0, 0))],
        out_specs=[pl.BlockSpec((ROUTE_ROWS, tp), lambda i: (0, i)), pl.BlockSpec((ROUTE_ROWS, LANE), lambda i: (0, 0))],
        out_shape=[jax.ShapeDtypeStruct((ROUTE_ROWS, m), jnp.int32), jax.ShapeDtypeStruct((ROUTE_ROWS, LANE), jnp.int32)],
        compiler_params=_cparams(("arbitrary",), 32),
        name="plan",
    )(pairs, counts)


ROW_DMA_UNROLL = 8


def _row_copy(src, src_row, dst, dst_row, sem):
    return pltpu.make_async_copy(src.at[pl.ds(src_row, 1), :], dst.at[pl.ds(dst_row, 1), :], sem)


def _dispatch_kernel(dest_ref, meta_ref, h_ref, xs_ref, zero_ref, sem, zsem, *, tm, n_tok, n_experts, n_tiles_max):
    tile_rows = zero_ref.shape[0]
    i = pl.program_id(0)
    last = pl.num_programs(0) - 1
    base = i * tm

    def wait_rows():
        for _ in range(2):
            pltpu.make_async_copy(h_ref, xs_ref.at[pl.ds(0, tm), :], sem).wait()

    def issue(blk, carry):
        for j in range(ROW_DMA_UNROLL):
            r = blk * ROW_DMA_UNROLL + j
            _row_copy(h_ref, r, xs_ref, dest_ref[base + r], sem).start(priority=0)
            _row_copy(h_ref, r, xs_ref, dest_ref[n_tok + base + r], sem).start(priority=1)
        return carry

    lax.fori_loop(0, tm // ROW_DMA_UNROLL, issue, 0)

    @pl.when(i < last)
    def _():
        wait_rows()

    @pl.when(i == last)
    def _():
        zero_ref[...] = jnp.zeros_like(zero_ref)

        def fill_padding(e, start):
            first = meta_ref[META_PAD_START + e]
            n_pad = meta_ref[META_PAD_LEN + e]
            n_single = jnp.minimum((-first) & (SUBLANE - 1), n_pad)

            def single(r, carry):
                cp = _row_copy(zero_ref, 0, xs_ref, first + r, zsem)
                cp.start() if start else cp.wait()
                return carry

            def piece(c, carry):
                off = pl.multiple_of(first + n_single + c * SUBLANE, SUBLANE)
                cp = pltpu.make_async_copy(zero_ref.at[pl.ds(0, SUBLANE), :], xs_ref.at[pl.ds(off, SUBLANE), :], zsem)
                cp.start() if start else cp.wait()
                return carry

            lax.fori_loop(0, n_single, single, 0)
            lax.fori_loop(0, (n_pad - n_single) // SUBLANE, piece, 0)

        def tail_copy(k):
            return pltpu.make_async_copy(zero_ref, xs_ref.at[pl.ds(k * tile_rows, tile_rows), :], zsem)

        def zero_tile(k, carry):
            tail_copy(k).start()
            return carry

        def wait_tile(k, carry):
            tail_copy(k).wait()
            return carry

        for e in range(n_experts):
            fill_padding(e, start=True)
        lax.fori_loop(meta_ref[META_TILES], n_tiles_max, zero_tile, 0)
        for e in range(n_experts):
            fill_padding(e, start=False)
        lax.fori_loop(meta_ref[META_TILES], n_tiles_max, wait_tile, 0)
        wait_rows()


def _dispatch(h2, dest, meta, n_rows, *, tm, n_experts):
    m, d = h2.shape
    tt = min(512, m)
    return pl.pallas_call(
        functools.partial(_dispatch_kernel, tm=tt, n_tok=m, n_experts=n_experts, n_tiles_max=n_rows // tm),
        grid_spec=pltpu.PrefetchScalarGridSpec(
            num_scalar_prefetch=2,
            grid=(m // tt,),
            in_specs=[pl.BlockSpec((tt, d), lambda i, dr, mt: (i, 0))],
            out_specs=pl.BlockSpec(memory_space=pl.ANY),
            scratch_shapes=[pltpu.VMEM((tm, d), F32), pltpu.SemaphoreType.DMA, pltpu.SemaphoreType.DMA],
        ),
        out_shape=jax.ShapeDtypeStruct((n_rows, d), F32),
        compiler_params=_cparams(("arbitrary",), 32),
        name="dispatch",
    )(dest, meta, h2)


def _gmm_kernel(meta_ref, xs_ref, wg_ref, wu_ref, wd_ref, o_ref, acc_ref):
    f = pl.program_id(1)
    nf = pl.num_programs(1)

    @pl.when((pl.program_id(0) == 0) & (f == 0))
    def _():
        acc_ref[...] = jnp.zeros_like(acc_ref)

    @pl.when(pl.program_id(0) < meta_ref[LANE])
    def _():
        xb = xs_ref[...].astype(BF16)
        g = jnp.dot(xb, wg_ref[...], preferred_element_type=F32)
        u = jnp.dot(xb, wu_ref[...], preferred_element_type=F32)
        a = (g * jax.nn.sigmoid(g) * u).astype(BF16)
        y = jnp.dot(a, wd_ref[...], preferred_element_type=F32)
        total = y + jnp.where(f > 0, acc_ref[...], 0.0)
        acc_ref[...] = total
        o_ref[...] = total

    @pl.when((pl.program_id(0) >= meta_ref[LANE]) & (f == nf - 1))
    def _():
        o_ref[...] = jnp.zeros_like(o_ref)


def _grouped_swiglu(xs, meta, w_gu, w_down, *, tm):
    n_rows, d = xs.shape
    n_experts, d_ff, _ = w_down.shape
    fc = _pick_chunk(d_ff, d_ff // 2)
    nf = d_ff // fc
    assert nf >= 2 and n_rows % tm == 0 and n_rows // tm <= LANE

    def tile(i, mt):
        return jnp.maximum(jnp.minimum(i, mt[LANE] - 1), 0)

    def expert(i, mt):
        return jnp.minimum(mt[tile(i, mt)], n_experts - 1)

    def chunk(i, f, mt):
        return jnp.where(i < mt[LANE], f, nf - 1)

    return pl.pallas_call(
        _gmm_kernel,
        grid_spec=pltpu.PrefetchScalarGridSpec(
            num_scalar_prefetch=1,
            grid=(n_rows // tm, nf),
            in_specs=[
                pl.BlockSpec((tm, d), lambda i, f, mt: (tile(i, mt), 0)),
                pl.BlockSpec((None, d, fc), lambda i, f, mt: (expert(i, mt), 0, chunk(i, f, mt))),
                pl.BlockSpec((None, d, fc), lambda i, f, mt: (expert(i, mt), 0, nf + chunk(i, f, mt))),
                pl.BlockSpec((None, fc, d), lambda i, f, mt: (expert(i, mt), chunk(i, f, mt), 0)),
            ],
            out_specs=pl.BlockSpec((tm, d), lambda i, f, mt: (i, 0)),
            scratch_shapes=[pltpu.VMEM((tm, d), F32)],
        ),
        out_shape=jax.ShapeDtypeStruct((n_rows, d), F32),
        compiler_params=_cparams(("arbitrary", "arbitrary"), 56),
        name="grouped_swiglu",
    )(meta, xs, w_gu, w_gu, w_down)


def _combine_kernel(dest_ref, x_ref, w_ref, mod_ref, fg_ref, ys_ref, o_ref, y_ref, sems, *, tm, n_tok, d,
                    final_norm):
    i = pl.program_id(0)
    slot = i % 2

    def gather(tile, to_slot):
        def issue(blk, carry):
            for j in range(ROW_DMA_UNROLL):
                r = blk * ROW_DMA_UNROLL + j
                for c in range(2):
                    _row_copy(ys_ref, dest_ref[c * n_tok + tile * tm + r], y_ref.at[to_slot, c], r,
                              sems.at[to_slot]).start(priority=c)
            return carry

        lax.fori_loop(0, tm // ROW_DMA_UNROLL, issue, 0)

    @pl.when(i == 0)
    def _():
        gather(0, 0)

    @pl.when(i + 1 < pl.num_programs(0))
    def _():
        gather(i + 1, 1 - slot)

    for c in range(2):
        pltpu.make_async_copy(ys_ref.at[pl.ds(0, tm), :], y_ref.at[slot, c], sems.at[slot]).wait()
    w = w_ref[...]
    moe = w[:, 0:1] * y_ref[slot, 0] + w[:, 1:2] * y_ref[slot, 1]
    xn = x_ref[...] + mod_ref[...][:, 5 * d:6 * d] * moe
    if final_norm:
        xn = xn * lax.rsqrt(jnp.mean(xn * xn, axis=-1, keepdims=True) + EPS) * fg_ref[...]
    o_ref[...] = xn


def _combine(x, wts, dest, ys, mod, final_g, *, seq_len, mod_row_of_tile, final_norm):
    m, d = x.shape
    tm = min(512, seq_len)
    return pl.pallas_call(
        functools.partial(_combine_kernel, tm=tm, n_tok=m, d=d, final_norm=final_norm),
        grid_spec=pltpu.PrefetchScalarGridSpec(
            num_scalar_prefetch=1,
            grid=(m // tm,),
            in_specs=[
                pl.BlockSpec((tm, d), lambda i, dr: (i, 0)),
                pl.BlockSpec((tm, LANE), lambda i, dr: (i, 0)),
                pl.BlockSpec((None, 1, 6 * d), lambda i, dr: (mod_row_of_tile(i, tm), 0, 0)),
                pl.BlockSpec((1, d), lambda i, dr: (0, 0)),
                pl.BlockSpec(memory_space=pl.ANY),
            ],
            out_specs=pl.BlockSpec((tm, d), lambda i, dr: (i, 0)),
            scratch_shapes=[pltpu.VMEM((2, 2, tm, d), F32), pltpu.SemaphoreType.DMA((2,))],
        ),
        out_shape=jax.ShapeDtypeStruct((m, d), F32),
        compiler_params=_cparams(("arbitrary",), 40),
        name="combine",
    )(dest, x, wts, mod, final_g.reshape(1, d), ys)


def _moe(h2, x, mod, final_g, router_w, w_gu, w_down, *, seq_len, mod_row_of_tile, final_norm):
    m, d = x.shape
    n_experts = router_w.shape[1]
    tm = 512
    n_rows = 2 * m + n_experts * tm
    wts, pairs, counts = _router(h2, router_w)
    dest2d, meta2d = _plan(pairs, counts, tile_rows=tm)
    dest = dest2d[0:2].reshape(2 * m)
    meta = meta2d[0:META_ROWS].reshape(META_ROWS * LANE)
    xs = _dispatch(h2, dest, meta, n_rows, tm=tm, n_experts=n_experts)
    ys = _grouped_swiglu(xs, meta, w_gu, w_down, tm=tm)
    return _combine(x, wts, dest, ys, mod, final_g, seq_len=seq_len, mod_row_of_tile=mod_row_of_tile,
                    final_norm=final_norm)


def kernel(x, c, ctx, c_ctx, norm1_g, norm2_g, final_g, w_mod, b_mod, w_in, conv_w, sink, pool_w, pool_scale,
           w_branch, w_out, ffn_w_gu, ffn_w_down, router_w, moe_w_gu, moe_w_down):
    bsz, seq, d = x.shape
    lc = ctx.shape[1]
    depth = w_in.shape[0]
    assert bsz + 1 <= 8 and seq % BLOCK == 0 and lc % BLOCK == 0 and seq % GRID_W == 0

    cvec = jnp.zeros((8, d), F32).at[:bsz].set(c).at[bsz].set(c_ctx)
    mods = _modvec(cvec, w_mod, b_mod)
    rope_tabs = _rope_tables(seq)

    lat_row = lambda i, tm: (i * tm) // seq
    ctx_row = lambda i, tm: bsz

    xl = x.reshape(bsz * seq, d)
    xc = ctx.reshape(bsz * lc, d)
    expert_w = None
    for l in range(depth):
        last = l == depth - 1
        mod = mods[l].reshape(8, 1, 6 * d)
        w_in_l = w_in[l].astype(BF16)
        wb_l, wo_l, pw_l = w_branch[l].astype(BF16), w_out[l].astype(BF16), pool_w[l].astype(BF16)
        routed = l % 2 == 1
        mixer = functools.partial(_merge, mod=mod, norm2_g=norm2_g[l], conv_w=conv_w[l], pool_w=pw_l,
                                  pool_scale=pool_scale[l], w_branch=wb_l, w_out=wo_l,
                                  h2_dtype=F32 if routed else BF16)
        if routed:
            wgu, wd = expert_w if expert_w is not None else (moe_w_gu[l // 2].astype(BF16),
                                                             moe_w_down[l // 2].astype(BF16))
        else:
            wgu, wd = ffn_w_gu[l // 2].astype(BF16), ffn_w_down[l // 2].astype(BF16)
        ride = not routed and not last
        ride_gu = moe_w_gu[(l + 1) // 2].reshape(-1, moe_w_gu.shape[-1]) if ride else None
        ride_d = moe_w_down[(l + 1) // 2].reshape(-1, d) if ride else None

        def channel_mix(h2, xm, *, seq_len, row_fn, final_norm):
            if routed:
                return _moe(h2, xm, mod, final_g, router_w[l // 2], wgu, wd, seq_len=seq_len,
                            mod_row_of_tile=row_fn, final_norm=final_norm)
            assert not final_norm
            return _ffn_dense(h2, xm, mod, wgu, wd, seq_len=seq_len, mod_row_of_tile=row_fn)

        if last:
            kvc = _inproj(xc, norm1_g[l], mod, w_in_l, seq_len=lc, mod_row_of_tile=ctx_row, kv_only=True)
        else:
            qc, kvc, mixc, gatec = _inproj(xc, norm1_g[l], mod, w_in_l, seq_len=lc, mod_row_of_tile=ctx_row)
            attn_c = _attention(qc.reshape(bsz, lc, -1), None, kvc.reshape(bsz, lc, -1), sink[l], band=False)
            xc_mid, h2c = mixer(attn_c.reshape(bsz * lc, -1), mixc, gatec, xc, seq_len=lc, mod_row_of_tile=ctx_row)
            xc_next = channel_mix(h2c, xc_mid, seq_len=lc, row_fn=ctx_row, final_norm=False)
        q, kv, mix, gate = _inproj(xl, norm1_g[l], mod, w_in_l, seq_len=seq, mod_row_of_tile=lat_row,
                                   rope_tabs=rope_tabs)
        attn = _attention(q.reshape(bsz, seq, -1), kv.reshape(bsz, seq, -1), kvc.reshape(bsz, lc, -1), sink[l],
                          band=True, cast=ride_gu)
        merged = mixer((attn[0] if ride else attn).reshape(bsz * seq, -1), mix, gate, xl, seq_len=seq,
                       mod_row_of_tile=lat_row, cast=ride_d)
        x_mid, h2 = merged[0], merged[1]
        xl = channel_mix(h2, x_mid, seq_len=seq, row_fn=lat_row, final_norm=last and l % 2 == 1)
        expert_w = (attn[1].reshape(moe_w_gu.shape[1:]), merged[2].reshape(moe_w_down.shape[1:])) if ride else None
        if not last:
            xc = xc_next
    if depth % 2 == 1:
        raise NotImplementedError("final norm is fused into the expert layer; depth must be even")
    return xl.reshape(bsz, seq, d)
```

```python
import functools

import jax
import jax.numpy as jnp
from jax import lax
from jax.experimental import pallas as pl
from jax.experimental.pallas import tpu as pltpu

F32 = jnp.float32
BF16 = jnp.bfloat16

GRID_W = 64
EPS = 1e-6
NEG_INF = -1e30
HEAD_DIM = 64
N_HEADS = 8
N_KV_HEADS = 2
GROUP = N_HEADS // N_KV_HEADS
WINDOW = 128
BLOCK = 128
ROPE_THETA = 10000.0
BRANCH = 512
POOL_SIZES = (2, 4, 8, 16)
POOL_GROUP = 128
Q_END = 512
V_END = 768
MIX_W = 4 * BRANCH
POOL_END = V_END + MIX_W

LANE = 128
SUBLANE = 8
MXU_DIM = 256
BF16_SUBLANE_TILE = 16
HALO = BF16_SUBLANE_TILE
MIB = 1024 * 1024


def _cparams(sem, vmem_mib):
    return pltpu.CompilerParams(dimension_semantics=sem, vmem_limit_bytes=vmem_mib * MIB)


def _pick_chunk(n, cap):
    for unit in (MXU_DIM, LANE):
        fits = [c for c in range(unit, min(n, cap) + 1, unit) if n % c == 0]
        if fits:
            return fits[-1]
    raise ValueError((n, cap))


def _resident(shape):
    nd = len(shape)
    return pl.BlockSpec(shape, lambda *_: (0,) * nd, pipeline_mode=pl.Buffered(1))


def _norm_mod(x, g, shift, scale):
    y = x * lax.rsqrt(jnp.mean(x * x, axis=-1, keepdims=True) + EPS) * g
    return y * (1.0 + scale) + shift


def _modvec_kernel(c_ref, w_ref, b_ref, o_ref):
    c = c_ref[...]
    s = c * jax.nn.sigmoid(c)
    o_ref[...] = jnp.dot(s, w_ref[...], preferred_element_type=F32) + b_ref[...]


def _modvec(cvec, w_mod, b_mod):
    depth, d, n = w_mod.shape
    nc = _pick_chunk(n, 1536)
    return pl.pallas_call(
        _modvec_kernel,
        grid=(depth, n // nc),
        in_specs=[
            pl.BlockSpec((8, d), lambda l, j: (0, 0)),
            pl.BlockSpec((None, d, nc), lambda l, j: (l, 0, j)),
            pl.BlockSpec((None, 1, nc), lambda l, j: (l, 0, j)),
        ],
        out_specs=pl.BlockSpec((None, 8, nc), lambda l, j: (l, 0, j)),
        out_shape=jax.ShapeDtypeStruct((depth, 8, n), F32),
        compiler_params=_cparams(("arbitrary", "arbitrary"), 32),
        name="modvec",
    )(cvec, w_mod, b_mod.reshape(depth, 1, n))


def _rope_tables(seq_len):
    n_freq = HEAD_DIM // 4
    inv = ROPE_THETA ** (-jnp.arange(n_freq, dtype=F32) / n_freq)
    pos = jnp.arange(seq_len)
    row = (pos // GRID_W).astype(F32)[:, None] * inv[None, :]
    col = (pos % GRID_W).astype(F32)[:, None] * inv[None, :]
    zero = jnp.zeros_like(row)
    cos = jnp.concatenate([jnp.cos(row)] * 2 + [jnp.cos(col)] * 2, axis=-1)
    s_lo = jnp.concatenate([-jnp.sin(row), zero, -jnp.sin(col), zero], axis=-1)
    s_hi = jnp.concatenate([zero, jnp.sin(row), zero, jnp.sin(col)], axis=-1)
    return tuple(jnp.tile(t, (1, LANE // HEAD_DIM)) for t in (cos, s_lo, s_hi))


def _inproj_kernel(*refs, rope, kv_only, d):
    x_ref, g_ref, mod_ref, w_ref = refs[:4]
    refs = refs[4:]
    if rope:
        cos_ref, slo_ref, shi_ref = refs[:3]
        refs = refs[3:]

        def rot(z):
            return (z * cos_ref[...] + pltpu.roll(z, LANE - 16, 1) * slo_ref[...]
                    + pltpu.roll(z, 16, 1) * shi_ref[...])
    else:
        def rot(z):
            return z

    mod = mod_ref[...]
    h = _norm_mod(x_ref[...], g_ref[...], mod[:, 0:d], mod[:, d:2 * d]).astype(BF16)

    def proj(c0, c1):
        return jnp.dot(h, w_ref[:, c0:c1], preferred_element_type=F32)

    if kv_only:
        (kv_ref,) = refs
        z = proj(0, 2 * LANE)
        kv_ref[:, 0:LANE] = rot(z[:, 0:LANE]).astype(BF16)
        kv_ref[:, LANE:] = z[:, LANE:].astype(BF16)
        return

    q_ref, kv_ref, mix_ref, gate_ref = refs
    cw = 512
    n_gate = gate_ref.shape[1]
    for c in range(n_gate // cw):
        zg = proj(POOL_END + c * cw, POOL_END + (c + 1) * cw)
        gate_ref[:, c * cw:(c + 1) * cw] = jax.nn.sigmoid(zg).astype(BF16)
    z = proj(0, Q_END)
    for j in range(Q_END // LANE):
        q_ref[:, j * LANE:(j + 1) * LANE] = (rot(z[:, j * LANE:(j + 1) * LANE]) * HEAD_DIM ** -0.5).astype(BF16)
    z = proj(Q_END, V_END)
    kv_ref[:, 0:LANE] = rot(z[:, 0:LANE]).astype(BF16)
    kv_ref[:, LANE:] = z[:, LANE:].astype(BF16)
    for c in range(MIX_W // cw):
        mix_ref[:, c * cw:(c + 1) * cw] = proj(V_END + c * cw, V_END + (c + 1) * cw).astype(BF16)


def _inproj(x, norm_g, mod, w_in, *, seq_len, mod_row_of_tile, rope_tabs=None, kv_only=False):
    m, d = x.shape
    tm = min(1024, seq_len)
    assert m % tm == 0 and seq_len % tm == 0
    tiles_per_seq = seq_len // tm
    rope = rope_tabs is not None
    in_w = w_in.shape[1]
    in_specs = [
        pl.BlockSpec((tm, d), lambda i: (i, 0)),
        pl.BlockSpec((1, d), lambda i: (0, 0)),
        pl.BlockSpec((None, 1, 6 * d), lambda i: (mod_row_of_tile(i, tm), 0, 0)),
        pl.BlockSpec((d, 2 * LANE), lambda i: (0, Q_END // (2 * LANE))) if kv_only else _resident((d, in_w)),
    ]
    args = [x, norm_g.reshape(1, d), mod, w_in]
    if rope:
        in_specs += [pl.BlockSpec((tm, LANE), lambda i: (i % tiles_per_seq, 0))] * 3
        args += list(rope_tabs)
    if kv_only:
        out_specs = pl.BlockSpec((tm, 2 * LANE), lambda i: (i, 0))
        out_shape = jax.ShapeDtypeStruct((m, 2 * LANE), BF16)
    else:
        widths = (Q_END, 2 * LANE, MIX_W, in_w - POOL_END)
        out_specs = [pl.BlockSpec((tm, w), lambda i: (i, 0)) for w in widths]
        out_shape = [jax.ShapeDtypeStruct((m, w), BF16) for w in widths]
    return pl.pallas_call(
        functools.partial(_inproj_kernel, rope=rope, kv_only=kv_only, d=d),
        grid=(m // tm,),
        in_specs=in_specs,
        out_specs=out_specs,
        out_shape=out_shape,
        compiler_params=_cparams(("parallel",), 56),
        name="inproj_kv" if kv_only else "inproj",
    )(*args)


ATTN_STRIP = 32
ATTN_QBLOCKS = 2
ATTN_AHEAD = 2


def _attn_kernel(sink_ref, q_ref, *refs, band, carry_cast):
    refs, (s_ref, p_ref) = list(refs[:-2]), refs[-2:]
    qblocks = q_ref.shape[1] // BLOCK
    if carry_cast:
        cast_out = refs.pop()
        cast_in = refs.pop(-2)
        cast_out[...] = cast_in[...].astype(BF16)
    if band:
        kvp_ref, kvm_ref, kvn_ref, kvc_ref, bias_ref, o_ref = refs
    else:
        kvc_ref, o_ref = refs
    kvc = kvc_ref[0]
    lc = kvc.shape[0]
    nloc = 3 * BLOCK if band else 0
    nt = (((1,), (1,)), ((), ()))
    w_ctx = jnp.concatenate([kvc[:, LANE:], jnp.ones((lc, LANE), BF16)], axis=1)
    if band:
        n = pl.program_id(1)
        kv4 = jnp.concatenate([kvp_ref[0], kvm_ref[0], kvn_ref[0]], axis=0)
        w4 = jnp.concatenate([kv4[:, LANE:], jnp.ones((kv4.shape[0], LANE), BF16)], axis=1)
        col = lax.broadcasted_iota(jnp.int32, (1, nloc), 1)
        head_edge = jnp.where((col < BLOCK) & (n == 0), NEG_INF, 0.0)
        tail_edge = jnp.where((col >= 2 * BLOCK) & (n == pl.num_programs(1) - 1), NEG_INF, 0.0)
        biases = ([bias_ref[...] + head_edge] + [bias_ref[...]] * (qblocks - 2)
                  + [bias_ref[...] + tail_edge])
    units = [(sb, h) for sb in range(qblocks) for h in range(N_HEADS)]
    def scores(u):
        sb, h = units[u]
        qh = q_ref[0, sb * BLOCK:(sb + 1) * BLOCK, h * HEAD_DIM:(h + 1) * HEAD_DIM]
        ks = slice(h // GROUP * HEAD_DIM, (h // GROUP + 1) * HEAD_DIM)
        if band:
            kl = kv4[sb * BLOCK:sb * BLOCK + nloc, ks]
            s_ref[u, :, 0:nloc] = lax.dot_general(qh, kl, nt, preferred_element_type=F32) + biases[sb]
        s_ref[u, :, nloc:] = lax.dot_general(qh, kvc[:, ks], nt, preferred_element_type=F32)

    def probs(u):
        sink = sink_ref[units[u][1]]
        esink = []
        for r in range(0, BLOCK, ATTN_STRIP):
            s = s_ref[u, r:r + ATTN_STRIP, :]
            m = jnp.maximum(jnp.max(s, axis=-1, keepdims=True), sink)
            p_ref[u, r:r + ATTN_STRIP, :] = jnp.exp(s - m).astype(BF16)
            esink.append(jnp.exp(sink - m))
        return jnp.concatenate(esink, axis=0)

    def weighted_values(u, esink):
        sb, h = units[u]
        ks = slice(h // GROUP * HEAD_DIM, (h // GROUP + 1) * HEAD_DIM)
        o2 = jnp.dot(p_ref[u, :, nloc:], w_ctx, preferred_element_type=F32)
        if band:
            o2 = o2 + jnp.dot(p_ref[u, :, 0:nloc], w4[sb * BLOCK:sb * BLOCK + nloc], preferred_element_type=F32)
        return o2[:, ks] / (o2[:, LANE:LANE + HEAD_DIM] + esink)

    outs, esinks = [], []
    for u in range(min(ATTN_AHEAD, len(units))):
        scores(u)
    for u in range(len(units)):
        if u + ATTN_AHEAD < len(units):
            scores(u + ATTN_AHEAD)
        esinks.append(probs(u))
        if u >= 1:
            outs.append(weighted_values(u - 1, esinks[u - 1]))
    outs.append(weighted_values(len(units) - 1, esinks[-1]))
    for sb in range(qblocks):
        o_ref[0, sb * BLOCK:(sb + 1) * BLOCK, :] = jnp.concatenate(
            outs[sb * N_HEADS:(sb + 1) * N_HEADS], axis=1).astype(BF16)


def _cast_rider(w, n_steps, index_map):
    rows, cols = w.shape
    assert rows % (n_steps * BF16_SUBLANE_TILE) == 0, (w.shape, n_steps)
    return pl.BlockSpec((rows // n_steps, cols), index_map), jax.ShapeDtypeStruct((rows, cols), BF16)


def _attention(q, kv, kvc, sink, *, band, cast=None):
    b, l, _ = q.shape
    lc = kvc.shape[1]
    nb = l // BLOCK
    qb = min(ATTN_QBLOCKS, nb)
    tq = qb * BLOCK
    assert l % tq == 0 and qb >= 2
    in_specs = [
        pl.BlockSpec(memory_space=pltpu.SMEM),
        pl.BlockSpec((1, tq, N_HEADS * HEAD_DIM), lambda bi, n: (bi, n, 0)),
    ]
    args = [sink, q]
    if band:
        in_specs += [
            pl.BlockSpec((1, BLOCK, 2 * LANE), lambda bi, n: (bi, jnp.maximum(qb * n - 1, 0), 0)),
            pl.BlockSpec((1, tq, 2 * LANE), lambda bi, n: (bi, n, 0)),
            pl.BlockSpec((1, BLOCK, 2 * LANE), lambda bi, n: (bi, jnp.minimum(qb * (n + 1), nb - 1), 0)),
        ]
        args += [kv, kv, kv]
    in_specs.append(pl.BlockSpec((1, lc, 2 * LANE), lambda bi, n: (bi, 0, 0)))
    args.append(kvc)
    nkeys = lc
    if band:
        rel = jnp.arange(3 * BLOCK)[None, :] - BLOCK - jnp.arange(BLOCK)[:, None]
        in_specs.append(pl.BlockSpec((BLOCK, 3 * BLOCK), lambda bi, n: (0, 0)))
        args.append(jnp.where(jnp.abs(rel) <= WINDOW, 0.0, NEG_INF).astype(F32))
        nkeys += 3 * BLOCK
    nq = l // tq
    out_specs = [pl.BlockSpec((1, tq, N_HEADS * HEAD_DIM), lambda bi, n: (bi, n, 0))]
    out_shape = [jax.ShapeDtypeStruct((b, l, N_HEADS * HEAD_DIM), BF16)]
    if cast is not None:
        spec, shape = _cast_rider(cast, b * nq, lambda bi, n: (bi * nq + n, 0))
        in_specs.append(spec)
        args.append(cast)
        out_specs.append(spec)
        out_shape.append(shape)
    outs = pl.pallas_call(
        functools.partial(_attn_kernel, band=band, carry_cast=cast is not None),
        grid=(b, nq),
        in_specs=in_specs,
        out_specs=out_specs,
        out_shape=out_shape,
        scratch_shapes=[pltpu.VMEM((qb * N_HEADS, BLOCK, nkeys), F32),
                        pltpu.VMEM((qb * N_HEADS, BLOCK, nkeys), BF16)],
        compiler_params=_cparams(("parallel", "parallel"), 56),
        name="attn_band" if band else "attn_ctx",
    )(*args)
    return outs[0] if cast is None else outs


def _merge_kernel(attn_ref, mix_ref, prev_ref, next_ref, gate_ref, x_ref, mod_ref, n2g_ref, convw_ref,
                  poolw_ref, pscale_ref, wb_ref, wo_ref, *refs, tm, seq_len, d):
    if len(refs) == 4:
        cast_in, xo_ref, h2_ref, cast_out = refs
        cast_out[...] = cast_in[...].astype(BF16)
    else:
        xo_ref, h2_ref = refs
    tile = pl.program_id(0) % (seq_len // tm)
    keep_prev = (tile != 0).astype(F32)
    keep_next = (tile != seq_len // tm - 1).astype(F32)
    mixm = mix_ref[...]
    prev = prev_ref[...].astype(F32) * keep_prev
    nxt = next_ref[...].astype(F32) * keep_next
    b = BRANCH
    cx, cb, cc = (mixm[:, j * b:(j + 1) * b].astype(F32) for j in range(3))

    p = cc * cx
    p_prev = prev[HALO - 1:HALO, 2 * b:3 * b] * prev[HALO - 1:HALO, 0:b]
    p_next = nxt[0:1, 2 * b:3 * b] * nxt[0:1, 0:b]
    ridx = lax.broadcasted_iota(jnp.int32, (tm, b), 0)
    p_dn = jnp.where(ridx == 0, p_prev, pltpu.roll(p, 1, 0))
    p_up = jnp.where(ridx == tm - 1, p_next, pltpu.roll(p, tm - 1, 0))
    cw = convw_ref[...]
    conv_out = (cb * (p_dn * cw[0:1] + p * cw[1:2] + p_up * cw[2:3])).astype(BF16)

    u_main = mixm[:, 3 * b:4 * b].astype(F32)
    u_ext = jnp.concatenate([prev[:, 3 * b:4 * b], u_main, nxt[:, 3 * b:4 * b]], axis=0)
    ext = tm + 2 * HALO

    def shift(a, s):
        return pltpu.roll(a, s % ext, 0)

    tpos = tile * tm + lax.broadcasted_iota(jnp.int32, (tm, 1), 0)
    pooled = []
    for gi, w in enumerate(POOL_SIZES):
        gs = slice(gi * POOL_GROUP, (gi + 1) * POOL_GROUP)
        ug = u_ext[:, gs]
        a = ug + shift(ug, 1)
        ww = 2
        while ww < w:
            a = shift(a, ww // 2) + shift(a, -(ww // 2))
            ww *= 2
        cnt = jnp.minimum(tpos + w // 2, seq_len) - jnp.maximum(tpos - w // 2, 0)
        dlt = a[HALO:HALO + tm] / cnt.astype(F32) - u_main[:, gs]
        pooled.append(jnp.dot(dlt.astype(BF16), poolw_ref[gi], preferred_element_type=F32))
    pool_out = (jnp.concatenate(pooled, axis=1) * pscale_ref[...]).astype(BF16)

    y = (gate_ref[:, 0:d].astype(F32) * jnp.dot(attn_ref[...], wb_ref[0], preferred_element_type=F32)
         + gate_ref[:, d:2 * d].astype(F32) * jnp.dot(conv_out, wb_ref[1], preferred_element_type=F32)
         + gate_ref[:, 2 * d:3 * d].astype(F32) * jnp.dot(pool_out, wb_ref[2], preferred_element_type=F32))
    o = jnp.dot(y.astype(BF16), wo_ref[...], preferred_element_type=F32)
    mod = mod_ref[...]
    xn = x_ref[...] + mod[:, 2 * d:3 * d] * o
    xo_ref[...] = xn
    h2_ref[...] = _norm_mod(xn, n2g_ref[...], mod[:, 3 * d:4 * d], mod[:, 4 * d:5 * d]).astype(h2_ref.dtype)


def _merge(attn, mix, gate, x, mod, norm2_g, conv_w, pool_w, pool_scale, w_branch, w_out, *, seq_len,
           mod_row_of_tile, h2_dtype, cast=None):
    m, d = x.shape
    tm = min(512, seq_len)
    assert m % tm == 0 and seq_len % tm == 0 and tm % HALO == 0
    hb = tm // HALO
    n_halo = m // HALO
    row = lambda w: pl.BlockSpec((tm, w), lambda i: (i, 0))
    in_specs = [
        row(BRANCH),
        row(MIX_W),
        pl.BlockSpec((HALO, MIX_W), lambda i: (jnp.maximum(i * hb - 1, 0), 0)),
        pl.BlockSpec((HALO, MIX_W), lambda i: (jnp.minimum((i + 1) * hb, n_halo - 1), 0)),
        row(3 * d),
        row(d),
        pl.BlockSpec((None, 1, 6 * d), lambda i: (mod_row_of_tile(i, tm), 0, 0)),
        pl.BlockSpec((1, d), lambda i: (0, 0)),
        _resident(conv_w.shape),
        _resident(pool_w.shape),
        pl.BlockSpec((1, BRANCH), lambda i: (0, 0)),
        _resident(w_branch.shape),
        _resident(w_out.shape),
    ]
    args = [attn, mix, mix, mix, gate, x, mod, norm2_g.reshape(1, d), conv_w, pool_w,
            pool_scale.reshape(1, BRANCH), w_branch, w_out]
    out_specs = [row(d), row(d)]
    out_shape = [jax.ShapeDtypeStruct((m, d), F32), jax.ShapeDtypeStruct((m, d), h2_dtype)]
    if cast is not None:
        spec, shape = _cast_rider(cast, m // tm, lambda i: (i, 0))
        in_specs.append(spec)
        args.append(cast)
        out_specs.append(spec)
        out_shape.append(shape)
    return pl.pallas_call(
        functools.partial(_merge_kernel, tm=tm, seq_len=seq_len, d=d),
        grid=(m // tm,),
        in_specs=in_specs,
        out_specs=out_specs,
        out_shape=out_shape,
        compiler_params=_cparams(("parallel",), 48),
        name="merge",
    )(*args)


def _ffn_kernel(h_ref, x_ref, mod_ref, wgu_ref, wd_ref, *refs, d, d_ff, fc):
    n_cast = len(refs) // 2
    o_ref = refs[n_cast]
    for cast_in, cast_out in zip(refs[:n_cast], refs[n_cast + 1:]):
        cast_out[...] = cast_in[...].astype(BF16)
    h = h_ref[...]
    acc = None
    for f in range(0, d_ff, fc):
        g = jnp.dot(h, wgu_ref[:, f:f + fc], preferred_element_type=F32)
        u = jnp.dot(h, wgu_ref[:, d_ff + f:d_ff + f + fc], preferred_element_type=F32)
        a = (g * jax.nn.sigmoid(g) * u).astype(BF16)
        y = jnp.dot(a, wd_ref[f:f + fc, :], preferred_element_type=F32)
        acc = y if acc is None else acc + y
    o_ref[...] = x_ref[...] + mod_ref[...][:, 5 * d:6 * d] * acc


def _ffn_dense(h2, x, mod, w_gu, w_down, *, seq_len, mod_row_of_tile, casts=()):
    m, d = x.shape
    d_ff = w_down.shape[0]
    tm = min(512, seq_len)
    assert m % tm == 0 and seq_len % tm == 0
    fc = _pick_chunk(d_ff, 3072)
    n_steps = m // tm
    in_specs = [
        pl.BlockSpec((tm, d), lambda i: (i, 0)),
        pl.BlockSpec((tm, d), lambda i: (i, 0)),
        pl.BlockSpec((None, 1, 6 * d), lambda i: (mod_row_of_tile(i, tm), 0, 0)),
        _resident(w_gu.shape),
        _resident(w_down.shape),
    ]
    out_specs = [pl.BlockSpec((tm, d), lambda i: (i, 0))]
    out_shape = [jax.ShapeDtypeStruct((m, d), F32)]
    for w, layer in casts:
        _, rows, cols = w.shape
        assert rows % (n_steps * BF16_SUBLANE_TILE) == 0, (w.shape, n_steps)
        in_specs.append(pl.BlockSpec((None, rows // n_steps, cols), lambda i, layer=layer: (layer, i, 0)))
        out_specs.append(pl.BlockSpec((rows // n_steps, cols), lambda i: (i, 0)))
        out_shape.append(jax.ShapeDtypeStruct((rows, cols), BF16))
    outs = pl.pallas_call(
        functools.partial(_ffn_kernel, d=d, d_ff=d_ff, fc=fc),
        grid=(n_steps,),
        in_specs=in_specs,
        out_specs=out_specs,
        out_shape=out_shape,
        compiler_params=_cparams(("parallel",), 56),
        name="ffn_dense",
    )(h2, x, mod, w_gu, w_down, *[w for w, _ in casts])
    return outs if casts else outs[0]


ROUTE_ROWS = 8
META_ROWS = 4
META_TILES, META_PAD_START, META_PAD_LEN = LANE, 2 * LANE, 3 * LANE


def _router_kernel(h_ref, rw_ref, tri_ref, wts_ref, pairs_ref, cnt_ref, carry_ref, *, n_experts):
    @pl.when(pl.program_id(0) == 0)
    def _():
        carry_ref[...] = jnp.zeros_like(carry_ref)

    h, w = h_ref[...], rw_ref[...]
    h_hi, w_hi = h.astype(BF16), w.astype(BF16)
    h_lo, w_lo = (h - h_hi.astype(F32)).astype(BF16), (w - w_hi.astype(F32)).astype(BF16)
    logits = (jnp.dot(h_hi, w_hi, preferred_element_type=F32) + jnp.dot(h_lo, w_hi, preferred_element_type=F32)
              + jnp.dot(h_hi, w_lo, preferred_element_type=F32))
    lane = lax.broadcasted_iota(jnp.int32, logits.shape, 1).astype(F32)
    logits = jnp.where(lane < n_experts, logits, -jnp.inf)
    m1 = jnp.max(logits, axis=-1, keepdims=True)
    i1 = jnp.min(jnp.where(logits == m1, lane, float(LANE)), axis=-1, keepdims=True)
    rest = jnp.where(lane == i1, -jnp.inf, logits)
    m2 = jnp.max(rest, axis=-1, keepdims=True)
    i2 = jnp.min(jnp.where(rest == m2, lane, float(LANE)), axis=-1, keepdims=True)
    e2 = jnp.exp(m2 - m1)
    den = 1.0 + e2
    wts_ref[...] = jnp.where(lane == 0, 1.0 / den, jnp.where(lane == 1, e2 / den, 0.0))

    m1t = (lane == i1).astype(F32).T[0:ROUTE_ROWS]
    m2t = (lane == i2).astype(F32).T[0:ROUTE_ROWS]
    mem = m1t + m2t
    before = jnp.dot(mem.astype(BF16), tri_ref[...], preferred_element_type=F32) + carry_ref[:, 0:1]
    eid = lax.broadcasted_iota(jnp.int32, mem.shape, 0).astype(F32)
    rows = [jnp.sum(m1t * eid, axis=0, keepdims=True), jnp.sum(m2t * eid, axis=0, keepdims=True),
            jnp.sum(m1t * before, axis=0, keepdims=True), jnp.sum(m2t * before, axis=0, keepdims=True)]
    rows += [jnp.zeros_like(rows[0])] * (ROUTE_ROWS - len(rows))
    pairs_ref[...] = jnp.concatenate(rows, axis=0)
    carry_ref[...] = carry_ref[...] + jnp.sum(mem, axis=1, keepdims=True)
    cnt_ref[...] = carry_ref[...]


def _router(h2, router_w):
    m, d = h2.shape
    n_experts = router_w.shape[1]
    assert n_experts <= ROUTE_ROWS
    rw = jnp.pad(router_w, ((0, 0), (0, LANE - n_experts)))
    tm = min(512, m)
    tri = jnp.triu(jnp.ones((tm, tm), BF16), k=1)
    return pl.pallas_call(
        functools.partial(_router_kernel, n_experts=n_experts),
        grid=(m // tm,),
        in_specs=[
            pl.BlockSpec((tm, d), lambda i: (i, 0)),
            pl.BlockSpec((d, LANE), lambda i: (0, 0)),
            pl.BlockSpec((tm, tm), lambda i: (0, 0)),
        ],
        out_specs=[
            pl.BlockSpec((tm, LANE), lambda i: (i, 0)),
            pl.BlockSpec((ROUTE_ROWS, tm), lambda i: (0, i)),
            pl.BlockSpec((ROUTE_ROWS, LANE), lambda i: (0, 0)),
        ],
        out_shape=[jax.ShapeDtypeStruct((m, LANE), F32), jax.ShapeDtypeStruct((ROUTE_ROWS, m), F32),
                   jax.ShapeDtypeStruct((ROUTE_ROWS, LANE), F32)],
        scratch_shapes=[pltpu.VMEM((ROUTE_ROWS, LANE), F32)],
        compiler_params=_cparams(("arbitrary",), 32),
        name="router",
    )(h2, rw, tri)


def _plan_kernel(pairs_ref, cnt_ref, dest_ref, meta_ref, *, tile_rows):
    cnt = cnt_ref[...]
    padded = jnp.floor((cnt + (tile_rows - 1)) * (1.0 / tile_rows)) * tile_rows
    sub = lax.broadcasted_iota(jnp.int32, cnt.shape, 0)
    lane = lax.broadcasted_iota(jnp.int32, cnt.shape, 1)
    end_row = jnp.sum(jnp.where(sub <= lane, padded, 0.0), axis=0, keepdims=True)
    end_col = jnp.sum(jnp.where(lane == sub, end_row, 0.0), axis=1, keepdims=True)
    start_col = end_col - padded[:, 0:1]
    p = pairs_ref[...]
    eid = lax.broadcasted_iota(jnp.int32, p.shape, 0).astype(F32)
    d1 = jnp.sum(jnp.where(eid == p[0:1], start_col, 0.0), axis=0, keepdims=True) + p[2:3]
    d2 = jnp.sum(jnp.where(eid == p[1:2], start_col, 0.0), axis=0, keepdims=True) + p[3:4]
    dest_ref[...] = jnp.concatenate([d1, d2] + [jnp.zeros_like(d1)] * (ROUTE_ROWS - 2), axis=0).astype(jnp.int32)
    tile_start = (lane * tile_rows).astype(F32)
    tile_e = jnp.sum((end_col <= tile_start).astype(F32), axis=0, keepdims=True)
    n_tiles = jnp.max(end_col, axis=0, keepdims=True) * (1.0 / tile_rows)
    pad_start = jnp.sum(jnp.where(lane == sub, start_col + cnt, 0.0), axis=0, keepdims=True)
    pad_len = jnp.sum(jnp.where(lane == sub, padded - cnt, 0.0), axis=0, keepdims=True)
    meta = jnp.concatenate([tile_e, jnp.broadcast_to(n_tiles, tile_e.shape), pad_start, pad_len]
                           + [jnp.zeros_like(tile_e)] * (ROUTE_ROWS - META_ROWS), axis=0)
    meta_ref[...] = meta.astype(jnp.int32)


def _plan(pairs, counts, *, tile_rows):
    m = pairs.shape[1]
    tp = min(2048, m)
    return pl.pallas_call(
        functools.partial(_plan_kernel, tile_rows=tile_rows),
        grid=(m // tp,),
        in_specs=[pl.BlockSpec((ROUTE_ROWS, tp), lambda i: (0, i)), pl.BlockSpec((ROUTE_ROWS, LANE), lambda i: (0, 0))],
        out_specs=[pl.BlockSpec((ROUTE_ROWS, tp), lambda i: (0, i)), pl.BlockSpec((ROUTE_ROWS, LANE), lambda i: (0, 0))],
        out_shape=[jax.ShapeDtypeStruct((ROUTE_ROWS, m), jnp.int32), jax.ShapeDtypeStruct((ROUTE_ROWS, LANE), jnp.int32)],
        compiler_params=_cparams(("arbitrary",), 32),
        name="plan",
    )(pairs, counts)


ROW_DMA_UNROLL = 8


assert ROW_DMA_UNROLL == SUBLANE


def _row_group(ref, blk):
    return ref.at[pl.ds(pl.multiple_of(blk * ROW_DMA_UNROLL, ROW_DMA_UNROLL), ROW_DMA_UNROLL), :]


def _row_copy(src, src_row, dst, dst_row, sem):
    return pltpu.make_async_copy(src.at[pl.ds(src_row, 1), :], dst.at[pl.ds(dst_row, 1), :], sem)


def _dispatch_kernel(dest_ref, meta_ref, h_ref, xs_ref, zero_ref, sem, zsem, *, tm, n_tok, n_experts, n_tiles_max):
    tile_rows = zero_ref.shape[0]
    i = pl.program_id(0)
    last = pl.num_programs(0) - 1
    base = i * tm

    def wait_rows():
        for _ in range(2):
            pltpu.make_async_copy(h_ref, xs_ref.at[pl.ds(0, tm), :], sem).wait()

    def issue(blk, carry):
        rows = _row_group(h_ref, blk)
        for j in range(ROW_DMA_UNROLL):
            r = blk * ROW_DMA_UNROLL + j
            _row_copy(rows, j, xs_ref, dest_ref[base + r], sem).start(priority=0)
            _row_copy(rows, j, xs_ref, dest_ref[n_tok + base + r], sem).start(priority=1)
        return carry

    lax.fori_loop(0, tm // ROW_DMA_UNROLL, issue, 0)

    @pl.when(i < last)
    def _():
        wait_rows()

    @pl.when(i == last)
    def _():
        zero_ref[...] = jnp.zeros_like(zero_ref)

        def fill_padding(e, start):
            first = meta_ref[META_PAD_START + e]
            n_pad = meta_ref[META_PAD_LEN + e]
            n_single = jnp.minimum((-first) & (SUBLANE - 1), n_pad)

            def single(r, carry):
                cp = _row_copy(zero_ref, 0, xs_ref, first + r, zsem)
                cp.start() if start else cp.wait()
                return carry

            def piece(c, carry):
                off = pl.multiple_of(first + n_single + c * SUBLANE, SUBLANE)
                cp = pltpu.make_async_copy(zero_ref.at[pl.ds(0, SUBLANE), :], xs_ref.at[pl.ds(off, SUBLANE), :], zsem)
                cp.start() if start else cp.wait()
                return carry

            lax.fori_loop(0, n_single, single, 0)
            lax.fori_loop(0, (n_pad - n_single) // SUBLANE, piece, 0)

        def tail_copy(k):
            return pltpu.make_async_copy(zero_ref, xs_ref.at[pl.ds(k * tile_rows, tile_rows), :], zsem)

        def zero_tile(k, carry):
            tail_copy(k).start()
            return carry

        def wait_tile(k, carry):
            tail_copy(k).wait()
            return carry

        for e in range(n_experts):
            fill_padding(e, start=True)
        lax.fori_loop(meta_ref[META_TILES], n_tiles_max, zero_tile, 0)
        for e in range(n_experts):
            fill_padding(e, start=False)
        lax.fori_loop(meta_ref[META_TILES], n_tiles_max, wait_tile, 0)
        wait_rows()


def _dispatch(h2, dest, meta, n_rows, *, tm, n_experts):
    m, d = h2.shape
    tt = min(512, m)
    return pl.pallas_call(
        functools.partial(_dispatch_kernel, tm=tt, n_tok=m, n_experts=n_experts, n_tiles_max=n_rows // tm),
        grid_spec=pltpu.PrefetchScalarGridSpec(
            num_scalar_prefetch=2,
            grid=(m // tt,),
            in_specs=[pl.BlockSpec((tt, d), lambda i, dr, mt: (i, 0))],
            out_specs=pl.BlockSpec(memory_space=pl.ANY),
            scratch_shapes=[pltpu.VMEM((tm, d), F32), pltpu.SemaphoreType.DMA, pltpu.SemaphoreType.DMA],
        ),
        out_shape=jax.ShapeDtypeStruct((n_rows, d), F32),
        compiler_params=_cparams(("arbitrary",), 32),
        name="dispatch",
    )(dest, meta, h2)


def _gmm_kernel(meta_ref, xs_ref, wg_ref, wu_ref, wd_ref, o_ref, acc_ref):
    f = pl.program_id(1)
    nf = pl.num_programs(1)

    @pl.when((pl.program_id(0) == 0) & (f == 0))
    def _():
        acc_ref[...] = jnp.zeros_like(acc_ref)

    @pl.when(pl.program_id(0) < meta_ref[LANE])
    def _():
        xb = xs_ref[...].astype(BF16)
        g = jnp.dot(xb, wg_ref[...], preferred_element_type=F32)
        u = jnp.dot(xb, wu_ref[...], preferred_element_type=F32)
        a = (g * jax.nn.sigmoid(g) * u).astype(BF16)
        y = jnp.dot(a, wd_ref[...], preferred_element_type=F32)
        total = y + jnp.where(f > 0, acc_ref[...], 0.0)
        acc_ref[...] = total
        o_ref[...] = total

    @pl.when((pl.program_id(0) >= meta_ref[LANE]) & (f == nf - 1))
    def _():
        o_ref[...] = jnp.zeros_like(o_ref)


def _grouped_swiglu(xs, meta, w_gu, w_down, *, tm):
    n_rows, d = xs.shape
    n_experts, d_ff, _ = w_down.shape
    fc = _pick_chunk(d_ff, d_ff // 2)
    nf = d_ff // fc
    assert nf >= 2 and n_rows % tm == 0 and n_rows // tm <= LANE

    def tile(i, mt):
        return jnp.maximum(jnp.minimum(i, mt[LANE] - 1), 0)

    def expert(i, mt):
        return jnp.minimum(mt[tile(i, mt)], n_experts - 1)

    def chunk(i, f, mt):
        return jnp.where(i < mt[LANE], f, nf - 1)

    return pl.pallas_call(
        _gmm_kernel,
        grid_spec=pltpu.PrefetchScalarGridSpec(
            num_scalar_prefetch=1,
            grid=(n_rows // tm, nf),
            in_specs=[
                pl.BlockSpec((tm, d), lambda i, f, mt: (tile(i, mt), 0)),
                pl.BlockSpec((None, d, fc), lambda i, f, mt: (expert(i, mt), 0, chunk(i, f, mt))),
                pl.BlockSpec((None, d, fc), lambda i, f, mt: (expert(i, mt), 0, nf + chunk(i, f, mt))),
                pl.BlockSpec((None, fc, d), lambda i, f, mt: (expert(i, mt), chunk(i, f, mt), 0)),
            ],
            out_specs=pl.BlockSpec((tm, d), lambda i, f, mt: (i, 0)),
            scratch_shapes=[pltpu.VMEM((tm, d), F32)],
        ),
        out_shape=jax.ShapeDtypeStruct((n_rows, d), F32),
        compiler_params=_cparams(("arbitrary", "arbitrary"), 56),
        name="grouped_swiglu",
    )(meta, xs, w_gu, w_gu, w_down)


def _combine_kernel(dest_ref, x_ref, w_ref, mod_ref, fg_ref, ys_ref, o_ref, y_ref, sems, *, tm, n_tok, d,
                    final_norm):
    i = pl.program_id(0)
    slot = i % 2

    def gather(tile, to_slot):
        def issue(blk, carry):
            rows = [_row_group(y_ref.at[to_slot, c], blk) for c in range(2)]
            for j in range(ROW_DMA_UNROLL):
                r = blk * ROW_DMA_UNROLL + j
                for c in range(2):
                    _row_copy(ys_ref, dest_ref[c * n_tok + tile * tm + r], rows[c], j,
                              sems.at[to_slot]).start(priority=c)
            return carry

        lax.fori_loop(0, tm // ROW_DMA_UNROLL, issue, 0)

    @pl.when(i == 0)
    def _():
        gather(0, 0)

    @pl.when(i + 1 < pl.num_programs(0))
    def _():
        gather(i + 1, 1 - slot)

    for c in range(2):
        pltpu.make_async_copy(ys_ref.at[pl.ds(0, tm), :], y_ref.at[slot, c], sems.at[slot]).wait()
    w = w_ref[...]
    moe = w[:, 0:1] * y_ref[slot, 0] + w[:, 1:2] * y_ref[slot, 1]
    xn = x_ref[...] + mod_ref[...][:, 5 * d:6 * d] * moe
    if final_norm:
        xn = xn * lax.rsqrt(jnp.mean(xn * xn, axis=-1, keepdims=True) + EPS) * fg_ref[...]
    o_ref[...] = xn


def _combine(x, wts, dest, ys, mod, final_g, *, seq_len, mod_row_of_tile, final_norm):
    m, d = x.shape
    tm = min(512, seq_len)
    return pl.pallas_call(
        functools.partial(_combine_kernel, tm=tm, n_tok=m, d=d, final_norm=final_norm),
        grid_spec=pltpu.PrefetchScalarGridSpec(
            num_scalar_prefetch=1,
            grid=(m // tm,),
            in_specs=[
                pl.BlockSpec((tm, d), lambda i, dr: (i, 0)),
                pl.BlockSpec((tm, LANE), lambda i, dr: (i, 0)),
                pl.BlockSpec((None, 1, 6 * d), lambda i, dr: (mod_row_of_tile(i, tm), 0, 0)),
                pl.BlockSpec((1, d), lambda i, dr: (0, 0)),
                pl.BlockSpec(memory_space=pl.ANY),
            ],
            out_specs=pl.BlockSpec((tm, d), lambda i, dr: (i, 0)),
            scratch_shapes=[pltpu.VMEM((2, 2, tm, d), F32), pltpu.SemaphoreType.DMA((2,))],
        ),
        out_shape=jax.ShapeDtypeStruct((m, d), F32),
        compiler_params=_cparams(("arbitrary",), 40),
        name="combine",
    )(dest, x, wts, mod, final_g.reshape(1, d), ys)


def _moe(h2, x, mod, final_g, router_w, w_gu, w_down, *, seq_len, mod_row_of_tile, final_norm):
    m, d = x.shape
    n_experts = router_w.shape[1]
    tm = 512
    n_rows = 2 * m + n_experts * tm
    wts, pairs, counts = _router(h2, router_w)
    dest2d, meta2d = _plan(pairs, counts, tile_rows=tm)
    dest = dest2d[0:2].reshape(2 * m)
    meta = meta2d[0:META_ROWS].reshape(META_ROWS * LANE)
    xs = _dispatch(h2, dest, meta, n_rows, tm=tm, n_experts=n_experts)
    ys = _grouped_swiglu(xs, meta, w_gu, w_down, tm=tm)
    return _combine(x, wts, dest, ys, mod, final_g, seq_len=seq_len, mod_row_of_tile=mod_row_of_tile,
                    final_norm=final_norm)


def kernel(x, c, ctx, c_ctx, norm1_g, norm2_g, final_g, w_mod, b_mod, w_in, conv_w, sink, pool_w, pool_scale,
           w_branch, w_out, ffn_w_gu, ffn_w_down, router_w, moe_w_gu, moe_w_down):
    bsz, seq, d = x.shape
    lc = ctx.shape[1]
    depth = w_in.shape[0]
    assert bsz + 1 <= 8 and seq % BLOCK == 0 and lc % BLOCK == 0 and seq % GRID_W == 0

    cvec = jnp.zeros((8, d), F32).at[:bsz].set(c).at[bsz].set(c_ctx)
    mods = _modvec(cvec, w_mod, b_mod)
    rope_tabs = _rope_tables(seq)

    lat_row = lambda i, tm: (i * tm) // seq
    ctx_row = lambda i, tm: bsz

    xl = x.reshape(bsz * seq, d)
    xc = ctx.reshape(bsz * lc, d)
    expert_w = next_w = None
    for l in range(depth):
        last = l == depth - 1
        mod = mods[l].reshape(8, 1, 6 * d)
        if next_w is not None:
            w_in_l, wb_l, wo_l = next_w[0], next_w[1].reshape(w_branch.shape[1:]), next_w[2]
        else:
            w_in_l, wb_l, wo_l = w_in[l].astype(BF16), w_branch[l].astype(BF16), w_out[l].astype(BF16)
        pw_l = pool_w[l].astype(BF16)
        routed = l % 2 == 1
        mixer = functools.partial(_merge, mod=mod, norm2_g=norm2_g[l], conv_w=conv_w[l], pool_w=pw_l,
                                  pool_scale=pool_scale[l], w_branch=wb_l, w_out=wo_l,
                                  h2_dtype=F32 if routed else BF16)
        if routed:
            wgu, wd = expert_w if expert_w is not None else (moe_w_gu[l // 2].astype(BF16),
                                                             moe_w_down[l // 2].astype(BF16))
        else:
            wgu, wd = ffn_w_gu[l // 2].astype(BF16), ffn_w_down[l // 2].astype(BF16)
        ride = not routed and not last
        ride_gu = moe_w_gu[(l + 1) // 2].reshape(-1, moe_w_gu.shape[-1]) if ride else None
        ride_d = moe_w_down[(l + 1) // 2].reshape(-1, d) if ride else None

        def channel_mix(h2, xm, *, seq_len, row_fn, final_norm, casts=()):
            if routed:
                return _moe(h2, xm, mod, final_g, router_w[l // 2], wgu, wd, seq_len=seq_len,
                            mod_row_of_tile=row_fn, final_norm=final_norm)
            assert not final_norm
            return _ffn_dense(h2, xm, mod, wgu, wd, seq_len=seq_len, mod_row_of_tile=row_fn, casts=casts)

        if last:
            kvc = _inproj(xc, norm1_g[l], mod, w_in_l, seq_len=lc, mod_row_of_tile=ctx_row, kv_only=True)
        else:
            qc, kvc, mixc, gatec = _inproj(xc, norm1_g[l], mod, w_in_l, seq_len=lc, mod_row_of_tile=ctx_row)
            attn_c = _attention(qc.reshape(bsz, lc, -1), None, kvc.reshape(bsz, lc, -1), sink[l], band=False)
            xc_mid, h2c = mixer(attn_c.reshape(bsz * lc, -1), mixc, gatec, xc, seq_len=lc, mod_row_of_tile=ctx_row)
            xc_next = channel_mix(h2c, xc_mid, seq_len=lc, row_fn=ctx_row, final_norm=False)
        q, kv, mix, gate = _inproj(xl, norm1_g[l], mod, w_in_l, seq_len=seq, mod_row_of_tile=lat_row,
                                   rope_tabs=rope_tabs)
        attn = _attention(q.reshape(bsz, seq, -1), kv.reshape(bsz, seq, -1), kvc.reshape(bsz, lc, -1), sink[l],
                          band=True, cast=ride_gu)
        merged = mixer((attn[0] if ride else attn).reshape(bsz * seq, -1), mix, gate, xl, seq_len=seq,
                       mod_row_of_tile=lat_row, cast=ride_d)
        x_mid, h2 = merged[0], merged[1]
        if routed or last:
            xl = channel_mix(h2, x_mid, seq_len=seq, row_fn=lat_row, final_norm=last and routed)
            next_w = None
        else:
            stacked = [w_in, w_branch.reshape(depth, -1, d), w_out]
            xl, *next_w = channel_mix(h2, x_mid, seq_len=seq, row_fn=lat_row, final_norm=False,
                                      casts=[(w, l + 1) for w in stacked])
        expert_w = (attn[1].reshape(moe_w_gu.shape[1:]), merged[2].reshape(moe_w_down.shape[1:])) if ride else None
        if not last:
            xc = xc_next
    if depth % 2 == 1:
        raise NotImplementedError("final norm is fused into the expert layer; depth must be even")
    return xl.reshape(bsz, seq, d)
```

```python
import functools

import jax
import jax.numpy as jnp
from jax import lax
from jax.experimental import pallas as pl
from jax.experimental.pallas import tpu as pltpu

F32 = jnp.float32
BF16 = jnp.bfloat16

GRID_W = 64
EPS = 1e-6
NEG_INF = -1e30
HEAD_DIM = 64
N_HEADS = 8
N_KV_HEADS = 2
GROUP = N_HEADS // N_KV_HEADS
WINDOW = 128
BLOCK = 128
ROPE_THETA = 10000.0
BRANCH = 512
POOL_SIZES = (2, 4, 8, 16)
POOL_GROUP = 128
Q_END = 512
V_END = 768
MIX_W = 4 * BRANCH
POOL_END = V_END + MIX_W

LANE = 128
SUBLANE = 8
MXU_DIM = 256
BF16_SUBLANE_TILE = 16
HALO = BF16_SUBLANE_TILE
MIB = 1024 * 1024


def _cparams(sem, vmem_mib):
    return pltpu.CompilerParams(dimension_semantics=sem, vmem_limit_bytes=vmem_mib * MIB)


def _pick_chunk(n, cap):
    for unit in (MXU_DIM, LANE):
        fits = [c for c in range(unit, min(n, cap) + 1, unit) if n % c == 0]
        if fits:
            return fits[-1]
    raise ValueError((n, cap))


def _resident(shape):
    nd = len(shape)
    return pl.BlockSpec(shape, lambda *_: (0,) * nd, pipeline_mode=pl.Buffered(1))


def _norm_mod(x, g, shift, scale):
    y = x * lax.rsqrt(jnp.mean(x * x, axis=-1, keepdims=True) + EPS) * g
    return y * (1.0 + scale) + shift


def _modvec_kernel(c_ref, w_ref, b_ref, o_ref):
    c = c_ref[...]
    s = c * jax.nn.sigmoid(c)
    o_ref[...] = jnp.dot(s, w_ref[...], preferred_element_type=F32) + b_ref[...]


def _modvec(cvec, w_mod, b_mod):
    depth, d, n = w_mod.shape
    nc = _pick_chunk(n, 1536)
    return pl.pallas_call(
        _modvec_kernel,
        grid=(depth, n // nc),
        in_specs=[
            pl.BlockSpec((8, d), lambda l, j: (0, 0)),
            pl.BlockSpec((None, d, nc), lambda l, j: (l, 0, j)),
            pl.BlockSpec((None, 1, nc), lambda l, j: (l, 0, j)),
        ],
        out_specs=pl.BlockSpec((None, 8, nc), lambda l, j: (l, 0, j)),
        out_shape=jax.ShapeDtypeStruct((depth, 8, n), F32),
        compiler_params=_cparams(("arbitrary", "arbitrary"), 32),
        name="modvec",
    )(cvec, w_mod, b_mod.reshape(depth, 1, n))


def _rope_tables(seq_len):
    n_freq = HEAD_DIM // 4
    inv = ROPE_THETA ** (-jnp.arange(n_freq, dtype=F32) / n_freq)
    pos = jnp.arange(seq_len)
    row = (pos // GRID_W).astype(F32)[:, None] * inv[None, :]
    col = (pos % GRID_W).astype(F32)[:, None] * inv[None, :]
    zero = jnp.zeros_like(row)
    cos = jnp.concatenate([jnp.cos(row)] * 2 + [jnp.cos(col)] * 2, axis=-1)
    s_lo = jnp.concatenate([-jnp.sin(row), zero, -jnp.sin(col), zero], axis=-1)
    s_hi = jnp.concatenate([zero, jnp.sin(row), zero, jnp.sin(col)], axis=-1)
    return tuple(jnp.tile(t, (1, LANE // HEAD_DIM)) for t in (cos, s_lo, s_hi))


def _inproj_kernel(*refs, rope, kv_only, d):
    x_ref, g_ref, mod_ref, w_ref = refs[:4]
    refs = refs[4:]
    if rope:
        cos_ref, slo_ref, shi_ref = refs[:3]
        refs = refs[3:]

        def rot(z):
            return (z * cos_ref[...] + pltpu.roll(z, LANE - 16, 1) * slo_ref[...]
                    + pltpu.roll(z, 16, 1) * shi_ref[...])
    else:
        def rot(z):
            return z

    mod = mod_ref[...]
    h = _norm_mod(x_ref[...], g_ref[...], mod[:, 0:d], mod[:, d:2 * d]).astype(BF16)

    def proj(c0, c1):
        return jnp.dot(h, w_ref[:, c0:c1], preferred_element_type=F32)

    if kv_only:
        (kv_ref,) = refs
        z = proj(0, 2 * LANE)
        kv_ref[:, 0:LANE] = rot(z[:, 0:LANE]).astype(BF16)
        kv_ref[:, LANE:] = z[:, LANE:].astype(BF16)
        return

    q_ref, kv_ref, mix_ref, gate_ref = refs
    cw = 512
    n_gate = gate_ref.shape[1]
    for c in range(n_gate // cw):
        zg = proj(POOL_END + c * cw, POOL_END + (c + 1) * cw)
        gate_ref[:, c * cw:(c + 1) * cw] = jax.nn.sigmoid(zg).astype(BF16)
    z = proj(0, Q_END)
    for j in range(Q_END // LANE):
        q_ref[:, j * LANE:(j + 1) * LANE] = (rot(z[:, j * LANE:(j + 1) * LANE]) * HEAD_DIM ** -0.5).astype(BF16)
    z = proj(Q_END, V_END)
    kv_ref[:, 0:LANE] = rot(z[:, 0:LANE]).astype(BF16)
    kv_ref[:, LANE:] = z[:, LANE:].astype(BF16)
    for c in range(MIX_W // cw):
        mix_ref[:, c * cw:(c + 1) * cw] = proj(V_END + c * cw, V_END + (c + 1) * cw).astype(BF16)


def _inproj(x, norm_g, mod, w_in, *, seq_len, mod_row_of_tile, rope_tabs=None, kv_only=False):
    m, d = x.shape
    tm = min(1024, seq_len)
    assert m % tm == 0 and seq_len % tm == 0
    tiles_per_seq = seq_len // tm
    rope = rope_tabs is not None
    in_w = w_in.shape[1]
    in_specs = [
        pl.BlockSpec((tm, d), lambda i: (i, 0)),
        pl.BlockSpec((1, d), lambda i: (0, 0)),
        pl.BlockSpec((None, 1, 6 * d), lambda i: (mod_row_of_tile(i, tm), 0, 0)),
        pl.BlockSpec((d, 2 * LANE), lambda i: (0, Q_END // (2 * LANE))) if kv_only else _resident((d, in_w)),
    ]
    args = [x, norm_g.reshape(1, d), mod, w_in]
    if rope:
        in_specs += [pl.BlockSpec((tm, LANE), lambda i: (i % tiles_per_seq, 0))] * 3
        args += list(rope_tabs)
    if kv_only:
        out_specs = pl.BlockSpec((tm, 2 * LANE), lambda i: (i, 0))
        out_shape = jax.ShapeDtypeStruct((m, 2 * LANE), BF16)
    else:
        widths = (Q_END, 2 * LANE, MIX_W, in_w - POOL_END)
        out_specs = [pl.BlockSpec((tm, w), lambda i: (i, 0)) for w in widths]
        out_shape = [jax.ShapeDtypeStruct((m, w), BF16) for w in widths]
    return pl.pallas_call(
        functools.partial(_inproj_kernel, rope=rope, kv_only=kv_only, d=d),
        grid=(m // tm,),
        in_specs=in_specs,
        out_specs=out_specs,
        out_shape=out_shape,
        compiler_params=_cparams(("parallel",), 56),
        name="inproj_kv" if kv_only else "inproj",
    )(*args)


ATTN_STRIP = 32
ATTN_QBLOCKS = 2
ATTN_AHEAD = 2


def _attn_kernel(sink_ref, q_ref, *refs, band, carry_cast):
    refs, (s_ref, p_ref) = list(refs[:-2]), refs[-2:]
    qblocks = q_ref.shape[1] // BLOCK
    if carry_cast:
        cast_out = refs.pop()
        cast_in = refs.pop(-2)
        cast_out[...] = cast_in[...].astype(BF16)
    if band:
        kvp_ref, kvm_ref, kvn_ref, kvc_ref, bias_ref, o_ref = refs
    else:
        kvc_ref, o_ref = refs
    kvc = kvc_ref[0]
    lc = kvc.shape[0]
    nloc = 3 * BLOCK if band else 0
    nt = (((1,), (1,)), ((), ()))
    w_ctx = jnp.concatenate([kvc[:, LANE:], jnp.ones((lc, LANE), BF16)], axis=1)
    if band:
        n = pl.program_id(1)
        kv4 = jnp.concatenate([kvp_ref[0], kvm_ref[0], kvn_ref[0]], axis=0)
        w4 = jnp.concatenate([kv4[:, LANE:], jnp.ones((kv4.shape[0], LANE), BF16)], axis=1)
        col = lax.broadcasted_iota(jnp.int32, (1, nloc), 1)
        head_edge = jnp.where((col < BLOCK) & (n == 0), NEG_INF, 0.0)
        tail_edge = jnp.where((col >= 2 * BLOCK) & (n == pl.num_programs(1) - 1), NEG_INF, 0.0)
        biases = ([bias_ref[...] + head_edge] + [bias_ref[...]] * (qblocks - 2)
                  + [bias_ref[...] + tail_edge])
    units = [(sb, h) for sb in range(qblocks) for h in range(N_HEADS)]
    def scores(u):
        sb, h = units[u]
        qh = q_ref[0, sb * BLOCK:(sb + 1) * BLOCK, h * HEAD_DIM:(h + 1) * HEAD_DIM]
        ks = slice(h // GROUP * HEAD_DIM, (h // GROUP + 1) * HEAD_DIM)
        if band:
            kl = kv4[sb * BLOCK:sb * BLOCK + nloc, ks]
            s_ref[u, :, 0:nloc] = lax.dot_general(qh, kl, nt, preferred_element_type=F32) + biases[sb]
        s_ref[u, :, nloc:] = lax.dot_general(qh, kvc[:, ks], nt, preferred_element_type=F32)

    def probs(u):
        sink = sink_ref[units[u][1]]
        esink = []
        for r in range(0, BLOCK, ATTN_STRIP):
            s = s_ref[u, r:r + ATTN_STRIP, :]
            m = jnp.maximum(jnp.max(s, axis=-1, keepdims=True), sink)
            p_ref[u, r:r + ATTN_STRIP, :] = jnp.exp(s - m).astype(BF16)
            esink.append(jnp.exp(sink - m))
        return jnp.concatenate(esink, axis=0)

    def weighted_values(u, esink):
        sb, h = units[u]
        ks = slice(h // GROUP * HEAD_DIM, (h // GROUP + 1) * HEAD_DIM)
        o2 = jnp.dot(p_ref[u, :, nloc:], w_ctx, preferred_element_type=F32)
        if band:
            o2 = o2 + jnp.dot(p_ref[u, :, 0:nloc], w4[sb * BLOCK:sb * BLOCK + nloc], preferred_element_type=F32)
        return o2[:, ks] / (o2[:, LANE:LANE + HEAD_DIM] + esink)

    outs, esinks = [], []
    for u in range(min(ATTN_AHEAD, len(units))):
        scores(u)
    for u in range(len(units)):
        if u + ATTN_AHEAD < len(units):
            scores(u + ATTN_AHEAD)
        esinks.append(probs(u))
        if u >= 1:
            outs.append(weighted_values(u - 1, esinks[u - 1]))
    outs.append(weighted_values(len(units) - 1, esinks[-1]))
    for sb in range(qblocks):
        o_ref[0, sb * BLOCK:(sb + 1) * BLOCK, :] = jnp.concatenate(
            outs[sb * N_HEADS:(sb + 1) * N_HEADS], axis=1).astype(BF16)


def _cast_rider(w, n_steps, index_map):
    rows, cols = w.shape
    assert rows % (n_steps * BF16_SUBLANE_TILE) == 0, (w.shape, n_steps)
    return pl.BlockSpec((rows // n_steps, cols), index_map), jax.ShapeDtypeStruct((rows, cols), BF16)


def _attention(q, kv, kvc, sink, *, band, cast=None):
    b, l, _ = q.shape
    lc = kvc.shape[1]
    nb = l // BLOCK
    qb = min(ATTN_QBLOCKS, nb)
    tq = qb * BLOCK
    assert l % tq == 0 and qb >= 2
    in_specs = [
        pl.BlockSpec(memory_space=pltpu.SMEM),
        pl.BlockSpec((1, tq, N_HEADS * HEAD_DIM), lambda bi, n: (bi, n, 0)),
    ]
    args = [sink, q]
    if band:
        in_specs += [
            pl.BlockSpec((1, BLOCK, 2 * LANE), lambda bi, n: (bi, jnp.maximum(qb * n - 1, 0), 0)),
            pl.BlockSpec((1, tq, 2 * LANE), lambda bi, n: (bi, n, 0)),
            pl.BlockSpec((1, BLOCK, 2 * LANE), lambda bi, n: (bi, jnp.minimum(qb * (n + 1), nb - 1), 0)),
        ]
        args += [kv, kv, kv]
    in_specs.append(pl.BlockSpec((1, lc, 2 * LANE), lambda bi, n: (bi, 0, 0)))
    args.append(kvc)
    nkeys = lc
    if band:
        rel = jnp.arange(3 * BLOCK)[None, :] - BLOCK - jnp.arange(BLOCK)[:, None]
        in_specs.append(pl.BlockSpec((BLOCK, 3 * BLOCK), lambda bi, n: (0, 0)))
        args.append(jnp.where(jnp.abs(rel) <= WINDOW, 0.0, NEG_INF).astype(F32))
        nkeys += 3 * BLOCK
    nq = l // tq
    out_specs = [pl.BlockSpec((1, tq, N_HEADS * HEAD_DIM), lambda bi, n: (bi, n, 0))]
    out_shape = [jax.ShapeDtypeStruct((b, l, N_HEADS * HEAD_DIM), BF16)]
    if cast is not None:
        spec, shape = _cast_rider(cast, b * nq, lambda bi, n: (bi * nq + n, 0))
        in_specs.append(spec)
        args.append(cast)
        out_specs.append(spec)
        out_shape.append(shape)
    outs = pl.pallas_call(
        functools.partial(_attn_kernel, band=band, carry_cast=cast is not None),
        grid=(b, nq),
        in_specs=in_specs,
        out_specs=out_specs,
        out_shape=out_shape,
        scratch_shapes=[pltpu.VMEM((qb * N_HEADS, BLOCK, nkeys), F32),
                        pltpu.VMEM((qb * N_HEADS, BLOCK, nkeys), BF16)],
        compiler_params=_cparams(("parallel", "parallel"), 56),
        name="attn_band" if band else "attn_ctx",
    )(*args)
    return outs[0] if cast is None else outs


MERGE_PARTS = 2


def _merge_kernel(attn_ref, mix_ref, prev_ref, next_ref, gate_ref, x_ref, mod_ref, n2g_ref, convw_ref,
                  poolw_ref, pscale_ref, wb_ref, wo_ref, *refs, tm, seq_len, d, carry_cast):
    refs = list(refs)
    if carry_cast:
        cast_out = refs.pop()
        cast_in = refs.pop(0)
        cast_out[...] = cast_in[...].astype(BF16)
    xo_ref, h2_ref = refs[:2]
    tile = pl.program_id(0) % (seq_len // tm)
    keep_prev = (tile != 0).astype(F32)
    keep_next = (tile != seq_len // tm - 1).astype(F32)
    mixm = mix_ref[...]
    prev = prev_ref[...].astype(F32) * keep_prev
    nxt = next_ref[...].astype(F32) * keep_next
    b = BRANCH
    cx, cb, cc = (mixm[:, j * b:(j + 1) * b].astype(F32) for j in range(3))

    p = cc * cx
    p_prev = prev[HALO - 1:HALO, 2 * b:3 * b] * prev[HALO - 1:HALO, 0:b]
    p_next = nxt[0:1, 2 * b:3 * b] * nxt[0:1, 0:b]
    ridx = lax.broadcasted_iota(jnp.int32, (tm, b), 0)
    p_dn = jnp.where(ridx == 0, p_prev, pltpu.roll(p, 1, 0))
    p_up = jnp.where(ridx == tm - 1, p_next, pltpu.roll(p, tm - 1, 0))
    cw = convw_ref[...]
    conv_out = (cb * (p_dn * cw[0:1] + p * cw[1:2] + p_up * cw[2:3])).astype(BF16)

    u_main = mixm[:, 3 * b:4 * b].astype(F32)
    u_ext = jnp.concatenate([prev[:, 3 * b:4 * b], u_main, nxt[:, 3 * b:4 * b]], axis=0)
    ext = tm + 2 * HALO

    def shift(a, s):
        return pltpu.roll(a, s % ext, 0)

    tpos = tile * tm + lax.broadcasted_iota(jnp.int32, (tm, 1), 0)
    pooled = []
    for gi, w in enumerate(POOL_SIZES):
        gs = slice(gi * POOL_GROUP, (gi + 1) * POOL_GROUP)
        ug = u_ext[:, gs]
        a = ug + shift(ug, 1)
        ww = 2
        while ww < w:
            a = shift(a, ww // 2) + shift(a, -(ww // 2))
            ww *= 2
        cnt = jnp.minimum(tpos + w // 2, seq_len) - jnp.maximum(tpos - w // 2, 0)
        dlt = a[HALO:HALO + tm] / cnt.astype(F32) - u_main[:, gs]
        pooled.append(jnp.dot(dlt.astype(BF16), poolw_ref[gi], preferred_element_type=F32))
    pool_out = (jnp.concatenate(pooled, axis=1) * pscale_ref[...]).astype(BF16)

    mod = mod_ref[...]
    parts = [slice(r, r + tm // MERGE_PARTS) for r in range(0, tm, tm // MERGE_PARTS)]

    def branches(rs):
        return (jnp.dot(attn_ref[rs, :], wb_ref[0], preferred_element_type=F32),
                jnp.dot(conv_out[rs], wb_ref[1], preferred_element_type=F32),
                jnp.dot(pool_out[rs], wb_ref[2], preferred_element_type=F32))

    def gated(rs, br):
        return (gate_ref[rs, 0:d].astype(F32) * br[0] + gate_ref[rs, d:2 * d].astype(F32) * br[1]
                + gate_ref[rs, 2 * d:3 * d].astype(F32) * br[2]).astype(BF16)

    def finish(rs, o):
        xn = x_ref[rs, :] + mod[:, 2 * d:3 * d] * o
        xo_ref[rs, :] = xn
        h2_ref[rs, :] = _norm_mod(xn, n2g_ref[...], mod[:, 3 * d:4 * d], mod[:, 4 * d:5 * d]).astype(h2_ref.dtype)

    br = [branches(rs) for rs in parts]
    outs = []
    for k, rs in enumerate(parts):
        outs.append(jnp.dot(gated(rs, br[k]), wo_ref[...], preferred_element_type=F32))
        if k >= 1:
            finish(parts[k - 1], outs[k - 1])
    finish(parts[-1], outs[-1])


def _merge(attn, mix, gate, x, mod, norm2_g, conv_w, pool_w, pool_scale, w_branch, w_out, *, seq_len,
           mod_row_of_tile, h2_dtype, cast=None):
    m, d = x.shape
    tm = min(512, seq_len)
    assert m % tm == 0 and seq_len % tm == 0 and tm % HALO == 0
    hb = tm // HALO
    n_halo = m // HALO
    row = lambda w: pl.BlockSpec((tm, w), lambda i: (i, 0))
    in_specs = [
        row(BRANCH),
        row(MIX_W),
        pl.BlockSpec((HALO, MIX_W), lambda i: (jnp.maximum(i * hb - 1, 0), 0)),
        pl.BlockSpec((HALO, MIX_W), lambda i: (jnp.minimum((i + 1) * hb, n_halo - 1), 0)),
        row(3 * d),
        row(d),
        pl.BlockSpec((None, 1, 6 * d), lambda i: (mod_row_of_tile(i, tm), 0, 0)),
        pl.BlockSpec((1, d), lambda i: (0, 0)),
        _resident(conv_w.shape),
        _resident(pool_w.shape),
        pl.BlockSpec((1, BRANCH), lambda i: (0, 0)),
        _resident(w_branch.shape),
        _resident(w_out.shape),
    ]
    args = [attn, mix, mix, mix, gate, x, mod, norm2_g.reshape(1, d), conv_w, pool_w,
            pool_scale.reshape(1, BRANCH), w_branch, w_out]
    out_specs = [row(d), row(d)]
    out_shape = [jax.ShapeDtypeStruct((m, d), F32), jax.ShapeDtypeStruct((m, d), h2_dtype)]
    if cast is not None:
        spec, shape = _cast_rider(cast, m // tm, lambda i: (i, 0))
        in_specs.append(spec)
        args.append(cast)
        out_specs.append(spec)
        out_shape.append(shape)
    return pl.pallas_call(
        functools.partial(_merge_kernel, tm=tm, seq_len=seq_len, d=d, carry_cast=cast is not None),
        grid=(m // tm,),
        in_specs=in_specs,
        out_specs=out_specs,
        out_shape=out_shape,
        compiler_params=_cparams(("parallel",), 48),
        name="merge",
    )(*args)


def _ffn_kernel(h_ref, x_ref, mod_ref, wgu_ref, wd_ref, *refs, d, d_ff, fc):
    n_cast = len(refs) // 2
    o_ref = refs[n_cast]
    for cast_in, cast_out in zip(refs[:n_cast], refs[n_cast + 1:]):
        cast_out[...] = cast_in[...].astype(BF16)
    h = h_ref[...]
    acc = None
    for f in range(0, d_ff, fc):
        g = jnp.dot(h, wgu_ref[:, f:f + fc], preferred_element_type=F32)
        u = jnp.dot(h, wgu_ref[:, d_ff + f:d_ff + f + fc], preferred_element_type=F32)
        a = (g * jax.nn.sigmoid(g) * u).astype(BF16)
        y = jnp.dot(a, wd_ref[f:f + fc, :], preferred_element_type=F32)
        acc = y if acc is None else acc + y
    o_ref[...] = x_ref[...] + mod_ref[...][:, 5 * d:6 * d] * acc


def _ffn_dense(h2, x, mod, w_gu, w_down, *, seq_len, mod_row_of_tile, casts=()):
    m, d = x.shape
    d_ff = w_down.shape[0]
    tm = min(512, seq_len)
    assert m % tm == 0 and seq_len % tm == 0
    fc = _pick_chunk(d_ff, 3072)
    n_steps = m // tm
    in_specs = [
        pl.BlockSpec((tm, d), lambda i: (i, 0)),
        pl.BlockSpec((tm, d), lambda i: (i, 0)),
        pl.BlockSpec((None, 1, 6 * d), lambda i: (mod_row_of_tile(i, tm), 0, 0)),
        _resident(w_gu.shape),
        _resident(w_down.shape),
    ]
    out_specs = [pl.BlockSpec((tm, d), lambda i: (i, 0))]
    out_shape = [jax.ShapeDtypeStruct((m, d), F32)]
    for w, layer in casts:
        _, rows, cols = w.shape
        assert rows % (n_steps * BF16_SUBLANE_TILE) == 0, (w.shape, n_steps)
        in_specs.append(pl.BlockSpec((None, rows // n_steps, cols), lambda i, layer=layer: (layer, i, 0)))
        out_specs.append(pl.BlockSpec((rows // n_steps, cols), lambda i: (i, 0)))
        out_shape.append(jax.ShapeDtypeStruct((rows, cols), BF16))
    outs = pl.pallas_call(
        functools.partial(_ffn_kernel, d=d, d_ff=d_ff, fc=fc),
        grid=(n_steps,),
        in_specs=in_specs,
        out_specs=out_specs,
        out_shape=out_shape,
        compiler_params=_cparams(("parallel",), 56),
        name="ffn_dense",
    )(h2, x, mod, w_gu, w_down, *[w for w, _ in casts])
    return outs if casts else outs[0]


ROUTE_ROWS = 8
META_ROWS = 4
META_TILES, META_PAD_START, META_PAD_LEN = LANE, 2 * LANE, 3 * LANE


def _route_tile(h, rs, rw_ref, tri_ref, carry_ref, wts_ref, pairs_ref, cnt_ref, *, n_experts):
    w = rw_ref[...]
    h_hi, w_hi = h.astype(BF16), w.astype(BF16)
    h_lo, w_lo = (h - h_hi.astype(F32)).astype(BF16), (w - w_hi.astype(F32)).astype(BF16)
    logits = (jnp.dot(h_hi, w_hi, preferred_element_type=F32) + jnp.dot(h_lo, w_hi, preferred_element_type=F32)
              + jnp.dot(h_hi, w_lo, preferred_element_type=F32))
    lane = lax.broadcasted_iota(jnp.int32, logits.shape, 1).astype(F32)
    logits = jnp.where(lane < n_experts, logits, -jnp.inf)
    m1 = jnp.max(logits, axis=-1, keepdims=True)
    i1 = jnp.min(jnp.where(logits == m1, lane, float(LANE)), axis=-1, keepdims=True)
    rest = jnp.where(lane == i1, -jnp.inf, logits)
    m2 = jnp.max(rest, axis=-1, keepdims=True)
    i2 = jnp.min(jnp.where(rest == m2, lane, float(LANE)), axis=-1, keepdims=True)
    e2 = jnp.exp(m2 - m1)
    den = 1.0 + e2
    wts_ref[rs, :] = jnp.where(lane == 0, 1.0 / den, jnp.where(lane == 1, e2 / den, 0.0))

    m1t = (lane == i1).astype(F32).T[0:ROUTE_ROWS]
    m2t = (lane == i2).astype(F32).T[0:ROUTE_ROWS]
    mem = m1t + m2t
    before = jnp.dot(mem.astype(BF16), tri_ref[...], preferred_element_type=F32) + carry_ref[:, 0:1]
    eid = lax.broadcasted_iota(jnp.int32, mem.shape, 0).astype(F32)
    rows = [jnp.sum(m1t * eid, axis=0, keepdims=True), jnp.sum(m2t * eid, axis=0, keepdims=True),
            jnp.sum(m1t * before, axis=0, keepdims=True), jnp.sum(m2t * before, axis=0, keepdims=True)]
    rows += [jnp.zeros_like(rows[0])] * (ROUTE_ROWS - len(rows))
    pairs_ref[:, rs] = jnp.concatenate(rows, axis=0)
    carry_ref[...] = carry_ref[...] + jnp.sum(mem, axis=1, keepdims=True)
    cnt_ref[...] = carry_ref[...]


def _router_kernel(h_ref, rw_ref, tri_ref, wts_ref, pairs_ref, cnt_ref, carry_ref, *, n_experts):
    @pl.when(pl.program_id(0) == 0)
    def _():
        carry_ref[...] = jnp.zeros_like(carry_ref)

    _route_tile(h_ref[...], slice(None), rw_ref, tri_ref, carry_ref, wts_ref, pairs_ref, cnt_ref, n_experts=n_experts)


def _router(h2, router_w):
    m, d = h2.shape
    n_experts = router_w.shape[1]
    assert n_experts <= ROUTE_ROWS
    rw = jnp.pad(router_w, ((0, 0), (0, LANE - n_experts)))
    tm = min(512, m)
    tri = jnp.triu(jnp.ones((tm, tm), BF16), k=1)
    return pl.pallas_call(
        functools.partial(_router_kernel, n_experts=n_experts),
        grid=(m // tm,),
        in_specs=[
            pl.BlockSpec((tm, d), lambda i: (i, 0)),
            pl.BlockSpec((d, LANE), lambda i: (0, 0)),
            pl.BlockSpec((tm, tm), lambda i: (0, 0)),
        ],
        out_specs=[
            pl.BlockSpec((tm, LANE), lambda i: (i, 0)),
            pl.BlockSpec((ROUTE_ROWS, tm), lambda i: (0, i)),
            pl.BlockSpec((ROUTE_ROWS, LANE), lambda i: (0, 0)),
        ],
        out_shape=[jax.ShapeDtypeStruct((m, LANE), F32), jax.ShapeDtypeStruct((ROUTE_ROWS, m), F32),
                   jax.ShapeDtypeStruct((ROUTE_ROWS, LANE), F32)],
        scratch_shapes=[pltpu.VMEM((ROUTE_ROWS, LANE), F32)],
        compiler_params=_cparams(("arbitrary",), 32),
        name="router",
    )(h2, rw, tri)


def _plan_kernel(pairs_ref, cnt_ref, dest_ref, meta_ref, *, tile_rows):
    cnt = cnt_ref[...]
    padded = jnp.floor((cnt + (tile_rows - 1)) * (1.0 / tile_rows)) * tile_rows
    sub = lax.broadcasted_iota(jnp.int32, cnt.shape, 0)
    lane = lax.broadcasted_iota(jnp.int32, cnt.shape, 1)
    end_row = jnp.sum(jnp.where(sub <= lane, padded, 0.0), axis=0, keepdims=True)
    end_col = jnp.sum(jnp.where(lane == sub, end_row, 0.0), axis=1, keepdims=True)
    start_col = end_col - padded[:, 0:1]
    p = pairs_ref[...]
    eid = lax.broadcasted_iota(jnp.int32, p.shape, 0).astype(F32)
    d1 = jnp.sum(jnp.where(eid == p[0:1], start_col, 0.0), axis=0, keepdims=True) + p[2:3]
    d2 = jnp.sum(jnp.where(eid == p[1:2], start_col, 0.0), axis=0, keepdims=True) + p[3:4]
    dest_ref[...] = jnp.concatenate([d1, d2] + [jnp.zeros_like(d1)] * (ROUTE_ROWS - 2), axis=0).astype(jnp.int32)
    tile_start = (lane * tile_rows).astype(F32)
    tile_e = jnp.sum((end_col <= tile_start).astype(F32), axis=0, keepdims=True)
    n_tiles = jnp.max(end_col, axis=0, keepdims=True) * (1.0 / tile_rows)
    pad_start = jnp.sum(jnp.where(lane == sub, start_col + cnt, 0.0), axis=0, keepdims=True)
    pad_len = jnp.sum(jnp.where(lane == sub, padded - cnt, 0.0), axis=0, keepdims=True)
    meta = jnp.concatenate([tile_e, jnp.broadcast_to(n_tiles, tile_e.shape), pad_start, pad_len]
                           + [jnp.zeros_like(tile_e)] * (ROUTE_ROWS - META_ROWS), axis=0)
    meta_ref[...] = meta.astype(jnp.int32)


def _plan(pairs, counts, *, tile_rows):
    m = pairs.shape[1]
    tp = min(2048, m)
    return pl.pallas_call(
        functools.partial(_plan_kernel, tile_rows=tile_rows),
        grid=(m // tp,),
        in_specs=[pl.BlockSpec((ROUTE_ROWS, tp), lambda i: (0, i)), pl.BlockSpec((ROUTE_ROWS, LANE), lambda i: (0, 0))],
        out_specs=[pl.BlockSpec((ROUTE_ROWS, tp), lambda i: (0, i)), pl.BlockSpec((ROUTE_ROWS, LANE), lambda i: (0, 0))],
        out_shape=[jax.ShapeDtypeStruct((ROUTE_ROWS, m), jnp.int32), jax.ShapeDtypeStruct((ROUTE_ROWS, LANE), jnp.int32)],
        compiler_params=_cparams(("arbitrary",), 32),
        name="plan",
    )(pairs, counts)


ROW_DMA_UNROLL = 8


assert ROW_DMA_UNROLL == SUBLANE


def _row_group(ref, blk):
    return ref.at[pl.ds(pl.multiple_of(blk * ROW_DMA_UNROLL, ROW_DMA_UNROLL), ROW_DMA_UNROLL), :]


def _row_copy(src, src_row, dst, dst_row, sem):
    return pltpu.make_async_copy(src.at[pl.ds(src_row, 1), :], dst.at[pl.ds(dst_row, 1), :], sem)


def _dispatch_kernel(dest_ref, meta_ref, h_ref, xs_ref, zero_ref, sem, zsem, *, tm, n_tok, n_experts, n_tiles_max):
    tile_rows = zero_ref.shape[0]
    i = pl.program_id(0)
    last = pl.num_programs(0) - 1
    base = i * tm

    def wait_rows():
        for _ in range(2):
            pltpu.make_async_copy(h_ref, xs_ref.at[pl.ds(0, tm), :], sem).wait()

    def issue(blk, carry):
        rows = _row_group(h_ref, blk)
        for j in range(ROW_DMA_UNROLL):
            r = blk * ROW_DMA_UNROLL + j
            _row_copy(rows, j, xs_ref, dest_ref[base + r], sem).start(priority=0)
            _row_copy(rows, j, xs_ref, dest_ref[n_tok + base + r], sem).start(priority=1)
        return carry

    lax.fori_loop(0, tm // ROW_DMA_UNROLL, issue, 0)

    @pl.when(i < last)
    def _():
        wait_rows()

    @pl.when(i == last)
    def _():
        zero_ref[...] = jnp.zeros_like(zero_ref)

        def fill_padding(e, start):
            first = meta_ref[META_PAD_START + e]
            n_pad = meta_ref[META_PAD_LEN + e]
            n_single = jnp.minimum((-first) & (SUBLANE - 1), n_pad)

            def single(r, carry):
                cp = _row_copy(zero_ref, 0, xs_ref, first + r, zsem)
                cp.start() if start else cp.wait()
                return carry

            def piece(c, carry):
                off = pl.multiple_of(first + n_single + c * SUBLANE, SUBLANE)
                cp = pltpu.make_async_copy(zero_ref.at[pl.ds(0, SUBLANE), :], xs_ref.at[pl.ds(off, SUBLANE), :], zsem)
                cp.start() if start else cp.wait()
                return carry

            lax.fori_loop(0, n_single, single, 0)
            lax.fori_loop(0, (n_pad - n_single) // SUBLANE, piece, 0)

        def tail_copy(k):
            return pltpu.make_async_copy(zero_ref, xs_ref.at[pl.ds(k * tile_rows, tile_rows), :], zsem)

        def zero_tile(k, carry):
            tail_copy(k).start()
            return carry

        def wait_tile(k, carry):
            tail_copy(k).wait()
            return carry

        for e in range(n_experts):
            fill_padding(e, start=True)
        lax.fori_loop(meta_ref[META_TILES], n_tiles_max, zero_tile, 0)
        for e in range(n_experts):
            fill_padding(e, start=False)
        lax.fori_loop(meta_ref[META_TILES], n_tiles_max, wait_tile, 0)
        wait_rows()


def _dispatch(h2, dest, meta, n_rows, *, tm, n_experts):
    m, d = h2.shape
    tt = min(1024, m)
    return pl.pallas_call(
        functools.partial(_dispatch_kernel, tm=tt, n_tok=m, n_experts=n_experts, n_tiles_max=n_rows // tm),
        grid_spec=pltpu.PrefetchScalarGridSpec(
            num_scalar_prefetch=2,
            grid=(m // tt,),
            in_specs=[pl.BlockSpec((tt, d), lambda i, dr, mt: (i, 0))],
            out_specs=pl.BlockSpec(memory_space=pl.ANY),
            scratch_shapes=[pltpu.VMEM((tm, d), F32), pltpu.SemaphoreType.DMA, pltpu.SemaphoreType.DMA],
        ),
        out_shape=jax.ShapeDtypeStruct((n_rows, d), F32),
        compiler_params=_cparams(("arbitrary",), 32),
        name="dispatch",
    )(dest, meta, h2)


def _gmm_kernel(meta_ref, xs_ref, wg_ref, wu_ref, wd_ref, o_ref, acc_ref):
    f = pl.program_id(1)
    nf = pl.num_programs(1)

    @pl.when((pl.program_id(0) == 0) & (f == 0))
    def _():
        acc_ref[...] = jnp.zeros_like(acc_ref)

    @pl.when(pl.program_id(0) < meta_ref[LANE])
    def _():
        xb = xs_ref[...].astype(BF16)
        g = jnp.dot(xb, wg_ref[...], preferred_element_type=F32)
        u = jnp.dot(xb, wu_ref[...], preferred_element_type=F32)
        a = (g * jax.nn.sigmoid(g) * u).astype(BF16)
        y = jnp.dot(a, wd_ref[...], preferred_element_type=F32)
        total = y + jnp.where(f > 0, acc_ref[...], 0.0)
        acc_ref[...] = total
        o_ref[...] = total

    @pl.when((pl.program_id(0) >= meta_ref[LANE]) & (f == nf - 1))
    def _():
        o_ref[...] = jnp.zeros_like(o_ref)


def _grouped_swiglu(xs, meta, w_gu, w_down, *, tm):
    n_rows, d = xs.shape
    n_experts, d_ff, _ = w_down.shape
    fc = _pick_chunk(d_ff, d_ff // 2)
    nf = d_ff // fc
    assert nf >= 2 and n_rows % tm == 0 and n_rows // tm <= LANE

    def tile(i, mt):
        return jnp.maximum(jnp.minimum(i, mt[LANE] - 1), 0)

    def expert(i, mt):
        return jnp.minimum(mt[tile(i, mt)], n_experts - 1)

    def chunk(i, f, mt):
        return jnp.where(i < mt[LANE], f, nf - 1)

    return pl.pallas_call(
        _gmm_kernel,
        grid_spec=pltpu.PrefetchScalarGridSpec(
            num_scalar_prefetch=1,
            grid=(n_rows // tm, nf),
            in_specs=[
                pl.BlockSpec((tm, d), lambda i, f, mt: (tile(i, mt), 0)),
                pl.BlockSpec((None, d, fc), lambda i, f, mt: (expert(i, mt), 0, chunk(i, f, mt))),
                pl.BlockSpec((None, d, fc), lambda i, f, mt: (expert(i, mt), 0, nf + chunk(i, f, mt))),
                pl.BlockSpec((None, fc, d), lambda i, f, mt: (expert(i, mt), chunk(i, f, mt), 0)),
            ],
            out_specs=pl.BlockSpec((tm, d), lambda i, f, mt: (i, 0)),
            scratch_shapes=[pltpu.VMEM((tm, d), F32)],
        ),
        out_shape=jax.ShapeDtypeStruct((n_rows, d), F32),
        compiler_params=_cparams(("arbitrary", "arbitrary"), 56),
        name="grouped_swiglu",
    )(meta, xs, w_gu, w_gu, w_down)


def _combine_kernel(dest_ref, x_ref, w_ref, mod_ref, fg_ref, ys_ref, o_ref, y_ref, sems, *, tm, n_tok, d,
                    final_norm):
    i = pl.program_id(0)
    slot = i % 2

    def gather(tile, to_slot):
        def issue(blk, carry):
            rows = [_row_group(y_ref.at[to_slot, c], blk) for c in range(2)]
            for j in range(ROW_DMA_UNROLL):
                r = blk * ROW_DMA_UNROLL + j
                for c in range(2):
                    _row_copy(ys_ref, dest_ref[c * n_tok + tile * tm + r], rows[c], j,
                              sems.at[to_slot]).start(priority=c)
            return carry

        lax.fori_loop(0, tm // ROW_DMA_UNROLL, issue, 0)

    @pl.when(i == 0)
    def _():
        gather(0, 0)

    @pl.when(i + 1 < pl.num_programs(0))
    def _():
        gather(i + 1, 1 - slot)

    for c in range(2):
        pltpu.make_async_copy(ys_ref.at[pl.ds(0, tm), :], y_ref.at[slot, c], sems.at[slot]).wait()
    w = w_ref[...]
    moe = w[:, 0:1] * y_ref[slot, 0] + w[:, 1:2] * y_ref[slot, 1]
    xn = x_ref[...] + mod_ref[...][:, 5 * d:6 * d] * moe
    if final_norm:
        xn = xn * lax.rsqrt(jnp.mean(xn * xn, axis=-1, keepdims=True) + EPS) * fg_ref[...]
    o_ref[...] = xn


def _combine(x, wts, dest, ys, mod, final_g, *, seq_len, mod_row_of_tile, final_norm):
    m, d = x.shape
    tm = min(1024, seq_len)
    return pl.pallas_call(
        functools.partial(_combine_kernel, tm=tm, n_tok=m, d=d, final_norm=final_norm),
        grid_spec=pltpu.PrefetchScalarGridSpec(
            num_scalar_prefetch=1,
            grid=(m // tm,),
            in_specs=[
                pl.BlockSpec((tm, d), lambda i, dr: (i, 0)),
                pl.BlockSpec((tm, LANE), lambda i, dr: (i, 0)),
                pl.BlockSpec((None, 1, 6 * d), lambda i, dr: (mod_row_of_tile(i, tm), 0, 0)),
                pl.BlockSpec((1, d), lambda i, dr: (0, 0)),
                pl.BlockSpec(memory_space=pl.ANY),
            ],
            out_specs=pl.BlockSpec((tm, d), lambda i, dr: (i, 0)),
            scratch_shapes=[pltpu.VMEM((2, 2, tm, d), F32), pltpu.SemaphoreType.DMA((2,))],
        ),
        out_shape=jax.ShapeDtypeStruct((m, d), F32),
        compiler_params=_cparams(("arbitrary",), 40),
        name="combine",
    )(dest, x, wts, mod, final_g.reshape(1, d), ys)


def _moe(h2, x, routing, mod, final_g, w_gu, w_down, *, seq_len, mod_row_of_tile, final_norm):
    m, d = x.shape
    n_experts = w_down.shape[0]
    tm = 512
    n_rows = 2 * m + n_experts * tm
    wts, pairs, counts = routing
    dest2d, meta2d = _plan(pairs, counts, tile_rows=tm)
    dest = dest2d[0:2].reshape(2 * m)
    meta = meta2d[0:META_ROWS].reshape(META_ROWS * LANE)
    xs = _dispatch(h2, dest, meta, n_rows, tm=tm, n_experts=n_experts)
    ys = _grouped_swiglu(xs, meta, w_gu, w_down, tm=tm)
    return _combine(x, wts, dest, ys, mod, final_g, seq_len=seq_len, mod_row_of_tile=mod_row_of_tile,
                    final_norm=final_norm)


def kernel(x, c, ctx, c_ctx, norm1_g, norm2_g, final_g, w_mod, b_mod, w_in, conv_w, sink, pool_w, pool_scale,
           w_branch, w_out, ffn_w_gu, ffn_w_down, router_w, moe_w_gu, moe_w_down):
    bsz, seq, d = x.shape
    lc = ctx.shape[1]
    depth = w_in.shape[0]
    assert bsz + 1 <= 8 and seq % BLOCK == 0 and lc % BLOCK == 0 and seq % GRID_W == 0

    cvec = jnp.zeros((8, d), F32).at[:bsz].set(c).at[bsz].set(c_ctx)
    mods = _modvec(cvec, w_mod, b_mod)
    rope_tabs = _rope_tables(seq)

    lat_row = lambda i, tm: (i * tm) // seq
    ctx_row = lambda i, tm: bsz

    xl = x.reshape(bsz * seq, d)
    xc = ctx.reshape(bsz * lc, d)
    expert_w = next_w = None
    for l in range(depth):
        last = l == depth - 1
        mod = mods[l].reshape(8, 1, 6 * d)
        if next_w is not None:
            w_in_l, wb_l, wo_l = next_w[0], next_w[1].reshape(w_branch.shape[1:]), next_w[2]
        else:
            w_in_l, wb_l, wo_l = w_in[l].astype(BF16), w_branch[l].astype(BF16), w_out[l].astype(BF16)
        pw_l = pool_w[l].astype(BF16)
        routed = l % 2 == 1
        mixer = functools.partial(_merge, mod=mod, norm2_g=norm2_g[l], conv_w=conv_w[l], pool_w=pw_l,
                                  pool_scale=pool_scale[l], w_branch=wb_l, w_out=wo_l,
                                  h2_dtype=F32 if routed else BF16)
        if routed:
            wgu, wd = expert_w if expert_w is not None else (moe_w_gu[l // 2].astype(BF16),
                                                             moe_w_down[l // 2].astype(BF16))
        else:
            wgu, wd = ffn_w_gu[l // 2].astype(BF16), ffn_w_down[l // 2].astype(BF16)
        ride = not routed and not last
        ride_gu = moe_w_gu[(l + 1) // 2].reshape(-1, moe_w_gu.shape[-1]) if ride else None
        ride_d = moe_w_down[(l + 1) // 2].reshape(-1, d) if ride else None

        def channel_mix(mixed, *, seq_len, row_fn, final_norm, casts=()):
            xm, h2 = mixed[0], mixed[1]
            if routed:
                return _moe(h2, xm, _router(h2, router_w[l // 2]), mod, final_g, wgu, wd, seq_len=seq_len,
                            mod_row_of_tile=row_fn, final_norm=final_norm)
            assert not final_norm
            return _ffn_dense(h2, xm, mod, wgu, wd, seq_len=seq_len, mod_row_of_tile=row_fn, casts=casts)

        if last:
            kvc = _inproj(xc, norm1_g[l], mod, w_in_l, seq_len=lc, mod_row_of_tile=ctx_row, kv_only=True)
        else:
            qc, kvc, mixc, gatec = _inproj(xc, norm1_g[l], mod, w_in_l, seq_len=lc, mod_row_of_tile=ctx_row)
            attn_c = _attention(qc.reshape(bsz, lc, -1), None, kvc.reshape(bsz, lc, -1), sink[l], band=False)
            mixed_c = mixer(attn_c.reshape(bsz * lc, -1), mixc, gatec, xc, seq_len=lc, mod_row_of_tile=ctx_row)
            xc_next = channel_mix(mixed_c, seq_len=lc, row_fn=ctx_row, final_norm=False)
        q, kv, mix, gate = _inproj(xl, norm1_g[l], mod, w_in_l, seq_len=seq, mod_row_of_tile=lat_row,
                                   rope_tabs=rope_tabs)
        attn = _attention(q.reshape(bsz, seq, -1), kv.reshape(bsz, seq, -1), kvc.reshape(bsz, lc, -1), sink[l],
                          band=True, cast=ride_gu)
        merged = mixer((attn[0] if ride else attn).reshape(bsz * seq, -1), mix, gate, xl, seq_len=seq,
                       mod_row_of_tile=lat_row, cast=ride_d)
        if routed or last:
            xl = channel_mix(merged, seq_len=seq, row_fn=lat_row, final_norm=last and routed)
            next_w = None
        else:
            stacked = [w_in, w_branch.reshape(depth, -1, d), w_out]
            xl, *next_w = channel_mix(merged, seq_len=seq, row_fn=lat_row, final_norm=False,
                                      casts=[(w, l + 1) for w in stacked])
        expert_w = (attn[1].reshape(moe_w_gu.shape[1:]), merged[2].reshape(moe_w_down.shape[1:])) if ride else None
        if not last:
            xc = xc_next
    if depth % 2 == 1:
        raise NotImplementedError("final norm is fused into the expert layer; depth must be even")
    return xl.reshape(bsz, seq, d)
```

```python
import functools

import jax
import jax.numpy as jnp
from jax import lax
from jax.experimental import pallas as pl
from jax.experimental.pallas import tpu as pltpu

F32 = jnp.float32
BF16 = jnp.bfloat16

GRID_W = 64
EPS = 1e-6
NEG_INF = -1e30
HEAD_DIM = 64
N_HEADS = 8
N_KV_HEADS = 2
GROUP = N_HEADS // N_KV_HEADS
WINDOW = 128
BLOCK = 128
ROPE_THETA = 10000.0
BRANCH = 512
POOL_SIZES = (2, 4, 8, 16)
POOL_GROUP = 128
Q_END = 512
V_END = 768
MIX_W = 4 * BRANCH
POOL_END = V_END + MIX_W

LANE = 128
MXU_DIM = 256
BF16_SUBLANE_TILE = 16
HALO = BF16_SUBLANE_TILE
MIB = 1024 * 1024


def _cparams(sem, vmem_mib):
    return pltpu.CompilerParams(dimension_semantics=sem, vmem_limit_bytes=vmem_mib * MIB)


def _pick_chunk(n, cap):
    for unit in (MXU_DIM, LANE):
        fits = [c for c in range(unit, min(n, cap) + 1, unit) if n % c == 0]
        if fits:
            return fits[-1]
    raise ValueError((n, cap))


def _resident(shape):
    nd = len(shape)
    return pl.BlockSpec(shape, lambda *_: (0,) * nd, pipeline_mode=pl.Buffered(1))


def _norm_mod(x, g, shift, scale):
    y = x * lax.rsqrt(jnp.mean(x * x, axis=-1, keepdims=True) + EPS) * g
    return y * (1.0 + scale) + shift


def _modvec_kernel(c_ref, w_ref, b_ref, o_ref):
    c = c_ref[...]
    s = c * jax.nn.sigmoid(c)
    o_ref[...] = jnp.dot(s, w_ref[...], preferred_element_type=F32) + b_ref[...]


def _modvec(cvec, w_mod, b_mod):
    depth, d, n = w_mod.shape
    nc = _pick_chunk(n, 1536)
    return pl.pallas_call(
        _modvec_kernel,
        grid=(depth, n // nc),
        in_specs=[
            pl.BlockSpec((8, d), lambda l, j: (0, 0)),
            pl.BlockSpec((None, d, nc), lambda l, j: (l, 0, j)),
            pl.BlockSpec((None, 1, nc), lambda l, j: (l, 0, j)),
        ],
        out_specs=pl.BlockSpec((None, 8, nc), lambda l, j: (l, 0, j)),
        out_shape=jax.ShapeDtypeStruct((depth, 8, n), F32),
        compiler_params=_cparams(("arbitrary", "arbitrary"), 32),
        name="modvec",
    )(cvec, w_mod, b_mod.reshape(depth, 1, n))


def _rope_tables(seq_len):
    n_freq = HEAD_DIM // 4
    inv = ROPE_THETA ** (-jnp.arange(n_freq, dtype=F32) / n_freq)
    pos = jnp.arange(seq_len)
    row = (pos // GRID_W).astype(F32)[:, None] * inv[None, :]
    col = (pos % GRID_W).astype(F32)[:, None] * inv[None, :]
    zero = jnp.zeros_like(row)
    cos = jnp.concatenate([jnp.cos(row)] * 2 + [jnp.cos(col)] * 2, axis=-1)
    s_lo = jnp.concatenate([-jnp.sin(row), zero, -jnp.sin(col), zero], axis=-1)
    s_hi = jnp.concatenate([zero, jnp.sin(row), zero, jnp.sin(col)], axis=-1)
    return tuple(jnp.tile(t, (1, LANE // HEAD_DIM)) for t in (cos, s_lo, s_hi))


def _inproj_kernel(*refs, rope, kv_only, d):
    x_ref, g_ref, mod_ref, w_ref = refs[:4]
    refs = refs[4:]
    if rope:
        cos_ref, slo_ref, shi_ref = refs[:3]
        refs = refs[3:]

        def rot(z):
            return (z * cos_ref[...] + pltpu.roll(z, LANE - 16, 1) * slo_ref[...]
                    + pltpu.roll(z, 16, 1) * shi_ref[...])
    else:
        def rot(z):
            return z

    mod = mod_ref[...]
    h = _norm_mod(x_ref[...], g_ref[...], mod[:, 0:d], mod[:, d:2 * d]).astype(BF16)

    def proj(c0, c1):
        return jnp.dot(h, w_ref[:, c0:c1], preferred_element_type=F32)

    if kv_only:
        (kv_ref,) = refs
        z = proj(0, 2 * LANE)
        kv_ref[:, 0:LANE] = rot(z[:, 0:LANE]).astype(BF16)
        kv_ref[:, LANE:] = z[:, LANE:].astype(BF16)
        return

    q_ref, kv_ref, mix_ref, gate_ref = refs
    cw = 512
    n_gate = gate_ref.shape[1]
    for c in range(n_gate // cw):
        zg = proj(POOL_END + c * cw, POOL_END + (c + 1) * cw)
        gate_ref[:, c * cw:(c + 1) * cw] = jax.nn.sigmoid(zg).astype(BF16)
    z = proj(0, Q_END)
    for j in range(Q_END // LANE):
        q_ref[:, j * LANE:(j + 1) * LANE] = (rot(z[:, j * LANE:(j + 1) * LANE]) * HEAD_DIM ** -0.5).astype(BF16)
    z = proj(Q_END, V_END)
    kv_ref[:, 0:LANE] = rot(z[:, 0:LANE]).astype(BF16)
    kv_ref[:, LANE:] = z[:, LANE:].astype(BF16)
    for c in range(MIX_W // cw):
        mix_ref[:, c * cw:(c + 1) * cw] = proj(V_END + c * cw, V_END + (c + 1) * cw).astype(BF16)


def _inproj(x, norm_g, mod, w_in, *, seq_len, mod_row_of_tile, rope_tabs=None, kv_only=False):
    m, d = x.shape
    tm = min(1024, seq_len)
    assert m % tm == 0 and seq_len % tm == 0
    tiles_per_seq = seq_len // tm
    rope = rope_tabs is not None
    in_w = w_in.shape[1]
    in_specs = [
        pl.BlockSpec((tm, d), lambda i: (i, 0)),
        pl.BlockSpec((1, d), lambda i: (0, 0)),
        pl.BlockSpec((None, 1, 6 * d), lambda i: (mod_row_of_tile(i, tm), 0, 0)),
        pl.BlockSpec((d, 2 * LANE), lambda i: (0, Q_END // (2 * LANE))) if kv_only else _resident((d, in_w)),
    ]
    args = [x, norm_g.reshape(1, d), mod, w_in]
    if rope:
        in_specs += [pl.BlockSpec((tm, LANE), lambda i: (i % tiles_per_seq, 0))] * 3
        args += list(rope_tabs)
    if kv_only:
        out_specs = pl.BlockSpec((tm, 2 * LANE), lambda i: (i, 0))
        out_shape = jax.ShapeDtypeStruct((m, 2 * LANE), BF16)
    else:
        widths = (Q_END, 2 * LANE, MIX_W, in_w - POOL_END)
        out_specs = [pl.BlockSpec((tm, w), lambda i: (i, 0)) for w in widths]
        out_shape = [jax.ShapeDtypeStruct((m, w), BF16) for w in widths]
    return pl.pallas_call(
        functools.partial(_inproj_kernel, rope=rope, kv_only=kv_only, d=d),
        grid=(m // tm,),
        in_specs=in_specs,
        out_specs=out_specs,
        out_shape=out_shape,
        compiler_params=_cparams(("parallel",), 56),
        name="inproj_kv" if kv_only else "inproj",
    )(*args)


ATTN_STRIP = 32
ATTN_QBLOCKS = 2
ATTN_AHEAD = 2


def _attn_kernel(sink_ref, q_ref, *refs, band, carry_cast):
    refs, (s_ref, p_ref) = list(refs[:-2]), refs[-2:]
    qblocks = q_ref.shape[1] // BLOCK
    if carry_cast:
        cast_out = refs.pop()
        cast_in = refs.pop(-2)
        cast_out[...] = cast_in[...].astype(BF16)
    if band:
        kvp_ref, kvm_ref, kvn_ref, kvc_ref, bias_ref, o_ref = refs
    else:
        kvc_ref, o_ref = refs
    kvc = kvc_ref[0]
    lc = kvc.shape[0]
    nloc = 3 * BLOCK if band else 0
    nt = (((1,), (1,)), ((), ()))
    w_ctx = jnp.concatenate([kvc[:, LANE:], jnp.ones((lc, LANE), BF16)], axis=1)
    if band:
        n = pl.program_id(1)
        kv4 = jnp.concatenate([kvp_ref[0], kvm_ref[0], kvn_ref[0]], axis=0)
        w4 = jnp.concatenate([kv4[:, LANE:], jnp.ones((kv4.shape[0], LANE), BF16)], axis=1)
        col = lax.broadcasted_iota(jnp.int32, (1, nloc), 1)
        head_edge = jnp.where((col < BLOCK) & (n == 0), NEG_INF, 0.0)
        tail_edge = jnp.where((col >= 2 * BLOCK) & (n == pl.num_programs(1) - 1), NEG_INF, 0.0)
        biases = ([bias_ref[...] + head_edge] + [bias_ref[...]] * (qblocks - 2)
                  + [bias_ref[...] + tail_edge])
    units = [(sb, h) for sb in range(qblocks) for h in range(N_HEADS)]
    def scores(u):
        sb, h = units[u]
        qh = q_ref[0, sb * BLOCK:(sb + 1) * BLOCK, h * HEAD_DIM:(h + 1) * HEAD_DIM]
        ks = slice(h // GROUP * HEAD_DIM, (h // GROUP + 1) * HEAD_DIM)
        if band:
            kl = kv4[sb * BLOCK:sb * BLOCK + nloc, ks]
            s_ref[u, :, 0:nloc] = lax.dot_general(qh, kl, nt, preferred_element_type=F32) + biases[sb]
        s_ref[u, :, nloc:] = lax.dot_general(qh, kvc[:, ks], nt, preferred_element_type=F32)

    def probs(u):
        sink = sink_ref[units[u][1]]
        esink = []
        for r in range(0, BLOCK, ATTN_STRIP):
            s = s_ref[u, r:r + ATTN_STRIP, :]
            m = jnp.maximum(jnp.max(s, axis=-1, keepdims=True), sink)
            p_ref[u, r:r + ATTN_STRIP, :] = jnp.exp(s - m).astype(BF16)
            esink.append(jnp.exp(sink - m))
        return jnp.concatenate(esink, axis=0)

    def weighted_values(u, esink):
        sb, h = units[u]
        ks = slice(h // GROUP * HEAD_DIM, (h // GROUP + 1) * HEAD_DIM)
        o2 = jnp.dot(p_ref[u, :, nloc:], w_ctx, preferred_element_type=F32)
        if band:
            o2 = o2 + jnp.dot(p_ref[u, :, 0:nloc], w4[sb * BLOCK:sb * BLOCK + nloc], preferred_element_type=F32)
        return o2[:, ks] / (o2[:, LANE:LANE + HEAD_DIM] + esink)

    outs, esinks = [], []
    for u in range(min(ATTN_AHEAD, len(units))):
        scores(u)
    for u in range(len(units)):
        if u + ATTN_AHEAD < len(units):
            scores(u + ATTN_AHEAD)
        esinks.append(probs(u))
        if u >= 1:
            outs.append(weighted_values(u - 1, esinks[u - 1]))
    outs.append(weighted_values(len(units) - 1, esinks[-1]))
    for sb in range(qblocks):
        o_ref[0, sb * BLOCK:(sb + 1) * BLOCK, :] = jnp.concatenate(
            outs[sb * N_HEADS:(sb + 1) * N_HEADS], axis=1).astype(BF16)


def _cast_rider(w, n_steps, index_map):
    rows, cols = w.shape
    assert rows % (n_steps * BF16_SUBLANE_TILE) == 0, (w.shape, n_steps)
    return pl.BlockSpec((rows // n_steps, cols), index_map), jax.ShapeDtypeStruct((rows, cols), BF16)


def _attention(q, kv, kvc, sink, *, band, cast=None):
    b, l, _ = q.shape
    lc = kvc.shape[1]
    nb = l // BLOCK
    qb = min(ATTN_QBLOCKS, nb)
    tq = qb * BLOCK
    assert l % tq == 0 and qb >= 2
    in_specs = [
        pl.BlockSpec(memory_space=pltpu.SMEM),
        pl.BlockSpec((1, tq, N_HEADS * HEAD_DIM), lambda bi, n: (bi, n, 0)),
    ]
    args = [sink, q]
    if band:
        in_specs += [
            pl.BlockSpec((1, BLOCK, 2 * LANE), lambda bi, n: (bi, jnp.maximum(qb * n - 1, 0), 0)),
            pl.BlockSpec((1, tq, 2 * LANE), lambda bi, n: (bi, n, 0)),
            pl.BlockSpec((1, BLOCK, 2 * LANE), lambda bi, n: (bi, jnp.minimum(qb * (n + 1), nb - 1), 0)),
        ]
        args += [kv, kv, kv]
    in_specs.append(pl.BlockSpec((1, lc, 2 * LANE), lambda bi, n: (bi, 0, 0)))
    args.append(kvc)
    nkeys = lc
    if band:
        rel = jnp.arange(3 * BLOCK)[None, :] - BLOCK - jnp.arange(BLOCK)[:, None]
        in_specs.append(pl.BlockSpec((BLOCK, 3 * BLOCK), lambda bi, n: (0, 0)))
        args.append(jnp.where(jnp.abs(rel) <= WINDOW, 0.0, NEG_INF).astype(F32))
        nkeys += 3 * BLOCK
    nq = l // tq
    out_specs = [pl.BlockSpec((1, tq, N_HEADS * HEAD_DIM), lambda bi, n: (bi, n, 0))]
    out_shape = [jax.ShapeDtypeStruct((b, l, N_HEADS * HEAD_DIM), BF16)]
    if cast is not None:
        spec, shape = _cast_rider(cast, b * nq, lambda bi, n: (bi * nq + n, 0))
        in_specs.append(spec)
        args.append(cast)
        out_specs.append(spec)
        out_shape.append(shape)
    outs = pl.pallas_call(
        functools.partial(_attn_kernel, band=band, carry_cast=cast is not None),
        grid=(b, nq),
        in_specs=in_specs,
        out_specs=out_specs,
        out_shape=out_shape,
        scratch_shapes=[pltpu.VMEM((qb * N_HEADS, BLOCK, nkeys), F32),
                        pltpu.VMEM((qb * N_HEADS, BLOCK, nkeys), BF16)],
        compiler_params=_cparams(("parallel", "parallel"), 56),
        name="attn_band" if band else "attn_ctx",
    )(*args)
    return outs[0] if cast is None else outs


MERGE_PARTS = 2


def _merge_kernel(attn_ref, mix_ref, prev_ref, next_ref, gate_ref, x_ref, mod_ref, n2g_ref, convw_ref,
                  poolw_ref, pscale_ref, wb_ref, wo_ref, *refs, tm, seq_len, d, carry_cast):
    refs = list(refs)
    if carry_cast:
        cast_out = refs.pop()
        cast_in = refs.pop(0)
        cast_out[...] = cast_in[...].astype(BF16)
    xo_ref, h2_ref = refs[:2]
    tile = pl.program_id(0) % (seq_len // tm)
    keep_prev = (tile != 0).astype(F32)
    keep_next = (tile != seq_len // tm - 1).astype(F32)
    mixm = mix_ref[...]
    prev = prev_ref[...].astype(F32) * keep_prev
    nxt = next_ref[...].astype(F32) * keep_next
    b = BRANCH
    cx, cb, cc = (mixm[:, j * b:(j + 1) * b].astype(F32) for j in range(3))

    p = cc * cx
    p_prev = prev[HALO - 1:HALO, 2 * b:3 * b] * prev[HALO - 1:HALO, 0:b]
    p_next = nxt[0:1, 2 * b:3 * b] * nxt[0:1, 0:b]
    ridx = lax.broadcasted_iota(jnp.int32, (tm, b), 0)
    p_dn = jnp.where(ridx == 0, p_prev, pltpu.roll(p, 1, 0))
    p_up = jnp.where(ridx == tm - 1, p_next, pltpu.roll(p, tm - 1, 0))
    cw = convw_ref[...]
    conv_out = (cb * (p_dn * cw[0:1] + p * cw[1:2] + p_up * cw[2:3])).astype(BF16)

    u_main = mixm[:, 3 * b:4 * b].astype(F32)
    u_ext = jnp.concatenate([prev[:, 3 * b:4 * b], u_main, nxt[:, 3 * b:4 * b]], axis=0)
    ext = tm + 2 * HALO

    def shift(a, s):
        return pltpu.roll(a, s % ext, 0)

    tpos = tile * tm + lax.broadcasted_iota(jnp.int32, (tm, 1), 0)
    pooled = []
    for gi, w in enumerate(POOL_SIZES):
        gs = slice(gi * POOL_GROUP, (gi + 1) * POOL_GROUP)
        ug = u_ext[:, gs]
        a = ug + shift(ug, 1)
        ww = 2
        while ww < w:
            a = shift(a, ww // 2) + shift(a, -(ww // 2))
            ww *= 2
        cnt = jnp.minimum(tpos + w // 2, seq_len) - jnp.maximum(tpos - w // 2, 0)
        dlt = a[HALO:HALO + tm] / cnt.astype(F32) - u_main[:, gs]
        pooled.append(jnp.dot(dlt.astype(BF16), poolw_ref[gi], preferred_element_type=F32))
    pool_out = (jnp.concatenate(pooled, axis=1) * pscale_ref[...]).astype(BF16)

    mod = mod_ref[...]
    parts = [slice(r, r + tm // MERGE_PARTS) for r in range(0, tm, tm // MERGE_PARTS)]

    def branches(rs):
        return (jnp.dot(attn_ref[rs, :], wb_ref[0], preferred_element_type=F32),
                jnp.dot(conv_out[rs], wb_ref[1], preferred_element_type=F32),
                jnp.dot(pool_out[rs], wb_ref[2], preferred_element_type=F32))

    def gated(rs, br):
        return (gate_ref[rs, 0:d].astype(F32) * br[0] + gate_ref[rs, d:2 * d].astype(F32) * br[1]
                + gate_ref[rs, 2 * d:3 * d].astype(F32) * br[2]).astype(BF16)

    def finish(rs, o):
        xn = x_ref[rs, :] + mod[:, 2 * d:3 * d] * o
        xo_ref[rs, :] = xn
        h2_ref[rs, :] = _norm_mod(xn, n2g_ref[...], mod[:, 3 * d:4 * d], mod[:, 4 * d:5 * d]).astype(h2_ref.dtype)

    br = [branches(rs) for rs in parts]
    outs = []
    for k, rs in enumerate(parts):
        outs.append(jnp.dot(gated(rs, br[k]), wo_ref[...], preferred_element_type=F32))
        if k >= 1:
            finish(parts[k - 1], outs[k - 1])
    finish(parts[-1], outs[-1])


def _merge(attn, mix, gate, x, mod, norm2_g, conv_w, pool_w, pool_scale, w_branch, w_out, *, seq_len,
           mod_row_of_tile, h2_dtype, cast=None):
    m, d = x.shape
    tm = min(512, seq_len)
    assert m % tm == 0 and seq_len % tm == 0 and tm % HALO == 0
    hb = tm // HALO
    n_halo = m // HALO
    row = lambda w: pl.BlockSpec((tm, w), lambda i: (i, 0))
    in_specs = [
        row(BRANCH),
        row(MIX_W),
        pl.BlockSpec((HALO, MIX_W), lambda i: (jnp.maximum(i * hb - 1, 0), 0)),
        pl.BlockSpec((HALO, MIX_W), lambda i: (jnp.minimum((i + 1) * hb, n_halo - 1), 0)),
        row(3 * d),
        row(d),
        pl.BlockSpec((None, 1, 6 * d), lambda i: (mod_row_of_tile(i, tm), 0, 0)),
        pl.BlockSpec((1, d), lambda i: (0, 0)),
        _resident(conv_w.shape),
        _resident(pool_w.shape),
        pl.BlockSpec((1, BRANCH), lambda i: (0, 0)),
        _resident(w_branch.shape),
        _resident(w_out.shape),
    ]
    args = [attn, mix, mix, mix, gate, x, mod, norm2_g.reshape(1, d), conv_w, pool_w,
            pool_scale.reshape(1, BRANCH), w_branch, w_out]
    out_specs = [row(d), row(d)]
    out_shape = [jax.ShapeDtypeStruct((m, d), F32), jax.ShapeDtypeStruct((m, d), h2_dtype)]
    if cast is not None:
        spec, shape = _cast_rider(cast, m // tm, lambda i: (i, 0))
        in_specs.append(spec)
        args.append(cast)
        out_specs.append(spec)
        out_shape.append(shape)
    return pl.pallas_call(
        functools.partial(_merge_kernel, tm=tm, seq_len=seq_len, d=d, carry_cast=cast is not None),
        grid=(m // tm,),
        in_specs=in_specs,
        out_specs=out_specs,
        out_shape=out_shape,
        compiler_params=_cparams(("parallel",), 48),
        name="merge",
    )(*args)


def _ffn_kernel(h_ref, x_ref, mod_ref, wgu_ref, wd_ref, *refs, d, d_ff, fc):
    n_cast = len(refs) // 2
    o_ref = refs[n_cast]
    for cast_in, cast_out in zip(refs[:n_cast], refs[n_cast + 1:]):
        cast_out[...] = cast_in[...].astype(BF16)
    h = h_ref[...]
    acc = None
    for f in range(0, d_ff, fc):
        g = jnp.dot(h, wgu_ref[:, f:f + fc], preferred_element_type=F32)
        u = jnp.dot(h, wgu_ref[:, d_ff + f:d_ff + f + fc], preferred_element_type=F32)
        a = (g * jax.nn.sigmoid(g) * u).astype(BF16)
        y = jnp.dot(a, wd_ref[f:f + fc, :], preferred_element_type=F32)
        acc = y if acc is None else acc + y
    o_ref[...] = x_ref[...] + mod_ref[...][:, 5 * d:6 * d] * acc


def _ffn_dense(h2, x, mod, w_gu, w_down, *, seq_len, mod_row_of_tile, casts=()):
    m, d = x.shape
    d_ff = w_down.shape[0]
    tm = min(512, seq_len)
    assert m % tm == 0 and seq_len % tm == 0
    fc = _pick_chunk(d_ff, 3072)
    n_steps = m // tm
    in_specs = [
        pl.BlockSpec((tm, d), lambda i: (i, 0)),
        pl.BlockSpec((tm, d), lambda i: (i, 0)),
        pl.BlockSpec((None, 1, 6 * d), lambda i: (mod_row_of_tile(i, tm), 0, 0)),
        _resident(w_gu.shape),
        _resident(w_down.shape),
    ]
    out_specs = [pl.BlockSpec((tm, d), lambda i: (i, 0))]
    out_shape = [jax.ShapeDtypeStruct((m, d), F32)]
    for w, layer in casts:
        _, rows, cols = w.shape
        assert rows % (n_steps * BF16_SUBLANE_TILE) == 0, (w.shape, n_steps)
        in_specs.append(pl.BlockSpec((None, rows // n_steps, cols), lambda i, layer=layer: (layer, i, 0)))
        out_specs.append(pl.BlockSpec((rows // n_steps, cols), lambda i: (i, 0)))
        out_shape.append(jax.ShapeDtypeStruct((rows, cols), BF16))
    outs = pl.pallas_call(
        functools.partial(_ffn_kernel, d=d, d_ff=d_ff, fc=fc),
        grid=(n_steps,),
        in_specs=in_specs,
        out_specs=out_specs,
        out_shape=out_shape,
        compiler_params=_cparams(("parallel",), 56),
        name="ffn_dense",
    )(h2, x, mod, w_gu, w_down, *[w for w, _ in casts])
    return outs if casts else outs[0]


ROUTE_ROWS = 8
RUN_ALIGN = BF16_SUBLANE_TILE
META_ROWS = 4
META_TILES, META_PAD_START, META_PAD_LEN = LANE, 2 * LANE, 3 * LANE


def _router_kernel(h_ref, rw_ref, tri_ref, wts_ref, pairs_ref, runs_ref, cnt_ref, carry_ref, *, n_experts):
    @pl.when(pl.program_id(0) == 0)
    def _():
        carry_ref[...] = jnp.zeros_like(carry_ref)

    h, rs = h_ref[...], slice(None)
    w = rw_ref[...]
    h_hi, w_hi = h.astype(BF16), w.astype(BF16)
    h_lo, w_lo = (h - h_hi.astype(F32)).astype(BF16), (w - w_hi.astype(F32)).astype(BF16)
    logits = (jnp.dot(h_hi, w_hi, preferred_element_type=F32) + jnp.dot(h_lo, w_hi, preferred_element_type=F32)
              + jnp.dot(h_hi, w_lo, preferred_element_type=F32))
    lane = lax.broadcasted_iota(jnp.int32, logits.shape, 1).astype(F32)
    logits = jnp.where(lane < n_experts, logits, -jnp.inf)
    m1 = jnp.max(logits, axis=-1, keepdims=True)
    i1 = jnp.min(jnp.where(logits == m1, lane, float(LANE)), axis=-1, keepdims=True)
    rest = jnp.where(lane == i1, -jnp.inf, logits)
    m2 = jnp.max(rest, axis=-1, keepdims=True)
    i2 = jnp.min(jnp.where(rest == m2, lane, float(LANE)), axis=-1, keepdims=True)
    e2 = jnp.exp(m2 - m1)
    den = 1.0 + e2
    weights = jnp.where(lane == 0, 1.0 / den, jnp.where(lane == 1, e2 / den, 0.0))

    m1t = (lane == i1).astype(F32).T[0:ROUTE_ROWS]
    m2t = (lane == i2).astype(F32).T[0:ROUTE_ROWS]
    mem = m1t + m2t
    local = jnp.dot(mem.astype(BF16), tri_ref[...], preferred_element_type=F32)
    run = jnp.floor((jnp.sum(mem, axis=1, keepdims=True) + (RUN_ALIGN - 1)) * (1.0 / RUN_ALIGN)) * RUN_ALIGN
    sub8 = lax.broadcasted_iota(jnp.int32, (ROUTE_ROWS, LANE), 0)
    lane8 = lax.broadcasted_iota(jnp.int32, (ROUTE_ROWS, LANE), 1)
    incl = jnp.sum(jnp.where(sub8 <= lane8, run, 0.0), axis=0, keepdims=True)
    run_start = jnp.sum(jnp.where(lane8 == sub8, incl, 0.0), axis=1, keepdims=True) - run
    done = carry_ref[:, 0:1]
    eid = lax.broadcasted_iota(jnp.int32, mem.shape, 0).astype(F32)
    pick = lambda sel, val: jnp.sum(sel * val, axis=0, keepdims=True)
    rows = [pick(m1t, eid), pick(m2t, eid), pick(m1t, local + done), pick(m2t, local + done),
            pick(m1t, local + run_start), pick(m2t, local + run_start)]
    rows += [jnp.zeros_like(rows[0])] * (ROUTE_ROWS - len(rows))
    pairs = jnp.concatenate(rows, axis=0)
    pairs_ref[:, rs] = pairs
    by_token = jnp.concatenate([pairs, jnp.zeros((LANE - ROUTE_ROWS, pairs.shape[1]), F32)], axis=0).T
    wts_ref[rs, :] = jnp.where((lane == 4) | (lane == 5), by_token, weights)
    runs_ref[...] = jnp.where(lane8 == 0, run_start, jnp.where(lane8 == 1, run * (1.0 / RUN_ALIGN),
                                                               jnp.where(lane8 == 2, done, 0.0)))
    carry_ref[...] = carry_ref[...] + run
    cnt_ref[...] = carry_ref[...]


def _router(h2, router_w, *, tm):
    m, d = h2.shape
    n_experts = router_w.shape[1]
    assert n_experts <= ROUTE_ROWS and m % tm == 0
    rw = jnp.pad(router_w, ((0, 0), (0, LANE - n_experts)))
    tri = jnp.triu(jnp.ones((tm, tm), BF16), k=1)
    return pl.pallas_call(
        functools.partial(_router_kernel, n_experts=n_experts),
        grid=(m // tm,),
        in_specs=[
            pl.BlockSpec((tm, d), lambda i: (i, 0)),
            pl.BlockSpec((d, LANE), lambda i: (0, 0)),
            pl.BlockSpec((tm, tm), lambda i: (0, 0)),
        ],
        out_specs=[
            pl.BlockSpec((tm, LANE), lambda i: (i, 0)),
            pl.BlockSpec((ROUTE_ROWS, tm), lambda i: (0, i)),
            pl.BlockSpec((ROUTE_ROWS, LANE), lambda i: (i, 0)),
            pl.BlockSpec((ROUTE_ROWS, LANE), lambda i: (0, 0)),
        ],
        out_shape=[jax.ShapeDtypeStruct((m, LANE), F32), jax.ShapeDtypeStruct((ROUTE_ROWS, m), F32),
                   jax.ShapeDtypeStruct((m // tm * ROUTE_ROWS, LANE), F32),
                   jax.ShapeDtypeStruct((ROUTE_ROWS, LANE), F32)],
        scratch_shapes=[pltpu.VMEM((ROUTE_ROWS, LANE), F32)],
        compiler_params=_cparams(("arbitrary",), 32),
        name="router",
    )(h2, rw, tri)


def _plan_kernel(runs_ref, cnt_ref, rundest_ref, meta_ref, *, tile_rows):
    cnt = cnt_ref[...]
    padded = jnp.floor((cnt + (tile_rows - 1)) * (1.0 / tile_rows)) * tile_rows
    sub = lax.broadcasted_iota(jnp.int32, cnt.shape, 0)
    lane = lax.broadcasted_iota(jnp.int32, cnt.shape, 1)
    end_row = jnp.sum(jnp.where(sub <= lane, padded, 0.0), axis=0, keepdims=True)
    end_col = jnp.sum(jnp.where(lane == sub, end_row, 0.0), axis=1, keepdims=True)
    start_col = end_col - padded[:, 0:1]
    runs = runs_ref[...]
    starts = jnp.concatenate([start_col] * (runs.shape[0] // ROUTE_ROWS), axis=0)
    rlane = lax.broadcasted_iota(jnp.int32, runs.shape, 1)
    rundest_ref[...] = jnp.where(rlane == 2, runs + starts, runs).astype(jnp.int32)
    tile_start = (lane * tile_rows).astype(F32)
    tile_e = jnp.sum((end_col <= tile_start).astype(F32), axis=0, keepdims=True)
    n_tiles = jnp.max(end_col, axis=0, keepdims=True) * (1.0 / tile_rows)
    pad_start = jnp.sum(jnp.where(lane == sub, start_col + cnt, 0.0), axis=0, keepdims=True)
    pad_len = jnp.sum(jnp.where(lane == sub, padded - cnt, 0.0), axis=0, keepdims=True)
    meta = jnp.concatenate([tile_e, jnp.broadcast_to(n_tiles, tile_e.shape), pad_start, pad_len]
                           + [jnp.zeros_like(tile_e)] * (ROUTE_ROWS - META_ROWS), axis=0)
    meta_ref[...] = meta.astype(jnp.int32)


def _plan(runs, counts, *, tile_rows):
    small = pl.BlockSpec((ROUTE_ROWS, LANE), lambda i: (0, 0))
    whole = pl.BlockSpec(runs.shape, lambda i: (0, 0))
    return pl.pallas_call(
        functools.partial(_plan_kernel, tile_rows=tile_rows),
        grid=(1,),
        in_specs=[whole, small],
        out_specs=[whole, small],
        out_shape=[jax.ShapeDtypeStruct(runs.shape, jnp.int32), jax.ShapeDtypeStruct((ROUTE_ROWS, LANE), jnp.int32)],
        compiler_params=_cparams(("arbitrary",), 32),
        name="plan",
    )(runs, counts)


def _dispatch_kernel(runs_ref, meta_ref, h_ref, pos_ref, xs_ref, buf_ref, zero_ref, sem, zsem, *, n_experts,
                     n_tiles_max):
    i = pl.program_id(0)
    tile_rows = zero_ref.shape[0]
    pos = pos_ref[...]
    row = lax.broadcasted_iota(jnp.int32, (buf_ref.shape[0], pos.shape[1]), 0).astype(F32)
    take = ((row == pos[4:5]).astype(F32) + (row == pos[5:6]).astype(F32)).astype(BF16)
    buf_ref[...] = jnp.dot(take, h_ref[...].astype(BF16), preferred_element_type=F32).astype(BF16)

    def piece_copy(src, src_row, dst_row, s):
        return pltpu.make_async_copy(src.at[pl.ds(pl.multiple_of(src_row, RUN_ALIGN), RUN_ALIGN), :],
                                     xs_ref.at[pl.ds(pl.multiple_of(dst_row, RUN_ALIGN), RUN_ALIGN), :], s)

    def run_copies(start):
        for e in range(n_experts):
            entry = (i * n_experts + e) * LANE
            src0, dst0 = runs_ref[entry], runs_ref[entry + 2]

            def piece(c, carry):
                cp = piece_copy(buf_ref, src0 + c * RUN_ALIGN, dst0 + c * RUN_ALIGN, sem)
                cp.start() if start else cp.wait()
                return carry

            lax.fori_loop(0, runs_ref[entry + 1], piece, 0)

    run_copies(start=True)
    run_copies(start=False)

    @pl.when(i == pl.num_programs(0) - 1)
    def _():
        zero_ref[...] = jnp.zeros_like(zero_ref)

        def fill_padding(e, start):
            first = meta_ref[META_PAD_START + e]

            def piece(c, carry):
                cp = piece_copy(zero_ref, 0, first + c * RUN_ALIGN, zsem)
                cp.start() if start else cp.wait()
                return carry

            lax.fori_loop(0, meta_ref[META_PAD_LEN + e] // RUN_ALIGN, piece, 0)

        def tail_copy(k):
            return pltpu.make_async_copy(zero_ref, xs_ref.at[pl.ds(k * tile_rows, tile_rows), :], zsem)

        def zero_tile(k, carry):
            tail_copy(k).start()
            return carry

        def wait_tile(k, carry):
            tail_copy(k).wait()
            return carry

        for e in range(n_experts):
            fill_padding(e, start=True)
        lax.fori_loop(meta_ref[META_TILES], n_tiles_max, zero_tile, 0)
        for e in range(n_experts):
            fill_padding(e, start=False)
        lax.fori_loop(meta_ref[META_TILES], n_tiles_max, wait_tile, 0)


def _dispatch(h2, pairs, runs, meta, n_rows, *, tt, tm, n_experts):
    m, d = h2.shape
    buf_rows = 2 * tt + n_experts * RUN_ALIGN
    return pl.pallas_call(
        functools.partial(_dispatch_kernel, n_experts=n_experts, n_tiles_max=n_rows // tm),
        grid_spec=pltpu.PrefetchScalarGridSpec(
            num_scalar_prefetch=2,
            grid=(m // tt,),
            in_specs=[pl.BlockSpec((tt, d), lambda i, rn, mt: (i, 0)),
                      pl.BlockSpec((ROUTE_ROWS, tt), lambda i, rn, mt: (0, i))],
            out_specs=pl.BlockSpec(memory_space=pl.ANY),
            scratch_shapes=[pltpu.VMEM((buf_rows, d), BF16), pltpu.VMEM((tm, d), BF16),
                            pltpu.SemaphoreType.DMA, pltpu.SemaphoreType.DMA],
        ),
        out_shape=jax.ShapeDtypeStruct((n_rows, d), BF16),
        compiler_params=_cparams(("arbitrary",), 40),
        name="dispatch",
    )(runs, meta, h2, pairs)


def _gmm_kernel(meta_ref, xs_ref, wg_ref, wu_ref, wd_ref, o_ref, acc_ref):
    f = pl.program_id(1)
    nf = pl.num_programs(1)

    @pl.when((pl.program_id(0) == 0) & (f == 0))
    def _():
        acc_ref[...] = jnp.zeros_like(acc_ref)

    @pl.when(pl.program_id(0) < meta_ref[LANE])
    def _():
        xb = xs_ref[...].astype(BF16)
        g = jnp.dot(xb, wg_ref[...], preferred_element_type=F32)
        u = jnp.dot(xb, wu_ref[...], preferred_element_type=F32)
        a = (g * jax.nn.sigmoid(g) * u).astype(BF16)
        y = jnp.dot(a, wd_ref[...], preferred_element_type=F32)
        total = y + jnp.where(f > 0, acc_ref[...], 0.0)
        acc_ref[...] = total
        o_ref[...] = total.astype(o_ref.dtype)

    @pl.when((pl.program_id(0) >= meta_ref[LANE]) & (f == nf - 1))
    def _():
        o_ref[...] = jnp.zeros_like(o_ref)


def _grouped_swiglu(xs, meta, w_gu, w_down, *, tm):
    n_rows, d = xs.shape
    n_experts, d_ff, _ = w_down.shape
    fc = _pick_chunk(d_ff, d_ff // 2)
    nf = d_ff // fc
    assert nf >= 2 and n_rows % tm == 0 and n_rows // tm <= LANE

    def tile(i, mt):
        return jnp.maximum(jnp.minimum(i, mt[LANE] - 1), 0)

    def expert(i, mt):
        return jnp.minimum(mt[tile(i, mt)], n_experts - 1)

    def chunk(i, f, mt):
        return jnp.where(i < mt[LANE], f, nf - 1)

    return pl.pallas_call(
        _gmm_kernel,
        grid_spec=pltpu.PrefetchScalarGridSpec(
            num_scalar_prefetch=1,
            grid=(n_rows // tm, nf),
            in_specs=[
                pl.BlockSpec((tm, d), lambda i, f, mt: (tile(i, mt), 0)),
                pl.BlockSpec((None, d, fc), lambda i, f, mt: (expert(i, mt), 0, chunk(i, f, mt))),
                pl.BlockSpec((None, d, fc), lambda i, f, mt: (expert(i, mt), 0, nf + chunk(i, f, mt))),
                pl.BlockSpec((None, fc, d), lambda i, f, mt: (expert(i, mt), chunk(i, f, mt), 0)),
            ],
            out_specs=pl.BlockSpec((tm, d), lambda i, f, mt: (i, 0)),
            scratch_shapes=[pltpu.VMEM((tm, d), F32)],
        ),
        out_shape=jax.ShapeDtypeStruct((n_rows, d), BF16),
        compiler_params=_cparams(("arbitrary", "arbitrary"), 56),
        name="grouped_swiglu",
    )(meta, xs, w_gu, w_gu, w_down)


def _combine_kernel(runs_ref, x_ref, w_ref, mod_ref, fg_ref, ys_ref, o_ref, y_ref, sems, *, n_experts, d,
                    final_norm):
    i = pl.program_id(0)
    slot = i % 2

    @pl.when(i == 0)
    def _():
        y_ref[...] = jnp.zeros_like(y_ref)

    def run_copies(tile, to_slot, start):
        for e in range(n_experts):
            entry = (tile * n_experts + e) * LANE
            loc0, src0 = runs_ref[entry], runs_ref[entry + 2]

            def piece(c, carry):
                cp = pltpu.make_async_copy(
                    ys_ref.at[pl.ds(pl.multiple_of(src0 + c * RUN_ALIGN, RUN_ALIGN), RUN_ALIGN), :],
                    y_ref.at[to_slot, pl.ds(pl.multiple_of(loc0 + c * RUN_ALIGN, RUN_ALIGN), RUN_ALIGN), :],
                    sems.at[to_slot])
                cp.start() if start else cp.wait()
                return carry

            lax.fori_loop(0, runs_ref[entry + 1], piece, 0)

    @pl.when(i == 0)
    def _():
        run_copies(0, 0, start=True)

    @pl.when(i + 1 < pl.num_programs(0))
    def _():
        run_copies(i + 1, 1 - slot, start=True)

    run_copies(i, slot, start=False)
    w = w_ref[...]
    col = lax.broadcasted_iota(jnp.int32, (w.shape[0], y_ref.shape[1]), 1).astype(F32)
    sel = jnp.where(col == w[:, 4:5], w[:, 0:1], 0.0) + jnp.where(col == w[:, 5:6], w[:, 1:2], 0.0)
    moe = jnp.dot(sel.astype(BF16), y_ref[slot], preferred_element_type=F32)
    xn = x_ref[...] + mod_ref[...][:, 5 * d:6 * d] * moe
    if final_norm:
        xn = xn * lax.rsqrt(jnp.mean(xn * xn, axis=-1, keepdims=True) + EPS) * fg_ref[...]
    o_ref[...] = xn


def _combine(x, wts, runs, ys, mod, final_g, *, tt, n_experts, mod_row_of_tile, final_norm):
    m, d = x.shape
    buf_rows = 2 * tt + n_experts * RUN_ALIGN
    return pl.pallas_call(
        functools.partial(_combine_kernel, n_experts=n_experts, d=d, final_norm=final_norm),
        grid_spec=pltpu.PrefetchScalarGridSpec(
            num_scalar_prefetch=1,
            grid=(m // tt,),
            in_specs=[
                pl.BlockSpec((tt, d), lambda i, rn: (i, 0)),
                pl.BlockSpec((tt, LANE), lambda i, rn: (i, 0)),
                pl.BlockSpec((None, 1, 6 * d), lambda i, rn: (mod_row_of_tile(i, tt), 0, 0)),
                pl.BlockSpec((1, d), lambda i, rn: (0, 0)),
                pl.BlockSpec(memory_space=pl.ANY),
            ],
            out_specs=pl.BlockSpec((tt, d), lambda i, rn: (i, 0)),
            scratch_shapes=[pltpu.VMEM((2, buf_rows, d), BF16), pltpu.SemaphoreType.DMA((2,))],
        ),
        out_shape=jax.ShapeDtypeStruct((m, d), F32),
        compiler_params=_cparams(("arbitrary",), 40),
        name="combine",
    )(runs, x, wts, mod, final_g.reshape(1, d), ys)


def _moe(h2, x, router_w, mod, final_g, w_gu, w_down, *, seq_len, mod_row_of_tile, final_norm):
    m, d = x.shape
    n_experts = w_down.shape[0]
    tm = 512
    tt = min(512, m)
    n_rows = pl.cdiv(2 * m + m // tt * n_experts * RUN_ALIGN, tm) * tm + n_experts * tm
    assert n_rows // tm <= LANE
    wts, pairs, runs, counts = _router(h2, router_w, tm=tt)
    rundest, meta2d = _plan(runs, counts, tile_rows=tm)
    run_table = rundest.reshape(-1)
    meta = meta2d[0:META_ROWS].reshape(META_ROWS * LANE)
    xs = _dispatch(h2, pairs, run_table, meta, n_rows, tt=tt, tm=tm, n_experts=n_experts)
    ys = _grouped_swiglu(xs, meta, w_gu, w_down, tm=tm)
    return _combine(x, wts, run_table, ys, mod, final_g, tt=tt, n_experts=n_experts,
                    mod_row_of_tile=mod_row_of_tile, final_norm=final_norm)


def kernel(x, c, ctx, c_ctx, norm1_g, norm2_g, final_g, w_mod, b_mod, w_in, conv_w, sink, pool_w, pool_scale,
           w_branch, w_out, ffn_w_gu, ffn_w_down, router_w, moe_w_gu, moe_w_down):
    bsz, seq, d = x.shape
    lc = ctx.shape[1]
    depth = w_in.shape[0]
    assert bsz + 1 <= 8 and seq % BLOCK == 0 and lc % BLOCK == 0 and seq % GRID_W == 0

    cvec = jnp.zeros((8, d), F32).at[:bsz].set(c).at[bsz].set(c_ctx)
    mods = _modvec(cvec, w_mod, b_mod)
    rope_tabs = _rope_tables(seq)

    lat_row = lambda i, tm: (i * tm) // seq
    ctx_row = lambda i, tm: bsz

    xl = x.reshape(bsz * seq, d)
    xc = ctx.reshape(bsz * lc, d)
    expert_w = next_w = None
    for l in range(depth):
        last = l == depth - 1
        mod = mods[l].reshape(8, 1, 6 * d)
        if next_w is not None:
            w_in_l, wb_l, wo_l = next_w[0], next_w[1].reshape(w_branch.shape[1:]), next_w[2]
        else:
            w_in_l, wb_l, wo_l = w_in[l].astype(BF16), w_branch[l].astype(BF16), w_out[l].astype(BF16)
        pw_l = pool_w[l].astype(BF16)
        routed = l % 2 == 1
        mixer = functools.partial(_merge, mod=mod, norm2_g=norm2_g[l], conv_w=conv_w[l], pool_w=pw_l,
                                  pool_scale=pool_scale[l], w_branch=wb_l, w_out=wo_l,
                                  h2_dtype=F32 if routed else BF16)
        if routed:
            wgu, wd = expert_w if expert_w is not None else (moe_w_gu[l // 2].astype(BF16),
                                                             moe_w_down[l // 2].astype(BF16))
        else:
            wgu, wd = ffn_w_gu[l // 2].astype(BF16), ffn_w_down[l // 2].astype(BF16)
        ride = not routed and not last
        ride_gu = moe_w_gu[(l + 1) // 2].reshape(-1, moe_w_gu.shape[-1]) if ride else None
        ride_d = moe_w_down[(l + 1) // 2].reshape(-1, d) if ride else None

        def channel_mix(mixed, *, seq_len, row_fn, final_norm, casts=()):
            xm, h2 = mixed[0], mixed[1]
            if routed:
                return _moe(h2, xm, router_w[l // 2], mod, final_g, wgu, wd, seq_len=seq_len,
                            mod_row_of_tile=row_fn, final_norm=final_norm)
            assert not final_norm
            return _ffn_dense(h2, xm, mod, wgu, wd, seq_len=seq_len, mod_row_of_tile=row_fn, casts=casts)

        if last:
            kvc = _inproj(xc, norm1_g[l], mod, w_in_l, seq_len=lc, mod_row_of_tile=ctx_row, kv_only=True)
        else:
            qc, kvc, mixc, gatec = _inproj(xc, norm1_g[l], mod, w_in_l, seq_len=lc, mod_row_of_tile=ctx_row)
            attn_c = _attention(qc.reshape(bsz, lc, -1), None, kvc.reshape(bsz, lc, -1), sink[l], band=False)
            mixed_c = mixer(attn_c.reshape(bsz * lc, -1), mixc, gatec, xc, seq_len=lc, mod_row_of_tile=ctx_row)
            xc_next = channel_mix(mixed_c, seq_len=lc, row_fn=ctx_row, final_norm=False)
        q, kv, mix, gate = _inproj(xl, norm1_g[l], mod, w_in_l, seq_len=seq, mod_row_of_tile=lat_row,
                                   rope_tabs=rope_tabs)
        attn = _attention(q.reshape(bsz, seq, -1), kv.reshape(bsz, seq, -1), kvc.reshape(bsz, lc, -1), sink[l],
                          band=True, cast=ride_gu)
        merged = mixer((attn[0] if ride else attn).reshape(bsz * seq, -1), mix, gate, xl, seq_len=seq,
                       mod_row_of_tile=lat_row, cast=ride_d)
        if routed or last:
            xl = channel_mix(merged, seq_len=seq, row_fn=lat_row, final_norm=last and routed)
            next_w = None
        else:
            stacked = [w_in, w_branch.reshape(depth, -1, d), w_out]
            xl, *next_w = channel_mix(merged, seq_len=seq, row_fn=lat_row, final_norm=False,
                                      casts=[(w, l + 1) for w in stacked])
        expert_w = (attn[1].reshape(moe_w_gu.shape[1:]), merged[2].reshape(moe_w_down.shape[1:])) if ride else None
        if not last:
            xc = xc_next
    if depth % 2 == 1:
        raise NotImplementedError("final norm is fused into the expert layer; depth must be even")
    return xl.reshape(bsz, seq, d)
```

```python
import functools

import jax
import jax.numpy as jnp
from jax import lax
from jax.experimental import pallas as pl
from jax.experimental.pallas import tpu as pltpu

F32 = jnp.float32
BF16 = jnp.bfloat16

GRID_W = 64
EPS = 1e-6
NEG_INF = -1e30
HEAD_DIM = 64
N_HEADS = 8
N_KV_HEADS = 2
GROUP = N_HEADS // N_KV_HEADS
WINDOW = 128
BLOCK = 128
ROPE_THETA = 10000.0
BRANCH = 512
POOL_SIZES = (2, 4, 8, 16)
POOL_GROUP = 128
Q_END = 512
V_END = 768
MIX_W = 4 * BRANCH
POOL_END = V_END + MIX_W

LANE = 128
MXU_DIM = 256
BF16_SUBLANE_TILE = 16
HALO = BF16_SUBLANE_TILE
MIB = 1024 * 1024


def _cparams(sem, vmem_mib):
    return pltpu.CompilerParams(dimension_semantics=sem, vmem_limit_bytes=vmem_mib * MIB)


def _pick_chunk(n, cap):
    for unit in (MXU_DIM, LANE):
        fits = [c for c in range(unit, min(n, cap) + 1, unit) if n % c == 0]
        if fits:
            return fits[-1]
    raise ValueError((n, cap))


def _resident(shape):
    nd = len(shape)
    return pl.BlockSpec(shape, lambda *_: (0,) * nd, pipeline_mode=pl.Buffered(1))


def _norm_mod(x, g, shift, scale):
    y = x * lax.rsqrt(jnp.mean(x * x, axis=-1, keepdims=True) + EPS) * g
    return y * (1.0 + scale) + shift


def _modvec_kernel(c_ref, w_ref, b_ref, o_ref):
    c = c_ref[...]
    s = c * jax.nn.sigmoid(c)
    o_ref[...] = jnp.dot(s, w_ref[...], preferred_element_type=F32) + b_ref[...]


def _modvec(cvec, w_mod, b_mod):
    depth, d, n = w_mod.shape
    nc = _pick_chunk(n, 1536)
    return pl.pallas_call(
        _modvec_kernel,
        grid=(depth, n // nc),
        in_specs=[
            pl.BlockSpec((8, d), lambda l, j: (0, 0)),
            pl.BlockSpec((None, d, nc), lambda l, j: (l, 0, j)),
            pl.BlockSpec((None, 1, nc), lambda l, j: (l, 0, j)),
        ],
        out_specs=pl.BlockSpec((None, 8, nc), lambda l, j: (l, 0, j)),
        out_shape=jax.ShapeDtypeStruct((depth, 8, n), F32),
        compiler_params=_cparams(("arbitrary", "arbitrary"), 32),
        name="modvec",
    )(cvec, w_mod, b_mod.reshape(depth, 1, n))


def _rope_tables(seq_len):
    n_freq = HEAD_DIM // 4
    inv = ROPE_THETA ** (-jnp.arange(n_freq, dtype=F32) / n_freq)
    pos = jnp.arange(seq_len)
    row = (pos // GRID_W).astype(F32)[:, None] * inv[None, :]
    col = (pos % GRID_W).astype(F32)[:, None] * inv[None, :]
    zero = jnp.zeros_like(row)
    cos = jnp.concatenate([jnp.cos(row)] * 2 + [jnp.cos(col)] * 2, axis=-1)
    s_lo = jnp.concatenate([-jnp.sin(row), zero, -jnp.sin(col), zero], axis=-1)
    s_hi = jnp.concatenate([zero, jnp.sin(row), zero, jnp.sin(col)], axis=-1)
    return tuple(jnp.tile(t, (1, LANE // HEAD_DIM)) for t in (cos, s_lo, s_hi))


def _inproj_kernel(*refs, rope, kv_only, d):
    x_ref, g_ref, mod_ref, w_ref = refs[:4]
    refs = refs[4:]
    if rope:
        cos_ref, slo_ref, shi_ref = refs[:3]
        refs = refs[3:]

        def rot(z):
            return (z * cos_ref[...] + pltpu.roll(z, LANE - 16, 1) * slo_ref[...]
                    + pltpu.roll(z, 16, 1) * shi_ref[...])
    else:
        def rot(z):
            return z

    mod = mod_ref[...]
    h = _norm_mod(x_ref[...], g_ref[...], mod[:, 0:d], mod[:, d:2 * d]).astype(BF16)

    def proj(c0, c1):
        return jnp.dot(h, w_ref[:, c0:c1], preferred_element_type=F32)

    if kv_only:
        (kv_ref,) = refs
        z = proj(0, 2 * LANE)
        kv_ref[:, 0:LANE] = rot(z[:, 0:LANE]).astype(BF16)
        kv_ref[:, LANE:] = z[:, LANE:].astype(BF16)
        return

    q_ref, kv_ref, mix_ref, gate_ref = refs
    cw = 512
    n_gate = gate_ref.shape[1]
    for c in range(n_gate // cw):
        zg = proj(POOL_END + c * cw, POOL_END + (c + 1) * cw)
        gate_ref[:, c * cw:(c + 1) * cw] = jax.nn.sigmoid(zg).astype(BF16)
    z = proj(0, Q_END)
    for j in range(Q_END // LANE):
        q_ref[:, j * LANE:(j + 1) * LANE] = (rot(z[:, j * LANE:(j + 1) * LANE]) * HEAD_DIM ** -0.5).astype(BF16)
    z = proj(Q_END, V_END)
    kv_ref[:, 0:LANE] = rot(z[:, 0:LANE]).astype(BF16)
    kv_ref[:, LANE:] = z[:, LANE:].astype(BF16)
    for c in range(MIX_W // cw):
        mix_ref[:, c * cw:(c + 1) * cw] = proj(V_END + c * cw, V_END + (c + 1) * cw).astype(BF16)


def _inproj(x, norm_g, mod, w_in, *, seq_len, mod_row_of_tile, rope_tabs=None, kv_only=False):
    m, d = x.shape
    tm = min(1024, seq_len)
    assert m % tm == 0 and seq_len % tm == 0
    tiles_per_seq = seq_len // tm
    rope = rope_tabs is not None
    in_w = w_in.shape[1]
    in_specs = [
        pl.BlockSpec((tm, d), lambda i: (i, 0)),
        pl.BlockSpec((1, d), lambda i: (0, 0)),
        pl.BlockSpec((None, 1, 6 * d), lambda i: (mod_row_of_tile(i, tm), 0, 0)),
        pl.BlockSpec((d, 2 * LANE), lambda i: (0, Q_END // (2 * LANE))) if kv_only else _resident((d, in_w)),
    ]
    args = [x, norm_g.reshape(1, d), mod, w_in]
    if rope:
        in_specs += [pl.BlockSpec((tm, LANE), lambda i: (i % tiles_per_seq, 0))] * 3
        args += list(rope_tabs)
    if kv_only:
        out_specs = pl.BlockSpec((tm, 2 * LANE), lambda i: (i, 0))
        out_shape = jax.ShapeDtypeStruct((m, 2 * LANE), BF16)
    else:
        widths = (Q_END, 2 * LANE, MIX_W, in_w - POOL_END)
        out_specs = [pl.BlockSpec((tm, w), lambda i: (i, 0)) for w in widths]
        out_shape = [jax.ShapeDtypeStruct((m, w), BF16) for w in widths]
    return pl.pallas_call(
        functools.partial(_inproj_kernel, rope=rope, kv_only=kv_only, d=d),
        grid=(m // tm,),
        in_specs=in_specs,
        out_specs=out_specs,
        out_shape=out_shape,
        compiler_params=_cparams(("parallel",), 56),
        name="inproj_kv" if kv_only else "inproj",
    )(*args)


ATTN_STRIP = 32
ATTN_QBLOCKS = 2
ATTN_AHEAD = 2


def _attn_kernel(sink_ref, q_ref, *refs, band, carry_cast):
    refs, (s_ref, p_ref) = list(refs[:-2]), refs[-2:]
    qblocks = q_ref.shape[1] // BLOCK
    if carry_cast:
        cast_out = refs.pop()
        cast_in = refs.pop(-2)
        cast_out[...] = cast_in[...].astype(BF16)
    if band:
        kvp_ref, kvm_ref, kvn_ref, kvc_ref, bias_ref, o_ref = refs
    else:
        kvc_ref, o_ref = refs
    kvc = kvc_ref[0]
    lc = kvc.shape[0]
    nloc = 3 * BLOCK if band else 0
    nt = (((1,), (1,)), ((), ()))
    w_ctx = jnp.concatenate([kvc[:, LANE:], jnp.ones((lc, LANE), BF16)], axis=1)
    if band:
        n = pl.program_id(1)
        kv4 = jnp.concatenate([kvp_ref[0], kvm_ref[0], kvn_ref[0]], axis=0)
        w4 = jnp.concatenate([kv4[:, LANE:], jnp.ones((kv4.shape[0], LANE), BF16)], axis=1)
        col = lax.broadcasted_iota(jnp.int32, (1, nloc), 1)
        head_edge = jnp.where((col < BLOCK) & (n == 0), NEG_INF, 0.0)
        tail_edge = jnp.where((col >= 2 * BLOCK) & (n == pl.num_programs(1) - 1), NEG_INF, 0.0)
        biases = ([bias_ref[...] + head_edge] + [bias_ref[...]] * (qblocks - 2)
                  + [bias_ref[...] + tail_edge])
    units = [(sb, h) for sb in range(qblocks) for h in range(N_HEADS)]
    def scores(u):
        sb, h = units[u]
        qh = q_ref[0, sb * BLOCK:(sb + 1) * BLOCK, h * HEAD_DIM:(h + 1) * HEAD_DIM]
        ks = slice(h // GROUP * HEAD_DIM, (h // GROUP + 1) * HEAD_DIM)
        if band:
            kl = kv4[sb * BLOCK:sb * BLOCK + nloc, ks]
            s_ref[u, :, 0:nloc] = lax.dot_general(qh, kl, nt, preferred_element_type=F32) + biases[sb]
        s_ref[u, :, nloc:] = lax.dot_general(qh, kvc[:, ks], nt, preferred_element_type=F32)

    def probs(u):
        sink = sink_ref[units[u][1]]
        esink = []
        for r in range(0, BLOCK, ATTN_STRIP):
            s = s_ref[u, r:r + ATTN_STRIP, :]
            m = jnp.maximum(jnp.max(s, axis=-1, keepdims=True), sink)
            p_ref[u, r:r + ATTN_STRIP, :] = jnp.exp(s - m).astype(BF16)
            esink.append(jnp.exp(sink - m))
        return jnp.concatenate(esink, axis=0)

    def weighted_values(u, esink):
        sb, h = units[u]
        ks = slice(h // GROUP * HEAD_DIM, (h // GROUP + 1) * HEAD_DIM)
        o2 = jnp.dot(p_ref[u, :, nloc:], w_ctx, preferred_element_type=F32)
        if band:
            o2 = o2 + jnp.dot(p_ref[u, :, 0:nloc], w4[sb * BLOCK:sb * BLOCK + nloc], preferred_element_type=F32)
        return o2[:, ks] / (o2[:, LANE:LANE + HEAD_DIM] + esink)

    outs, esinks = [], []
    for u in range(min(ATTN_AHEAD, len(units))):
        scores(u)
    for u in range(len(units)):
        if u + ATTN_AHEAD < len(units):
            scores(u + ATTN_AHEAD)
        esinks.append(probs(u))
        if u >= 1:
            outs.append(weighted_values(u - 1, esinks[u - 1]))
    outs.append(weighted_values(len(units) - 1, esinks[-1]))
    for sb in range(qblocks):
        o_ref[0, sb * BLOCK:(sb + 1) * BLOCK, :] = jnp.concatenate(
            outs[sb * N_HEADS:(sb + 1) * N_HEADS], axis=1).astype(BF16)


def _cast_rider(w, n_steps, index_map):
    rows, cols = w.shape
    assert rows % (n_steps * BF16_SUBLANE_TILE) == 0, (w.shape, n_steps)
    return pl.BlockSpec((rows // n_steps, cols), index_map), jax.ShapeDtypeStruct((rows, cols), BF16)


def _attention(q, kv, kvc, sink, *, band, cast=None):
    b, l, _ = q.shape
    lc = kvc.shape[1]
    nb = l // BLOCK
    qb = min(ATTN_QBLOCKS, nb)
    tq = qb * BLOCK
    assert l % tq == 0 and qb >= 2
    in_specs = [
        pl.BlockSpec(memory_space=pltpu.SMEM),
        pl.BlockSpec((1, tq, N_HEADS * HEAD_DIM), lambda bi, n: (bi, n, 0)),
    ]
    args = [sink, q]
    if band:
        in_specs += [
            pl.BlockSpec((1, BLOCK, 2 * LANE), lambda bi, n: (bi, jnp.maximum(qb * n - 1, 0), 0)),
            pl.BlockSpec((1, tq, 2 * LANE), lambda bi, n: (bi, n, 0)),
            pl.BlockSpec((1, BLOCK, 2 * LANE), lambda bi, n: (bi, jnp.minimum(qb * (n + 1), nb - 1), 0)),
        ]
        args += [kv, kv, kv]
    in_specs.append(pl.BlockSpec((1, lc, 2 * LANE), lambda bi, n: (bi, 0, 0)))
    args.append(kvc)
    nkeys = lc
    if band:
        rel = jnp.arange(3 * BLOCK)[None, :] - BLOCK - jnp.arange(BLOCK)[:, None]
        in_specs.append(pl.BlockSpec((BLOCK, 3 * BLOCK), lambda bi, n: (0, 0)))
        args.append(jnp.where(jnp.abs(rel) <= WINDOW, 0.0, NEG_INF).astype(F32))
        nkeys += 3 * BLOCK
    nq = l // tq
    out_specs = [pl.BlockSpec((1, tq, N_HEADS * HEAD_DIM), lambda bi, n: (bi, n, 0))]
    out_shape = [jax.ShapeDtypeStruct((b, l, N_HEADS * HEAD_DIM), BF16)]
    if cast is not None:
        spec, shape = _cast_rider(cast, b * nq, lambda bi, n: (bi * nq + n, 0))
        in_specs.append(spec)
        args.append(cast)
        out_specs.append(spec)
        out_shape.append(shape)
    outs = pl.pallas_call(
        functools.partial(_attn_kernel, band=band, carry_cast=cast is not None),
        grid=(b, nq),
        in_specs=in_specs,
        out_specs=out_specs,
        out_shape=out_shape,
        scratch_shapes=[pltpu.VMEM((qb * N_HEADS, BLOCK, nkeys), F32),
                        pltpu.VMEM((qb * N_HEADS, BLOCK, nkeys), BF16)],
        compiler_params=_cparams(("parallel", "parallel"), 56),
        name="attn_band" if band else "attn_ctx",
    )(*args)
    return outs[0] if cast is None else outs


MERGE_PARTS = 2


def _merge_kernel(attn_ref, mix_ref, prev_ref, next_ref, gate_ref, x_ref, mod_ref, n2g_ref, convw_ref,
                  poolw_ref, pscale_ref, wb_ref, wo_ref, *refs, tm, seq_len, d, carry_cast):
    refs = list(refs)
    if carry_cast:
        cast_out = refs.pop()
        cast_in = refs.pop(0)
        cast_out[...] = cast_in[...].astype(BF16)
    xo_ref, h2_ref = refs[:2]
    tile = pl.program_id(0) % (seq_len // tm)
    keep_prev = (tile != 0).astype(F32)
    keep_next = (tile != seq_len // tm - 1).astype(F32)
    mixm = mix_ref[...]
    prev = prev_ref[...].astype(F32) * keep_prev
    nxt = next_ref[...].astype(F32) * keep_next
    b = BRANCH
    cx, cb, cc = (mixm[:, j * b:(j + 1) * b].astype(F32) for j in range(3))

    p = cc * cx
    p_prev = prev[HALO - 1:HALO, 2 * b:3 * b] * prev[HALO - 1:HALO, 0:b]
    p_next = nxt[0:1, 2 * b:3 * b] * nxt[0:1, 0:b]
    ridx = lax.broadcasted_iota(jnp.int32, (tm, b), 0)
    p_dn = jnp.where(ridx == 0, p_prev, pltpu.roll(p, 1, 0))
    p_up = jnp.where(ridx == tm - 1, p_next, pltpu.roll(p, tm - 1, 0))
    cw = convw_ref[...]
    conv_out = (cb * (p_dn * cw[0:1] + p * cw[1:2] + p_up * cw[2:3])).astype(BF16)

    u_main = mixm[:, 3 * b:4 * b].astype(F32)
    u_ext = jnp.concatenate([prev[:, 3 * b:4 * b], u_main, nxt[:, 3 * b:4 * b]], axis=0)
    ext = tm + 2 * HALO

    def shift(a, s):
        return pltpu.roll(a, s % ext, 0)

    tpos = tile * tm + lax.broadcasted_iota(jnp.int32, (tm, 1), 0)
    pooled = []
    for gi, w in enumerate(POOL_SIZES):
        gs = slice(gi * POOL_GROUP, (gi + 1) * POOL_GROUP)
        ug = u_ext[:, gs]
        a = ug + shift(ug, 1)
        ww = 2
        while ww < w:
            a = shift(a, ww // 2) + shift(a, -(ww // 2))
            ww *= 2
        cnt = jnp.minimum(tpos + w // 2, seq_len) - jnp.maximum(tpos - w // 2, 0)
        dlt = a[HALO:HALO + tm] / cnt.astype(F32) - u_main[:, gs]
        pooled.append(jnp.dot(dlt.astype(BF16), poolw_ref[gi], preferred_element_type=F32))
    pool_out = (jnp.concatenate(pooled, axis=1) * pscale_ref[...]).astype(BF16)

    mod = mod_ref[...]
    parts = [slice(r, r + tm // MERGE_PARTS) for r in range(0, tm, tm // MERGE_PARTS)]

    def branches(rs):
        return (jnp.dot(attn_ref[rs, :], wb_ref[0], preferred_element_type=F32),
                jnp.dot(conv_out[rs], wb_ref[1], preferred_element_type=F32),
                jnp.dot(pool_out[rs], wb_ref[2], preferred_element_type=F32))

    def gated(rs, br):
        return (gate_ref[rs, 0:d].astype(F32) * br[0] + gate_ref[rs, d:2 * d].astype(F32) * br[1]
                + gate_ref[rs, 2 * d:3 * d].astype(F32) * br[2]).astype(BF16)

    def finish(rs, o):
        xn = x_ref[rs, :] + mod[:, 2 * d:3 * d] * o
        xo_ref[rs, :] = xn
        h2_ref[rs, :] = _norm_mod(xn, n2g_ref[...], mod[:, 3 * d:4 * d], mod[:, 4 * d:5 * d]).astype(h2_ref.dtype)

    br = [branches(rs) for rs in parts]
    outs = []
    for k, rs in enumerate(parts):
        outs.append(jnp.dot(gated(rs, br[k]), wo_ref[...], preferred_element_type=F32))
        if k >= 1:
            finish(parts[k - 1], outs[k - 1])
    finish(parts[-1], outs[-1])


def _merge(attn, mix, gate, x, mod, norm2_g, conv_w, pool_w, pool_scale, w_branch, w_out, *, seq_len,
           mod_row_of_tile, h2_dtype, cast=None):
    m, d = x.shape
    tm = min(512, seq_len)
    assert m % tm == 0 and seq_len % tm == 0 and tm % HALO == 0
    hb = tm // HALO
    n_halo = m // HALO
    row = lambda w: pl.BlockSpec((tm, w), lambda i: (i, 0))
    in_specs = [
        row(BRANCH),
        row(MIX_W),
        pl.BlockSpec((HALO, MIX_W), lambda i: (jnp.maximum(i * hb - 1, 0), 0)),
        pl.BlockSpec((HALO, MIX_W), lambda i: (jnp.minimum((i + 1) * hb, n_halo - 1), 0)),
        row(3 * d),
        row(d),
        pl.BlockSpec((None, 1, 6 * d), lambda i: (mod_row_of_tile(i, tm), 0, 0)),
        pl.BlockSpec((1, d), lambda i: (0, 0)),
        _resident(conv_w.shape),
        _resident(pool_w.shape),
        pl.BlockSpec((1, BRANCH), lambda i: (0, 0)),
        _resident(w_branch.shape),
        _resident(w_out.shape),
    ]
    args = [attn, mix, mix, mix, gate, x, mod, norm2_g.reshape(1, d), conv_w, pool_w,
            pool_scale.reshape(1, BRANCH), w_branch, w_out]
    out_specs = [row(d), row(d)]
    out_shape = [jax.ShapeDtypeStruct((m, d), F32), jax.ShapeDtypeStruct((m, d), h2_dtype)]
    if cast is not None:
        spec, shape = _cast_rider(cast, m // tm, lambda i: (i, 0))
        in_specs.append(spec)
        args.append(cast)
        out_specs.append(spec)
        out_shape.append(shape)
    return pl.pallas_call(
        functools.partial(_merge_kernel, tm=tm, seq_len=seq_len, d=d, carry_cast=cast is not None),
        grid=(m // tm,),
        in_specs=in_specs,
        out_specs=out_specs,
        out_shape=out_shape,
        compiler_params=_cparams(("parallel",), 48),
        name="merge",
    )(*args)


def _ffn_kernel(h_ref, x_ref, mod_ref, wgu_ref, wd_ref, *refs, d, d_ff, fc):
    n_cast = len(refs) // 2
    o_ref = refs[n_cast]
    for cast_in, cast_out in zip(refs[:n_cast], refs[n_cast + 1:]):
        cast_out[...] = cast_in[...].astype(BF16)
    h = h_ref[...]
    acc = None
    for f in range(0, d_ff, fc):
        g = jnp.dot(h, wgu_ref[:, f:f + fc], preferred_element_type=F32)
        u = jnp.dot(h, wgu_ref[:, d_ff + f:d_ff + f + fc], preferred_element_type=F32)
        a = (g * jax.nn.sigmoid(g) * u).astype(BF16)
        y = jnp.dot(a, wd_ref[f:f + fc, :], preferred_element_type=F32)
        acc = y if acc is None else acc + y
    o_ref[...] = x_ref[...] + mod_ref[...][:, 5 * d:6 * d] * acc


def _ffn_dense(h2, x, mod, w_gu, w_down, *, seq_len, mod_row_of_tile, casts=()):
    m, d = x.shape
    d_ff = w_down.shape[0]
    tm = min(512, seq_len)
    assert m % tm == 0 and seq_len % tm == 0
    fc = _pick_chunk(d_ff, 3072)
    n_steps = m // tm
    in_specs = [
        pl.BlockSpec((tm, d), lambda i: (i, 0)),
        pl.BlockSpec((tm, d), lambda i: (i, 0)),
        pl.BlockSpec((None, 1, 6 * d), lambda i: (mod_row_of_tile(i, tm), 0, 0)),
        _resident(w_gu.shape),
        _resident(w_down.shape),
    ]
    out_specs = [pl.BlockSpec((tm, d), lambda i: (i, 0))]
    out_shape = [jax.ShapeDtypeStruct((m, d), F32)]
    for w, layer in casts:
        _, rows, cols = w.shape
        assert rows % (n_steps * BF16_SUBLANE_TILE) == 0, (w.shape, n_steps)
        in_specs.append(pl.BlockSpec((None, rows // n_steps, cols), lambda i, layer=layer: (layer, i, 0)))
        out_specs.append(pl.BlockSpec((rows // n_steps, cols), lambda i: (i, 0)))
        out_shape.append(jax.ShapeDtypeStruct((rows, cols), BF16))
    outs = pl.pallas_call(
        functools.partial(_ffn_kernel, d=d, d_ff=d_ff, fc=fc),
        grid=(n_steps,),
        in_specs=in_specs,
        out_specs=out_specs,
        out_shape=out_shape,
        compiler_params=_cparams(("parallel",), 56),
        name="ffn_dense",
    )(h2, x, mod, w_gu, w_down, *[w for w, _ in casts])
    return outs if casts else outs[0]


ROUTE_ROWS = 8
RUN_ALIGN = BF16_SUBLANE_TILE
META_ROWS = 4
META_TILES, META_PAD_START, META_PAD_LEN = LANE, 2 * LANE, 3 * LANE


def _router_kernel(h_ref, rw_ref, tri_ref, wts_ref, pairs_ref, runs_ref, cnt_ref, carry_ref, *, n_experts):
    @pl.when(pl.program_id(0) == 0)
    def _():
        carry_ref[...] = jnp.zeros_like(carry_ref)

    h, rs = h_ref[...], slice(None)
    w = rw_ref[...]
    h_hi, w_hi = h.astype(BF16), w.astype(BF16)
    h_lo, w_lo = (h - h_hi.astype(F32)).astype(BF16), (w - w_hi.astype(F32)).astype(BF16)
    logits = (jnp.dot(h_hi, w_hi, preferred_element_type=F32) + jnp.dot(h_lo, w_hi, preferred_element_type=F32)
              + jnp.dot(h_hi, w_lo, preferred_element_type=F32))
    lane = lax.broadcasted_iota(jnp.int32, logits.shape, 1).astype(F32)
    logits = jnp.where(lane < n_experts, logits, -jnp.inf)
    m1 = jnp.max(logits, axis=-1, keepdims=True)
    i1 = jnp.min(jnp.where(logits == m1, lane, float(LANE)), axis=-1, keepdims=True)
    rest = jnp.where(lane == i1, -jnp.inf, logits)
    m2 = jnp.max(rest, axis=-1, keepdims=True)
    i2 = jnp.min(jnp.where(rest == m2, lane, float(LANE)), axis=-1, keepdims=True)
    e2 = jnp.exp(m2 - m1)
    den = 1.0 + e2
    weights = jnp.where(lane == 0, 1.0 / den, jnp.where(lane == 1, e2 / den, 0.0))

    m1t = (lane == i1).astype(F32).T[0:ROUTE_ROWS]
    m2t = (lane == i2).astype(F32).T[0:ROUTE_ROWS]
    mem = m1t + m2t
    local = jnp.dot(mem.astype(BF16), tri_ref[...], preferred_element_type=F32)
    run = jnp.floor((jnp.sum(mem, axis=1, keepdims=True) + (RUN_ALIGN - 1)) * (1.0 / RUN_ALIGN)) * RUN_ALIGN
    sub8 = lax.broadcasted_iota(jnp.int32, (ROUTE_ROWS, LANE), 0)
    lane8 = lax.broadcasted_iota(jnp.int32, (ROUTE_ROWS, LANE), 1)
    incl = jnp.sum(jnp.where(sub8 <= lane8, run, 0.0), axis=0, keepdims=True)
    run_start = jnp.sum(jnp.where(lane8 == sub8, incl, 0.0), axis=1, keepdims=True) - run
    done = carry_ref[:, 0:1]
    eid = lax.broadcasted_iota(jnp.int32, mem.shape, 0).astype(F32)
    pick = lambda sel, val: jnp.sum(sel * val, axis=0, keepdims=True)
    rows = [pick(m1t, eid), pick(m2t, eid), pick(m1t, local + done), pick(m2t, local + done),
            pick(m1t, local + run_start), pick(m2t, local + run_start)]
    rows += [jnp.zeros_like(rows[0])] * (ROUTE_ROWS - len(rows))
    pairs = jnp.concatenate(rows, axis=0)
    pairs_ref[:, rs] = pairs
    by_token = jnp.concatenate([pairs, jnp.zeros((LANE - ROUTE_ROWS, pairs.shape[1]), F32)], axis=0).T
    wts_ref[rs, :] = jnp.where((lane == 4) | (lane == 5), by_token, weights)
    runs_ref[...] = jnp.where(lane8 == 0, run_start, jnp.where(lane8 == 1, run * (1.0 / RUN_ALIGN),
                                                               jnp.where(lane8 == 2, done, 0.0)))
    carry_ref[...] = carry_ref[...] + run
    cnt_ref[...] = carry_ref[...]


def _router(h2, router_w, *, tm):
    m, d = h2.shape
    n_experts = router_w.shape[1]
    assert n_experts <= ROUTE_ROWS and m % tm == 0
    rw = jnp.pad(router_w, ((0, 0), (0, LANE - n_experts)))
    tri = jnp.triu(jnp.ones((tm, tm), BF16), k=1)
    return pl.pallas_call(
        functools.partial(_router_kernel, n_experts=n_experts),
        grid=(m // tm,),
        in_specs=[
            pl.BlockSpec((tm, d), lambda i: (i, 0)),
            pl.BlockSpec((d, LANE), lambda i: (0, 0)),
            pl.BlockSpec((tm, tm), lambda i: (0, 0)),
        ],
        out_specs=[
            pl.BlockSpec((tm, LANE), lambda i: (i, 0)),
            pl.BlockSpec((ROUTE_ROWS, tm), lambda i: (0, i)),
            pl.BlockSpec((ROUTE_ROWS, LANE), lambda i: (i, 0)),
            pl.BlockSpec((ROUTE_ROWS, LANE), lambda i: (0, 0)),
        ],
        out_shape=[jax.ShapeDtypeStruct((m, LANE), F32), jax.ShapeDtypeStruct((ROUTE_ROWS, m), F32),
                   jax.ShapeDtypeStruct((m // tm * ROUTE_ROWS, LANE), F32),
                   jax.ShapeDtypeStruct((ROUTE_ROWS, LANE), F32)],
        scratch_shapes=[pltpu.VMEM((ROUTE_ROWS, LANE), F32)],
        compiler_params=_cparams(("arbitrary",), 32),
        name="router",
    )(h2, rw, tri)


def _plan_kernel(runs_ref, cnt_ref, rundest_ref, meta_ref, *, tile_rows):
    cnt = cnt_ref[...]
    padded = jnp.floor((cnt + (tile_rows - 1)) * (1.0 / tile_rows)) * tile_rows
    sub = lax.broadcasted_iota(jnp.int32, cnt.shape, 0)
    lane = lax.broadcasted_iota(jnp.int32, cnt.shape, 1)
    end_row = jnp.sum(jnp.where(sub <= lane, padded, 0.0), axis=0, keepdims=True)
    end_col = jnp.sum(jnp.where(lane == sub, end_row, 0.0), axis=1, keepdims=True)
    start_col = end_col - padded[:, 0:1]
    runs = runs_ref[...]
    starts = jnp.concatenate([start_col] * (runs.shape[0] // ROUTE_ROWS), axis=0)
    rlane = lax.broadcasted_iota(jnp.int32, runs.shape, 1)
    rundest_ref[...] = jnp.where(rlane == 2, runs + starts, runs).astype(jnp.int32)
    tile_start = (lane * tile_rows).astype(F32)
    tile_e = jnp.sum((end_col <= tile_start).astype(F32), axis=0, keepdims=True)
    n_tiles = jnp.max(end_col, axis=0, keepdims=True) * (1.0 / tile_rows)
    pad_start = jnp.sum(jnp.where(lane == sub, start_col + cnt, 0.0), axis=0, keepdims=True)
    pad_len = jnp.sum(jnp.where(lane == sub, padded - cnt, 0.0), axis=0, keepdims=True)
    meta = jnp.concatenate([tile_e, jnp.broadcast_to(n_tiles, tile_e.shape), pad_start, pad_len]
                           + [jnp.zeros_like(tile_e)] * (ROUTE_ROWS - META_ROWS), axis=0)
    meta_ref[...] = meta.astype(jnp.int32)


def _plan(runs, counts, *, tile_rows):
    small = pl.BlockSpec((ROUTE_ROWS, LANE), lambda i: (0, 0))
    whole = pl.BlockSpec(runs.shape, lambda i: (0, 0))
    return pl.pallas_call(
        functools.partial(_plan_kernel, tile_rows=tile_rows),
        grid=(1,),
        in_specs=[whole, small],
        out_specs=[whole, small],
        out_shape=[jax.ShapeDtypeStruct(runs.shape, jnp.int32), jax.ShapeDtypeStruct((ROUTE_ROWS, LANE), jnp.int32)],
        compiler_params=_cparams(("arbitrary",), 32),
        name="plan",
    )(runs, counts)


def _run_rows(runs_ref, tile, n_experts):
    pieces = runs_ref[tile * n_experts * LANE + 1]
    for e in range(1, n_experts):
        pieces = pieces + runs_ref[(tile * n_experts + e) * LANE + 1]
    return pl.multiple_of(pieces * RUN_ALIGN, RUN_ALIGN)


def _dispatch_kernel(runs_ref, meta_ref, h_ref, pos_ref, xs_ref, buf_ref, zero_ref, sems, zsem, *, n_experts,
                     n_tiles_max):
    i = pl.program_id(0)
    tile_rows = zero_ref.shape[0]
    slot = i % 2
    buf = buf_ref.at[slot]
    pos = pos_ref[...]
    row = lax.broadcasted_iota(jnp.int32, (buf.shape[0], pos.shape[1]), 0).astype(F32)
    take = ((row == pos[4:5]).astype(F32) + (row == pos[5:6]).astype(F32)).astype(BF16)
    buf[...] = jnp.dot(take, h_ref[...].astype(BF16), preferred_element_type=F32).astype(BF16)

    def piece_copy(src, src_row, dst_row, s):
        return pltpu.make_async_copy(src.at[pl.ds(pl.multiple_of(src_row, RUN_ALIGN), RUN_ALIGN), :],
                                     xs_ref.at[pl.ds(pl.multiple_of(dst_row, RUN_ALIGN), RUN_ALIGN), :], s)

    for e in range(n_experts):
        entry = (i * n_experts + e) * LANE
        src0, dst0 = runs_ref[entry], runs_ref[entry + 2]

        def piece(c, carry):
            piece_copy(buf, src0 + c * RUN_ALIGN, dst0 + c * RUN_ALIGN, sems.at[slot]).start()
            return carry

        lax.fori_loop(0, runs_ref[entry + 1], piece, 0)

    def wait_runs(tile, sl):
        rows = _run_rows(runs_ref, tile, n_experts)
        pltpu.make_async_copy(buf_ref.at[sl, pl.ds(0, rows), :], xs_ref.at[pl.ds(0, rows), :], sems.at[sl]).wait()

    @pl.when(i > 0)
    def _():
        wait_runs(i - 1, 1 - slot)

    @pl.when(i == pl.num_programs(0) - 1)
    def _():
        wait_runs(i, slot)
        zero_ref[...] = jnp.zeros_like(zero_ref)

        def fill_padding(e, start):
            first = meta_ref[META_PAD_START + e]

            def piece(c, carry):
                cp = piece_copy(zero_ref, 0, first + c * RUN_ALIGN, zsem)
                cp.start() if start else cp.wait()
                return carry

            lax.fori_loop(0, meta_ref[META_PAD_LEN + e] // RUN_ALIGN, piece, 0)

        def tail_copy(k):
            return pltpu.make_async_copy(zero_ref, xs_ref.at[pl.ds(k * tile_rows, tile_rows), :], zsem)

        def zero_tile(k, carry):
            tail_copy(k).start()
            return carry

        def wait_tile(k, carry):
            tail_copy(k).wait()
            return carry

        for e in range(n_experts):
            fill_padding(e, start=True)
        lax.fori_loop(meta_ref[META_TILES], n_tiles_max, zero_tile, 0)
        for e in range(n_experts):
            fill_padding(e, start=False)
        lax.fori_loop(meta_ref[META_TILES], n_tiles_max, wait_tile, 0)


def _dispatch(h2, pairs, runs, meta, n_rows, *, tt, tm, n_experts):
    m, d = h2.shape
    buf_rows = 2 * tt + n_experts * RUN_ALIGN
    return pl.pallas_call(
        functools.partial(_dispatch_kernel, n_experts=n_experts, n_tiles_max=n_rows // tm),
        grid_spec=pltpu.PrefetchScalarGridSpec(
            num_scalar_prefetch=2,
            grid=(m // tt,),
            in_specs=[pl.BlockSpec((tt, d), lambda i, rn, mt: (i, 0)),
                      pl.BlockSpec((ROUTE_ROWS, tt), lambda i, rn, mt: (0, i))],
            out_specs=pl.BlockSpec(memory_space=pl.ANY),
            scratch_shapes=[pltpu.VMEM((2, buf_rows, d), BF16), pltpu.VMEM((tm, d), BF16),
                            pltpu.SemaphoreType.DMA((2,)), pltpu.SemaphoreType.DMA],
        ),
        out_shape=jax.ShapeDtypeStruct((n_rows, d), BF16),
        compiler_params=_cparams(("arbitrary",), 40),
        name="dispatch",
    )(runs, meta, h2, pairs)


def _gmm_kernel(meta_ref, xs_ref, wg_ref, wu_ref, wd_ref, o_ref, acc_ref):
    f = pl.program_id(1)
    nf = pl.num_programs(1)

    @pl.when((pl.program_id(0) == 0) & (f == 0))
    def _():
        acc_ref[...] = jnp.zeros_like(acc_ref)

    @pl.when(pl.program_id(0) < meta_ref[LANE])
    def _():
        xb = xs_ref[...].astype(BF16)
        g = jnp.dot(xb, wg_ref[...], preferred_element_type=F32)
        u = jnp.dot(xb, wu_ref[...], preferred_element_type=F32)
        a = (g * jax.nn.sigmoid(g) * u).astype(BF16)
        y = jnp.dot(a, wd_ref[...], preferred_element_type=F32)
        total = y + jnp.where(f > 0, acc_ref[...], 0.0)
        acc_ref[...] = total
        o_ref[...] = total.astype(o_ref.dtype)

    @pl.when((pl.program_id(0) >= meta_ref[LANE]) & (f == nf - 1))
    def _():
        o_ref[...] = jnp.zeros_like(o_ref)


def _grouped_swiglu(xs, meta, w_gu, w_down, *, tm):
    n_rows, d = xs.shape
    n_experts, d_ff, _ = w_down.shape
    fc = _pick_chunk(d_ff, d_ff // 2)
    nf = d_ff // fc
    assert nf >= 2 and n_rows % tm == 0 and n_rows // tm <= LANE

    def tile(i, mt):
        return jnp.maximum(jnp.minimum(i, mt[LANE] - 1), 0)

    def expert(i, mt):
        return jnp.minimum(mt[tile(i, mt)], n_experts - 1)

    def chunk(i, f, mt):
        return jnp.where(i < mt[LANE], f, nf - 1)

    return pl.pallas_call(
        _gmm_kernel,
        grid_spec=pltpu.PrefetchScalarGridSpec(
            num_scalar_prefetch=1,
            grid=(n_rows // tm, nf),
            in_specs=[
                pl.BlockSpec((tm, d), lambda i, f, mt: (tile(i, mt), 0)),
                pl.BlockSpec((None, d, fc), lambda i, f, mt: (expert(i, mt), 0, chunk(i, f, mt))),
                pl.BlockSpec((None, d, fc), lambda i, f, mt: (expert(i, mt), 0, nf + chunk(i, f, mt))),
                pl.BlockSpec((None, fc, d), lambda i, f, mt: (expert(i, mt), chunk(i, f, mt), 0)),
            ],
            out_specs=pl.BlockSpec((tm, d), lambda i, f, mt: (i, 0)),
            scratch_shapes=[pltpu.VMEM((tm, d), F32)],
        ),
        out_shape=jax.ShapeDtypeStruct((n_rows, d), BF16),
        compiler_params=_cparams(("arbitrary", "arbitrary"), 56),
        name="grouped_swiglu",
    )(meta, xs, w_gu, w_gu, w_down)


def _combine_kernel(runs_ref, x_ref, w_ref, mod_ref, fg_ref, ys_ref, o_ref, y_ref, sems, *, n_experts, d,
                    final_norm):
    i = pl.program_id(0)
    slot = i % 2

    @pl.when(i == 0)
    def _():
        y_ref[...] = jnp.zeros_like(y_ref)

    def fetch_runs(tile, to_slot):
        for e in range(n_experts):
            entry = (tile * n_experts + e) * LANE
            loc0, src0 = runs_ref[entry], runs_ref[entry + 2]

            def piece(c, carry):
                pltpu.make_async_copy(
                    ys_ref.at[pl.ds(pl.multiple_of(src0 + c * RUN_ALIGN, RUN_ALIGN), RUN_ALIGN), :],
                    y_ref.at[to_slot, pl.ds(pl.multiple_of(loc0 + c * RUN_ALIGN, RUN_ALIGN), RUN_ALIGN), :],
                    sems.at[to_slot]).start()
                return carry

            lax.fori_loop(0, runs_ref[entry + 1], piece, 0)

    @pl.when(i == 0)
    def _():
        fetch_runs(0, 0)

    @pl.when(i + 1 < pl.num_programs(0))
    def _():
        fetch_runs(i + 1, 1 - slot)

    rows = _run_rows(runs_ref, i, n_experts)
    pltpu.make_async_copy(ys_ref.at[pl.ds(0, rows), :], y_ref.at[slot, pl.ds(0, rows), :], sems.at[slot]).wait()
    w = w_ref[...]
    col = lax.broadcasted_iota(jnp.int32, (w.shape[0], y_ref.shape[1]), 1).astype(F32)
    sel = jnp.where(col == w[:, 4:5], w[:, 0:1], 0.0) + jnp.where(col == w[:, 5:6], w[:, 1:2], 0.0)
    moe = jnp.dot(sel.astype(BF16), y_ref[slot], preferred_element_type=F32)
    xn = x_ref[...] + mod_ref[...][:, 5 * d:6 * d] * moe
    if final_norm:
        xn = xn * lax.rsqrt(jnp.mean(xn * xn, axis=-1, keepdims=True) + EPS) * fg_ref[...]
    o_ref[...] = xn


def _combine(x, wts, runs, ys, mod, final_g, *, tt, n_experts, mod_row_of_tile, final_norm):
    m, d = x.shape
    buf_rows = 2 * tt + n_experts * RUN_ALIGN
    return pl.pallas_call(
        functools.partial(_combine_kernel, n_experts=n_experts, d=d, final_norm=final_norm),
        grid_spec=pltpu.PrefetchScalarGridSpec(
            num_scalar_prefetch=1,
            grid=(m // tt,),
            in_specs=[
                pl.BlockSpec((tt, d), lambda i, rn: (i, 0)),
                pl.BlockSpec((tt, LANE), lambda i, rn: (i, 0)),
                pl.BlockSpec((None, 1, 6 * d), lambda i, rn: (mod_row_of_tile(i, tt), 0, 0)),
                pl.BlockSpec((1, d), lambda i, rn: (0, 0)),
                pl.BlockSpec(memory_space=pl.ANY),
            ],
            out_specs=pl.BlockSpec((tt, d), lambda i, rn: (i, 0)),
            scratch_shapes=[pltpu.VMEM((2, buf_rows, d), BF16), pltpu.SemaphoreType.DMA((2,))],
        ),
        out_shape=jax.ShapeDtypeStruct((m, d), F32),
        compiler_params=_cparams(("arbitrary",), 40),
        name="combine",
    )(runs, x, wts, mod, final_g.reshape(1, d), ys)


def _moe(h2, x, router_w, mod, final_g, w_gu, w_down, *, seq_len, mod_row_of_tile, final_norm):
    m, d = x.shape
    n_experts = w_down.shape[0]
    tm = 512
    tt = min(512, m)
    n_rows = pl.cdiv(2 * m + m // tt * n_experts * RUN_ALIGN, tm) * tm + n_experts * tm
    assert n_rows // tm <= LANE
    wts, pairs, runs, counts = _router(h2, router_w, tm=tt)
    rundest, meta2d = _plan(runs, counts, tile_rows=tm)
    run_table = rundest.reshape(-1)
    meta = meta2d[0:META_ROWS].reshape(META_ROWS * LANE)
    xs = _dispatch(h2, pairs, run_table, meta, n_rows, tt=tt, tm=tm, n_experts=n_experts)
    ys = _grouped_swiglu(xs, meta, w_gu, w_down, tm=tm)
    return _combine(x, wts, run_table, ys, mod, final_g, tt=tt, n_experts=n_experts,
                    mod_row_of_tile=mod_row_of_tile, final_norm=final_norm)


def kernel(x, c, ctx, c_ctx, norm1_g, norm2_g, final_g, w_mod, b_mod, w_in, conv_w, sink, pool_w, pool_scale,
           w_branch, w_out, ffn_w_gu, ffn_w_down, router_w, moe_w_gu, moe_w_down):
    bsz, seq, d = x.shape
    lc = ctx.shape[1]
    depth = w_in.shape[0]
    assert bsz + 1 <= 8 and seq % BLOCK == 0 and lc % BLOCK == 0 and seq % GRID_W == 0

    cvec = jnp.zeros((8, d), F32).at[:bsz].set(c).at[bsz].set(c_ctx)
    mods = _modvec(cvec, w_mod, b_mod)
    rope_tabs = _rope_tables(seq)

    lat_row = lambda i, tm: (i * tm) // seq
    ctx_row = lambda i, tm: bsz

    xl = x.reshape(bsz * seq, d)
    xc = ctx.reshape(bsz * lc, d)
    expert_w = next_w = None
    for l in range(depth):
        last = l == depth - 1
        mod = mods[l].reshape(8, 1, 6 * d)
        if next_w is not None:
            w_in_l, wb_l, wo_l = next_w[0], next_w[1].reshape(w_branch.shape[1:]), next_w[2]
        else:
            w_in_l, wb_l, wo_l = w_in[l].astype(BF16), w_branch[l].astype(BF16), w_out[l].astype(BF16)
        pw_l = pool_w[l].astype(BF16)
        routed = l % 2 == 1
        mixer = functools.partial(_merge, mod=mod, norm2_g=norm2_g[l], conv_w=conv_w[l], pool_w=pw_l,
                                  pool_scale=pool_scale[l], w_branch=wb_l, w_out=wo_l,
                                  h2_dtype=F32 if routed else BF16)
        if routed:
            wgu, wd = expert_w if expert_w is not None else (moe_w_gu[l // 2].astype(BF16),
                                                             moe_w_down[l // 2].astype(BF16))
        else:
            wgu, wd = ffn_w_gu[l // 2].astype(BF16), ffn_w_down[l // 2].astype(BF16)
        ride = not routed and not last
        ride_gu = moe_w_gu[(l + 1) // 2].reshape(-1, moe_w_gu.shape[-1]) if ride else None
        ride_d = moe_w_down[(l + 1) // 2].reshape(-1, d) if ride else None

        def channel_mix(mixed, *, seq_len, row_fn, final_norm, casts=()):
            xm, h2 = mixed[0], mixed[1]
            if routed:
                return _moe(h2, xm, router_w[l // 2], mod, final_g, wgu, wd, seq_len=seq_len,
                            mod_row_of_tile=row_fn, final_norm=final_norm)
            assert not final_norm
            return _ffn_dense(h2, xm, mod, wgu, wd, seq_len=seq_len, mod_row_of_tile=row_fn, casts=casts)

        if last:
            kvc = _inproj(xc, norm1_g[l], mod, w_in_l, seq_len=lc, mod_row_of_tile=ctx_row, kv_only=True)
        else:
            qc, kvc, mixc, gatec = _inproj(xc, norm1_g[l], mod, w_in_l, seq_len=lc, mod_row_of_tile=ctx_row)
            attn_c = _attention(qc.reshape(bsz, lc, -1), None, kvc.reshape(bsz, lc, -1), sink[l], band=False)
            mixed_c = mixer(attn_c.reshape(bsz * lc, -1), mixc, gatec, xc, seq_len=lc, mod_row_of_tile=ctx_row)
            xc_next = channel_mix(mixed_c, seq_len=lc, row_fn=ctx_row, final_norm=False)
        q, kv, mix, gate = _inproj(xl, norm1_g[l], mod, w_in_l, seq_len=seq, mod_row_of_tile=lat_row,
                                   rope_tabs=rope_tabs)
        attn = _attention(q.reshape(bsz, seq, -1), kv.reshape(bsz, seq, -1), kvc.reshape(bsz, lc, -1), sink[l],
                          band=True, cast=ride_gu)
        merged = mixer((attn[0] if ride else attn).reshape(bsz * seq, -1), mix, gate, xl, seq_len=seq,
                       mod_row_of_tile=lat_row, cast=ride_d)
        if routed or last:
            xl = channel_mix(merged, seq_len=seq, row_fn=lat_row, final_norm=last and routed)
            next_w = None
        else:
            stacked = [w_in, w_branch.reshape(depth, -1, d), w_out]
            xl, *next_w = channel_mix(merged, seq_len=seq, row_fn=lat_row, final_norm=False,
                                      casts=[(w, l + 1) for w in stacked])
        expert_w = (attn[1].reshape(moe_w_gu.shape[1:]), merged[2].reshape(moe_w_down.shape[1:])) if ride else None
        if not last:
            xc = xc_next
    if depth % 2 == 1:
        raise NotImplementedError("final norm is fused into the expert layer; depth must be even")
    return xl.reshape(bsz, seq, d)
```

```python
import functools

import jax
import jax.numpy as jnp
from jax import lax
from jax.experimental import pallas as pl
from jax.experimental.pallas import tpu as pltpu

F32 = jnp.float32
BF16 = jnp.bfloat16

GRID_W = 64
EPS = 1e-6
NEG_INF = -1e30
HEAD_DIM = 64
N_HEADS = 8
N_KV_HEADS = 2
GROUP = N_HEADS // N_KV_HEADS
WINDOW = 128
BLOCK = 128
ROPE_THETA = 10000.0
BRANCH = 512
POOL_SIZES = (2, 4, 8, 16)
POOL_GROUP = 128
Q_END = 512
V_END = 768
MIX_W = 4 * BRANCH
POOL_END = V_END + MIX_W

LANE = 128
MXU_DIM = 256
BF16_SUBLANE_TILE = 16
HALO = BF16_SUBLANE_TILE
MIB = 1024 * 1024
VMEM_MIB = 64
VMEM_LIMIT_RESIDENT = (VMEM_MIB - 8) * MIB
VMEM_LIMIT_STREAMING = (VMEM_MIB - 24) * MIB


def _cparams(sem, resident_weights=False):
    limit = VMEM_LIMIT_RESIDENT if resident_weights else VMEM_LIMIT_STREAMING
    return pltpu.CompilerParams(dimension_semantics=sem, vmem_limit_bytes=limit)


def _pick_chunk(n, cap):
    for unit in (MXU_DIM, LANE):
        fits = [c for c in range(unit, min(n, cap) + 1, unit) if n % c == 0]
        if fits:
            return fits[-1]
    raise ValueError((n, cap))


def _resident(shape):
    nd = len(shape)
    return pl.BlockSpec(shape, lambda *_: (0,) * nd, pipeline_mode=pl.Buffered(1))


def _norm_mod(x, g, shift, scale):
    y = x * lax.rsqrt(jnp.mean(x * x, axis=-1, keepdims=True) + EPS) * g
    return y * (1.0 + scale) + shift


def _modvec_kernel(c_ref, w_ref, b_ref, o_ref):
    c = c_ref[...]
    s = c * jax.nn.sigmoid(c)
    o_ref[...] = jnp.dot(s, w_ref[...], preferred_element_type=F32) + b_ref[...]


def _modvec(cvec, w_mod, b_mod):
    depth, d, n = w_mod.shape
    nc = _pick_chunk(n, 1536)
    return pl.pallas_call(
        _modvec_kernel,
        grid=(depth, n // nc),
        in_specs=[
            pl.BlockSpec((8, d), lambda l, j: (0, 0)),
            pl.BlockSpec((None, d, nc), lambda l, j: (l, 0, j)),
            pl.BlockSpec((None, 1, nc), lambda l, j: (l, 0, j)),
        ],
        out_specs=pl.BlockSpec((None, 8, nc), lambda l, j: (l, 0, j)),
        out_shape=jax.ShapeDtypeStruct((depth, 8, n), F32),
        compiler_params=_cparams(("arbitrary", "arbitrary")),
        name="modvec",
    )(cvec, w_mod, b_mod.reshape(depth, 1, n))


def _rope_tables(seq_len):
    n_freq = HEAD_DIM // 4
    inv = ROPE_THETA ** (-jnp.arange(n_freq, dtype=F32) / n_freq)
    pos = jnp.arange(seq_len)
    row = (pos // GRID_W).astype(F32)[:, None] * inv[None, :]
    col = (pos % GRID_W).astype(F32)[:, None] * inv[None, :]
    zero = jnp.zeros_like(row)
    cos = jnp.concatenate([jnp.cos(row)] * 2 + [jnp.cos(col)] * 2, axis=-1)
    s_lo = jnp.concatenate([-jnp.sin(row), zero, -jnp.sin(col), zero], axis=-1)
    s_hi = jnp.concatenate([zero, jnp.sin(row), zero, jnp.sin(col)], axis=-1)
    return tuple(jnp.tile(t, (1, LANE // HEAD_DIM)) for t in (cos, s_lo, s_hi))


def _inproj_kernel(*refs, rope, kv_only, d):
    x_ref, g_ref, mod_ref, w_ref = refs[:4]
    refs = refs[4:]
    if rope:
        cos_ref, slo_ref, shi_ref = refs[:3]
        refs = refs[3:]

        def rot(z):
            return (z * cos_ref[...] + pltpu.roll(z, LANE - 16, 1) * slo_ref[...]
                    + pltpu.roll(z, 16, 1) * shi_ref[...])
    else:
        def rot(z):
            return z

    mod = mod_ref[...]
    h = _norm_mod(x_ref[...], g_ref[...], mod[:, 0:d], mod[:, d:2 * d]).astype(BF16)

    def proj(c0, c1):
        return jnp.dot(h, w_ref[:, c0:c1], preferred_element_type=F32)

    if kv_only:
        (kv_ref,) = refs
        z = proj(0, 2 * LANE)
        kv_ref[:, 0:LANE] = rot(z[:, 0:LANE]).astype(BF16)
        kv_ref[:, LANE:] = z[:, LANE:].astype(BF16)
        return

    q_ref, kv_ref, mix_ref, gate_ref = refs
    cw = 512
    n_gate = gate_ref.shape[1]
    for c in range(n_gate // cw):
        zg = proj(POOL_END + c * cw, POOL_END + (c + 1) * cw)
        gate_ref[:, c * cw:(c + 1) * cw] = jax.nn.sigmoid(zg).astype(BF16)
    z = proj(0, Q_END)
    for j in range(Q_END // LANE):
        q_ref[:, j * LANE:(j + 1) * LANE] = (rot(z[:, j * LANE:(j + 1) * LANE]) * HEAD_DIM ** -0.5).astype(BF16)
    z = proj(Q_END, V_END)
    kv_ref[:, 0:LANE] = rot(z[:, 0:LANE]).astype(BF16)
    kv_ref[:, LANE:] = z[:, LANE:].astype(BF16)
    for c in range(MIX_W // cw):
        mix_ref[:, c * cw:(c + 1) * cw] = proj(V_END + c * cw, V_END + (c + 1) * cw).astype(BF16)


def _inproj(x, norm_g, mod, w_in, *, seq_len, mod_row_of_tile, rope_tabs=None, kv_only=False):
    m, d = x.shape
    tm = min(1024, seq_len)
    assert m % tm == 0 and seq_len % tm == 0
    tiles_per_seq = seq_len // tm
    rope = rope_tabs is not None
    in_w = w_in.shape[1]
    in_specs = [
        pl.BlockSpec((tm, d), lambda i: (i, 0)),
        pl.BlockSpec((1, d), lambda i: (0, 0)),
        pl.BlockSpec((None, 1, 6 * d), lambda i: (mod_row_of_tile(i, tm), 0, 0)),
        pl.BlockSpec((d, 2 * LANE), lambda i: (0, Q_END // (2 * LANE))) if kv_only else _resident((d, in_w)),
    ]
    args = [x, norm_g.reshape(1, d), mod, w_in]
    if rope:
        in_specs += [pl.BlockSpec((tm, LANE), lambda i: (i % tiles_per_seq, 0))] * 3
        args += list(rope_tabs)
    if kv_only:
        out_specs = pl.BlockSpec((tm, 2 * LANE), lambda i: (i, 0))
        out_shape = jax.ShapeDtypeStruct((m, 2 * LANE), BF16)
    else:
        widths = (Q_END, 2 * LANE, MIX_W, in_w - POOL_END)
        out_specs = [pl.BlockSpec((tm, w), lambda i: (i, 0)) for w in widths]
        out_shape = [jax.ShapeDtypeStruct((m, w), BF16) for w in widths]
    return pl.pallas_call(
        functools.partial(_inproj_kernel, rope=rope, kv_only=kv_only, d=d),
        grid=(m // tm,),
        in_specs=in_specs,
        out_specs=out_specs,
        out_shape=out_shape,
        compiler_params=_cparams(("parallel",), resident_weights=True),
        name="inproj_kv" if kv_only else "inproj",
    )(*args)


ATTN_STRIP = 32
ATTN_QBLOCKS = 2
ATTN_AHEAD = 2


def _attn_kernel(sink_ref, q_ref, *refs, band, carry_cast):
    refs, (s_ref, p_ref) = list(refs[:-2]), refs[-2:]
    qblocks = q_ref.shape[1] // BLOCK
    if carry_cast:
        cast_out = refs.pop()
        cast_in = refs.pop(-2)
        cast_out[...] = cast_in[...].astype(BF16)
    if band:
        kvp_ref, kvm_ref, kvn_ref, kvc_ref, bias_ref, o_ref = refs
    else:
        kvc_ref, o_ref = refs
    kvc = kvc_ref[0]
    lc = kvc.shape[0]
    nloc = 3 * BLOCK if band else 0
    nt = (((1,), (1,)), ((), ()))
    w_ctx = jnp.concatenate([kvc[:, LANE:], jnp.ones((lc, LANE), BF16)], axis=1)
    if band:
        n = pl.program_id(1)
        kv4 = jnp.concatenate([kvp_ref[0], kvm_ref[0], kvn_ref[0]], axis=0)
        w4 = jnp.concatenate([kv4[:, LANE:], jnp.ones((kv4.shape[0], LANE), BF16)], axis=1)
        col = lax.broadcasted_iota(jnp.int32, (1, nloc), 1)
        head_edge = jnp.where((col < BLOCK) & (n == 0), NEG_INF, 0.0)
        tail_edge = jnp.where((col >= 2 * BLOCK) & (n == pl.num_programs(1) - 1), NEG_INF, 0.0)
        biases = ([bias_ref[...] + head_edge] + [bias_ref[...]] * (qblocks - 2)
                  + [bias_ref[...] + tail_edge])
    units = [(sb, h) for sb in range(qblocks) for h in range(N_HEADS)]
    def scores(u):
        sb, h = units[u]
        qh = q_ref[0, sb * BLOCK:(sb + 1) * BLOCK, h * HEAD_DIM:(h + 1) * HEAD_DIM]
        ks = slice(h // GROUP * HEAD_DIM, (h // GROUP + 1) * HEAD_DIM)
        if band:
            kl = kv4[sb * BLOCK:sb * BLOCK + nloc, ks]
            s_ref[u, :, 0:nloc] = lax.dot_general(qh, kl, nt, preferred_element_type=F32) + biases[sb]
        s_ref[u, :, nloc:] = lax.dot_general(qh, kvc[:, ks], nt, preferred_element_type=F32)

    def probs(u):
        sink = sink_ref[units[u][1]]
        esink = []
        for r in range(0, BLOCK, ATTN_STRIP):
            s = s_ref[u, r:r + ATTN_STRIP, :]
            m = jnp.maximum(jnp.max(s, axis=-1, keepdims=True), sink)
            p_ref[u, r:r + ATTN_STRIP, :] = jnp.exp(s - m).astype(BF16)
            esink.append(jnp.exp(sink - m))
        return jnp.concatenate(esink, axis=0)

    def weighted_values(u, esink):
        sb, h = units[u]
        ks = slice(h // GROUP * HEAD_DIM, (h // GROUP + 1) * HEAD_DIM)
        o2 = jnp.dot(p_ref[u, :, nloc:], w_ctx, preferred_element_type=F32)
        if band:
            o2 = o2 + jnp.dot(p_ref[u, :, 0:nloc], w4[sb * BLOCK:sb * BLOCK + nloc], preferred_element_type=F32)
        return o2[:, ks] / (o2[:, LANE:LANE + HEAD_DIM] + esink)

    outs, esinks = [], []
    for u in range(min(ATTN_AHEAD, len(units))):
        scores(u)
    for u in range(len(units)):
        if u + ATTN_AHEAD < len(units):
            scores(u + ATTN_AHEAD)
        esinks.append(probs(u))
        if u >= 1:
            outs.append(weighted_values(u - 1, esinks[u - 1]))
    outs.append(weighted_values(len(units) - 1, esinks[-1]))
    for sb in range(qblocks):
        o_ref[0, sb * BLOCK:(sb + 1) * BLOCK, :] = jnp.concatenate(
            outs[sb * N_HEADS:(sb + 1) * N_HEADS], axis=1).astype(BF16)


def _cast_rider(w, n_steps, index_map):
    rows, cols = w.shape
    assert rows % (n_steps * BF16_SUBLANE_TILE) == 0, (w.shape, n_steps)
    return pl.BlockSpec((rows // n_steps, cols), index_map), jax.ShapeDtypeStruct((rows, cols), BF16)


def _attention(q, kv, kvc, sink, *, band, cast=None):
    b, l, _ = q.shape
    lc = kvc.shape[1]
    nb = l // BLOCK
    qb = min(ATTN_QBLOCKS, nb)
    tq = qb * BLOCK
    assert l % tq == 0 and qb >= 2
    in_specs = [
        pl.BlockSpec(memory_space=pltpu.SMEM),
        pl.BlockSpec((1, tq, N_HEADS * HEAD_DIM), lambda bi, n: (bi, n, 0)),
    ]
    args = [sink, q]
    if band:
        in_specs += [
            pl.BlockSpec((1, BLOCK, 2 * LANE), lambda bi, n: (bi, jnp.maximum(qb * n - 1, 0), 0)),
            pl.BlockSpec((1, tq, 2 * LANE), lambda bi, n: (bi, n, 0)),
            pl.BlockSpec((1, BLOCK, 2 * LANE), lambda bi, n: (bi, jnp.minimum(qb * (n + 1), nb - 1), 0)),
        ]
        args += [kv, kv, kv]
    in_specs.append(pl.BlockSpec((1, lc, 2 * LANE), lambda bi, n: (bi, 0, 0)))
    args.append(kvc)
    nkeys = lc
    if band:
        rel = jnp.arange(3 * BLOCK)[None, :] - BLOCK - jnp.arange(BLOCK)[:, None]
        in_specs.append(pl.BlockSpec((BLOCK, 3 * BLOCK), lambda bi, n: (0, 0)))
        args.append(jnp.where(jnp.abs(rel) <= WINDOW, 0.0, NEG_INF).astype(F32))
        nkeys += 3 * BLOCK
    nq = l // tq
    out_specs = [pl.BlockSpec((1, tq, N_HEADS * HEAD_DIM), lambda bi, n: (bi, n, 0))]
    out_shape = [jax.ShapeDtypeStruct((b, l, N_HEADS * HEAD_DIM), BF16)]
    if cast is not None:
        spec, shape = _cast_rider(cast, b * nq, lambda bi, n: (bi * nq + n, 0))
        in_specs.append(spec)
        args.append(cast)
        out_specs.append(spec)
        out_shape.append(shape)
    outs = pl.pallas_call(
        functools.partial(_attn_kernel, band=band, carry_cast=cast is not None),
        grid=(b, nq),
        in_specs=in_specs,
        out_specs=out_specs,
        out_shape=out_shape,
        scratch_shapes=[pltpu.VMEM((qb * N_HEADS, BLOCK, nkeys), F32),
                        pltpu.VMEM((qb * N_HEADS, BLOCK, nkeys), BF16)],
        compiler_params=_cparams(("parallel", "parallel"), resident_weights=True),
        name="attn_band" if band else "attn_ctx",
    )(*args)
    return outs[0] if cast is None else outs


MERGE_PARTS = 2


def _merge_kernel(attn_ref, mix_ref, prev_ref, next_ref, gate_ref, x_ref, mod_ref, n2g_ref, convw_ref,
                  poolw_ref, pscale_ref, wb_ref, wo_ref, *refs, tm, seq_len, d, carry_cast):
    refs = list(refs)
    if carry_cast:
        cast_out = refs.pop()
        cast_in = refs.pop(0)
        cast_out[...] = cast_in[...].astype(BF16)
    xo_ref, h2_ref = refs[:2]
    tile = pl.program_id(0) % (seq_len // tm)
    keep_prev = (tile != 0).astype(F32)
    keep_next = (tile != seq_len // tm - 1).astype(F32)
    mixm = mix_ref[...]
    prev = prev_ref[...].astype(F32) * keep_prev
    nxt = next_ref[...].astype(F32) * keep_next
    b = BRANCH
    cx, cb, cc = (mixm[:, j * b:(j + 1) * b].astype(F32) for j in range(3))

    p = cc * cx
    p_prev = prev[HALO - 1:HALO, 2 * b:3 * b] * prev[HALO - 1:HALO, 0:b]
    p_next = nxt[0:1, 2 * b:3 * b] * nxt[0:1, 0:b]
    ridx = lax.broadcasted_iota(jnp.int32, (tm, b), 0)
    p_dn = jnp.where(ridx == 0, p_prev, pltpu.roll(p, 1, 0))
    p_up = jnp.where(ridx == tm - 1, p_next, pltpu.roll(p, tm - 1, 0))
    cw = convw_ref[...]
    conv_out = (cb * (p_dn * cw[0:1] + p * cw[1:2] + p_up * cw[2:3])).astype(BF16)

    u_main = mixm[:, 3 * b:4 * b].astype(F32)
    u_ext = jnp.concatenate([prev[:, 3 * b:4 * b], u_main, nxt[:, 3 * b:4 * b]], axis=0)
    ext = tm + 2 * HALO

    def shift(a, s):
        return pltpu.roll(a, s % ext, 0)

    tpos = tile * tm + lax.broadcasted_iota(jnp.int32, (tm, 1), 0)
    pooled = []
    for gi, w in enumerate(POOL_SIZES):
        gs = slice(gi * POOL_GROUP, (gi + 1) * POOL_GROUP)
        ug = u_ext[:, gs]
        a = ug + shift(ug, 1)
        ww = 2
        while ww < w:
            a = shift(a, ww // 2) + shift(a, -(ww // 2))
            ww *= 2
        cnt = jnp.minimum(tpos + w // 2, seq_len) - jnp.maximum(tpos - w // 2, 0)
        dlt = a[HALO:HALO + tm] / cnt.astype(F32) - u_main[:, gs]
        pooled.append(jnp.dot(dlt.astype(BF16), poolw_ref[gi], preferred_element_type=F32))
    pool_out = (jnp.concatenate(pooled, axis=1) * pscale_ref[...]).astype(BF16)

    mod = mod_ref[...]
    parts = [slice(r, r + tm // MERGE_PARTS) for r in range(0, tm, tm // MERGE_PARTS)]

    def branches(rs):
        return (jnp.dot(attn_ref[rs, :], wb_ref[0], preferred_element_type=F32),
                jnp.dot(conv_out[rs], wb_ref[1], preferred_element_type=F32),
                jnp.dot(pool_out[rs], wb_ref[2], preferred_element_type=F32))

    def gated(rs, br):
        return (gate_ref[rs, 0:d].astype(F32) * br[0] + gate_ref[rs, d:2 * d].astype(F32) * br[1]
                + gate_ref[rs, 2 * d:3 * d].astype(F32) * br[2]).astype(BF16)

    def finish(rs, o):
        xn = x_ref[rs, :] + mod[:, 2 * d:3 * d] * o
        xo_ref[rs, :] = xn
        h2_ref[rs, :] = _norm_mod(xn, n2g_ref[...], mod[:, 3 * d:4 * d], mod[:, 4 * d:5 * d]).astype(h2_ref.dtype)

    br = [branches(rs) for rs in parts]
    outs = []
    for k, rs in enumerate(parts):
        outs.append(jnp.dot(gated(rs, br[k]), wo_ref[...], preferred_element_type=F32))
        if k >= 1:
            finish(parts[k - 1], outs[k - 1])
    finish(parts[-1], outs[-1])


def _merge(attn, mix, gate, x, mod, norm2_g, conv_w, pool_w, pool_scale, w_branch, w_out, *, seq_len,
           mod_row_of_tile, h2_dtype, cast=None):
    m, d = x.shape
    tm = min(512, seq_len)
    assert m % tm == 0 and seq_len % tm == 0 and tm % HALO == 0
    hb = tm // HALO
    n_halo = m // HALO
    row = lambda w: pl.BlockSpec((tm, w), lambda i: (i, 0))
    in_specs = [
        row(BRANCH),
        row(MIX_W),
        pl.BlockSpec((HALO, MIX_W), lambda i: (jnp.maximum(i * hb - 1, 0), 0)),
        pl.BlockSpec((HALO, MIX_W), lambda i: (jnp.minimum((i + 1) * hb, n_halo - 1), 0)),
        row(3 * d),
        row(d),
        pl.BlockSpec((None, 1, 6 * d), lambda i: (mod_row_of_tile(i, tm), 0, 0)),
        pl.BlockSpec((1, d), lambda i: (0, 0)),
        _resident(conv_w.shape),
        _resident(pool_w.shape),
        pl.BlockSpec((1, BRANCH), lambda i: (0, 0)),
        _resident(w_branch.shape),
        _resident(w_out.shape),
    ]
    args = [attn, mix, mix, mix, gate, x, mod, norm2_g.reshape(1, d), conv_w, pool_w,
            pool_scale.reshape(1, BRANCH), w_branch, w_out]
    out_specs = [row(d), row(d)]
    out_shape = [jax.ShapeDtypeStruct((m, d), F32), jax.ShapeDtypeStruct((m, d), h2_dtype)]
    if cast is not None:
        spec, shape = _cast_rider(cast, m // tm, lambda i: (i, 0))
        in_specs.append(spec)
        args.append(cast)
        out_specs.append(spec)
        out_shape.append(shape)
    return pl.pallas_call(
        functools.partial(_merge_kernel, tm=tm, seq_len=seq_len, d=d, carry_cast=cast is not None),
        grid=(m // tm,),
        in_specs=in_specs,
        out_specs=out_specs,
        out_shape=out_shape,
        compiler_params=_cparams(("parallel",), resident_weights=True),
        name="merge",
    )(*args)


def _ffn_kernel(h_ref, x_ref, mod_ref, wgu_ref, wd_ref, *refs, d, d_ff, fc):
    n_cast = len(refs) // 2
    o_ref = refs[n_cast]
    for cast_in, cast_out in zip(refs[:n_cast], refs[n_cast + 1:]):
        cast_out[...] = cast_in[...].astype(BF16)
    h = h_ref[...]
    acc = None
    for f in range(0, d_ff, fc):
        g = jnp.dot(h, wgu_ref[:, f:f + fc], preferred_element_type=F32)
        u = jnp.dot(h, wgu_ref[:, d_ff + f:d_ff + f + fc], preferred_element_type=F32)
        a = (g * jax.nn.sigmoid(g) * u).astype(BF16)
        y = jnp.dot(a, wd_ref[f:f + fc, :], preferred_element_type=F32)
        acc = y if acc is None else acc + y
    o_ref[...] = x_ref[...] + mod_ref[...][:, 5 * d:6 * d] * acc


def _ffn_dense(h2, x, mod, w_gu, w_down, *, seq_len, mod_row_of_tile, casts=()):
    m, d = x.shape
    d_ff = w_down.shape[0]
    tm = min(512, seq_len)
    assert m % tm == 0 and seq_len % tm == 0
    fc = _pick_chunk(d_ff, 3072)
    n_steps = m // tm
    in_specs = [
        pl.BlockSpec((tm, d), lambda i: (i, 0)),
        pl.BlockSpec((tm, d), lambda i: (i, 0)),
        pl.BlockSpec((None, 1, 6 * d), lambda i: (mod_row_of_tile(i, tm), 0, 0)),
        _resident(w_gu.shape),
        _resident(w_down.shape),
    ]
    out_specs = [pl.BlockSpec((tm, d), lambda i: (i, 0))]
    out_shape = [jax.ShapeDtypeStruct((m, d), F32)]
    for w, layer in casts:
        _, rows, cols = w.shape
        assert rows % (n_steps * BF16_SUBLANE_TILE) == 0, (w.shape, n_steps)
        in_specs.append(pl.BlockSpec((None, rows // n_steps, cols), lambda i, layer=layer: (layer, i, 0)))
        out_specs.append(pl.BlockSpec((rows // n_steps, cols), lambda i: (i, 0)))
        out_shape.append(jax.ShapeDtypeStruct((rows, cols), BF16))
    outs = pl.pallas_call(
        functools.partial(_ffn_kernel, d=d, d_ff=d_ff, fc=fc),
        grid=(n_steps,),
        in_specs=in_specs,
        out_specs=out_specs,
        out_shape=out_shape,
        compiler_params=_cparams(("parallel",), resident_weights=True),
        name="ffn_dense",
    )(h2, x, mod, w_gu, w_down, *[w for w, _ in casts])
    return outs if casts else outs[0]


ROUTE_ROWS = 8
RUN_ALIGN = BF16_SUBLANE_TILE
META_ROWS = 5
META_TILES, META_PAD_START, META_PAD_LEN, META_USED = LANE, 2 * LANE, 3 * LANE, 4 * LANE


def _router_kernel(h_ref, rw_ref, tri_ref, wts_ref, pairs_ref, runs_ref, cnt_ref, carry_ref, *, n_experts):
    @pl.when(pl.program_id(0) == 0)
    def _():
        carry_ref[...] = jnp.zeros_like(carry_ref)

    h, rs = h_ref[...], slice(None)
    w = rw_ref[...]
    h_hi, w_hi = h.astype(BF16), w.astype(BF16)
    h_lo, w_lo = (h - h_hi.astype(F32)).astype(BF16), (w - w_hi.astype(F32)).astype(BF16)
    logits = (jnp.dot(h_hi, w_hi, preferred_element_type=F32) + jnp.dot(h_lo, w_hi, preferred_element_type=F32)
              + jnp.dot(h_hi, w_lo, preferred_element_type=F32))
    lane = lax.broadcasted_iota(jnp.int32, logits.shape, 1).astype(F32)
    logits = jnp.where(lane < n_experts, logits, -jnp.inf)
    m1 = jnp.max(logits, axis=-1, keepdims=True)
    i1 = jnp.min(jnp.where(logits == m1, lane, float(LANE)), axis=-1, keepdims=True)
    rest = jnp.where(lane == i1, -jnp.inf, logits)
    m2 = jnp.max(rest, axis=-1, keepdims=True)
    i2 = jnp.min(jnp.where(rest == m2, lane, float(LANE)), axis=-1, keepdims=True)
    e2 = jnp.exp(m2 - m1)
    den = 1.0 + e2
    weights = jnp.where(lane == 0, 1.0 / den, jnp.where(lane == 1, e2 / den, 0.0))

    m1t = (lane == i1).astype(F32).T[0:ROUTE_ROWS]
    m2t = (lane == i2).astype(F32).T[0:ROUTE_ROWS]
    mem = m1t + m2t
    local = jnp.dot(mem.astype(BF16), tri_ref[...], preferred_element_type=F32)
    run = jnp.floor((jnp.sum(mem, axis=1, keepdims=True) + (RUN_ALIGN - 1)) * (1.0 / RUN_ALIGN)) * RUN_ALIGN
    sub8 = lax.broadcasted_iota(jnp.int32, (ROUTE_ROWS, LANE), 0)
    lane8 = lax.broadcasted_iota(jnp.int32, (ROUTE_ROWS, LANE), 1)
    incl = jnp.sum(jnp.where(sub8 <= lane8, run, 0.0), axis=0, keepdims=True)
    run_start = jnp.sum(jnp.where(lane8 == sub8, incl, 0.0), axis=1, keepdims=True) - run
    done = carry_ref[:, 0:1]
    eid = lax.broadcasted_iota(jnp.int32, mem.shape, 0).astype(F32)
    pick = lambda sel, val: jnp.sum(sel * val, axis=0, keepdims=True)
    rows = [pick(m1t, eid), pick(m2t, eid), pick(m1t, local + done), pick(m2t, local + done),
            pick(m1t, local + run_start), pick(m2t, local + run_start)]
    rows += [jnp.zeros_like(rows[0])] * (ROUTE_ROWS - len(rows))
    pairs = jnp.concatenate(rows, axis=0)
    pairs_ref[:, rs] = pairs
    by_token = jnp.concatenate([pairs, jnp.zeros((LANE - ROUTE_ROWS, pairs.shape[1]), F32)], axis=0).T
    wts_ref[rs, :] = jnp.where((lane == 4) | (lane == 5), by_token, weights)
    runs_ref[...] = jnp.where(lane8 == 0, run_start, jnp.where(lane8 == 1, run * (1.0 / RUN_ALIGN),
                                                               jnp.where(lane8 == 2, done, 0.0)))
    carry_ref[...] = carry_ref[...] + run
    cnt_ref[...] = carry_ref[...]


def _router(h2, router_w, *, tm):
    m, d = h2.shape
    n_experts = router_w.shape[1]
    assert n_experts <= ROUTE_ROWS and m % tm == 0
    rw = jnp.pad(router_w, ((0, 0), (0, LANE - n_experts)))
    tri = jnp.triu(jnp.ones((tm, tm), BF16), k=1)
    return pl.pallas_call(
        functools.partial(_router_kernel, n_experts=n_experts),
        grid=(m // tm,),
        in_specs=[
            pl.BlockSpec((tm, d), lambda i: (i, 0)),
            pl.BlockSpec((d, LANE), lambda i: (0, 0)),
            pl.BlockSpec((tm, tm), lambda i: (0, 0)),
        ],
        out_specs=[
            pl.BlockSpec((tm, LANE), lambda i: (i, 0)),
            pl.BlockSpec((ROUTE_ROWS, tm), lambda i: (0, i)),
            pl.BlockSpec((ROUTE_ROWS, LANE), lambda i: (i, 0)),
            pl.BlockSpec((ROUTE_ROWS, LANE), lambda i: (0, 0)),
        ],
        out_shape=[jax.ShapeDtypeStruct((m, LANE), F32), jax.ShapeDtypeStruct((ROUTE_ROWS, m), F32),
                   jax.ShapeDtypeStruct((m // tm * ROUTE_ROWS, LANE), F32),
                   jax.ShapeDtypeStruct((ROUTE_ROWS, LANE), F32)],
        scratch_shapes=[pltpu.VMEM((ROUTE_ROWS, LANE), F32)],
        compiler_params=_cparams(("arbitrary",)),
        name="router",
    )(h2, rw, tri)


def _plan_kernel(runs_ref, cnt_ref, rundest_ref, meta_ref, *, tile_rows):
    cnt = cnt_ref[...]
    padded = jnp.floor((cnt + (tile_rows - 1)) * (1.0 / tile_rows)) * tile_rows
    sub = lax.broadcasted_iota(jnp.int32, cnt.shape, 0)
    lane = lax.broadcasted_iota(jnp.int32, cnt.shape, 1)
    end_row = jnp.sum(jnp.where(sub <= lane, padded, 0.0), axis=0, keepdims=True)
    end_col = jnp.sum(jnp.where(lane == sub, end_row, 0.0), axis=1, keepdims=True)
    start_col = end_col - padded[:, 0:1]
    runs = runs_ref[...]
    starts = jnp.concatenate([start_col] * (runs.shape[0] // ROUTE_ROWS), axis=0)
    rlane = lax.broadcasted_iota(jnp.int32, runs.shape, 1)
    rundest_ref[...] = jnp.where(rlane == 2, runs + starts, runs).astype(jnp.int32)
    tile_start = (lane * tile_rows).astype(F32)
    tile_e = jnp.sum((end_col <= tile_start).astype(F32), axis=0, keepdims=True)
    n_tiles = jnp.max(end_col, axis=0, keepdims=True) * (1.0 / tile_rows)
    pad_start = jnp.sum(jnp.where(lane == sub, start_col + cnt, 0.0), axis=0, keepdims=True)
    pad_len = jnp.sum(jnp.where(lane == sub, padded - cnt, 0.0), axis=0, keepdims=True)
    group_rows_end = jnp.sum(jnp.where(sub.astype(F32) == tile_e, start_col + cnt, 0.0), axis=0, keepdims=True)
    tile_used = jnp.clip(group_rows_end - tile_start[0:1], 0.0, float(tile_rows))
    meta = jnp.concatenate([tile_e, jnp.broadcast_to(n_tiles, tile_e.shape), pad_start, pad_len, tile_used]
                           + [jnp.zeros_like(tile_e)] * (ROUTE_ROWS - META_ROWS), axis=0)
    meta_ref[...] = meta.astype(jnp.int32)


def _plan(runs, counts, *, tile_rows):
    small = pl.BlockSpec((ROUTE_ROWS, LANE), lambda i: (0, 0))
    whole = pl.BlockSpec(runs.shape, lambda i: (0, 0))
    return pl.pallas_call(
        functools.partial(_plan_kernel, tile_rows=tile_rows),
        grid=(1,),
        in_specs=[whole, small],
        out_specs=[whole, small],
        out_shape=[jax.ShapeDtypeStruct(runs.shape, jnp.int32), jax.ShapeDtypeStruct((ROUTE_ROWS, LANE), jnp.int32)],
        compiler_params=_cparams(("arbitrary",)),
        name="plan",
    )(runs, counts)


def _run_rows(runs_ref, tile, n_experts):
    pieces = runs_ref[tile * n_experts * LANE + 1]
    for e in range(1, n_experts):
        pieces = pieces + runs_ref[(tile * n_experts + e) * LANE + 1]
    return pl.multiple_of(pieces * RUN_ALIGN, RUN_ALIGN)


def _dispatch_kernel(runs_ref, meta_ref, h_ref, pos_ref, xs_ref, buf_ref, zero_ref, sems, zsem, *, n_experts,
                     n_tiles_max):
    i = pl.program_id(0)
    tile_rows = zero_ref.shape[0]
    slot = i % 2
    buf = buf_ref.at[slot]
    pos = pos_ref[...]
    row = lax.broadcasted_iota(jnp.int32, (buf.shape[0], pos.shape[1]), 0).astype(F32)
    take = ((row == pos[4:5]).astype(F32) + (row == pos[5:6]).astype(F32)).astype(BF16)
    buf[...] = jnp.dot(take, h_ref[...].astype(BF16), preferred_element_type=F32).astype(BF16)

    def piece_copy(src, src_row, dst_row, s):
        return pltpu.make_async_copy(src.at[pl.ds(pl.multiple_of(src_row, RUN_ALIGN), RUN_ALIGN), :],
                                     xs_ref.at[pl.ds(pl.multiple_of(dst_row, RUN_ALIGN), RUN_ALIGN), :], s)

    for e in range(n_experts):
        entry = (i * n_experts + e) * LANE
        src0, dst0 = runs_ref[entry], runs_ref[entry + 2]

        def piece(c, carry):
            piece_copy(buf, src0 + c * RUN_ALIGN, dst0 + c * RUN_ALIGN, sems.at[slot]).start()
            return carry

        lax.fori_loop(0, runs_ref[entry + 1], piece, 0)

    def wait_runs(tile, sl):
        rows = _run_rows(runs_ref, tile, n_experts)
        pltpu.make_async_copy(buf_ref.at[sl, pl.ds(0, rows), :], xs_ref.at[pl.ds(0, rows), :], sems.at[sl]).wait()

    @pl.when(i > 0)
    def _():
        wait_runs(i - 1, 1 - slot)

    @pl.when(i == pl.num_programs(0) - 1)
    def _():
        wait_runs(i, slot)
        zero_ref[...] = jnp.zeros_like(zero_ref)

        def fill_padding(e, start):
            first = meta_ref[META_PAD_START + e]

            def piece(c, carry):
                cp = piece_copy(zero_ref, 0, first + c * RUN_ALIGN, zsem)
                cp.start() if start else cp.wait()
                return carry

            lax.fori_loop(0, meta_ref[META_PAD_LEN + e] // RUN_ALIGN, piece, 0)

        def tail_copy(k):
            return pltpu.make_async_copy(zero_ref, xs_ref.at[pl.ds(k * tile_rows, tile_rows), :], zsem)

        def zero_tile(k, carry):
            tail_copy(k).start()
            return carry

        def wait_tile(k, carry):
            tail_copy(k).wait()
            return carry

        for e in range(n_experts):
            fill_padding(e, start=True)
        lax.fori_loop(meta_ref[META_TILES], n_tiles_max, zero_tile, 0)
        for e in range(n_experts):
            fill_padding(e, start=False)
        lax.fori_loop(meta_ref[META_TILES], n_tiles_max, wait_tile, 0)


def _dispatch(h2, pairs, runs, meta, n_rows, *, tt, tm, n_experts):
    m, d = h2.shape
    buf_rows = 2 * tt + n_experts * RUN_ALIGN
    return pl.pallas_call(
        functools.partial(_dispatch_kernel, n_experts=n_experts, n_tiles_max=n_rows // tm),
        grid_spec=pltpu.PrefetchScalarGridSpec(
            num_scalar_prefetch=2,
            grid=(m // tt,),
            in_specs=[pl.BlockSpec((tt, d), lambda i, rn, mt: (i, 0)),
                      pl.BlockSpec((ROUTE_ROWS, tt), lambda i, rn, mt: (0, i))],
            out_specs=pl.BlockSpec(memory_space=pl.ANY),
            scratch_shapes=[pltpu.VMEM((2, buf_rows, d), BF16), pltpu.VMEM((tm, d), BF16),
                            pltpu.SemaphoreType.DMA((2,)), pltpu.SemaphoreType.DMA],
        ),
        out_shape=jax.ShapeDtypeStruct((n_rows, d), BF16),
        compiler_params=_cparams(("arbitrary",)),
        name="dispatch",
    )(runs, meta, h2, pairs)


def _gmm_kernel(meta_ref, xs_ref, wg_ref, wu_ref, wd_ref, o_ref, acc_ref):
    f = pl.program_id(1)
    nf = pl.num_programs(1)

    @pl.when((pl.program_id(0) == 0) & (f == 0))
    def _():
        acc_ref[...] = jnp.zeros_like(acc_ref)

    tm = xs_ref.shape[0]
    in_use = pl.program_id(0) < meta_ref[META_TILES]
    used = meta_ref[META_USED + pl.program_id(0)]

    def swiglu_rows(rows):
        xb = xs_ref[0:rows, :]
        g = jnp.dot(xb, wg_ref[...], preferred_element_type=F32)
        u = jnp.dot(xb, wu_ref[...], preferred_element_type=F32)
        a = (g * jax.nn.sigmoid(g) * u).astype(BF16)
        y = jnp.dot(a, wd_ref[...], preferred_element_type=F32)
        total = y + jnp.where(f > 0, acc_ref[0:rows, :], 0.0)
        acc_ref[0:rows, :] = total
        o_ref[0:rows, :] = total.astype(o_ref.dtype)
        if rows < tm:
            o_ref[rows:, :] = jnp.zeros((tm - rows, o_ref.shape[1]), o_ref.dtype)

    @pl.when(in_use & (used > tm // 2))
    def _():
        swiglu_rows(tm)

    @pl.when(in_use & (used <= tm // 2))
    def _():
        swiglu_rows(tm // 2)

    @pl.when((pl.program_id(0) >= meta_ref[META_TILES]) & (f == nf - 1))
    def _():
        o_ref[...] = jnp.zeros_like(o_ref)


def _grouped_swiglu(xs, meta, w_gu, w_down, *, tm):
    n_rows, d = xs.shape
    n_experts, d_ff, _ = w_down.shape
    fc = _pick_chunk(d_ff, d_ff // 2)
    nf = d_ff // fc
    assert nf >= 2 and n_rows % tm == 0 and n_rows // tm <= LANE

    def tile(i, mt):
        return jnp.maximum(jnp.minimum(i, mt[LANE] - 1), 0)

    def expert(i, mt):
        return jnp.minimum(mt[tile(i, mt)], n_experts - 1)

    def chunk(i, f, mt):
        return jnp.where(i < mt[LANE], f, nf - 1)

    return pl.pallas_call(
        _gmm_kernel,
        grid_spec=pltpu.PrefetchScalarGridSpec(
            num_scalar_prefetch=1,
            grid=(n_rows // tm, nf),
            in_specs=[
                pl.BlockSpec((tm, d), lambda i, f, mt: (tile(i, mt), 0)),
                pl.BlockSpec((None, d, fc), lambda i, f, mt: (expert(i, mt), 0, chunk(i, f, mt))),
                pl.BlockSpec((None, d, fc), lambda i, f, mt: (expert(i, mt), 0, nf + chunk(i, f, mt))),
                pl.BlockSpec((None, fc, d), lambda i, f, mt: (expert(i, mt), chunk(i, f, mt), 0)),
            ],
            out_specs=pl.BlockSpec((tm, d), lambda i, f, mt: (i, 0)),
            scratch_shapes=[pltpu.VMEM((tm, d), F32)],
        ),
        out_shape=jax.ShapeDtypeStruct((n_rows, d), BF16),
        compiler_params=_cparams(("arbitrary", "arbitrary"), resident_weights=True),
        name="grouped_swiglu",
    )(meta, xs, w_gu, w_gu, w_down)


def _combine_kernel(runs_ref, x_ref, w_ref, mod_ref, fg_ref, ys_ref, o_ref, y_ref, sems, *, n_experts, d,
                    final_norm):
    i = pl.program_id(0)
    slot = i % 2

    @pl.when(i == 0)
    def _():
        y_ref[...] = jnp.zeros_like(y_ref)

    def fetch_runs(tile, to_slot):
        for e in range(n_experts):
            entry = (tile * n_experts + e) * LANE
            loc0, src0 = runs_ref[entry], runs_ref[entry + 2]

            def piece(c, carry):
                pltpu.make_async_copy(
                    ys_ref.at[pl.ds(pl.multiple_of(src0 + c * RUN_ALIGN, RUN_ALIGN), RUN_ALIGN), :],
                    y_ref.at[to_slot, pl.ds(pl.multiple_of(loc0 + c * RUN_ALIGN, RUN_ALIGN), RUN_ALIGN), :],
                    sems.at[to_slot]).start()
                return carry

            lax.fori_loop(0, runs_ref[entry + 1], piece, 0)

    @pl.when(i == 0)
    def _():
        fetch_runs(0, 0)

    @pl.when(i + 1 < pl.num_programs(0))
    def _():
        fetch_runs(i + 1, 1 - slot)

    rows = _run_rows(runs_ref, i, n_experts)
    pltpu.make_async_copy(ys_ref.at[pl.ds(0, rows), :], y_ref.at[slot, pl.ds(0, rows), :], sems.at[slot]).wait()
    w = w_ref[...]
    col = lax.broadcasted_iota(jnp.int32, (w.shape[0], y_ref.shape[1]), 1).astype(F32)
    sel = jnp.where(col == w[:, 4:5], w[:, 0:1], 0.0) + jnp.where(col == w[:, 5:6], w[:, 1:2], 0.0)
    moe = jnp.dot(sel.astype(BF16), y_ref[slot], preferred_element_type=F32)
    xn = x_ref[...] + mod_ref[...][:, 5 * d:6 * d] * moe
    if final_norm:
        xn = xn * lax.rsqrt(jnp.mean(xn * xn, axis=-1, keepdims=True) + EPS) * fg_ref[...]
    o_ref[...] = xn


def _combine(x, wts, runs, ys, mod, final_g, *, tt, n_experts, mod_row_of_tile, final_norm):
    m, d = x.shape
    buf_rows = 2 * tt + n_experts * RUN_ALIGN
    return pl.pallas_call(
        functools.partial(_combine_kernel, n_experts=n_experts, d=d, final_norm=final_norm),
        grid_spec=pltpu.PrefetchScalarGridSpec(
            num_scalar_prefetch=1,
            grid=(m // tt,),
            in_specs=[
                pl.BlockSpec((tt, d), lambda i, rn: (i, 0)),
                pl.BlockSpec((tt, LANE), lambda i, rn: (i, 0)),
                pl.BlockSpec((None, 1, 6 * d), lambda i, rn: (mod_row_of_tile(i, tt), 0, 0)),
                pl.BlockSpec((1, d), lambda i, rn: (0, 0)),
                pl.BlockSpec(memory_space=pl.ANY),
            ],
            out_specs=pl.BlockSpec((tt, d), lambda i, rn: (i, 0)),
            scratch_shapes=[pltpu.VMEM((2, buf_rows, d), BF16), pltpu.SemaphoreType.DMA((2,))],
        ),
        out_shape=jax.ShapeDtypeStruct((m, d), F32),
        compiler_params=_cparams(("arbitrary",)),
        name="combine",
    )(runs, x, wts, mod, final_g.reshape(1, d), ys)


def _moe(h2, x, router_w, mod, final_g, w_gu, w_down, *, seq_len, mod_row_of_tile, final_norm):
    m, d = x.shape
    n_experts = w_down.shape[0]
    tm = 512
    tt = min(512, m)
    n_rows = pl.cdiv(2 * m + m // tt * n_experts * RUN_ALIGN, tm) * tm + n_experts * tm
    assert n_rows // tm <= LANE
    wts, pairs, runs, counts = _router(h2, router_w, tm=tt)
    rundest, meta2d = _plan(runs, counts, tile_rows=tm)
    run_table = rundest.reshape(-1)
    meta = meta2d[0:META_ROWS].reshape(META_ROWS * LANE)
    xs = _dispatch(h2, pairs, run_table, meta, n_rows, tt=tt, tm=tm, n_experts=n_experts)
    ys = _grouped_swiglu(xs, meta, w_gu, w_down, tm=tm)
    return _combine(x, wts, run_table, ys, mod, final_g, tt=tt, n_experts=n_experts,
                    mod_row_of_tile=mod_row_of_tile, final_norm=final_norm)


def kernel(x, c, ctx, c_ctx, norm1_g, norm2_g, final_g, w_mod, b_mod, w_in, conv_w, sink, pool_w, pool_scale,
           w_branch, w_out, ffn_w_gu, ffn_w_down, router_w, moe_w_gu, moe_w_down):
    bsz, seq, d = x.shape
    lc = ctx.shape[1]
    depth = w_in.shape[0]
    assert bsz + 1 <= 8 and seq % BLOCK == 0 and lc % BLOCK == 0 and seq % GRID_W == 0

    cvec = jnp.zeros((8, d), F32).at[:bsz].set(c).at[bsz].set(c_ctx)
    mods = _modvec(cvec, w_mod, b_mod)
    rope_tabs = _rope_tables(seq)

    lat_row = lambda i, tm: (i * tm) // seq
    ctx_row = lambda i, tm: bsz

    xl = x.reshape(bsz * seq, d)
    xc = ctx.reshape(bsz * lc, d)
    expert_w = next_w = None
    for l in range(depth):
        last = l == depth - 1
        mod = mods[l].reshape(8, 1, 6 * d)
        if next_w is not None:
            w_in_l, wb_l, wo_l = next_w[0], next_w[1].reshape(w_branch.shape[1:]), next_w[2]
        else:
            w_in_l, wb_l, wo_l = w_in[l].astype(BF16), w_branch[l].astype(BF16), w_out[l].astype(BF16)
        pw_l = pool_w[l].astype(BF16)
        routed = l % 2 == 1
        mixer = functools.partial(_merge, mod=mod, norm2_g=norm2_g[l], conv_w=conv_w[l], pool_w=pw_l,
                                  pool_scale=pool_scale[l], w_branch=wb_l, w_out=wo_l,
                                  h2_dtype=F32 if routed else BF16)
        if routed:
            wgu, wd = expert_w if expert_w is not None else (moe_w_gu[l // 2].astype(BF16),
                                                             moe_w_down[l // 2].astype(BF16))
        else:
            wgu, wd = ffn_w_gu[l // 2].astype(BF16), ffn_w_down[l // 2].astype(BF16)
        ride = not routed and not last
        ride_gu = moe_w_gu[(l + 1) // 2].reshape(-1, moe_w_gu.shape[-1]) if ride else None
        ride_d = moe_w_down[(l + 1) // 2].reshape(-1, d) if ride else None

        def channel_mix(mixed, *, seq_len, row_fn, final_norm, casts=()):
            xm, h2 = mixed[0], mixed[1]
            if routed:
                return _moe(h2, xm, router_w[l // 2], mod, final_g, wgu, wd, seq_len=seq_len,
                            mod_row_of_tile=row_fn, final_norm=final_norm)
            assert not final_norm
            return _ffn_dense(h2, xm, mod, wgu, wd, seq_len=seq_len, mod_row_of_tile=row_fn, casts=casts)

        if last:
            kvc = _inproj(xc, norm1_g[l], mod, w_in_l, seq_len=lc, mod_row_of_tile=ctx_row, kv_only=True)
        else:
            qc, kvc, mixc, gatec = _inproj(xc, norm1_g[l], mod, w_in_l, seq_len=lc, mod_row_of_tile=ctx_row)
            attn_c = _attention(qc.reshape(bsz, lc, -1), None, kvc.reshape(bsz, lc, -1), sink[l], band=False)
            mixed_c = mixer(attn_c.reshape(bsz * lc, -1), mixc, gatec, xc, seq_len=lc, mod_row_of_tile=ctx_row)
            xc_next = channel_mix(mixed_c, seq_len=lc, row_fn=ctx_row, final_norm=False)
        q, kv, mix, gate = _inproj(xl, norm1_g[l], mod, w_in_l, seq_len=seq, mod_row_of_tile=lat_row,
                                   rope_tabs=rope_tabs)
        attn = _attention(q.reshape(bsz, seq, -1), kv.reshape(bsz, seq, -1), kvc.reshape(bsz, lc, -1), sink[l],
                          band=True, cast=ride_gu)
        merged = mixer((attn[0] if ride else attn).reshape(bsz * seq, -1), mix, gate, xl, seq_len=seq,
                       mod_row_of_tile=lat_row, cast=ride_d)
        if routed or last:
            xl = channel_mix(merged, seq_len=seq, row_fn=lat_row, final_norm=last and routed)
            next_w = None
        else:
            stacked = [w_in, w_branch.reshape(depth, -1, d), w_out]
            xl, *next_w = channel_mix(merged, seq_len=seq, row_fn=lat_row, final_norm=False,
                                      casts=[(w, l + 1) for w in stacked])
        expert_w = (attn[1].reshape(moe_w_gu.shape[1:]), merged[2].reshape(moe_w_down.shape[1:])) if ride else None
        if not last:
            xc = xc_next
    if depth % 2 == 1:
        raise NotImplementedError("final norm is fused into the expert layer; depth must be even")
    return xl.reshape(bsz, seq, d)
```

```python
import functools

import jax
import jax.numpy as jnp
from jax import lax
from jax.experimental import pallas as pl
from jax.experimental.pallas import tpu as pltpu

F32 = jnp.float32
BF16 = jnp.bfloat16

GRID_W = 64
EPS = 1e-6
NEG_INF = -1e30
HEAD_DIM = 64
N_HEADS = 8
N_KV_HEADS = 2
GROUP = N_HEADS // N_KV_HEADS
WINDOW = 128
BLOCK = 128
ROPE_THETA = 10000.0
BRANCH = 512
POOL_SIZES = (2, 4, 8, 16)
POOL_GROUP = 128
Q_END = 512
V_END = 768
MIX_W = 4 * BRANCH
POOL_END = V_END + MIX_W

LANE = 128
MXU_DIM = 256
BF16_SUBLANE_TILE = 16
HALO = BF16_SUBLANE_TILE
MIB = 1024 * 1024
VMEM_MIB = 64
VMEM_LIMIT_RESIDENT = (VMEM_MIB - 8) * MIB
VMEM_LIMIT_STREAMING = (VMEM_MIB - 24) * MIB


def _cparams(sem, resident_weights=False):
    limit = VMEM_LIMIT_RESIDENT if resident_weights else VMEM_LIMIT_STREAMING
    return pltpu.CompilerParams(dimension_semantics=sem, vmem_limit_bytes=limit)


def _pick_chunk(n, cap):
    for unit in (MXU_DIM, LANE):
        fits = [c for c in range(unit, min(n, cap) + 1, unit) if n % c == 0]
        if fits:
            return fits[-1]
    raise ValueError((n, cap))


def _resident(shape):
    nd = len(shape)
    return pl.BlockSpec(shape, lambda *_: (0,) * nd, pipeline_mode=pl.Buffered(1))


def _norm_mod(x, g, shift, scale):
    y = x * lax.rsqrt(jnp.mean(x * x, axis=-1, keepdims=True) + EPS) * g
    return y * (1.0 + scale) + shift


def _modvec_kernel(c_ref, w_ref, b_ref, o_ref):
    c = c_ref[...]
    s = c * jax.nn.sigmoid(c)
    o_ref[...] = jnp.dot(s, w_ref[...], preferred_element_type=F32) + b_ref[...]


def _modvec(cvec, w_mod, b_mod):
    depth, d, n = w_mod.shape
    nc = _pick_chunk(n, 1536)
    return pl.pallas_call(
        _modvec_kernel,
        grid=(depth, n // nc),
        in_specs=[
            pl.BlockSpec((8, d), lambda l, j: (0, 0)),
            pl.BlockSpec((None, d, nc), lambda l, j: (l, 0, j)),
            pl.BlockSpec((None, 1, nc), lambda l, j: (l, 0, j)),
        ],
        out_specs=pl.BlockSpec((None, 8, nc), lambda l, j: (l, 0, j)),
        out_shape=jax.ShapeDtypeStruct((depth, 8, n), F32),
        compiler_params=_cparams(("arbitrary", "arbitrary")),
        name="modvec",
    )(cvec, w_mod, b_mod.reshape(depth, 1, n))


def _rope_tables(seq_len):
    n_freq = HEAD_DIM // 4
    inv = ROPE_THETA ** (-jnp.arange(n_freq, dtype=F32) / n_freq)
    pos = jnp.arange(seq_len)
    row = (pos // GRID_W).astype(F32)[:, None] * inv[None, :]
    col = (pos % GRID_W).astype(F32)[:, None] * inv[None, :]
    zero = jnp.zeros_like(row)
    cos = jnp.concatenate([jnp.cos(row)] * 2 + [jnp.cos(col)] * 2, axis=-1)
    s_lo = jnp.concatenate([-jnp.sin(row), zero, -jnp.sin(col), zero], axis=-1)
    s_hi = jnp.concatenate([zero, jnp.sin(row), zero, jnp.sin(col)], axis=-1)
    return tuple(jnp.tile(t, (1, LANE // HEAD_DIM)) for t in (cos, s_lo, s_hi))


def _inproj_kernel(*refs, rope, kv_only, d):
    x_ref, g_ref, mod_ref, w_ref = refs[:4]
    refs = refs[4:]
    if rope:
        cos_ref, slo_ref, shi_ref = refs[:3]
        refs = refs[3:]

        def rot(z):
            return (z * cos_ref[...] + pltpu.roll(z, LANE - 16, 1) * slo_ref[...]
                    + pltpu.roll(z, 16, 1) * shi_ref[...])
    else:
        def rot(z):
            return z

    mod = mod_ref[...]
    h = _norm_mod(x_ref[...], g_ref[...], mod[:, 0:d], mod[:, d:2 * d]).astype(BF16)

    def proj(c0, c1):
        return jnp.dot(h, w_ref[:, c0:c1], preferred_element_type=F32)

    if kv_only:
        (kv_ref,) = refs
        z = proj(0, 2 * LANE)
        kv_ref[:, 0:LANE] = rot(z[:, 0:LANE]).astype(BF16)
        kv_ref[:, LANE:] = z[:, LANE:].astype(BF16)
        return

    q_ref, kv_ref, mix_ref, gate_ref = refs
    cw = 512
    n_gate = gate_ref.shape[1]
    for c in range(n_gate // cw):
        zg = proj(POOL_END + c * cw, POOL_END + (c + 1) * cw)
        gate_ref[:, c * cw:(c + 1) * cw] = jax.nn.sigmoid(zg).astype(BF16)
    z = proj(0, Q_END)
    for j in range(Q_END // LANE):
        q_ref[:, j * LANE:(j + 1) * LANE] = (rot(z[:, j * LANE:(j + 1) * LANE]) * HEAD_DIM ** -0.5).astype(BF16)
    z = proj(Q_END, V_END)
    kv_ref[:, 0:LANE] = rot(z[:, 0:LANE]).astype(BF16)
    kv_ref[:, LANE:] = z[:, LANE:].astype(BF16)
    for c in range(MIX_W // cw):
        mix_ref[:, c * cw:(c + 1) * cw] = proj(V_END + c * cw, V_END + (c + 1) * cw).astype(BF16)


def _inproj(x, norm_g, mod, w_in, *, seq_len, mod_row_of_tile, rope_tabs=None, kv_only=False):
    m, d = x.shape
    tm = min(1024, seq_len)
    assert m % tm == 0 and seq_len % tm == 0
    tiles_per_seq = seq_len // tm
    rope = rope_tabs is not None
    in_w = w_in.shape[1]
    in_specs = [
        pl.BlockSpec((tm, d), lambda i: (i, 0)),
        pl.BlockSpec((1, d), lambda i: (0, 0)),
        pl.BlockSpec((None, 1, 6 * d), lambda i: (mod_row_of_tile(i, tm), 0, 0)),
        pl.BlockSpec((d, 2 * LANE), lambda i: (0, Q_END // (2 * LANE))) if kv_only else _resident((d, in_w)),
    ]
    args = [x, norm_g.reshape(1, d), mod, w_in]
    if rope:
        in_specs += [pl.BlockSpec((tm, LANE), lambda i: (i % tiles_per_seq, 0))] * 3
        args += list(rope_tabs)
    if kv_only:
        out_specs = pl.BlockSpec((tm, 2 * LANE), lambda i: (i, 0))
        out_shape = jax.ShapeDtypeStruct((m, 2 * LANE), BF16)
    else:
        widths = (Q_END, 2 * LANE, MIX_W, in_w - POOL_END)
        out_specs = [pl.BlockSpec((tm, w), lambda i: (i, 0)) for w in widths]
        out_shape = [jax.ShapeDtypeStruct((m, w), BF16) for w in widths]
    return pl.pallas_call(
        functools.partial(_inproj_kernel, rope=rope, kv_only=kv_only, d=d),
        grid=(m // tm,),
        in_specs=in_specs,
        out_specs=out_specs,
        out_shape=out_shape,
        compiler_params=_cparams(("parallel",), resident_weights=True),
        name="inproj_kv" if kv_only else "inproj",
    )(*args)


ATTN_STRIP = 32
ATTN_QBLOCKS = 2
ATTN_AHEAD = 2


def _attn_kernel(sink_ref, q_ref, *refs, band, carry_cast):
    refs, (s_ref, p_ref) = list(refs[:-2]), refs[-2:]
    qblocks = q_ref.shape[1] // BLOCK
    if carry_cast:
        cast_out = refs.pop()
        cast_in = refs.pop(-2)
        cast_out[...] = cast_in[...].astype(BF16)
    if band:
        kvp_ref, kvm_ref, kvn_ref, kvc_ref, bias_ref, o_ref = refs
    else:
        kvc_ref, o_ref = refs
    kvc = kvc_ref[0]
    lc = kvc.shape[0]
    nloc = 3 * BLOCK if band else 0
    nt = (((1,), (1,)), ((), ()))
    w_ctx = jnp.concatenate([kvc[:, LANE:], jnp.ones((lc, LANE), BF16)], axis=1)
    if band:
        n = pl.program_id(1)
        kv4 = jnp.concatenate([kvp_ref[0], kvm_ref[0], kvn_ref[0]], axis=0)
        w4 = jnp.concatenate([kv4[:, LANE:], jnp.ones((kv4.shape[0], LANE), BF16)], axis=1)
        col = lax.broadcasted_iota(jnp.int32, (1, nloc), 1)
        head_edge = jnp.where((col < BLOCK) & (n == 0), NEG_INF, 0.0)
        tail_edge = jnp.where((col >= 2 * BLOCK) & (n == pl.num_programs(1) - 1), NEG_INF, 0.0)
        biases = ([bias_ref[...] + head_edge] + [bias_ref[...]] * (qblocks - 2)
                  + [bias_ref[...] + tail_edge])
    units = [(sb, h) for sb in range(qblocks) for h in range(N_HEADS)]
    def scores(u):
        sb, h = units[u]
        qh = q_ref[0, sb * BLOCK:(sb + 1) * BLOCK, h * HEAD_DIM:(h + 1) * HEAD_DIM]
        ks = slice(h // GROUP * HEAD_DIM, (h // GROUP + 1) * HEAD_DIM)
        if band:
            kl = kv4[sb * BLOCK:sb * BLOCK + nloc, ks]
            s_ref[u, :, 0:nloc] = lax.dot_general(qh, kl, nt, preferred_element_type=F32) + biases[sb]
        s_ref[u, :, nloc:] = lax.dot_general(qh, kvc[:, ks], nt, preferred_element_type=F32)

    def probs(u):
        sink = sink_ref[units[u][1]]
        esink = []
        for r in range(0, BLOCK, ATTN_STRIP):
            s = s_ref[u, r:r + ATTN_STRIP, :]
            m = jnp.maximum(jnp.max(s, axis=-1, keepdims=True), sink)
            p_ref[u, r:r + ATTN_STRIP, :] = jnp.exp(s - m).astype(BF16)
            esink.append(jnp.exp(sink - m))
        return jnp.concatenate(esink, axis=0)

    def weighted_values(u, esink):
        sb, h = units[u]
        ks = slice(h // GROUP * HEAD_DIM, (h // GROUP + 1) * HEAD_DIM)
        o2 = jnp.dot(p_ref[u, :, nloc:], w_ctx, preferred_element_type=F32)
        if band:
            o2 = o2 + jnp.dot(p_ref[u, :, 0:nloc], w4[sb * BLOCK:sb * BLOCK + nloc], preferred_element_type=F32)
        return o2[:, ks] / (o2[:, LANE:LANE + HEAD_DIM] + esink)

    outs, esinks = [], []
    for u in range(min(ATTN_AHEAD, len(units))):
        scores(u)
    for u in range(len(units)):
        if u + ATTN_AHEAD < len(units):
            scores(u + ATTN_AHEAD)
        esinks.append(probs(u))
        if u >= 1:
            outs.append(weighted_values(u - 1, esinks[u - 1]))
    outs.append(weighted_values(len(units) - 1, esinks[-1]))
    for sb in range(qblocks):
        o_ref[0, sb * BLOCK:(sb + 1) * BLOCK, :] = jnp.concatenate(
            outs[sb * N_HEADS:(sb + 1) * N_HEADS], axis=1).astype(BF16)


def _cast_rider(w, n_steps, index_map):
    rows, cols = w.shape
    assert rows % (n_steps * BF16_SUBLANE_TILE) == 0, (w.shape, n_steps)
    return pl.BlockSpec((rows // n_steps, cols), index_map), jax.ShapeDtypeStruct((rows, cols), BF16)


def _attention(q, kv, kvc, sink, *, band, cast=None):
    b, l, _ = q.shape
    lc = kvc.shape[1]
    nb = l // BLOCK
    qb = min(ATTN_QBLOCKS, nb)
    tq = qb * BLOCK
    assert l % tq == 0 and qb >= 2
    in_specs = [
        pl.BlockSpec(memory_space=pltpu.SMEM),
        pl.BlockSpec((1, tq, N_HEADS * HEAD_DIM), lambda bi, n: (bi, n, 0)),
    ]
    args = [sink, q]
    if band:
        in_specs += [
            pl.BlockSpec((1, BLOCK, 2 * LANE), lambda bi, n: (bi, jnp.maximum(qb * n - 1, 0), 0)),
            pl.BlockSpec((1, tq, 2 * LANE), lambda bi, n: (bi, n, 0)),
            pl.BlockSpec((1, BLOCK, 2 * LANE), lambda bi, n: (bi, jnp.minimum(qb * (n + 1), nb - 1), 0)),
        ]
        args += [kv, kv, kv]
    in_specs.append(pl.BlockSpec((1, lc, 2 * LANE), lambda bi, n: (bi, 0, 0)))
    args.append(kvc)
    nkeys = lc
    if band:
        rel = jnp.arange(3 * BLOCK)[None, :] - BLOCK - jnp.arange(BLOCK)[:, None]
        in_specs.append(pl.BlockSpec((BLOCK, 3 * BLOCK), lambda bi, n: (0, 0)))
        args.append(jnp.where(jnp.abs(rel) <= WINDOW, 0.0, NEG_INF).astype(F32))
        nkeys += 3 * BLOCK
    nq = l // tq
    out_specs = [pl.BlockSpec((1, tq, N_HEADS * HEAD_DIM), lambda bi, n: (bi, n, 0))]
    out_shape = [jax.ShapeDtypeStruct((b, l, N_HEADS * HEAD_DIM), BF16)]
    if cast is not None:
        spec, shape = _cast_rider(cast, b * nq, lambda bi, n: (bi * nq + n, 0))
        in_specs.append(spec)
        args.append(cast)
        out_specs.append(spec)
        out_shape.append(shape)
    outs = pl.pallas_call(
        functools.partial(_attn_kernel, band=band, carry_cast=cast is not None),
        grid=(b, nq),
        in_specs=in_specs,
        out_specs=out_specs,
        out_shape=out_shape,
        scratch_shapes=[pltpu.VMEM((qb * N_HEADS, BLOCK, nkeys), F32),
                        pltpu.VMEM((qb * N_HEADS, BLOCK, nkeys), BF16)],
        compiler_params=_cparams(("parallel", "parallel"), resident_weights=True),
        name="attn_band" if band else "attn_ctx",
    )(*args)
    return outs[0] if cast is None else outs


MERGE_PARTS = 2


def _merge_kernel(attn_ref, mix_ref, prev_ref, next_ref, gate_ref, x_ref, mod_ref, n2g_ref, convw_ref,
                  poolw_ref, pscale_ref, wb_ref, wo_ref, *refs, tm, seq_len, d, carry_cast):
    refs = list(refs)
    if carry_cast:
        cast_out = refs.pop()
        cast_in = refs.pop(0)
        cast_out[...] = cast_in[...].astype(BF16)
    xo_ref, h2_ref = refs[:2]
    tile = pl.program_id(0) % (seq_len // tm)
    keep_prev = (tile != 0).astype(F32)
    keep_next = (tile != seq_len // tm - 1).astype(F32)
    mixm = mix_ref[...]
    prev = prev_ref[...].astype(F32) * keep_prev
    nxt = next_ref[...].astype(F32) * keep_next
    b = BRANCH
    cx, cb, cc = (mixm[:, j * b:(j + 1) * b].astype(F32) for j in range(3))

    p = cc * cx
    p_prev = prev[HALO - 1:HALO, 2 * b:3 * b] * prev[HALO - 1:HALO, 0:b]
    p_next = nxt[0:1, 2 * b:3 * b] * nxt[0:1, 0:b]
    ridx = lax.broadcasted_iota(jnp.int32, (tm, b), 0)
    p_dn = jnp.where(ridx == 0, p_prev, pltpu.roll(p, 1, 0))
    p_up = jnp.where(ridx == tm - 1, p_next, pltpu.roll(p, tm - 1, 0))
    cw = convw_ref[...]
    conv_out = (cb * (p_dn * cw[0:1] + p * cw[1:2] + p_up * cw[2:3])).astype(BF16)

    u_main = mixm[:, 3 * b:4 * b].astype(F32)
    u_ext = jnp.concatenate([prev[:, 3 * b:4 * b], u_main, nxt[:, 3 * b:4 * b]], axis=0)
    ext = tm + 2 * HALO

    def shift(a, s):
        return pltpu.roll(a, s % ext, 0)

    tpos = tile * tm + lax.broadcasted_iota(jnp.int32, (tm, 1), 0)
    pooled = []
    for gi, w in enumerate(POOL_SIZES):
        gs = slice(gi * POOL_GROUP, (gi + 1) * POOL_GROUP)
        ug = u_ext[:, gs]
        a = ug + shift(ug, 1)
        ww = 2
        while ww < w:
            a = shift(a, ww // 2) + shift(a, -(ww // 2))
            ww *= 2
        cnt = jnp.minimum(tpos + w // 2, seq_len) - jnp.maximum(tpos - w // 2, 0)
        dlt = a[HALO:HALO + tm] / cnt.astype(F32) - u_main[:, gs]
        pooled.append(jnp.dot(dlt.astype(BF16), poolw_ref[gi], preferred_element_type=F32))
    pool_out = (jnp.concatenate(pooled, axis=1) * pscale_ref[...]).astype(BF16)

    mod = mod_ref[...]
    parts = [slice(r, r + tm // MERGE_PARTS) for r in range(0, tm, tm // MERGE_PARTS)]

    def branches(rs):
        return (jnp.dot(attn_ref[rs, :], wb_ref[0], preferred_element_type=F32),
                jnp.dot(conv_out[rs], wb_ref[1], preferred_element_type=F32),
                jnp.dot(pool_out[rs], wb_ref[2], preferred_element_type=F32))

    def gated(rs, br):
        return (gate_ref[rs, 0:d].astype(F32) * br[0] + gate_ref[rs, d:2 * d].astype(F32) * br[1]
                + gate_ref[rs, 2 * d:3 * d].astype(F32) * br[2]).astype(BF16)

    def finish(rs, o):
        xn = x_ref[rs, :] + mod[:, 2 * d:3 * d] * o
        xo_ref[rs, :] = xn
        h2_ref[rs, :] = _norm_mod(xn, n2g_ref[...], mod[:, 3 * d:4 * d], mod[:, 4 * d:5 * d]).astype(h2_ref.dtype)

    br = [branches(rs) for rs in parts]
    outs = []
    for k, rs in enumerate(parts):
        outs.append(jnp.dot(gated(rs, br[k]), wo_ref[...], preferred_element_type=F32))
        if k >= 1:
            finish(parts[k - 1], outs[k - 1])
    finish(parts[-1], outs[-1])


def _merge(attn, mix, gate, x, mod, norm2_g, conv_w, pool_w, pool_scale, w_branch, w_out, *, seq_len,
           mod_row_of_tile, h2_dtype, cast=None):
    m, d = x.shape
    tm = min(512, seq_len)
    assert m % tm == 0 and seq_len % tm == 0 and tm % HALO == 0
    hb = tm // HALO
    n_halo = m // HALO
    row = lambda w: pl.BlockSpec((tm, w), lambda i: (i, 0))
    in_specs = [
        row(BRANCH),
        row(MIX_W),
        pl.BlockSpec((HALO, MIX_W), lambda i: (jnp.maximum(i * hb - 1, 0), 0)),
        pl.BlockSpec((HALO, MIX_W), lambda i: (jnp.minimum((i + 1) * hb, n_halo - 1), 0)),
        row(3 * d),
        row(d),
        pl.BlockSpec((None, 1, 6 * d), lambda i: (mod_row_of_tile(i, tm), 0, 0)),
        pl.BlockSpec((1, d), lambda i: (0, 0)),
        _resident(conv_w.shape),
        _resident(pool_w.shape),
        pl.BlockSpec((1, BRANCH), lambda i: (0, 0)),
        _resident(w_branch.shape),
        _resident(w_out.shape),
    ]
    args = [attn, mix, mix, mix, gate, x, mod, norm2_g.reshape(1, d), conv_w, pool_w,
            pool_scale.reshape(1, BRANCH), w_branch, w_out]
    out_specs = [row(d), row(d)]
    out_shape = [jax.ShapeDtypeStruct((m, d), F32), jax.ShapeDtypeStruct((m, d), h2_dtype)]
    if cast is not None:
        spec, shape = _cast_rider(cast, m // tm, lambda i: (i, 0))
        in_specs.append(spec)
        args.append(cast)
        out_specs.append(spec)
        out_shape.append(shape)
    return pl.pallas_call(
        functools.partial(_merge_kernel, tm=tm, seq_len=seq_len, d=d, carry_cast=cast is not None),
        grid=(m // tm,),
        in_specs=in_specs,
        out_specs=out_specs,
        out_shape=out_shape,
        compiler_params=_cparams(("parallel",)),
        name="merge",
    )(*args)


def _ffn_kernel(h_ref, x_ref, mod_ref, wgu_ref, wd_ref, *refs, d, d_ff, fc):
    n_cast = len(refs) // 2
    o_ref = refs[n_cast]
    for cast_in, cast_out in zip(refs[:n_cast], refs[n_cast + 1:]):
        cast_out[...] = cast_in[...].astype(BF16)
    h = h_ref[...]
    acc = None
    for f in range(0, d_ff, fc):
        g = jnp.dot(h, wgu_ref[:, f:f + fc], preferred_element_type=F32)
        u = jnp.dot(h, wgu_ref[:, d_ff + f:d_ff + f + fc], preferred_element_type=F32)
        a = (g * jax.nn.sigmoid(g) * u).astype(BF16)
        y = jnp.dot(a, wd_ref[f:f + fc, :], preferred_element_type=F32)
        acc = y if acc is None else acc + y
    o_ref[...] = x_ref[...] + mod_ref[...][:, 5 * d:6 * d] * acc


def _ffn_dense(h2, x, mod, w_gu, w_down, *, seq_len, mod_row_of_tile, casts=()):
    m, d = x.shape
    d_ff = w_down.shape[0]
    tm = min(512, seq_len)
    assert m % tm == 0 and seq_len % tm == 0
    fc = _pick_chunk(d_ff, 3072)
    n_steps = m // tm
    in_specs = [
        pl.BlockSpec((tm, d), lambda i: (i, 0)),
        pl.BlockSpec((tm, d), lambda i: (i, 0)),
        pl.BlockSpec((None, 1, 6 * d), lambda i: (mod_row_of_tile(i, tm), 0, 0)),
        _resident(w_gu.shape),
        _resident(w_down.shape),
    ]
    out_specs = [pl.BlockSpec((tm, d), lambda i: (i, 0))]
    out_shape = [jax.ShapeDtypeStruct((m, d), F32)]
    for w, layer in casts:
        _, rows, cols = w.shape
        assert rows % (n_steps * BF16_SUBLANE_TILE) == 0, (w.shape, n_steps)
        in_specs.append(pl.BlockSpec((None, rows // n_steps, cols), lambda i, layer=layer: (layer, i, 0)))
        out_specs.append(pl.BlockSpec((rows // n_steps, cols), lambda i: (i, 0)))
        out_shape.append(jax.ShapeDtypeStruct((rows, cols), BF16))
    outs = pl.pallas_call(
        functools.partial(_ffn_kernel, d=d, d_ff=d_ff, fc=fc),
        grid=(n_steps,),
        in_specs=in_specs,
        out_specs=out_specs,
        out_shape=out_shape,
        compiler_params=_cparams(("parallel",), resident_weights=True),
        name="ffn_dense",
    )(h2, x, mod, w_gu, w_down, *[w for w, _ in casts])
    return outs if casts else outs[0]


ROUTE_ROWS = 8
RUN_ALIGN = BF16_SUBLANE_TILE
META_ROWS = 5
META_TILES, META_PAD_START, META_PAD_LEN, META_USED = LANE, 2 * LANE, 3 * LANE, 4 * LANE


def _router_kernel(h_ref, rw_ref, tri_ref, wts_ref, pairs_ref, runs_ref, cnt_ref, carry_ref, *, n_experts):
    @pl.when(pl.program_id(0) == 0)
    def _():
        carry_ref[...] = jnp.zeros_like(carry_ref)

    h, rs = h_ref[...], slice(None)
    w = rw_ref[...]
    h_hi, w_hi = h.astype(BF16), w.astype(BF16)
    h_lo, w_lo = (h - h_hi.astype(F32)).astype(BF16), (w - w_hi.astype(F32)).astype(BF16)
    logits = (jnp.dot(h_hi, w_hi, preferred_element_type=F32) + jnp.dot(h_lo, w_hi, preferred_element_type=F32)
              + jnp.dot(h_hi, w_lo, preferred_element_type=F32))
    lane = lax.broadcasted_iota(jnp.int32, logits.shape, 1).astype(F32)
    logits = jnp.where(lane < n_experts, logits, -jnp.inf)
    m1 = jnp.max(logits, axis=-1, keepdims=True)
    i1 = jnp.min(jnp.where(logits == m1, lane, float(LANE)), axis=-1, keepdims=True)
    rest = jnp.where(lane == i1, -jnp.inf, logits)
    m2 = jnp.max(rest, axis=-1, keepdims=True)
    i2 = jnp.min(jnp.where(rest == m2, lane, float(LANE)), axis=-1, keepdims=True)
    e2 = jnp.exp(m2 - m1)
    den = 1.0 + e2
    weights = jnp.where(lane == 0, 1.0 / den, jnp.where(lane == 1, e2 / den, 0.0))

    m1t = (lane == i1).astype(F32).T[0:ROUTE_ROWS]
    m2t = (lane == i2).astype(F32).T[0:ROUTE_ROWS]
    mem = m1t + m2t
    local = jnp.dot(mem.astype(BF16), tri_ref[...], preferred_element_type=F32)
    run = jnp.floor((jnp.sum(mem, axis=1, keepdims=True) + (RUN_ALIGN - 1)) * (1.0 / RUN_ALIGN)) * RUN_ALIGN
    sub8 = lax.broadcasted_iota(jnp.int32, (ROUTE_ROWS, LANE), 0)
    lane8 = lax.broadcasted_iota(jnp.int32, (ROUTE_ROWS, LANE), 1)
    incl = jnp.sum(jnp.where(sub8 <= lane8, run, 0.0), axis=0, keepdims=True)
    run_start = jnp.sum(jnp.where(lane8 == sub8, incl, 0.0), axis=1, keepdims=True) - run
    done = carry_ref[:, 0:1]
    eid = lax.broadcasted_iota(jnp.int32, mem.shape, 0).astype(F32)
    pick = lambda sel, val: jnp.sum(sel * val, axis=0, keepdims=True)
    rows = [pick(m1t, eid), pick(m2t, eid), pick(m1t, local + done), pick(m2t, local + done),
            pick(m1t, local + run_start), pick(m2t, local + run_start)]
    rows += [jnp.zeros_like(rows[0])] * (ROUTE_ROWS - len(rows))
    pairs = jnp.concatenate(rows, axis=0)
    pairs_ref[:, rs] = pairs
    by_token = jnp.concatenate([pairs, jnp.zeros((LANE - ROUTE_ROWS, pairs.shape[1]), F32)], axis=0).T
    wts_ref[rs, :] = jnp.where((lane == 4) | (lane == 5), by_token, weights)
    runs_ref[...] = jnp.where(lane8 == 0, run_start, jnp.where(lane8 == 1, run * (1.0 / RUN_ALIGN),
                                                               jnp.where(lane8 == 2, done, 0.0)))
    carry_ref[...] = carry_ref[...] + run
    cnt_ref[...] = carry_ref[...]


def _router(h2, router_w, *, tm):
    m, d = h2.shape
    n_experts = router_w.shape[1]
    assert n_experts <= ROUTE_ROWS and m % tm == 0
    rw = jnp.pad(router_w, ((0, 0), (0, LANE - n_experts)))
    tri = jnp.triu(jnp.ones((tm, tm), BF16), k=1)
    return pl.pallas_call(
        functools.partial(_router_kernel, n_experts=n_experts),
        grid=(m // tm,),
        in_specs=[
            pl.BlockSpec((tm, d), lambda i: (i, 0)),
            pl.BlockSpec((d, LANE), lambda i: (0, 0)),
            pl.BlockSpec((tm, tm), lambda i: (0, 0)),
        ],
        out_specs=[
            pl.BlockSpec((tm, LANE), lambda i: (i, 0)),
            pl.BlockSpec((ROUTE_ROWS, tm), lambda i: (0, i)),
            pl.BlockSpec((ROUTE_ROWS, LANE), lambda i: (i, 0)),
            pl.BlockSpec((ROUTE_ROWS, LANE), lambda i: (0, 0)),
        ],
        out_shape=[jax.ShapeDtypeStruct((m, LANE), F32), jax.ShapeDtypeStruct((ROUTE_ROWS, m), F32),
                   jax.ShapeDtypeStruct((m // tm * ROUTE_ROWS, LANE), F32),
                   jax.ShapeDtypeStruct((ROUTE_ROWS, LANE), F32)],
        scratch_shapes=[pltpu.VMEM((ROUTE_ROWS, LANE), F32)],
        compiler_params=_cparams(("arbitrary",)),
        name="router",
    )(h2, rw, tri)


def _plan_kernel(runs_ref, cnt_ref, rundest_ref, meta_ref, *, tile_rows):
    cnt = cnt_ref[...]
    padded = jnp.floor((cnt + (tile_rows - 1)) * (1.0 / tile_rows)) * tile_rows
    sub = lax.broadcasted_iota(jnp.int32, cnt.shape, 0)
    lane = lax.broadcasted_iota(jnp.int32, cnt.shape, 1)
    end_row = jnp.sum(jnp.where(sub <= lane, padded, 0.0), axis=0, keepdims=True)
    end_col = jnp.sum(jnp.where(lane == sub, end_row, 0.0), axis=1, keepdims=True)
    start_col = end_col - padded[:, 0:1]
    runs = runs_ref[...]
    starts = jnp.concatenate([start_col] * (runs.shape[0] // ROUTE_ROWS), axis=0)
    rlane = lax.broadcasted_iota(jnp.int32, runs.shape, 1)
    rundest_ref[...] = jnp.where(rlane == 2, runs + starts, runs).astype(jnp.int32)
    tile_start = (lane * tile_rows).astype(F32)
    tile_e = jnp.sum((end_col <= tile_start).astype(F32), axis=0, keepdims=True)
    n_tiles = jnp.max(end_col, axis=0, keepdims=True) * (1.0 / tile_rows)
    pad_start = jnp.sum(jnp.where(lane == sub, start_col + cnt, 0.0), axis=0, keepdims=True)
    pad_len = jnp.sum(jnp.where(lane == sub, padded - cnt, 0.0), axis=0, keepdims=True)
    group_rows_end = jnp.sum(jnp.where(sub.astype(F32) == tile_e, start_col + cnt, 0.0), axis=0, keepdims=True)
    tile_used = jnp.clip(group_rows_end - tile_start[0:1], 0.0, float(tile_rows))
    meta = jnp.concatenate([tile_e, jnp.broadcast_to(n_tiles, tile_e.shape), pad_start, pad_len, tile_used]
                           + [jnp.zeros_like(tile_e)] * (ROUTE_ROWS - META_ROWS), axis=0)
    meta_ref[...] = meta.astype(jnp.int32)


def _plan(runs, counts, *, tile_rows):
    small = pl.BlockSpec((ROUTE_ROWS, LANE), lambda i: (0, 0))
    whole = pl.BlockSpec(runs.shape, lambda i: (0, 0))
    return pl.pallas_call(
        functools.partial(_plan_kernel, tile_rows=tile_rows),
        grid=(1,),
        in_specs=[whole, small],
        out_specs=[whole, small],
        out_shape=[jax.ShapeDtypeStruct(runs.shape, jnp.int32), jax.ShapeDtypeStruct((ROUTE_ROWS, LANE), jnp.int32)],
        compiler_params=_cparams(("arbitrary",)),
        name="plan",
    )(runs, counts)


def _run_rows(runs_ref, tile, n_experts):
    pieces = runs_ref[tile * n_experts * LANE + 1]
    for e in range(1, n_experts):
        pieces = pieces + runs_ref[(tile * n_experts + e) * LANE + 1]
    return pl.multiple_of(pieces * RUN_ALIGN, RUN_ALIGN)


def _dispatch_kernel(runs_ref, meta_ref, h_ref, pos_ref, xs_ref, buf_ref, zero_ref, sems, zsem, *, n_experts,
                     n_tiles_max):
    i = pl.program_id(0)
    tile_rows = zero_ref.shape[0]
    slot = i % 2
    buf = buf_ref.at[slot]
    pos = pos_ref[...]
    row = lax.broadcasted_iota(jnp.int32, (buf.shape[0], pos.shape[1]), 0).astype(F32)
    take = ((row == pos[4:5]).astype(F32) + (row == pos[5:6]).astype(F32)).astype(BF16)
    buf[...] = jnp.dot(take, h_ref[...].astype(BF16), preferred_element_type=F32).astype(BF16)

    def piece_copy(src, src_row, dst_row, s):
        return pltpu.make_async_copy(src.at[pl.ds(pl.multiple_of(src_row, RUN_ALIGN), RUN_ALIGN), :],
                                     xs_ref.at[pl.ds(pl.multiple_of(dst_row, RUN_ALIGN), RUN_ALIGN), :], s)

    for e in range(n_experts):
        entry = (i * n_experts + e) * LANE
        src0, dst0 = runs_ref[entry], runs_ref[entry + 2]

        def piece(c, carry):
            piece_copy(buf, src0 + c * RUN_ALIGN, dst0 + c * RUN_ALIGN, sems.at[slot]).start()
            return carry

        lax.fori_loop(0, runs_ref[entry + 1], piece, 0)

    def wait_runs(tile, sl):
        rows = _run_rows(runs_ref, tile, n_experts)
        pltpu.make_async_copy(buf_ref.at[sl, pl.ds(0, rows), :], xs_ref.at[pl.ds(0, rows), :], sems.at[sl]).wait()

    @pl.when(i > 0)
    def _():
        wait_runs(i - 1, 1 - slot)

    @pl.when(i == pl.num_programs(0) - 1)
    def _():
        wait_runs(i, slot)
        zero_ref[...] = jnp.zeros_like(zero_ref)

        def fill_padding(e, start):
            first = meta_ref[META_PAD_START + e]

            def piece(c, carry):
                cp = piece_copy(zero_ref, 0, first + c * RUN_ALIGN, zsem)
                cp.start() if start else cp.wait()
                return carry

            lax.fori_loop(0, meta_ref[META_PAD_LEN + e] // RUN_ALIGN, piece, 0)

        def tail_copy(k):
            return pltpu.make_async_copy(zero_ref, xs_ref.at[pl.ds(k * tile_rows, tile_rows), :], zsem)

        def zero_tile(k, carry):
            tail_copy(k).start()
            return carry

        def wait_tile(k, carry):
            tail_copy(k).wait()
            return carry

        for e in range(n_experts):
            fill_padding(e, start=True)
        lax.fori_loop(meta_ref[META_TILES], n_tiles_max, zero_tile, 0)
        for e in range(n_experts):
            fill_padding(e, start=False)
        lax.fori_loop(meta_ref[META_TILES], n_tiles_max, wait_tile, 0)


def _dispatch(h2, pairs, runs, meta, n_rows, *, tt, tm, n_experts):
    m, d = h2.shape
    buf_rows = 2 * tt + n_experts * RUN_ALIGN
    return pl.pallas_call(
        functools.partial(_dispatch_kernel, n_experts=n_experts, n_tiles_max=n_rows // tm),
        grid_spec=pltpu.PrefetchScalarGridSpec(
            num_scalar_prefetch=2,
            grid=(m // tt,),
            in_specs=[pl.BlockSpec((tt, d), lambda i, rn, mt: (i, 0)),
                      pl.BlockSpec((ROUTE_ROWS, tt), lambda i, rn, mt: (0, i))],
            out_specs=pl.BlockSpec(memory_space=pl.ANY),
            scratch_shapes=[pltpu.VMEM((2, buf_rows, d), BF16), pltpu.VMEM((tm, d), BF16),
                            pltpu.SemaphoreType.DMA((2,)), pltpu.SemaphoreType.DMA],
        ),
        out_shape=jax.ShapeDtypeStruct((n_rows, d), BF16),
        compiler_params=_cparams(("arbitrary",)),
        name="dispatch",
    )(runs, meta, h2, pairs)


def _gmm_kernel(meta_ref, xs_ref, wg_ref, wu_ref, wd_ref, o_ref, acc_ref):
    f = pl.program_id(1)
    nf = pl.num_programs(1)

    @pl.when((pl.program_id(0) == 0) & (f == 0))
    def _():
        acc_ref[...] = jnp.zeros_like(acc_ref)

    tm = xs_ref.shape[0]
    in_use = pl.program_id(0) < meta_ref[META_TILES]
    used = meta_ref[META_USED + pl.program_id(0)]

    def swiglu_rows(rows):
        xb = xs_ref[0:rows, :]
        g = jnp.dot(xb, wg_ref[...], preferred_element_type=F32)
        u = jnp.dot(xb, wu_ref[...], preferred_element_type=F32)
        a = (g * jax.nn.sigmoid(g) * u).astype(BF16)
        y = jnp.dot(a, wd_ref[...], preferred_element_type=F32)
        total = y + jnp.where(f > 0, acc_ref[0:rows, :], 0.0)
        acc_ref[0:rows, :] = total
        o_ref[0:rows, :] = total.astype(o_ref.dtype)
        if rows < tm:
            o_ref[rows:, :] = jnp.zeros((tm - rows, o_ref.shape[1]), o_ref.dtype)

    @pl.when(in_use & (used > tm // 2))
    def _():
        swiglu_rows(tm)

    @pl.when(in_use & (used <= tm // 2))
    def _():
        swiglu_rows(tm // 2)

    @pl.when((pl.program_id(0) >= meta_ref[META_TILES]) & (f == nf - 1))
    def _():
        o_ref[...] = jnp.zeros_like(o_ref)


def _grouped_swiglu(xs, meta, w_gu, w_down, *, tm):
    n_rows, d = xs.shape
    n_experts, d_ff, _ = w_down.shape
    fc = _pick_chunk(d_ff, d_ff // 2)
    nf = d_ff // fc
    assert nf >= 2 and n_rows % tm == 0 and n_rows // tm <= LANE

    def tile(i, mt):
        return jnp.maximum(jnp.minimum(i, mt[LANE] - 1), 0)

    def expert(i, mt):
        return jnp.minimum(mt[tile(i, mt)], n_experts - 1)

    def chunk(i, f, mt):
        return jnp.where(i < mt[LANE], f, nf - 1)

    return pl.pallas_call(
        _gmm_kernel,
        grid_spec=pltpu.PrefetchScalarGridSpec(
            num_scalar_prefetch=1,
            grid=(n_rows // tm, nf),
            in_specs=[
                pl.BlockSpec((tm, d), lambda i, f, mt: (tile(i, mt), 0)),
                pl.BlockSpec((None, d, fc), lambda i, f, mt: (expert(i, mt), 0, chunk(i, f, mt))),
                pl.BlockSpec((None, d, fc), lambda i, f, mt: (expert(i, mt), 0, nf + chunk(i, f, mt))),
                pl.BlockSpec((None, fc, d), lambda i, f, mt: (expert(i, mt), chunk(i, f, mt), 0)),
            ],
            out_specs=pl.BlockSpec((tm, d), lambda i, f, mt: (i, 0)),
            scratch_shapes=[pltpu.VMEM((tm, d), F32)],
        ),
        out_shape=jax.ShapeDtypeStruct((n_rows, d), BF16),
        compiler_params=_cparams(("arbitrary", "arbitrary"), resident_weights=True),
        name="grouped_swiglu",
    )(meta, xs, w_gu, w_gu, w_down)


def _combine_kernel(runs_ref, x_ref, w_ref, mod_ref, fg_ref, ys_ref, o_ref, y_ref, sems, *, n_experts, d,
                    final_norm):
    i = pl.program_id(0)
    slot = i % 2

    @pl.when(i == 0)
    def _():
        y_ref[...] = jnp.zeros_like(y_ref)

    def fetch_runs(tile, to_slot):
        for e in range(n_experts):
            entry = (tile * n_experts + e) * LANE
            loc0, src0 = runs_ref[entry], runs_ref[entry + 2]

            def piece(c, carry):
                pltpu.make_async_copy(
                    ys_ref.at[pl.ds(pl.multiple_of(src0 + c * RUN_ALIGN, RUN_ALIGN), RUN_ALIGN), :],
                    y_ref.at[to_slot, pl.ds(pl.multiple_of(loc0 + c * RUN_ALIGN, RUN_ALIGN), RUN_ALIGN), :],
                    sems.at[to_slot]).start()
                return carry

            lax.fori_loop(0, runs_ref[entry + 1], piece, 0)

    @pl.when(i == 0)
    def _():
        fetch_runs(0, 0)

    @pl.when(i + 1 < pl.num_programs(0))
    def _():
        fetch_runs(i + 1, 1 - slot)

    rows = _run_rows(runs_ref, i, n_experts)
    pltpu.make_async_copy(ys_ref.at[pl.ds(0, rows), :], y_ref.at[slot, pl.ds(0, rows), :], sems.at[slot]).wait()
    w = w_ref[...]
    col = lax.broadcasted_iota(jnp.int32, (w.shape[0], y_ref.shape[1]), 1).astype(F32)
    sel = jnp.where(col == w[:, 4:5], w[:, 0:1], 0.0) + jnp.where(col == w[:, 5:6], w[:, 1:2], 0.0)
    moe = jnp.dot(sel.astype(BF16), y_ref[slot], preferred_element_type=F32)
    xn = x_ref[...] + mod_ref[...][:, 5 * d:6 * d] * moe
    if final_norm:
        xn = xn * lax.rsqrt(jnp.mean(xn * xn, axis=-1, keepdims=True) + EPS) * fg_ref[...]
    o_ref[...] = xn


def _combine(x, wts, runs, ys, mod, final_g, *, tt, n_experts, mod_row_of_tile, final_norm):
    m, d = x.shape
    buf_rows = 2 * tt + n_experts * RUN_ALIGN
    return pl.pallas_call(
        functools.partial(_combine_kernel, n_experts=n_experts, d=d, final_norm=final_norm),
        grid_spec=pltpu.PrefetchScalarGridSpec(
            num_scalar_prefetch=1,
            grid=(m // tt,),
            in_specs=[
                pl.BlockSpec((tt, d), lambda i, rn: (i, 0)),
                pl.BlockSpec((tt, LANE), lambda i, rn: (i, 0)),
                pl.BlockSpec((None, 1, 6 * d), lambda i, rn: (mod_row_of_tile(i, tt), 0, 0)),
                pl.BlockSpec((1, d), lambda i, rn: (0, 0)),
                pl.BlockSpec(memory_space=pl.ANY),
            ],
            out_specs=pl.BlockSpec((tt, d), lambda i, rn: (i, 0)),
            scratch_shapes=[pltpu.VMEM((2, buf_rows, d), BF16), pltpu.SemaphoreType.DMA((2,))],
        ),
        out_shape=jax.ShapeDtypeStruct((m, d), F32),
        compiler_params=_cparams(("arbitrary",)),
        name="combine",
    )(runs, x, wts, mod, final_g.reshape(1, d), ys)


def _moe(h2, x, router_w, mod, final_g, w_gu, w_down, *, seq_len, mod_row_of_tile, final_norm):
    m, d = x.shape
    n_experts = w_down.shape[0]
    tm = 512
    tt = min(512, m)
    n_rows = pl.cdiv(2 * m + m // tt * n_experts * RUN_ALIGN, tm) * tm + n_experts * tm
    assert n_rows // tm <= LANE
    wts, pairs, runs, counts = _router(h2, router_w, tm=tt)
    rundest, meta2d = _plan(runs, counts, tile_rows=tm)
    run_table = rundest.reshape(-1)
    meta = meta2d[0:META_ROWS].reshape(META_ROWS * LANE)
    xs = _dispatch(h2, pairs, run_table, meta, n_rows, tt=tt, tm=tm, n_experts=n_experts)
    ys = _grouped_swiglu(xs, meta, w_gu, w_down, tm=tm)
    return _combine(x, wts, run_table, ys, mod, final_g, tt=tt, n_experts=n_experts,
                    mod_row_of_tile=mod_row_of_tile, final_norm=final_norm)


def kernel(x, c, ctx, c_ctx, norm1_g, norm2_g, final_g, w_mod, b_mod, w_in, conv_w, sink, pool_w, pool_scale,
           w_branch, w_out, ffn_w_gu, ffn_w_down, router_w, moe_w_gu, moe_w_down):
    bsz, seq, d = x.shape
    lc = ctx.shape[1]
    depth = w_in.shape[0]
    assert bsz + 1 <= 8 and seq % BLOCK == 0 and lc % BLOCK == 0 and seq % GRID_W == 0

    cvec = jnp.zeros((8, d), F32).at[:bsz].set(c).at[bsz].set(c_ctx)
    mods = _modvec(cvec, w_mod, b_mod)
    rope_tabs = _rope_tables(seq)

    lat_row = lambda i, tm: (i * tm) // seq
    ctx_row = lambda i, tm: bsz

    xl = x.reshape(bsz * seq, d)
    xc = ctx.reshape(bsz * lc, d)
    expert_w = next_w = None
    for l in range(depth):
        last = l == depth - 1
        mod = mods[l].reshape(8, 1, 6 * d)
        if next_w is not None:
            w_in_l, wb_l, wo_l = next_w[0], next_w[1].reshape(w_branch.shape[1:]), next_w[2]
        else:
            w_in_l, wb_l, wo_l = w_in[l].astype(BF16), w_branch[l].astype(BF16), w_out[l].astype(BF16)
        pw_l = pool_w[l].astype(BF16)
        routed = l % 2 == 1
        mixer = functools.partial(_merge, mod=mod, norm2_g=norm2_g[l], conv_w=conv_w[l], pool_w=pw_l,
                                  pool_scale=pool_scale[l], w_branch=wb_l, w_out=wo_l,
                                  h2_dtype=F32 if routed else BF16)
        if routed:
            wgu, wd = expert_w if expert_w is not None else (moe_w_gu[l // 2].astype(BF16),
                                                             moe_w_down[l // 2].astype(BF16))
        else:
            wgu, wd = ffn_w_gu[l // 2].astype(BF16), ffn_w_down[l // 2].astype(BF16)
        ride = not routed and not last
        ride_gu = moe_w_gu[(l + 1) // 2].reshape(-1, moe_w_gu.shape[-1]) if ride else None
        ride_d = moe_w_down[(l + 1) // 2].reshape(-1, d) if ride else None

        def channel_mix(mixed, *, seq_len, row_fn, final_norm, casts=()):
            xm, h2 = mixed[0], mixed[1]
            if routed:
                return _moe(h2, xm, router_w[l // 2], mod, final_g, wgu, wd, seq_len=seq_len,
                            mod_row_of_tile=row_fn, final_norm=final_norm)
            assert not final_norm
            return _ffn_dense(h2, xm, mod, wgu, wd, seq_len=seq_len, mod_row_of_tile=row_fn, casts=casts)

        if last:
            kvc = _inproj(xc, norm1_g[l], mod, w_in_l, seq_len=lc, mod_row_of_tile=ctx_row, kv_only=True)
        else:
            qc, kvc, mixc, gatec = _inproj(xc, norm1_g[l], mod, w_in_l, seq_len=lc, mod_row_of_tile=ctx_row)
            attn_c = _attention(qc.reshape(bsz, lc, -1), None, kvc.reshape(bsz, lc, -1), sink[l], band=False)
            mixed_c = mixer(attn_c.reshape(bsz * lc, -1), mixc, gatec, xc, seq_len=lc, mod_row_of_tile=ctx_row)
            xc_next = channel_mix(mixed_c, seq_len=lc, row_fn=ctx_row, final_norm=False)
        q, kv, mix, gate = _inproj(xl, norm1_g[l], mod, w_in_l, seq_len=seq, mod_row_of_tile=lat_row,
                                   rope_tabs=rope_tabs)
        attn = _attention(q.reshape(bsz, seq, -1), kv.reshape(bsz, seq, -1), kvc.reshape(bsz, lc, -1), sink[l],
                          band=True, cast=ride_gu)
        merged = mixer((attn[0] if ride else attn).reshape(bsz * seq, -1), mix, gate, xl, seq_len=seq,
                       mod_row_of_tile=lat_row, cast=ride_d)
        if routed or last:
            xl = channel_mix(merged, seq_len=seq, row_fn=lat_row, final_norm=last and routed)
            next_w = None
        else:
            stacked = [w_in, w_branch.reshape(depth, -1, d), w_out]
            xl, *next_w = channel_mix(merged, seq_len=seq, row_fn=lat_row, final_norm=False,
                                      casts=[(w, l + 1) for w in stacked])
        expert_w = (attn[1].reshape(moe_w_gu.shape[1:]), merged[2].reshape(moe_w_down.shape[1:])) if ride else None
        if not last:
            xc = xc_next
    if depth % 2 == 1:
        raise NotImplementedError("final norm is fused into the expert layer; depth must be even")
    return xl.reshape(bsz, seq, d)
```

```python
import functools

import jax
import jax.numpy as jnp
from jax import lax
from jax.experimental import pallas as pl
from jax.experimental.pallas import tpu as pltpu

F32 = jnp.float32
BF16 = jnp.bfloat16

GRID_W = 64
EPS = 1e-6
NEG_INF = -1e30
HEAD_DIM = 64
N_HEADS = 8
N_KV_HEADS = 2
GROUP = N_HEADS // N_KV_HEADS
WINDOW = 128
BLOCK = 128
ROPE_THETA = 10000.0
BRANCH = 512
POOL_SIZES = (2, 4, 8, 16)
POOL_GROUP = 128
Q_END = 512
V_END = 768
MIX_W = 4 * BRANCH
POOL_END = V_END + MIX_W

LANE = 128
MXU_DIM = 256
BF16_SUBLANE_TILE = 16
HALO = BF16_SUBLANE_TILE
MIB = 1024 * 1024
VMEM_MIB = 64
VMEM_LIMIT_RESIDENT = (VMEM_MIB - 8) * MIB
VMEM_LIMIT_STREAMING = (VMEM_MIB - 24) * MIB


def _cparams(sem, resident_weights=False):
    limit = VMEM_LIMIT_RESIDENT if resident_weights else VMEM_LIMIT_STREAMING
    return pltpu.CompilerParams(dimension_semantics=sem, vmem_limit_bytes=limit)


def _pick_chunk(n, cap):
    for unit in (MXU_DIM, LANE):
        fits = [c for c in range(unit, min(n, cap) + 1, unit) if n % c == 0]
        if fits:
            return fits[-1]
    raise ValueError((n, cap))


def _resident(shape):
    nd = len(shape)
    return pl.BlockSpec(shape, lambda *_: (0,) * nd, pipeline_mode=pl.Buffered(1))


def _norm_mod(x, g, shift, scale):
    y = x * lax.rsqrt(jnp.mean(x * x, axis=-1, keepdims=True) + EPS) * g
    return y * (1.0 + scale) + shift


def _modvec_kernel(c_ref, w_ref, b_ref, o_ref):
    c = c_ref[...]
    s = c * jax.nn.sigmoid(c)
    o_ref[...] = jnp.dot(s, w_ref[...], preferred_element_type=F32) + b_ref[...]


def _modvec(cvec, w_mod, b_mod):
    depth, d, n = w_mod.shape
    nc = _pick_chunk(n, 1536)
    return pl.pallas_call(
        _modvec_kernel,
        grid=(depth, n // nc),
        in_specs=[
            pl.BlockSpec((8, d), lambda l, j: (0, 0)),
            pl.BlockSpec((None, d, nc), lambda l, j: (l, 0, j)),
            pl.BlockSpec((None, 1, nc), lambda l, j: (l, 0, j)),
        ],
        out_specs=pl.BlockSpec((None, 8, nc), lambda l, j: (l, 0, j)),
        out_shape=jax.ShapeDtypeStruct((depth, 8, n), F32),
        compiler_params=_cparams(("arbitrary", "arbitrary")),
        name="modvec",
    )(cvec, w_mod, b_mod.reshape(depth, 1, n))


def _rope_tables(seq_len):
    n_freq = HEAD_DIM // 4
    inv = ROPE_THETA ** (-jnp.arange(n_freq, dtype=F32) / n_freq)
    pos = jnp.arange(seq_len)
    row = (pos // GRID_W).astype(F32)[:, None] * inv[None, :]
    col = (pos % GRID_W).astype(F32)[:, None] * inv[None, :]
    zero = jnp.zeros_like(row)
    cos = jnp.concatenate([jnp.cos(row)] * 2 + [jnp.cos(col)] * 2, axis=-1)
    s_lo = jnp.concatenate([-jnp.sin(row), zero, -jnp.sin(col), zero], axis=-1)
    s_hi = jnp.concatenate([zero, jnp.sin(row), zero, jnp.sin(col)], axis=-1)
    return tuple(jnp.tile(t, (1, LANE // HEAD_DIM)) for t in (cos, s_lo, s_hi))


def _inproj_kernel(*refs, rope, kv_only, d):
    x_ref, g_ref, mod_ref, w_ref = refs[:4]
    refs = refs[4:]
    if rope:
        cos_ref, slo_ref, shi_ref = refs[:3]
        refs = refs[3:]

        def rot(z):
            return (z * cos_ref[...] + pltpu.roll(z, LANE - 16, 1) * slo_ref[...]
                    + pltpu.roll(z, 16, 1) * shi_ref[...])
    else:
        def rot(z):
            return z

    mod = mod_ref[...]
    h = _norm_mod(x_ref[...], g_ref[...], mod[:, 0:d], mod[:, d:2 * d]).astype(BF16)

    def proj(c0, c1):
        return jnp.dot(h, w_ref[:, c0:c1], preferred_element_type=F32)

    if kv_only:
        (kv_ref,) = refs
        z = proj(0, 2 * LANE)
        kv_ref[:, 0:LANE] = rot(z[:, 0:LANE]).astype(BF16)
        kv_ref[:, LANE:] = z[:, LANE:].astype(BF16)
        return

    q_ref, kv_ref, mix_ref, gate_ref = refs
    cw = 512
    n_gate = gate_ref.shape[1]
    for c in range(n_gate // cw):
        zg = proj(POOL_END + c * cw, POOL_END + (c + 1) * cw)
        gate_ref[:, c * cw:(c + 1) * cw] = jax.nn.sigmoid(zg).astype(BF16)
    z = proj(0, Q_END)
    for j in range(Q_END // LANE):
        q_ref[:, j * LANE:(j + 1) * LANE] = (rot(z[:, j * LANE:(j + 1) * LANE]) * HEAD_DIM ** -0.5).astype(BF16)
    z = proj(Q_END, V_END)
    kv_ref[:, 0:LANE] = rot(z[:, 0:LANE]).astype(BF16)
    kv_ref[:, LANE:] = z[:, LANE:].astype(BF16)
    for c in range(MIX_W // cw):
        mix_ref[:, c * cw:(c + 1) * cw] = proj(V_END + c * cw, V_END + (c + 1) * cw).astype(BF16)


def _inproj(x, norm_g, mod, w_in, *, seq_len, mod_row_of_tile, rope_tabs=None, kv_only=False):
    m, d = x.shape
    tm = min(1024, seq_len)
    assert m % tm == 0 and seq_len % tm == 0
    tiles_per_seq = seq_len // tm
    rope = rope_tabs is not None
    in_w = w_in.shape[1]
    in_specs = [
        pl.BlockSpec((tm, d), lambda i: (i, 0)),
        pl.BlockSpec((1, d), lambda i: (0, 0)),
        pl.BlockSpec((None, 1, 6 * d), lambda i: (mod_row_of_tile(i, tm), 0, 0)),
        pl.BlockSpec((d, 2 * LANE), lambda i: (0, Q_END // (2 * LANE))) if kv_only else _resident((d, in_w)),
    ]
    args = [x, norm_g.reshape(1, d), mod, w_in]
    if rope:
        in_specs += [pl.BlockSpec((tm, LANE), lambda i: (i % tiles_per_seq, 0))] * 3
        args += list(rope_tabs)
    if kv_only:
        out_specs = pl.BlockSpec((tm, 2 * LANE), lambda i: (i, 0))
        out_shape = jax.ShapeDtypeStruct((m, 2 * LANE), BF16)
    else:
        widths = (Q_END, 2 * LANE, MIX_W, in_w - POOL_END)
        out_specs = [pl.BlockSpec((tm, w), lambda i: (i, 0)) for w in widths]
        out_shape = [jax.ShapeDtypeStruct((m, w), BF16) for w in widths]
    return pl.pallas_call(
        functools.partial(_inproj_kernel, rope=rope, kv_only=kv_only, d=d),
        grid=(m // tm,),
        in_specs=in_specs,
        out_specs=out_specs,
        out_shape=out_shape,
        compiler_params=_cparams(("parallel",), resident_weights=True),
        name="inproj_kv" if kv_only else "inproj",
    )(*args)


ATTN_STRIP = 32
ATTN_QBLOCKS = 2
ATTN_AHEAD = 2


def _attn_kernel(sink_ref, q_ref, *refs, band, carry_cast):
    refs, (s_ref, p_ref) = list(refs[:-2]), refs[-2:]
    qblocks = q_ref.shape[1] // BLOCK
    if carry_cast:
        cast_out = refs.pop()
        cast_in = refs.pop(-2)
        cast_out[...] = cast_in[...].astype(BF16)
    if band:
        kvp_ref, kvm_ref, kvn_ref, kvc_ref, bias_ref, o_ref = refs
    else:
        kvc_ref, o_ref = refs
    kvc = kvc_ref[0]
    lc = kvc.shape[0]
    nloc = 3 * BLOCK if band else 0
    nt = (((1,), (1,)), ((), ()))
    w_ctx = jnp.concatenate([kvc[:, LANE:], jnp.ones((lc, LANE), BF16)], axis=1)
    if band:
        n = pl.program_id(1)
        kv4 = jnp.concatenate([kvp_ref[0], kvm_ref[0], kvn_ref[0]], axis=0)
        w4 = jnp.concatenate([kv4[:, LANE:], jnp.ones((kv4.shape[0], LANE), BF16)], axis=1)
        col = lax.broadcasted_iota(jnp.int32, (1, nloc), 1)
        head_edge = jnp.where((col < BLOCK) & (n == 0), NEG_INF, 0.0)
        tail_edge = jnp.where((col >= 2 * BLOCK) & (n == pl.num_programs(1) - 1), NEG_INF, 0.0)
        biases = ([bias_ref[...] + head_edge] + [bias_ref[...]] * (qblocks - 2)
                  + [bias_ref[...] + tail_edge])
    units = [(sb, h) for sb in range(qblocks) for h in range(N_HEADS)]
    def scores(u):
        sb, h = units[u]
        qh = q_ref[0, sb * BLOCK:(sb + 1) * BLOCK, h * HEAD_DIM:(h + 1) * HEAD_DIM]
        ks = slice(h // GROUP * HEAD_DIM, (h // GROUP + 1) * HEAD_DIM)
        if band:
            kl = kv4[sb * BLOCK:sb * BLOCK + nloc, ks]
            s_ref[u, :, 0:nloc] = lax.dot_general(qh, kl, nt, preferred_element_type=F32) + biases[sb]
        s_ref[u, :, nloc:] = lax.dot_general(qh, kvc[:, ks], nt, preferred_element_type=F32)

    def probs(u):
        sink = sink_ref[units[u][1]]
        esink = []
        for r in range(0, BLOCK, ATTN_STRIP):
            s = s_ref[u, r:r + ATTN_STRIP, :]
            m = jnp.maximum(jnp.max(s, axis=-1, keepdims=True), sink)
            p_ref[u, r:r + ATTN_STRIP, :] = jnp.exp(s - m).astype(BF16)
            esink.append(jnp.exp(sink - m))
        return jnp.concatenate(esink, axis=0)

    def weighted_values(u, esink):
        sb, h = units[u]
        ks = slice(h // GROUP * HEAD_DIM, (h // GROUP + 1) * HEAD_DIM)
        o2 = jnp.dot(p_ref[u, :, nloc:], w_ctx, preferred_element_type=F32)
        if band:
            o2 = o2 + jnp.dot(p_ref[u, :, 0:nloc], w4[sb * BLOCK:sb * BLOCK + nloc], preferred_element_type=F32)
        return o2[:, ks] / (o2[:, LANE:LANE + HEAD_DIM] + esink)

    outs, esinks = [], []
    for u in range(min(ATTN_AHEAD, len(units))):
        scores(u)
    for u in range(len(units)):
        if u + ATTN_AHEAD < len(units):
            scores(u + ATTN_AHEAD)
        esinks.append(probs(u))
        if u >= 1:
            outs.append(weighted_values(u - 1, esinks[u - 1]))
    outs.append(weighted_values(len(units) - 1, esinks[-1]))
    for sb in range(qblocks):
        o_ref[0, sb * BLOCK:(sb + 1) * BLOCK, :] = jnp.concatenate(
            outs[sb * N_HEADS:(sb + 1) * N_HEADS], axis=1).astype(BF16)


def _cast_rider(w, n_steps, index_map):
    rows, cols = w.shape
    assert rows % (n_steps * BF16_SUBLANE_TILE) == 0, (w.shape, n_steps)
    return pl.BlockSpec((rows // n_steps, cols), index_map), jax.ShapeDtypeStruct((rows, cols), BF16)


def _attention(q, kv, kvc, sink, *, band, cast=None):
    b, l, _ = q.shape
    lc = kvc.shape[1]
    nb = l // BLOCK
    qb = min(ATTN_QBLOCKS, nb)
    tq = qb * BLOCK
    assert l % tq == 0 and qb >= 2
    in_specs = [
        pl.BlockSpec(memory_space=pltpu.SMEM),
        pl.BlockSpec((1, tq, N_HEADS * HEAD_DIM), lambda bi, n: (bi, n, 0)),
    ]
    args = [sink, q]
    if band:
        in_specs += [
            pl.BlockSpec((1, BLOCK, 2 * LANE), lambda bi, n: (bi, jnp.maximum(qb * n - 1, 0), 0)),
            pl.BlockSpec((1, tq, 2 * LANE), lambda bi, n: (bi, n, 0)),
            pl.BlockSpec((1, BLOCK, 2 * LANE), lambda bi, n: (bi, jnp.minimum(qb * (n + 1), nb - 1), 0)),
        ]
        args += [kv, kv, kv]
    in_specs.append(pl.BlockSpec((1, lc, 2 * LANE), lambda bi, n: (bi, 0, 0)))
    args.append(kvc)
    nkeys = lc
    if band:
        rel = jnp.arange(3 * BLOCK)[None, :] - BLOCK - jnp.arange(BLOCK)[:, None]
        in_specs.append(pl.BlockSpec((BLOCK, 3 * BLOCK), lambda bi, n: (0, 0)))
        args.append(jnp.where(jnp.abs(rel) <= WINDOW, 0.0, NEG_INF).astype(F32))
        nkeys += 3 * BLOCK
    nq = l // tq
    out_specs = [pl.BlockSpec((1, tq, N_HEADS * HEAD_DIM), lambda bi, n: (bi, n, 0))]
    out_shape = [jax.ShapeDtypeStruct((b, l, N_HEADS * HEAD_DIM), BF16)]
    if cast is not None:
        spec, shape = _cast_rider(cast, b * nq, lambda bi, n: (bi * nq + n, 0))
        in_specs.append(spec)
        args.append(cast)
        out_specs.append(spec)
        out_shape.append(shape)
    outs = pl.pallas_call(
        functools.partial(_attn_kernel, band=band, carry_cast=cast is not None),
        grid=(b, nq),
        in_specs=in_specs,
        out_specs=out_specs,
        out_shape=out_shape,
        scratch_shapes=[pltpu.VMEM((qb * N_HEADS, BLOCK, nkeys), F32),
                        pltpu.VMEM((qb * N_HEADS, BLOCK, nkeys), BF16)],
        compiler_params=_cparams(("parallel", "parallel"), resident_weights=True),
        name="attn_band" if band else "attn_ctx",
    )(*args)
    return outs[0] if cast is None else outs


MERGE_PARTS = 2


def _merge_kernel(attn_ref, mix_ref, prev_ref, next_ref, gate_ref, x_ref, mod_ref, n2g_ref, convw_ref,
                  poolw_ref, pscale_ref, wb_ref, wo_ref, *refs, tm, seq_len, d, carry_cast):
    refs = list(refs)
    if carry_cast:
        cast_out = refs.pop()
        cast_in = refs.pop(0)
        cast_out[...] = cast_in[...].astype(BF16)
    xo_ref, h2_ref = refs[:2]
    tile = pl.program_id(0) % (seq_len // tm)
    keep_prev = (tile != 0).astype(F32)
    keep_next = (tile != seq_len // tm - 1).astype(F32)
    mixm = mix_ref[...]
    prev = prev_ref[...].astype(F32) * keep_prev
    nxt = next_ref[...].astype(F32) * keep_next
    b = BRANCH
    cx, cb, cc = (mixm[:, j * b:(j + 1) * b].astype(F32) for j in range(3))

    p = cc * cx
    p_prev = prev[HALO - 1:HALO, 2 * b:3 * b] * prev[HALO - 1:HALO, 0:b]
    p_next = nxt[0:1, 2 * b:3 * b] * nxt[0:1, 0:b]
    ridx = lax.broadcasted_iota(jnp.int32, (tm, b), 0)
    p_dn = jnp.where(ridx == 0, p_prev, pltpu.roll(p, 1, 0))
    p_up = jnp.where(ridx == tm - 1, p_next, pltpu.roll(p, tm - 1, 0))
    cw = convw_ref[...]
    conv_out = (cb * (p_dn * cw[0:1] + p * cw[1:2] + p_up * cw[2:3])).astype(BF16)

    u_main = mixm[:, 3 * b:4 * b].astype(F32)
    u_ext = jnp.concatenate([prev[:, 3 * b:4 * b], u_main, nxt[:, 3 * b:4 * b]], axis=0)
    ext = tm + 2 * HALO

    def shift(a, s):
        return pltpu.roll(a, s % ext, 0)

    tpos = tile * tm + lax.broadcasted_iota(jnp.int32, (tm, 1), 0)
    pooled = []
    for gi, w in enumerate(POOL_SIZES):
        gs = slice(gi * POOL_GROUP, (gi + 1) * POOL_GROUP)
        ug = u_ext[:, gs]
        a = ug + shift(ug, 1)
        ww = 2
        while ww < w:
            a = shift(a, ww // 2) + shift(a, -(ww // 2))
            ww *= 2
        cnt = jnp.minimum(tpos + w // 2, seq_len) - jnp.maximum(tpos - w // 2, 0)
        dlt = a[HALO:HALO + tm] / cnt.astype(F32) - u_main[:, gs]
        pooled.append(jnp.dot(dlt.astype(BF16), poolw_ref[gi], preferred_element_type=F32))
    pool_out = (jnp.concatenate(pooled, axis=1) * pscale_ref[...]).astype(BF16)

    mod = mod_ref[...]
    parts = [slice(r, r + tm // MERGE_PARTS) for r in range(0, tm, tm // MERGE_PARTS)]

    def branches(rs):
        return (jnp.dot(attn_ref[rs, :], wb_ref[0], preferred_element_type=F32),
                jnp.dot(conv_out[rs], wb_ref[1], preferred_element_type=F32),
                jnp.dot(pool_out[rs], wb_ref[2], preferred_element_type=F32))

    def gated(rs, br):
        return (gate_ref[rs, 0:d].astype(F32) * br[0] + gate_ref[rs, d:2 * d].astype(F32) * br[1]
                + gate_ref[rs, 2 * d:3 * d].astype(F32) * br[2]).astype(BF16)

    def finish(rs, o):
        xn = x_ref[rs, :] + mod[:, 2 * d:3 * d] * o
        xo_ref[rs, :] = xn
        h2_ref[rs, :] = _norm_mod(xn, n2g_ref[...], mod[:, 3 * d:4 * d], mod[:, 4 * d:5 * d]).astype(h2_ref.dtype)

    br = [branches(rs) for rs in parts]
    outs = []
    for k, rs in enumerate(parts):
        outs.append(jnp.dot(gated(rs, br[k]), wo_ref[...], preferred_element_type=F32))
        if k >= 1:
            finish(parts[k - 1], outs[k - 1])
    finish(parts[-1], outs[-1])


def _merge(attn, mix, gate, x, mod, norm2_g, conv_w, pool_w, pool_scale, w_branch, w_out, *, seq_len,
           mod_row_of_tile, h2_dtype, cast=None):
    m, d = x.shape
    tm = min(512, seq_len)
    assert m % tm == 0 and seq_len % tm == 0 and tm % HALO == 0
    hb = tm // HALO
    n_halo = m // HALO
    row = lambda w: pl.BlockSpec((tm, w), lambda i: (i, 0))
    in_specs = [
        row(BRANCH),
        row(MIX_W),
        pl.BlockSpec((HALO, MIX_W), lambda i: (jnp.maximum(i * hb - 1, 0), 0)),
        pl.BlockSpec((HALO, MIX_W), lambda i: (jnp.minimum((i + 1) * hb, n_halo - 1), 0)),
        row(3 * d),
        row(d),
        pl.BlockSpec((None, 1, 6 * d), lambda i: (mod_row_of_tile(i, tm), 0, 0)),
        pl.BlockSpec((1, d), lambda i: (0, 0)),
        _resident(conv_w.shape),
        _resident(pool_w.shape),
        pl.BlockSpec((1, BRANCH), lambda i: (0, 0)),
        _resident(w_branch.shape),
        _resident(w_out.shape),
    ]
    args = [attn, mix, mix, mix, gate, x, mod, norm2_g.reshape(1, d), conv_w, pool_w,
            pool_scale.reshape(1, BRANCH), w_branch, w_out]
    out_specs = [row(d), row(d)]
    out_shape = [jax.ShapeDtypeStruct((m, d), F32), jax.ShapeDtypeStruct((m, d), h2_dtype)]
    if cast is not None:
        spec, shape = _cast_rider(cast, m // tm, lambda i: (i, 0))
        in_specs.append(spec)
        args.append(cast)
        out_specs.append(spec)
        out_shape.append(shape)
    return pl.pallas_call(
        functools.partial(_merge_kernel, tm=tm, seq_len=seq_len, d=d, carry_cast=cast is not None),
        grid=(m // tm,),
        in_specs=in_specs,
        out_specs=out_specs,
        out_shape=out_shape,
        compiler_params=_cparams(("parallel",)),
        name="merge",
    )(*args)


def _ffn_kernel(h_ref, x_ref, mod_ref, wgu_ref, wd_ref, *refs, d, d_ff, fc):
    n_cast = len(refs) // 2
    o_ref = refs[n_cast]
    for cast_in, cast_out in zip(refs[:n_cast], refs[n_cast + 1:]):
        cast_out[...] = cast_in[...].astype(BF16)
    h = h_ref[...]
    acc = None
    for f in range(0, d_ff, fc):
        g = jnp.dot(h, wgu_ref[:, f:f + fc], preferred_element_type=F32)
        u = jnp.dot(h, wgu_ref[:, d_ff + f:d_ff + f + fc], preferred_element_type=F32)
        a = (g * jax.nn.sigmoid(g) * u).astype(BF16)
        y = jnp.dot(a, wd_ref[f:f + fc, :], preferred_element_type=F32)
        acc = y if acc is None else acc + y
    o_ref[...] = x_ref[...] + mod_ref[...][:, 5 * d:6 * d] * acc


def _ffn_dense(h2, x, mod, w_gu, w_down, *, seq_len, mod_row_of_tile, casts=()):
    m, d = x.shape
    d_ff = w_down.shape[0]
    tm = min(512, seq_len)
    assert m % tm == 0 and seq_len % tm == 0
    fc = _pick_chunk(d_ff, 3072)
    n_steps = m // tm
    in_specs = [
        pl.BlockSpec((tm, d), lambda i: (i, 0)),
        pl.BlockSpec((tm, d), lambda i: (i, 0)),
        pl.BlockSpec((None, 1, 6 * d), lambda i: (mod_row_of_tile(i, tm), 0, 0)),
        _resident(w_gu.shape),
        _resident(w_down.shape),
    ]
    out_specs = [pl.BlockSpec((tm, d), lambda i: (i, 0))]
    out_shape = [jax.ShapeDtypeStruct((m, d), F32)]
    for w, layer in casts:
        _, rows, cols = w.shape
        assert rows % (n_steps * BF16_SUBLANE_TILE) == 0, (w.shape, n_steps)
        in_specs.append(pl.BlockSpec((None, rows // n_steps, cols), lambda i, layer=layer: (layer, i, 0)))
        out_specs.append(pl.BlockSpec((rows // n_steps, cols), lambda i: (i, 0)))
        out_shape.append(jax.ShapeDtypeStruct((rows, cols), BF16))
    outs = pl.pallas_call(
        functools.partial(_ffn_kernel, d=d, d_ff=d_ff, fc=fc),
        grid=(n_steps,),
        in_specs=in_specs,
        out_specs=out_specs,
        out_shape=out_shape,
        compiler_params=_cparams(("parallel",), resident_weights=True),
        name="ffn_dense",
    )(h2, x, mod, w_gu, w_down, *[w for w, _ in casts])
    return outs if casts else outs[0]


ROUTE_ROWS = 8
RUN_ALIGN = BF16_SUBLANE_TILE
META_ROWS = 5
META_TILES, META_PAD_START, META_PAD_LEN, META_USED = LANE, 2 * LANE, 3 * LANE, 4 * LANE


def _router_kernel(h_ref, rw_ref, tri_ref, wts_ref, pairs_ref, runs_ref, cnt_ref, carry_ref, *, n_experts):
    @pl.when(pl.program_id(0) == 0)
    def _():
        carry_ref[...] = jnp.zeros_like(carry_ref)

    h, rs = h_ref[...], slice(None)
    w = rw_ref[...]
    h_hi, w_hi = h.astype(BF16), w.astype(BF16)
    h_lo, w_lo = (h - h_hi.astype(F32)).astype(BF16), (w - w_hi.astype(F32)).astype(BF16)
    both = jnp.dot(h_hi, jnp.concatenate([w_hi, w_lo], axis=1), preferred_element_type=F32)
    logits = both[:, :LANE] + both[:, LANE:] + jnp.dot(h_lo, w_hi, preferred_element_type=F32)
    lane = lax.broadcasted_iota(jnp.int32, logits.shape, 1).astype(F32)
    logits = jnp.where(lane < n_experts, logits, -jnp.inf)
    m1 = jnp.max(logits, axis=-1, keepdims=True)
    i1 = jnp.min(jnp.where(logits == m1, lane, float(LANE)), axis=-1, keepdims=True)
    rest = jnp.where(lane == i1, -jnp.inf, logits)
    m2 = jnp.max(rest, axis=-1, keepdims=True)
    i2 = jnp.min(jnp.where(rest == m2, lane, float(LANE)), axis=-1, keepdims=True)
    e2 = jnp.exp(m2 - m1)
    den = 1.0 + e2
    weights = jnp.where(lane == 0, 1.0 / den, jnp.where(lane == 1, e2 / den, 0.0))

    m1t = (lane == i1).astype(F32).T[0:ROUTE_ROWS]
    m2t = (lane == i2).astype(F32).T[0:ROUTE_ROWS]
    mem = m1t + m2t
    local = jnp.dot(mem.astype(BF16), tri_ref[...], preferred_element_type=F32)
    run = jnp.floor((jnp.sum(mem, axis=1, keepdims=True) + (RUN_ALIGN - 1)) * (1.0 / RUN_ALIGN)) * RUN_ALIGN
    sub8 = lax.broadcasted_iota(jnp.int32, (ROUTE_ROWS, LANE), 0)
    lane8 = lax.broadcasted_iota(jnp.int32, (ROUTE_ROWS, LANE), 1)
    incl = jnp.sum(jnp.where(sub8 <= lane8, run, 0.0), axis=0, keepdims=True)
    run_start = jnp.sum(jnp.where(lane8 == sub8, incl, 0.0), axis=1, keepdims=True) - run
    done = carry_ref[:, 0:1]
    eid = lax.broadcasted_iota(jnp.int32, mem.shape, 0).astype(F32)
    pick = lambda sel, val: jnp.sum(sel * val, axis=0, keepdims=True)
    rows = [pick(m1t, eid), pick(m2t, eid), pick(m1t, local + done), pick(m2t, local + done),
            pick(m1t, local + run_start), pick(m2t, local + run_start)]
    rows += [jnp.zeros_like(rows[0])] * (ROUTE_ROWS - len(rows))
    pairs = jnp.concatenate(rows, axis=0)
    pairs_ref[:, rs] = pairs
    by_token = jnp.concatenate([pairs, jnp.zeros((LANE - ROUTE_ROWS, pairs.shape[1]), F32)], axis=0).T
    wts_ref[rs, :] = jnp.where((lane == 4) | (lane == 5), by_token, weights)
    runs_ref[...] = jnp.where(lane8 == 0, run_start, jnp.where(lane8 == 1, run * (1.0 / RUN_ALIGN),
                                                               jnp.where(lane8 == 2, done, 0.0)))
    carry_ref[...] = carry_ref[...] + run
    cnt_ref[...] = carry_ref[...]


def _router(h2, router_w, *, tm):
    m, d = h2.shape
    n_experts = router_w.shape[1]
    assert n_experts <= ROUTE_ROWS and m % tm == 0
    rw = jnp.pad(router_w, ((0, 0), (0, LANE - n_experts)))
    tri = jnp.triu(jnp.ones((tm, tm), BF16), k=1)
    return pl.pallas_call(
        functools.partial(_router_kernel, n_experts=n_experts),
        grid=(m // tm,),
        in_specs=[
            pl.BlockSpec((tm, d), lambda i: (i, 0)),
            pl.BlockSpec((d, LANE), lambda i: (0, 0)),
            pl.BlockSpec((tm, tm), lambda i: (0, 0)),
        ],
        out_specs=[
            pl.BlockSpec((tm, LANE), lambda i: (i, 0)),
            pl.BlockSpec((ROUTE_ROWS, tm), lambda i: (0, i)),
            pl.BlockSpec((ROUTE_ROWS, LANE), lambda i: (i, 0)),
            pl.BlockSpec((ROUTE_ROWS, LANE), lambda i: (0, 0)),
        ],
        out_shape=[jax.ShapeDtypeStruct((m, LANE), F32), jax.ShapeDtypeStruct((ROUTE_ROWS, m), F32),
                   jax.ShapeDtypeStruct((m // tm * ROUTE_ROWS, LANE), F32),
                   jax.ShapeDtypeStruct((ROUTE_ROWS, LANE), F32)],
        scratch_shapes=[pltpu.VMEM((ROUTE_ROWS, LANE), F32)],
        compiler_params=_cparams(("arbitrary",)),
        name="router",
    )(h2, rw, tri)


def _plan_kernel(runs_ref, cnt_ref, rundest_ref, meta_ref, *, tile_rows):
    cnt = cnt_ref[...]
    padded = jnp.floor((cnt + (tile_rows - 1)) * (1.0 / tile_rows)) * tile_rows
    sub = lax.broadcasted_iota(jnp.int32, cnt.shape, 0)
    lane = lax.broadcasted_iota(jnp.int32, cnt.shape, 1)
    end_row = jnp.sum(jnp.where(sub <= lane, padded, 0.0), axis=0, keepdims=True)
    end_col = jnp.sum(jnp.where(lane == sub, end_row, 0.0), axis=1, keepdims=True)
    start_col = end_col - padded[:, 0:1]
    runs = runs_ref[...]
    starts = jnp.concatenate([start_col] * (runs.shape[0] // ROUTE_ROWS), axis=0)
    rlane = lax.broadcasted_iota(jnp.int32, runs.shape, 1)
    rundest_ref[...] = jnp.where(rlane == 2, runs + starts, runs).astype(jnp.int32)
    tile_start = (lane * tile_rows).astype(F32)
    tile_e = jnp.sum((end_col <= tile_start).astype(F32), axis=0, keepdims=True)
    n_tiles = jnp.max(end_col, axis=0, keepdims=True) * (1.0 / tile_rows)
    pad_start = jnp.sum(jnp.where(lane == sub, start_col + cnt, 0.0), axis=0, keepdims=True)
    pad_len = jnp.sum(jnp.where(lane == sub, padded - cnt, 0.0), axis=0, keepdims=True)
    group_rows_end = jnp.sum(jnp.where(sub.astype(F32) == tile_e, start_col + cnt, 0.0), axis=0, keepdims=True)
    tile_used = jnp.clip(group_rows_end - tile_start[0:1], 0.0, float(tile_rows))
    meta = jnp.concatenate([tile_e, jnp.broadcast_to(n_tiles, tile_e.shape), pad_start, pad_len, tile_used]
                           + [jnp.zeros_like(tile_e)] * (ROUTE_ROWS - META_ROWS), axis=0)
    meta_ref[...] = meta.astype(jnp.int32)


def _plan(runs, counts, *, tile_rows):
    small = pl.BlockSpec((ROUTE_ROWS, LANE), lambda i: (0, 0))
    whole = pl.BlockSpec(runs.shape, lambda i: (0, 0))
    return pl.pallas_call(
        functools.partial(_plan_kernel, tile_rows=tile_rows),
        grid=(1,),
        in_specs=[whole, small],
        out_specs=[whole, small],
        out_shape=[jax.ShapeDtypeStruct(runs.shape, jnp.int32), jax.ShapeDtypeStruct((ROUTE_ROWS, LANE), jnp.int32)],
        compiler_params=_cparams(("arbitrary",)),
        name="plan",
    )(runs, counts)


def _run_rows(runs_ref, tile, n_experts):
    pieces = runs_ref[tile * n_experts * LANE + 1]
    for e in range(1, n_experts):
        pieces = pieces + runs_ref[(tile * n_experts + e) * LANE + 1]
    return pl.multiple_of(pieces * RUN_ALIGN, RUN_ALIGN)


def _dispatch_kernel(runs_ref, meta_ref, h_ref, pos_ref, xs_ref, buf_ref, zero_ref, sems, zsem, *, n_experts,
                     n_tiles_max):
    i = pl.program_id(0)
    tile_rows = zero_ref.shape[0]
    slot = i % 2
    buf = buf_ref.at[slot]
    pos = pos_ref[...]
    row = lax.broadcasted_iota(jnp.int32, (buf.shape[0], pos.shape[1]), 0).astype(F32)
    take = ((row == pos[4:5]).astype(F32) + (row == pos[5:6]).astype(F32)).astype(BF16)
    buf[...] = jnp.dot(take, h_ref[...].astype(BF16), preferred_element_type=F32).astype(BF16)

    def piece_copy(src, src_row, dst_row, s):
        return pltpu.make_async_copy(src.at[pl.ds(pl.multiple_of(src_row, RUN_ALIGN), RUN_ALIGN), :],
                                     xs_ref.at[pl.ds(pl.multiple_of(dst_row, RUN_ALIGN), RUN_ALIGN), :], s)

    for e in range(n_experts):
        entry = (i * n_experts + e) * LANE
        src0, dst0 = runs_ref[entry], runs_ref[entry + 2]

        def piece(c, carry):
            piece_copy(buf, src0 + c * RUN_ALIGN, dst0 + c * RUN_ALIGN, sems.at[slot]).start()
            return carry

        lax.fori_loop(0, runs_ref[entry + 1], piece, 0)

    def wait_runs(tile, sl):
        rows = _run_rows(runs_ref, tile, n_experts)
        pltpu.make_async_copy(buf_ref.at[sl, pl.ds(0, rows), :], xs_ref.at[pl.ds(0, rows), :], sems.at[sl]).wait()

    @pl.when(i > 0)
    def _():
        wait_runs(i - 1, 1 - slot)

    @pl.when(i == pl.num_programs(0) - 1)
    def _():
        wait_runs(i, slot)
        zero_ref[...] = jnp.zeros_like(zero_ref)

        def fill_padding(e, start):
            first = meta_ref[META_PAD_START + e]

            def piece(c, carry):
                cp = piece_copy(zero_ref, 0, first + c * RUN_ALIGN, zsem)
                cp.start() if start else cp.wait()
                return carry

            lax.fori_loop(0, meta_ref[META_PAD_LEN + e] // RUN_ALIGN, piece, 0)

        def tail_copy(k):
            return pltpu.make_async_copy(zero_ref, xs_ref.at[pl.ds(k * tile_rows, tile_rows), :], zsem)

        def zero_tile(k, carry):
            tail_copy(k).start()
            return carry

        def wait_tile(k, carry):
            tail_copy(k).wait()
            return carry

        for e in range(n_experts):
            fill_padding(e, start=True)
        lax.fori_loop(meta_ref[META_TILES], n_tiles_max, zero_tile, 0)
        for e in range(n_experts):
            fill_padding(e, start=False)
        lax.fori_loop(meta_ref[META_TILES], n_tiles_max, wait_tile, 0)


def _dispatch(h2, pairs, runs, meta, n_rows, *, tt, tm, n_experts):
    m, d = h2.shape
    buf_rows = 2 * tt + n_experts * RUN_ALIGN
    return pl.pallas_call(
        functools.partial(_dispatch_kernel, n_experts=n_experts, n_tiles_max=n_rows // tm),
        grid_spec=pltpu.PrefetchScalarGridSpec(
            num_scalar_prefetch=2,
            grid=(m // tt,),
            in_specs=[pl.BlockSpec((tt, d), lambda i, rn, mt: (i, 0)),
                      pl.BlockSpec((ROUTE_ROWS, tt), lambda i, rn, mt: (0, i))],
            out_specs=pl.BlockSpec(memory_space=pl.ANY),
            scratch_shapes=[pltpu.VMEM((2, buf_rows, d), BF16), pltpu.VMEM((tm, d), BF16),
                            pltpu.SemaphoreType.DMA((2,)), pltpu.SemaphoreType.DMA],
        ),
        out_shape=jax.ShapeDtypeStruct((n_rows, d), BF16),
        compiler_params=_cparams(("arbitrary",)),
        name="dispatch",
    )(runs, meta, h2, pairs)


def _gmm_kernel(meta_ref, xs_ref, wg_ref, wu_ref, wd_ref, o_ref, acc_ref):
    f = pl.program_id(1)
    nf = pl.num_programs(1)

    @pl.when((pl.program_id(0) == 0) & (f == 0))
    def _():
        acc_ref[...] = jnp.zeros_like(acc_ref)

    tm = xs_ref.shape[0]
    in_use = pl.program_id(0) < meta_ref[META_TILES]
    used = meta_ref[META_USED + pl.program_id(0)]

    def swiglu_rows(rows):
        xb = xs_ref[0:rows, :]
        g = jnp.dot(xb, wg_ref[...], preferred_element_type=F32)
        u = jnp.dot(xb, wu_ref[...], preferred_element_type=F32)
        a = (g * jax.nn.sigmoid(g) * u).astype(BF16)
        y = jnp.dot(a, wd_ref[...], preferred_element_type=F32)
        total = y + jnp.where(f > 0, acc_ref[0:rows, :], 0.0)
        acc_ref[0:rows, :] = total
        o_ref[0:rows, :] = total.astype(o_ref.dtype)
        if rows < tm:
            o_ref[rows:, :] = jnp.zeros((tm - rows, o_ref.shape[1]), o_ref.dtype)

    @pl.when(in_use & (used > tm // 2))
    def _():
        swiglu_rows(tm)

    @pl.when(in_use & (used <= tm // 2))
    def _():
        swiglu_rows(tm // 2)

    @pl.when((pl.program_id(0) >= meta_ref[META_TILES]) & (f == nf - 1))
    def _():
        o_ref[...] = jnp.zeros_like(o_ref)


def _grouped_swiglu(xs, meta, w_gu, w_down, *, tm):
    n_rows, d = xs.shape
    n_experts, d_ff, _ = w_down.shape
    fc = _pick_chunk(d_ff, d_ff // 2)
    nf = d_ff // fc
    assert nf >= 2 and n_rows % tm == 0 and n_rows // tm <= LANE

    def tile(i, mt):
        return jnp.maximum(jnp.minimum(i, mt[LANE] - 1), 0)

    def expert(i, mt):
        return jnp.minimum(mt[tile(i, mt)], n_experts - 1)

    def chunk(i, f, mt):
        return jnp.where(i < mt[LANE], f, nf - 1)

    return pl.pallas_call(
        _gmm_kernel,
        grid_spec=pltpu.PrefetchScalarGridSpec(
            num_scalar_prefetch=1,
            grid=(n_rows // tm, nf),
            in_specs=[
                pl.BlockSpec((tm, d), lambda i, f, mt: (tile(i, mt), 0)),
                pl.BlockSpec((None, d, fc), lambda i, f, mt: (expert(i, mt), 0, chunk(i, f, mt))),
                pl.BlockSpec((None, d, fc), lambda i, f, mt: (expert(i, mt), 0, nf + chunk(i, f, mt))),
                pl.BlockSpec((None, fc, d), lambda i, f, mt: (expert(i, mt), chunk(i, f, mt), 0)),
            ],
            out_specs=pl.BlockSpec((tm, d), lambda i, f, mt: (i, 0)),
            scratch_shapes=[pltpu.VMEM((tm, d), F32)],
        ),
        out_shape=jax.ShapeDtypeStruct((n_rows, d), BF16),
        compiler_params=_cparams(("arbitrary", "arbitrary"), resident_weights=True),
        name="grouped_swiglu",
    )(meta, xs, w_gu, w_gu, w_down)


def _combine_kernel(runs_ref, x_ref, w_ref, mod_ref, fg_ref, ys_ref, o_ref, y_ref, sems, *, n_experts, d,
                    final_norm):
    i = pl.program_id(0)
    slot = i % 2

    @pl.when(i == 0)
    def _():
        y_ref[...] = jnp.zeros_like(y_ref)

    def fetch_runs(tile, to_slot):
        for e in range(n_experts):
            entry = (tile * n_experts + e) * LANE
            loc0, src0 = runs_ref[entry], runs_ref[entry + 2]

            def piece(c, carry):
                pltpu.make_async_copy(
                    ys_ref.at[pl.ds(pl.multiple_of(src0 + c * RUN_ALIGN, RUN_ALIGN), RUN_ALIGN), :],
                    y_ref.at[to_slot, pl.ds(pl.multiple_of(loc0 + c * RUN_ALIGN, RUN_ALIGN), RUN_ALIGN), :],
                    sems.at[to_slot]).start()
                return carry

            lax.fori_loop(0, runs_ref[entry + 1], piece, 0)

    @pl.when(i == 0)
    def _():
        fetch_runs(0, 0)

    @pl.when(i + 1 < pl.num_programs(0))
    def _():
        fetch_runs(i + 1, 1 - slot)

    rows = _run_rows(runs_ref, i, n_experts)
    pltpu.make_async_copy(ys_ref.at[pl.ds(0, rows), :], y_ref.at[slot, pl.ds(0, rows), :], sems.at[slot]).wait()
    w = w_ref[...]
    col = lax.broadcasted_iota(jnp.int32, (w.shape[0], y_ref.shape[1]), 1).astype(F32)
    sel = jnp.where(col == w[:, 4:5], w[:, 0:1], 0.0) + jnp.where(col == w[:, 5:6], w[:, 1:2], 0.0)
    moe = jnp.dot(sel.astype(BF16), y_ref[slot], preferred_element_type=F32)
    xn = x_ref[...] + mod_ref[...][:, 5 * d:6 * d] * moe
    if final_norm:
        xn = xn * lax.rsqrt(jnp.mean(xn * xn, axis=-1, keepdims=True) + EPS) * fg_ref[...]
    o_ref[...] = xn


def _combine(x, wts, runs, ys, mod, final_g, *, tt, n_experts, mod_row_of_tile, final_norm):
    m, d = x.shape
    buf_rows = 2 * tt + n_experts * RUN_ALIGN
    return pl.pallas_call(
        functools.partial(_combine_kernel, n_experts=n_experts, d=d, final_norm=final_norm),
        grid_spec=pltpu.PrefetchScalarGridSpec(
            num_scalar_prefetch=1,
            grid=(m // tt,),
            in_specs=[
                pl.BlockSpec((tt, d), lambda i, rn: (i, 0)),
                pl.BlockSpec((tt, LANE), lambda i, rn: (i, 0)),
                pl.BlockSpec((None, 1, 6 * d), lambda i, rn: (mod_row_of_tile(i, tt), 0, 0)),
                pl.BlockSpec((1, d), lambda i, rn: (0, 0)),
                pl.BlockSpec(memory_space=pl.ANY),
            ],
            out_specs=pl.BlockSpec((tt, d), lambda i, rn: (i, 0)),
            scratch_shapes=[pltpu.VMEM((2, buf_rows, d), BF16), pltpu.SemaphoreType.DMA((2,))],
        ),
        out_shape=jax.ShapeDtypeStruct((m, d), F32),
        compiler_params=_cparams(("arbitrary",)),
        name="combine",
    )(runs, x, wts, mod, final_g.reshape(1, d), ys)


def _moe(h2, x, router_w, mod, final_g, w_gu, w_down, *, seq_len, mod_row_of_tile, final_norm):
    m, d = x.shape
    n_experts = w_down.shape[0]
    tm = 512
    tt = min(512, m)
    n_rows = pl.cdiv(2 * m + m // tt * n_experts * RUN_ALIGN, tm) * tm + n_experts * tm
    assert n_rows // tm <= LANE
    wts, pairs, runs, counts = _router(h2, router_w, tm=tt)
    rundest, meta2d = _plan(runs, counts, tile_rows=tm)
    run_table = rundest.reshape(-1)
    meta = meta2d[0:META_ROWS].reshape(META_ROWS * LANE)
    xs = _dispatch(h2, pairs, run_table, meta, n_rows, tt=tt, tm=tm, n_experts=n_experts)
    ys = _grouped_swiglu(xs, meta, w_gu, w_down, tm=tm)
    return _combine(x, wts, run_table, ys, mod, final_g, tt=tt, n_experts=n_experts,
                    mod_row_of_tile=mod_row_of_tile, final_norm=final_norm)


def kernel(x, c, ctx, c_ctx, norm1_g, norm2_g, final_g, w_mod, b_mod, w_in, conv_w, sink, pool_w, pool_scale,
           w_branch, w_out, ffn_w_gu, ffn_w_down, router_w, moe_w_gu, moe_w_down):
    bsz, seq, d = x.shape
    lc = ctx.shape[1]
    depth = w_in.shape[0]
    assert bsz + 1 <= 8 and seq % BLOCK == 0 and lc % BLOCK == 0 and seq % GRID_W == 0

    cvec = jnp.zeros((8, d), F32).at[:bsz].set(c).at[bsz].set(c_ctx)
    mods = _modvec(cvec, w_mod, b_mod)
    rope_tabs = _rope_tables(seq)

    lat_row = lambda i, tm: (i * tm) // seq
    ctx_row = lambda i, tm: bsz

    xl = x.reshape(bsz * seq, d)
    xc = ctx.reshape(bsz * lc, d)
    expert_w = next_w = None
    for l in range(depth):
        last = l == depth - 1
        mod = mods[l].reshape(8, 1, 6 * d)
        if next_w is not None:
            w_in_l, wb_l, wo_l = next_w[0], next_w[1].reshape(w_branch.shape[1:]), next_w[2]
        else:
            w_in_l, wb_l, wo_l = w_in[l].astype(BF16), w_branch[l].astype(BF16), w_out[l].astype(BF16)
        pw_l = pool_w[l].astype(BF16)
        routed = l % 2 == 1
        mixer = functools.partial(_merge, mod=mod, norm2_g=norm2_g[l], conv_w=conv_w[l], pool_w=pw_l,
                                  pool_scale=pool_scale[l], w_branch=wb_l, w_out=wo_l,
                                  h2_dtype=F32 if routed else BF16)
        if routed:
            wgu, wd = expert_w if expert_w is not None else (moe_w_gu[l // 2].astype(BF16),
                                                             moe_w_down[l // 2].astype(BF16))
        else:
            wgu, wd = ffn_w_gu[l // 2].astype(BF16), ffn_w_down[l // 2].astype(BF16)
        ride = not routed and not last
        ride_gu = moe_w_gu[(l + 1) // 2].reshape(-1, moe_w_gu.shape[-1]) if ride else None
        ride_d = moe_w_down[(l + 1) // 2].reshape(-1, d) if ride else None

        def channel_mix(mixed, *, seq_len, row_fn, final_norm, casts=()):
            xm, h2 = mixed[0], mixed[1]
            if routed:
                return _moe(h2, xm, router_w[l // 2], mod, final_g, wgu, wd, seq_len=seq_len,
                            mod_row_of_tile=row_fn, final_norm=final_norm)
            assert not final_norm
            return _ffn_dense(h2, xm, mod, wgu, wd, seq_len=seq_len, mod_row_of_tile=row_fn, casts=casts)

        if last:
            kvc = _inproj(xc, norm1_g[l], mod, w_in_l, seq_len=lc, mod_row_of_tile=ctx_row, kv_only=True)
        else:
            qc, kvc, mixc, gatec = _inproj(xc, norm1_g[l], mod, w_in_l, seq_len=lc, mod_row_of_tile=ctx_row)
            attn_c = _attention(qc.reshape(bsz, lc, -1), None, kvc.reshape(bsz, lc, -1), sink[l], band=False)
            mixed_c = mixer(attn_c.reshape(bsz * lc, -1), mixc, gatec, xc, seq_len=lc, mod_row_of_tile=ctx_row)
            xc_next = channel_mix(mixed_c, seq_len=lc, row_fn=ctx_row, final_norm=False)
        q, kv, mix, gate = _inproj(xl, norm1_g[l], mod, w_in_l, seq_len=seq, mod_row_of_tile=lat_row,
                                   rope_tabs=rope_tabs)
        attn = _attention(q.reshape(bsz, seq, -1), kv.reshape(bsz, seq, -1), kvc.reshape(bsz, lc, -1), sink[l],
                          band=True, cast=ride_gu)
        merged = mixer((attn[0] if ride else attn).reshape(bsz * seq, -1), mix, gate, xl, seq_len=seq,
                       mod_row_of_tile=lat_row, cast=ride_d)
        if routed or last:
            xl = channel_mix(merged, seq_len=seq, row_fn=lat_row, final_norm=last and routed)
            next_w = None
        else:
            stacked = [w_in, w_branch.reshape(depth, -1, d), w_out]
            xl, *next_w = channel_mix(merged, seq_len=seq, row_fn=lat_row, final_norm=False,
                                      casts=[(w, l + 1) for w in stacked])
        expert_w = (attn[1].reshape(moe_w_gu.shape[1:]), merged[2].reshape(moe_w_down.shape[1:])) if ride else None
        if not last:
            xc = xc_next
    if depth % 2 == 1:
        raise NotImplementedError("final norm is fused into the expert layer; depth must be even")
    return xl.reshape(bsz, seq, d)
```

```python
import functools

import jax
import jax.numpy as jnp
from jax import lax
from jax.experimental import pallas as pl
from jax.experimental.pallas import tpu as pltpu

F32 = jnp.float32
BF16 = jnp.bfloat16

GRID_W = 64
EPS = 1e-6
NEG_INF = -1e30
LOG2_E = 1.4426950408889634
HEAD_DIM = 64
N_HEADS = 8
N_KV_HEADS = 2
GROUP = N_HEADS // N_KV_HEADS
WINDOW = 128
BLOCK = 128
ROPE_THETA = 10000.0
BRANCH = 512
POOL_SIZES = (2, 4, 8, 16)
POOL_GROUP = 128
Q_END = 512
V_END = 768
MIX_W = 4 * BRANCH
POOL_END = V_END + MIX_W

LANE = 128
MXU_DIM = 256
BF16_SUBLANE_TILE = 16
HALO = BF16_SUBLANE_TILE
MIB = 1024 * 1024
VMEM_MIB = 64
VMEM_LIMIT_RESIDENT = (VMEM_MIB - 8) * MIB
VMEM_LIMIT_STREAMING = (VMEM_MIB - 24) * MIB


def _cparams(sem, resident_weights=False):
    limit = VMEM_LIMIT_RESIDENT if resident_weights else VMEM_LIMIT_STREAMING
    return pltpu.CompilerParams(dimension_semantics=sem, vmem_limit_bytes=limit)


def _pick_chunk(n, cap):
    for unit in (MXU_DIM, LANE):
        fits = [c for c in range(unit, min(n, cap) + 1, unit) if n % c == 0]
        if fits:
            return fits[-1]
    raise ValueError((n, cap))


def _resident(shape):
    nd = len(shape)
    return pl.BlockSpec(shape, lambda *_: (0,) * nd, pipeline_mode=pl.Buffered(1))


def _norm_mod(x, g, shift, scale):
    y = x * lax.rsqrt(jnp.mean(x * x, axis=-1, keepdims=True) + EPS) * g
    return y * (1.0 + scale) + shift


def _modvec_kernel(c_ref, w_ref, b_ref, o_ref):
    c = c_ref[...]
    s = c * jax.nn.sigmoid(c)
    o_ref[...] = jnp.dot(s, w_ref[...], preferred_element_type=F32) + b_ref[...]


def _modvec(cvec, w_mod, b_mod):
    depth, d, n = w_mod.shape
    nc = _pick_chunk(n, 1536)
    return pl.pallas_call(
        _modvec_kernel,
        grid=(depth, n // nc),
        in_specs=[
            pl.BlockSpec((8, d), lambda l, j: (0, 0)),
            pl.BlockSpec((None, d, nc), lambda l, j: (l, 0, j)),
            pl.BlockSpec((None, 1, nc), lambda l, j: (l, 0, j)),
        ],
        out_specs=pl.BlockSpec((None, 8, nc), lambda l, j: (l, 0, j)),
        out_shape=jax.ShapeDtypeStruct((depth, 8, n), F32),
        compiler_params=_cparams(("arbitrary", "arbitrary")),
        name="modvec",
    )(cvec, w_mod, b_mod.reshape(depth, 1, n))


def _rope_tables(seq_len):
    n_freq = HEAD_DIM // 4
    inv = ROPE_THETA ** (-jnp.arange(n_freq, dtype=F32) / n_freq)
    pos = jnp.arange(seq_len)
    row = (pos // GRID_W).astype(F32)[:, None] * inv[None, :]
    col = (pos % GRID_W).astype(F32)[:, None] * inv[None, :]
    zero = jnp.zeros_like(row)
    cos = jnp.concatenate([jnp.cos(row)] * 2 + [jnp.cos(col)] * 2, axis=-1)
    s_lo = jnp.concatenate([-jnp.sin(row), zero, -jnp.sin(col), zero], axis=-1)
    s_hi = jnp.concatenate([zero, jnp.sin(row), zero, jnp.sin(col)], axis=-1)
    return tuple(jnp.tile(t, (1, LANE // HEAD_DIM)) for t in (cos, s_lo, s_hi))


def _inproj_kernel(*refs, rope, kv_only, d):
    x_ref, g_ref, mod_ref, w_ref = refs[:4]
    refs = refs[4:]
    if rope:
        cos_ref, slo_ref, shi_ref = refs[:3]
        refs = refs[3:]

        def rot(z):
            return (z * cos_ref[...] + pltpu.roll(z, LANE - 16, 1) * slo_ref[...]
                    + pltpu.roll(z, 16, 1) * shi_ref[...])
    else:
        def rot(z):
            return z

    mod = mod_ref[...]
    h = _norm_mod(x_ref[...], g_ref[...], mod[:, 0:d], mod[:, d:2 * d]).astype(BF16)

    def proj(c0, c1):
        return jnp.dot(h, w_ref[:, c0:c1], preferred_element_type=F32)

    if kv_only:
        (kv_ref,) = refs
        z = proj(0, 2 * LANE)
        kv_ref[:, 0:LANE] = rot(z[:, 0:LANE]).astype(BF16)
        kv_ref[:, LANE:] = z[:, LANE:].astype(BF16)
        return

    q_ref, kv_ref, mix_ref, gate_ref = refs
    cw = 512
    n_gate = gate_ref.shape[1]
    for c in range(n_gate // cw):
        zg = proj(POOL_END + c * cw, POOL_END + (c + 1) * cw)
        gate_ref[:, c * cw:(c + 1) * cw] = jax.nn.sigmoid(zg).astype(BF16)
    z = proj(0, Q_END)
    for j in range(Q_END // LANE):
        q_ref[:, j * LANE:(j + 1) * LANE] = (rot(z[:, j * LANE:(j + 1) * LANE])
                                             * (HEAD_DIM ** -0.5 * LOG2_E)).astype(BF16)
    z = proj(Q_END, V_END)
    kv_ref[:, 0:LANE] = rot(z[:, 0:LANE]).astype(BF16)
    kv_ref[:, LANE:] = z[:, LANE:].astype(BF16)
    for c in range(MIX_W // cw):
        mix_ref[:, c * cw:(c + 1) * cw] = proj(V_END + c * cw, V_END + (c + 1) * cw).astype(BF16)


def _inproj(x, norm_g, mod, w_in, *, seq_len, mod_row_of_tile, rope_tabs=None, kv_only=False):
    m, d = x.shape
    tm = min(1024, seq_len)
    assert m % tm == 0 and seq_len % tm == 0
    tiles_per_seq = seq_len // tm
    rope = rope_tabs is not None
    in_w = w_in.shape[1]
    in_specs = [
        pl.BlockSpec((tm, d), lambda i: (i, 0)),
        pl.BlockSpec((1, d), lambda i: (0, 0)),
        pl.BlockSpec((None, 1, 6 * d), lambda i: (mod_row_of_tile(i, tm), 0, 0)),
        pl.BlockSpec((d, 2 * LANE), lambda i: (0, Q_END // (2 * LANE))) if kv_only else _resident((d, in_w)),
    ]
    args = [x, norm_g.reshape(1, d), mod, w_in]
    if rope:
        in_specs += [pl.BlockSpec((tm, LANE), lambda i: (i % tiles_per_seq, 0))] * 3
        args += list(rope_tabs)
    if kv_only:
        out_specs = pl.BlockSpec((tm, 2 * LANE), lambda i: (i, 0))
        out_shape = jax.ShapeDtypeStruct((m, 2 * LANE), BF16)
    else:
        widths = (Q_END, 2 * LANE, MIX_W, in_w - POOL_END)
        out_specs = [pl.BlockSpec((tm, w), lambda i: (i, 0)) for w in widths]
        out_shape = [jax.ShapeDtypeStruct((m, w), BF16) for w in widths]
    return pl.pallas_call(
        functools.partial(_inproj_kernel, rope=rope, kv_only=kv_only, d=d),
        grid=(m // tm,),
        in_specs=in_specs,
        out_specs=out_specs,
        out_shape=out_shape,
        compiler_params=_cparams(("parallel",), resident_weights=True),
        name="inproj_kv" if kv_only else "inproj",
    )(*args)


ATTN_STRIP = 32
ATTN_QBLOCKS = 2
ATTN_AHEAD = 2


def _attn_kernel(sink_ref, q_ref, *refs, band, carry_cast):
    refs, (s_ref, p_ref) = list(refs[:-2]), refs[-2:]
    qblocks = q_ref.shape[1] // BLOCK
    if carry_cast:
        cast_out = refs.pop()
        cast_in = refs.pop(-2)
        cast_out[...] = cast_in[...].astype(BF16)
    if band:
        kvp_ref, kvm_ref, kvn_ref, kvc_ref, bias_ref, o_ref = refs
    else:
        kvc_ref, o_ref = refs
    kvc = kvc_ref[0]
    lc = kvc.shape[0]
    nloc = 3 * BLOCK if band else 0
    nt = (((1,), (1,)), ((), ()))
    w_ctx = jnp.concatenate([kvc[:, LANE:], jnp.ones((lc, LANE), BF16)], axis=1)
    if band:
        n = pl.program_id(1)
        kv4 = jnp.concatenate([kvp_ref[0], kvm_ref[0], kvn_ref[0]], axis=0)
        w4 = jnp.concatenate([kv4[:, LANE:], jnp.ones((kv4.shape[0], LANE), BF16)], axis=1)
        col = lax.broadcasted_iota(jnp.int32, (1, nloc), 1)
        head_edge = jnp.where((col < BLOCK) & (n == 0), NEG_INF, 0.0)
        tail_edge = jnp.where((col >= 2 * BLOCK) & (n == pl.num_programs(1) - 1), NEG_INF, 0.0)
        biases = ([bias_ref[...] + head_edge] + [bias_ref[...]] * (qblocks - 2)
                  + [bias_ref[...] + tail_edge])
    units = [(sb, h) for sb in range(qblocks) for h in range(N_HEADS)]
    def scores(u):
        sb, h = units[u]
        qh = q_ref[0, sb * BLOCK:(sb + 1) * BLOCK, h * HEAD_DIM:(h + 1) * HEAD_DIM]
        ks = slice(h // GROUP * HEAD_DIM, (h // GROUP + 1) * HEAD_DIM)
        if band:
            kl = kv4[sb * BLOCK:sb * BLOCK + nloc, ks]
            s_ref[u, :, 0:nloc] = lax.dot_general(qh, kl, nt, preferred_element_type=F32) + biases[sb]
        s_ref[u, :, nloc:] = lax.dot_general(qh, kvc[:, ks], nt, preferred_element_type=F32)

    def probs(u):
        sink = sink_ref[units[u][1]] * LOG2_E
        esink = []
        for r in range(0, BLOCK, ATTN_STRIP):
            s = s_ref[u, r:r + ATTN_STRIP, :]
            m = jnp.maximum(jnp.max(s, axis=-1, keepdims=True), sink)
            p_ref[u, r:r + ATTN_STRIP, :] = jnp.exp2(s - m).astype(BF16)
            esink.append(jnp.exp2(sink - m))
        return jnp.concatenate(esink, axis=0)

    def weighted_values(u, esink):
        sb, h = units[u]
        ks = slice(h // GROUP * HEAD_DIM, (h // GROUP + 1) * HEAD_DIM)
        o2 = jnp.dot(p_ref[u, :, nloc:], w_ctx, preferred_element_type=F32)
        if band:
            o2 = o2 + jnp.dot(p_ref[u, :, 0:nloc], w4[sb * BLOCK:sb * BLOCK + nloc], preferred_element_type=F32)
        return o2[:, ks] / (o2[:, LANE:LANE + HEAD_DIM] + esink)

    outs, esinks = [], []
    for u in range(min(ATTN_AHEAD, len(units))):
        scores(u)
    for u in range(len(units)):
        if u + ATTN_AHEAD < len(units):
            scores(u + ATTN_AHEAD)
        esinks.append(probs(u))
        if u >= 1:
            outs.append(weighted_values(u - 1, esinks[u - 1]))
    outs.append(weighted_values(len(units) - 1, esinks[-1]))
    for sb in range(qblocks):
        o_ref[0, sb * BLOCK:(sb + 1) * BLOCK, :] = jnp.concatenate(
            outs[sb * N_HEADS:(sb + 1) * N_HEADS], axis=1).astype(BF16)


def _cast_rider(w, n_steps, index_map):
    rows, cols = w.shape
    assert rows % (n_steps * BF16_SUBLANE_TILE) == 0, (w.shape, n_steps)
    return pl.BlockSpec((rows // n_steps, cols), index_map), jax.ShapeDtypeStruct((rows, cols), BF16)


def _attention(q, kv, kvc, sink, *, band, cast=None):
    b, l, _ = q.shape
    lc = kvc.shape[1]
    nb = l // BLOCK
    qb = min(ATTN_QBLOCKS, nb)
    tq = qb * BLOCK
    assert l % tq == 0 and qb >= 2
    in_specs = [
        pl.BlockSpec(memory_space=pltpu.SMEM),
        pl.BlockSpec((1, tq, N_HEADS * HEAD_DIM), lambda bi, n: (bi, n, 0)),
    ]
    args = [sink, q]
    if band:
        in_specs += [
            pl.BlockSpec((1, BLOCK, 2 * LANE), lambda bi, n: (bi, jnp.maximum(qb * n - 1, 0), 0)),
            pl.BlockSpec((1, tq, 2 * LANE), lambda bi, n: (bi, n, 0)),
            pl.BlockSpec((1, BLOCK, 2 * LANE), lambda bi, n: (bi, jnp.minimum(qb * (n + 1), nb - 1), 0)),
        ]
        args += [kv, kv, kv]
    in_specs.append(pl.BlockSpec((1, lc, 2 * LANE), lambda bi, n: (bi, 0, 0)))
    args.append(kvc)
    nkeys = lc
    if band:
        rel = jnp.arange(3 * BLOCK)[None, :] - BLOCK - jnp.arange(BLOCK)[:, None]
        in_specs.append(pl.BlockSpec((BLOCK, 3 * BLOCK), lambda bi, n: (0, 0)))
        args.append(jnp.where(jnp.abs(rel) <= WINDOW, 0.0, NEG_INF).astype(F32))
        nkeys += 3 * BLOCK
    nq = l // tq
    out_specs = [pl.BlockSpec((1, tq, N_HEADS * HEAD_DIM), lambda bi, n: (bi, n, 0))]
    out_shape = [jax.ShapeDtypeStruct((b, l, N_HEADS * HEAD_DIM), BF16)]
    if cast is not None:
        spec, shape = _cast_rider(cast, b * nq, lambda bi, n: (bi * nq + n, 0))
        in_specs.append(spec)
        args.append(cast)
        out_specs.append(spec)
        out_shape.append(shape)
    outs = pl.pallas_call(
        functools.partial(_attn_kernel, band=band, carry_cast=cast is not None),
        grid=(b, nq),
        in_specs=in_specs,
        out_specs=out_specs,
        out_shape=out_shape,
        scratch_shapes=[pltpu.VMEM((qb * N_HEADS, BLOCK, nkeys), F32),
                        pltpu.VMEM((qb * N_HEADS, BLOCK, nkeys), BF16)],
        compiler_params=_cparams(("parallel", "parallel"), resident_weights=True),
        name="attn_band" if band else "attn_ctx",
    )(*args)
    return outs[0] if cast is None else outs


MERGE_PARTS = 2


def _merge_kernel(attn_ref, mix_ref, prev_ref, next_ref, gate_ref, x_ref, mod_ref, n2g_ref, convw_ref,
                  poolw_ref, pscale_ref, wb_ref, wo_ref, *refs, tm, seq_len, d, carry_cast):
    refs = list(refs)
    if carry_cast:
        cast_out = refs.pop()
        cast_in = refs.pop(0)
        cast_out[...] = cast_in[...].astype(BF16)
    xo_ref, h2_ref = refs[:2]
    tile = pl.program_id(0) % (seq_len // tm)
    keep_prev = (tile != 0).astype(F32)
    keep_next = (tile != seq_len // tm - 1).astype(F32)
    mixm = mix_ref[...]
    prev = prev_ref[...].astype(F32) * keep_prev
    nxt = next_ref[...].astype(F32) * keep_next
    b = BRANCH
    cx, cb, cc = (mixm[:, j * b:(j + 1) * b].astype(F32) for j in range(3))

    p = cc * cx
    p_prev = prev[HALO - 1:HALO, 2 * b:3 * b] * prev[HALO - 1:HALO, 0:b]
    p_next = nxt[0:1, 2 * b:3 * b] * nxt[0:1, 0:b]
    ridx = lax.broadcasted_iota(jnp.int32, (tm, b), 0)
    p_dn = jnp.where(ridx == 0, p_prev, pltpu.roll(p, 1, 0))
    p_up = jnp.where(ridx == tm - 1, p_next, pltpu.roll(p, tm - 1, 0))
    cw = convw_ref[...]
    conv_out = (cb * (p_dn * cw[0:1] + p * cw[1:2] + p_up * cw[2:3])).astype(BF16)

    u_main = mixm[:, 3 * b:4 * b].astype(F32)
    u_ext = jnp.concatenate([prev[:, 3 * b:4 * b], u_main, nxt[:, 3 * b:4 * b]], axis=0)
    ext = tm + 2 * HALO

    def shift(a, s):
        return pltpu.roll(a, s % ext, 0)

    tpos = tile * tm + lax.broadcasted_iota(jnp.int32, (tm, 1), 0)
    pooled = []
    for gi, w in enumerate(POOL_SIZES):
        gs = slice(gi * POOL_GROUP, (gi + 1) * POOL_GROUP)
        ug = u_ext[:, gs]
        a = ug + shift(ug, 1)
        ww = 2
        while ww < w:
            a = shift(a, ww // 2) + shift(a, -(ww // 2))
            ww *= 2
        cnt = jnp.minimum(tpos + w // 2, seq_len) - jnp.maximum(tpos - w // 2, 0)
        dlt = a[HALO:HALO + tm] / cnt.astype(F32) - u_main[:, gs]
        pooled.append(jnp.dot(dlt.astype(BF16), poolw_ref[gi], preferred_element_type=F32))
    pool_out = (jnp.concatenate(pooled, axis=1) * pscale_ref[...]).astype(BF16)

    mod = mod_ref[...]
    parts = [slice(r, r + tm // MERGE_PARTS) for r in range(0, tm, tm // MERGE_PARTS)]

    def branches(rs):
        return (jnp.dot(attn_ref[rs, :], wb_ref[0], preferred_element_type=F32),
                jnp.dot(conv_out[rs], wb_ref[1], preferred_element_type=F32),
                jnp.dot(pool_out[rs], wb_ref[2], preferred_element_type=F32))

    def gated(rs, br):
        return (gate_ref[rs, 0:d].astype(F32) * br[0] + gate_ref[rs, d:2 * d].astype(F32) * br[1]
                + gate_ref[rs, 2 * d:3 * d].astype(F32) * br[2]).astype(BF16)

    def finish(rs, o):
        xn = x_ref[rs, :] + mod[:, 2 * d:3 * d] * o
        xo_ref[rs, :] = xn
        h2_ref[rs, :] = _norm_mod(xn, n2g_ref[...], mod[:, 3 * d:4 * d], mod[:, 4 * d:5 * d]).astype(h2_ref.dtype)

    br = [branches(rs) for rs in parts]
    outs = []
    for k, rs in enumerate(parts):
        outs.append(jnp.dot(gated(rs, br[k]), wo_ref[...], preferred_element_type=F32))
        if k >= 1:
            finish(parts[k - 1], outs[k - 1])
    finish(parts[-1], outs[-1])


def _merge(attn, mix, gate, x, mod, norm2_g, conv_w, pool_w, pool_scale, w_branch, w_out, *, seq_len,
           mod_row_of_tile, h2_dtype, cast=None):
    m, d = x.shape
    tm = min(512, seq_len)
    assert m % tm == 0 and seq_len % tm == 0 and tm % HALO == 0
    hb = tm // HALO
    n_halo = m // HALO
    row = lambda w: pl.BlockSpec((tm, w), lambda i: (i, 0))
    in_specs = [
        row(BRANCH),
        row(MIX_W),
        pl.BlockSpec((HALO, MIX_W), lambda i: (jnp.maximum(i * hb - 1, 0), 0)),
        pl.BlockSpec((HALO, MIX_W), lambda i: (jnp.minimum((i + 1) * hb, n_halo - 1), 0)),
        row(3 * d),
        row(d),
        pl.BlockSpec((None, 1, 6 * d), lambda i: (mod_row_of_tile(i, tm), 0, 0)),
        pl.BlockSpec((1, d), lambda i: (0, 0)),
        _resident(conv_w.shape),
        _resident(pool_w.shape),
        pl.BlockSpec((1, BRANCH), lambda i: (0, 0)),
        _resident(w_branch.shape),
        _resident(w_out.shape),
    ]
    args = [attn, mix, mix, mix, gate, x, mod, norm2_g.reshape(1, d), conv_w, pool_w,
            pool_scale.reshape(1, BRANCH), w_branch, w_out]
    out_specs = [row(d), row(d)]
    out_shape = [jax.ShapeDtypeStruct((m, d), F32), jax.ShapeDtypeStruct((m, d), h2_dtype)]
    if cast is not None:
        spec, shape = _cast_rider(cast, m // tm, lambda i: (i, 0))
        in_specs.append(spec)
        args.append(cast)
        out_specs.append(spec)
        out_shape.append(shape)
    return pl.pallas_call(
        functools.partial(_merge_kernel, tm=tm, seq_len=seq_len, d=d, carry_cast=cast is not None),
        grid=(m // tm,),
        in_specs=in_specs,
        out_specs=out_specs,
        out_shape=out_shape,
        compiler_params=_cparams(("parallel",)),
        name="merge",
    )(*args)


def _ffn_kernel(h_ref, x_ref, mod_ref, wgu_ref, wd_ref, *refs, d, d_ff, fc):
    n_cast = len(refs) // 2
    o_ref = refs[n_cast]
    for cast_in, cast_out in zip(refs[:n_cast], refs[n_cast + 1:]):
        cast_out[...] = cast_in[...].astype(BF16)
    h = h_ref[...]
    acc = None
    for f in range(0, d_ff, fc):
        g = jnp.dot(h, wgu_ref[:, f:f + fc], preferred_element_type=F32)
        u = jnp.dot(h, wgu_ref[:, d_ff + f:d_ff + f + fc], preferred_element_type=F32)
        a = (g * jax.nn.sigmoid(g) * u).astype(BF16)
        y = jnp.dot(a, wd_ref[f:f + fc, :], preferred_element_type=F32)
        acc = y if acc is None else acc + y
    o_ref[...] = x_ref[...] + mod_ref[...][:, 5 * d:6 * d] * acc


def _ffn_dense(h2, x, mod, w_gu, w_down, *, seq_len, mod_row_of_tile, casts=()):
    m, d = x.shape
    d_ff = w_down.shape[0]
    tm = min(512, seq_len)
    assert m % tm == 0 and seq_len % tm == 0
    fc = _pick_chunk(d_ff, 3072)
    n_steps = m // tm
    in_specs = [
        pl.BlockSpec((tm, d), lambda i: (i, 0)),
        pl.BlockSpec((tm, d), lambda i: (i, 0)),
        pl.BlockSpec((None, 1, 6 * d), lambda i: (mod_row_of_tile(i, tm), 0, 0)),
        _resident(w_gu.shape),
        _resident(w_down.shape),
    ]
    out_specs = [pl.BlockSpec((tm, d), lambda i: (i, 0))]
    out_shape = [jax.ShapeDtypeStruct((m, d), F32)]
    for w, layer in casts:
        _, rows, cols = w.shape
        assert rows % (n_steps * BF16_SUBLANE_TILE) == 0, (w.shape, n_steps)
        in_specs.append(pl.BlockSpec((None, rows // n_steps, cols), lambda i, layer=layer: (layer, i, 0)))
        out_specs.append(pl.BlockSpec((rows // n_steps, cols), lambda i: (i, 0)))
        out_shape.append(jax.ShapeDtypeStruct((rows, cols), BF16))
    outs = pl.pallas_call(
        functools.partial(_ffn_kernel, d=d, d_ff=d_ff, fc=fc),
        grid=(n_steps,),
        in_specs=in_specs,
        out_specs=out_specs,
        out_shape=out_shape,
        compiler_params=_cparams(("parallel",), resident_weights=True),
        name="ffn_dense",
    )(h2, x, mod, w_gu, w_down, *[w for w, _ in casts])
    return outs if casts else outs[0]


ROUTE_ROWS = 8
RUN_ALIGN = BF16_SUBLANE_TILE
META_ROWS = 5
META_TILES, META_PAD_START, META_PAD_LEN, META_USED = LANE, 2 * LANE, 3 * LANE, 4 * LANE


def _router_kernel(h_ref, rw_ref, tri_ref, wts_ref, pairs_ref, runs_ref, cnt_ref, carry_ref, *, n_experts):
    @pl.when(pl.program_id(0) == 0)
    def _():
        carry_ref[...] = jnp.zeros_like(carry_ref)

    h, rs = h_ref[...], slice(None)
    w = rw_ref[...]
    h_hi, w_hi = h.astype(BF16), w.astype(BF16)
    h_lo, w_lo = (h - h_hi.astype(F32)).astype(BF16), (w - w_hi.astype(F32)).astype(BF16)
    both = jnp.dot(h_hi, jnp.concatenate([w_hi, w_lo], axis=1), preferred_element_type=F32)
    logits = both[:, :LANE] + both[:, LANE:] + jnp.dot(h_lo, w_hi, preferred_element_type=F32)
    lane = lax.broadcasted_iota(jnp.int32, logits.shape, 1).astype(F32)
    logits = jnp.where(lane < n_experts, logits, -jnp.inf)
    m1 = jnp.max(logits, axis=-1, keepdims=True)
    i1 = jnp.min(jnp.where(logits == m1, lane, float(LANE)), axis=-1, keepdims=True)
    rest = jnp.where(lane == i1, -jnp.inf, logits)
    m2 = jnp.max(rest, axis=-1, keepdims=True)
    i2 = jnp.min(jnp.where(rest == m2, lane, float(LANE)), axis=-1, keepdims=True)
    e2 = jnp.exp(m2 - m1)
    den = 1.0 + e2
    weights = jnp.where(lane == 0, 1.0 / den, jnp.where(lane == 1, e2 / den, 0.0))

    m1t = (lane == i1).astype(F32).T[0:ROUTE_ROWS]
    m2t = (lane == i2).astype(F32).T[0:ROUTE_ROWS]
    mem = m1t + m2t
    local = jnp.dot(mem.astype(BF16), tri_ref[...], preferred_element_type=F32)
    run = jnp.floor((jnp.sum(mem, axis=1, keepdims=True) + (RUN_ALIGN - 1)) * (1.0 / RUN_ALIGN)) * RUN_ALIGN
    sub8 = lax.broadcasted_iota(jnp.int32, (ROUTE_ROWS, LANE), 0)
    lane8 = lax.broadcasted_iota(jnp.int32, (ROUTE_ROWS, LANE), 1)
    incl = jnp.sum(jnp.where(sub8 <= lane8, run, 0.0), axis=0, keepdims=True)
    run_start = jnp.sum(jnp.where(lane8 == sub8, incl, 0.0), axis=1, keepdims=True) - run
    done = carry_ref[:, 0:1]
    eid = lax.broadcasted_iota(jnp.int32, mem.shape, 0).astype(F32)
    pick = lambda sel, val: jnp.sum(sel * val, axis=0, keepdims=True)
    rows = [pick(m1t, eid), pick(m2t, eid), pick(m1t, local + done), pick(m2t, local + done),
            pick(m1t, local + run_start), pick(m2t, local + run_start)]
    rows += [jnp.zeros_like(rows[0])] * (ROUTE_ROWS - len(rows))
    pairs = jnp.concatenate(rows, axis=0)
    pairs_ref[:, rs] = pairs
    by_token = jnp.concatenate([pairs, jnp.zeros((LANE - ROUTE_ROWS, pairs.shape[1]), F32)], axis=0).T
    wts_ref[rs, :] = jnp.where((lane == 4) | (lane == 5), by_token, weights)
    runs_ref[...] = jnp.where(lane8 == 0, run_start, jnp.where(lane8 == 1, run * (1.0 / RUN_ALIGN),
                                                               jnp.where(lane8 == 2, done, 0.0)))
    carry_ref[...] = carry_ref[...] + run
    cnt_ref[...] = carry_ref[...]


def _router(h2, router_w, *, tm):
    m, d = h2.shape
    n_experts = router_w.shape[1]
    assert n_experts <= ROUTE_ROWS and m % tm == 0
    rw = jnp.pad(router_w, ((0, 0), (0, LANE - n_experts)))
    tri = jnp.triu(jnp.ones((tm, tm), BF16), k=1)
    return pl.pallas_call(
        functools.partial(_router_kernel, n_experts=n_experts),
        grid=(m // tm,),
        in_specs=[
            pl.BlockSpec((tm, d), lambda i: (i, 0)),
            pl.BlockSpec((d, LANE), lambda i: (0, 0)),
            pl.BlockSpec((tm, tm), lambda i: (0, 0)),
        ],
        out_specs=[
            pl.BlockSpec((tm, LANE), lambda i: (i, 0)),
            pl.BlockSpec((ROUTE_ROWS, tm), lambda i: (0, i)),
            pl.BlockSpec((ROUTE_ROWS, LANE), lambda i: (i, 0)),
            pl.BlockSpec((ROUTE_ROWS, LANE), lambda i: (0, 0)),
        ],
        out_shape=[jax.ShapeDtypeStruct((m, LANE), F32), jax.ShapeDtypeStruct((ROUTE_ROWS, m), F32),
                   jax.ShapeDtypeStruct((m // tm * ROUTE_ROWS, LANE), F32),
                   jax.ShapeDtypeStruct((ROUTE_ROWS, LANE), F32)],
        scratch_shapes=[pltpu.VMEM((ROUTE_ROWS, LANE), F32)],
        compiler_params=_cparams(("arbitrary",)),
        name="router",
    )(h2, rw, tri)


def _plan_kernel(runs_ref, cnt_ref, rundest_ref, meta_ref, *, tile_rows):
    cnt = cnt_ref[...]
    padded = jnp.floor((cnt + (tile_rows - 1)) * (1.0 / tile_rows)) * tile_rows
    sub = lax.broadcasted_iota(jnp.int32, cnt.shape, 0)
    lane = lax.broadcasted_iota(jnp.int32, cnt.shape, 1)
    end_row = jnp.sum(jnp.where(sub <= lane, padded, 0.0), axis=0, keepdims=True)
    end_col = jnp.sum(jnp.where(lane == sub, end_row, 0.0), axis=1, keepdims=True)
    start_col = end_col - padded[:, 0:1]
    runs = runs_ref[...]
    starts = jnp.concatenate([start_col] * (runs.shape[0] // ROUTE_ROWS), axis=0)
    rlane = lax.broadcasted_iota(jnp.int32, runs.shape, 1)
    rundest_ref[...] = jnp.where(rlane == 2, runs + starts, runs).astype(jnp.int32)
    tile_start = (lane * tile_rows).astype(F32)
    tile_e = jnp.sum((end_col <= tile_start).astype(F32), axis=0, keepdims=True)
    n_tiles = jnp.max(end_col, axis=0, keepdims=True) * (1.0 / tile_rows)
    pad_start = jnp.sum(jnp.where(lane == sub, start_col + cnt, 0.0), axis=0, keepdims=True)
    pad_len = jnp.sum(jnp.where(lane == sub, padded - cnt, 0.0), axis=0, keepdims=True)
    group_rows_end = jnp.sum(jnp.where(sub.astype(F32) == tile_e, start_col + cnt, 0.0), axis=0, keepdims=True)
    tile_used = jnp.clip(group_rows_end - tile_start[0:1], 0.0, float(tile_rows))
    meta = jnp.concatenate([tile_e, jnp.broadcast_to(n_tiles, tile_e.shape), pad_start, pad_len, tile_used]
                           + [jnp.zeros_like(tile_e)] * (ROUTE_ROWS - META_ROWS), axis=0)
    meta_ref[...] = meta.astype(jnp.int32)


def _plan(runs, counts, *, tile_rows):
    small = pl.BlockSpec((ROUTE_ROWS, LANE), lambda i: (0, 0))
    whole = pl.BlockSpec(runs.shape, lambda i: (0, 0))
    return pl.pallas_call(
        functools.partial(_plan_kernel, tile_rows=tile_rows),
        grid=(1,),
        in_specs=[whole, small],
        out_specs=[whole, small],
        out_shape=[jax.ShapeDtypeStruct(runs.shape, jnp.int32), jax.ShapeDtypeStruct((ROUTE_ROWS, LANE), jnp.int32)],
        compiler_params=_cparams(("arbitrary",)),
        name="plan",
    )(runs, counts)


def _run_rows(runs_ref, tile, n_experts):
    pieces = runs_ref[tile * n_experts * LANE + 1]
    for e in range(1, n_experts):
        pieces = pieces + runs_ref[(tile * n_experts + e) * LANE + 1]
    return pl.multiple_of(pieces * RUN_ALIGN, RUN_ALIGN)


def _dispatch_kernel(runs_ref, meta_ref, h_ref, pos_ref, xs_ref, buf_ref, zero_ref, sems, zsem, *, n_experts,
                     n_tiles_max):
    i = pl.program_id(0)
    tile_rows = zero_ref.shape[0]
    slot = i % 2
    buf = buf_ref.at[slot]
    pos = pos_ref[...]
    row = lax.broadcasted_iota(jnp.int32, (buf.shape[0], pos.shape[1]), 0).astype(F32)
    take = ((row == pos[4:5]).astype(F32) + (row == pos[5:6]).astype(F32)).astype(BF16)
    buf[...] = jnp.dot(take, h_ref[...].astype(BF16), preferred_element_type=F32).astype(BF16)

    def piece_copy(src, src_row, dst_row, s):
        return pltpu.make_async_copy(src.at[pl.ds(pl.multiple_of(src_row, RUN_ALIGN), RUN_ALIGN), :],
                                     xs_ref.at[pl.ds(pl.multiple_of(dst_row, RUN_ALIGN), RUN_ALIGN), :], s)

    for e in range(n_experts):
        entry = (i * n_experts + e) * LANE
        src0, dst0 = runs_ref[entry], runs_ref[entry + 2]

        def piece(c, carry):
            piece_copy(buf, src0 + c * RUN_ALIGN, dst0 + c * RUN_ALIGN, sems.at[slot]).start()
            return carry

        lax.fori_loop(0, runs_ref[entry + 1], piece, 0)

    def wait_runs(tile, sl):
        rows = _run_rows(runs_ref, tile, n_experts)
        pltpu.make_async_copy(buf_ref.at[sl, pl.ds(0, rows), :], xs_ref.at[pl.ds(0, rows), :], sems.at[sl]).wait()

    @pl.when(i > 0)
    def _():
        wait_runs(i - 1, 1 - slot)

    @pl.when(i == pl.num_programs(0) - 1)
    def _():
        wait_runs(i, slot)
        zero_ref[...] = jnp.zeros_like(zero_ref)

        def fill_padding(e, start):
            first = meta_ref[META_PAD_START + e]

            def piece(c, carry):
                cp = piece_copy(zero_ref, 0, first + c * RUN_ALIGN, zsem)
                cp.start() if start else cp.wait()
                return carry

            lax.fori_loop(0, meta_ref[META_PAD_LEN + e] // RUN_ALIGN, piece, 0)

        def tail_copy(k):
            return pltpu.make_async_copy(zero_ref, xs_ref.at[pl.ds(k * tile_rows, tile_rows), :], zsem)

        def zero_tile(k, carry):
            tail_copy(k).start()
            return carry

        def wait_tile(k, carry):
            tail_copy(k).wait()
            return carry

        for e in range(n_experts):
            fill_padding(e, start=True)
        lax.fori_loop(meta_ref[META_TILES], n_tiles_max, zero_tile, 0)
        for e in range(n_experts):
            fill_padding(e, start=False)
        lax.fori_loop(meta_ref[META_TILES], n_tiles_max, wait_tile, 0)


def _dispatch(h2, pairs, runs, meta, n_rows, *, tt, tm, n_experts):
    m, d = h2.shape
    buf_rows = 2 * tt + n_experts * RUN_ALIGN
    return pl.pallas_call(
        functools.partial(_dispatch_kernel, n_experts=n_experts, n_tiles_max=n_rows // tm),
        grid_spec=pltpu.PrefetchScalarGridSpec(
            num_scalar_prefetch=2,
            grid=(m // tt,),
            in_specs=[pl.BlockSpec((tt, d), lambda i, rn, mt: (i, 0)),
                      pl.BlockSpec((ROUTE_ROWS, tt), lambda i, rn, mt: (0, i))],
            out_specs=pl.BlockSpec(memory_space=pl.ANY),
            scratch_shapes=[pltpu.VMEM((2, buf_rows, d), BF16), pltpu.VMEM((tm, d), BF16),
                            pltpu.SemaphoreType.DMA((2,)), pltpu.SemaphoreType.DMA],
        ),
        out_shape=jax.ShapeDtypeStruct((n_rows, d), BF16),
        compiler_params=_cparams(("arbitrary",)),
        name="dispatch",
    )(runs, meta, h2, pairs)


def _gmm_kernel(meta_ref, xs_ref, wg_ref, wu_ref, wd_ref, o_ref, acc_ref):
    f = pl.program_id(1)
    nf = pl.num_programs(1)

    @pl.when((pl.program_id(0) == 0) & (f == 0))
    def _():
        acc_ref[...] = jnp.zeros_like(acc_ref)

    tm = xs_ref.shape[0]
    in_use = pl.program_id(0) < meta_ref[META_TILES]
    used = meta_ref[META_USED + pl.program_id(0)]

    def swiglu_rows(rows):
        xb = xs_ref[0:rows, :]
        g = jnp.dot(xb, wg_ref[...], preferred_element_type=F32)
        u = jnp.dot(xb, wu_ref[...], preferred_element_type=F32)
        a = (g * jax.nn.sigmoid(g) * u).astype(BF16)
        y = jnp.dot(a, wd_ref[...], preferred_element_type=F32)
        total = y + jnp.where(f > 0, acc_ref[0:rows, :], 0.0)
        acc_ref[0:rows, :] = total
        o_ref[0:rows, :] = total.astype(o_ref.dtype)
        if rows < tm:
            o_ref[rows:, :] = jnp.zeros((tm - rows, o_ref.shape[1]), o_ref.dtype)

    @pl.when(in_use & (used > tm // 2))
    def _():
        swiglu_rows(tm)

    @pl.when(in_use & (used <= tm // 2))
    def _():
        swiglu_rows(tm // 2)

    @pl.when((pl.program_id(0) >= meta_ref[META_TILES]) & (f == nf - 1))
    def _():
        o_ref[...] = jnp.zeros_like(o_ref)


def _grouped_swiglu(xs, meta, w_gu, w_down, *, tm):
    n_rows, d = xs.shape
    n_experts, d_ff, _ = w_down.shape
    fc = _pick_chunk(d_ff, d_ff // 2)
    nf = d_ff // fc
    assert nf >= 2 and n_rows % tm == 0 and n_rows // tm <= LANE

    def tile(i, mt):
        return jnp.maximum(jnp.minimum(i, mt[LANE] - 1), 0)

    def expert(i, mt):
        return jnp.minimum(mt[tile(i, mt)], n_experts - 1)

    def chunk(i, f, mt):
        return jnp.where(i < mt[LANE], f, nf - 1)

    return pl.pallas_call(
        _gmm_kernel,
        grid_spec=pltpu.PrefetchScalarGridSpec(
            num_scalar_prefetch=1,
            grid=(n_rows // tm, nf),
            in_specs=[
                pl.BlockSpec((tm, d), lambda i, f, mt: (tile(i, mt), 0)),
                pl.BlockSpec((None, d, fc), lambda i, f, mt: (expert(i, mt), 0, chunk(i, f, mt))),
                pl.BlockSpec((None, d, fc), lambda i, f, mt: (expert(i, mt), 0, nf + chunk(i, f, mt))),
                pl.BlockSpec((None, fc, d), lambda i, f, mt: (expert(i, mt), chunk(i, f, mt), 0)),
            ],
            out_specs=pl.BlockSpec((tm, d), lambda i, f, mt: (i, 0)),
            scratch_shapes=[pltpu.VMEM((tm, d), F32)],
        ),
        out_shape=jax.ShapeDtypeStruct((n_rows, d), BF16),
        compiler_params=_cparams(("arbitrary", "arbitrary"), resident_weights=True),
        name="grouped_swiglu",
    )(meta, xs, w_gu, w_gu, w_down)


def _combine_kernel(runs_ref, x_ref, w_ref, mod_ref, fg_ref, ys_ref, o_ref, y_ref, sems, *, n_experts, d,
                    final_norm):
    i = pl.program_id(0)
    slot = i % 2

    @pl.when(i == 0)
    def _():
        y_ref[...] = jnp.zeros_like(y_ref)

    def fetch_runs(tile, to_slot):
        for e in range(n_experts):
            entry = (tile * n_experts + e) * LANE
            loc0, src0 = runs_ref[entry], runs_ref[entry + 2]

            def piece(c, carry):
                pltpu.make_async_copy(
                    ys_ref.at[pl.ds(pl.multiple_of(src0 + c * RUN_ALIGN, RUN_ALIGN), RUN_ALIGN), :],
                    y_ref.at[to_slot, pl.ds(pl.multiple_of(loc0 + c * RUN_ALIGN, RUN_ALIGN), RUN_ALIGN), :],
                    sems.at[to_slot]).start()
                return carry

            lax.fori_loop(0, runs_ref[entry + 1], piece, 0)

    @pl.when(i == 0)
    def _():
        fetch_runs(0, 0)

    @pl.when(i + 1 < pl.num_programs(0))
    def _():
        fetch_runs(i + 1, 1 - slot)

    rows = _run_rows(runs_ref, i, n_experts)
    pltpu.make_async_copy(ys_ref.at[pl.ds(0, rows), :], y_ref.at[slot, pl.ds(0, rows), :], sems.at[slot]).wait()
    w = w_ref[...]
    col = lax.broadcasted_iota(jnp.int32, (w.shape[0], y_ref.shape[1]), 1).astype(F32)
    sel = jnp.where(col == w[:, 4:5], w[:, 0:1], 0.0) + jnp.where(col == w[:, 5:6], w[:, 1:2], 0.0)
    moe = jnp.dot(sel.astype(BF16), y_ref[slot], preferred_element_type=F32)
    xn = x_ref[...] + mod_ref[...][:, 5 * d:6 * d] * moe
    if final_norm:
        xn = xn * lax.rsqrt(jnp.mean(xn * xn, axis=-1, keepdims=True) + EPS) * fg_ref[...]
    o_ref[...] = xn


def _combine(x, wts, runs, ys, mod, final_g, *, tt, n_experts, mod_row_of_tile, final_norm):
    m, d = x.shape
    buf_rows = 2 * tt + n_experts * RUN_ALIGN
    return pl.pallas_call(
        functools.partial(_combine_kernel, n_experts=n_experts, d=d, final_norm=final_norm),
        grid_spec=pltpu.PrefetchScalarGridSpec(
            num_scalar_prefetch=1,
            grid=(m // tt,),
            in_specs=[
                pl.BlockSpec((tt, d), lambda i, rn: (i, 0)),
                pl.BlockSpec((tt, LANE), lambda i, rn: (i, 0)),
                pl.BlockSpec((None, 1, 6 * d), lambda i, rn: (mod_row_of_tile(i, tt), 0, 0)),
                pl.BlockSpec((1, d), lambda i, rn: (0, 0)),
                pl.BlockSpec(memory_space=pl.ANY),
            ],
            out_specs=pl.BlockSpec((tt, d), lambda i, rn: (i, 0)),
            scratch_shapes=[pltpu.VMEM((2, buf_rows, d), BF16), pltpu.SemaphoreType.DMA((2,))],
        ),
        out_shape=jax.ShapeDtypeStruct((m, d), F32),
        compiler_params=_cparams(("arbitrary",)),
        name="combine",
    )(runs, x, wts, mod, final_g.reshape(1, d), ys)


def _moe(h2, x, router_w, mod, final_g, w_gu, w_down, *, seq_len, mod_row_of_tile, final_norm):
    m, d = x.shape
    n_experts = w_down.shape[0]
    tm = 512
    tt = min(512, m)
    n_rows = pl.cdiv(2 * m + m // tt * n_experts * RUN_ALIGN, tm) * tm + n_experts * tm
    assert n_rows // tm <= LANE
    wts, pairs, runs, counts = _router(h2, router_w, tm=tt)
    rundest, meta2d = _plan(runs, counts, tile_rows=tm)
    run_table = rundest.reshape(-1)
    meta = meta2d[0:META_ROWS].reshape(META_ROWS * LANE)
    xs = _dispatch(h2, pairs, run_table, meta, n_rows, tt=tt, tm=tm, n_experts=n_experts)
    ys = _grouped_swiglu(xs, meta, w_gu, w_down, tm=tm)
    return _combine(x, wts, run_table, ys, mod, final_g, tt=tt, n_experts=n_experts,
                    mod_row_of_tile=mod_row_of_tile, final_norm=final_norm)


def kernel(x, c, ctx, c_ctx, norm1_g, norm2_g, final_g, w_mod, b_mod, w_in, conv_w, sink, pool_w, pool_scale,
           w_branch, w_out, ffn_w_gu, ffn_w_down, router_w, moe_w_gu, moe_w_down):
    bsz, seq, d = x.shape
    lc = ctx.shape[1]
    depth = w_in.shape[0]
    assert bsz + 1 <= 8 and seq % BLOCK == 0 and lc % BLOCK == 0 and seq % GRID_W == 0

    cvec = jnp.zeros((8, d), F32).at[:bsz].set(c).at[bsz].set(c_ctx)
    mods = _modvec(cvec, w_mod, b_mod)
    rope_tabs = _rope_tables(seq)

    lat_row = lambda i, tm: (i * tm) // seq
    ctx_row = lambda i, tm: bsz

    xl = x.reshape(bsz * seq, d)
    xc = ctx.reshape(bsz * lc, d)
    expert_w = next_w = None
    for l in range(depth):
        last = l == depth - 1
        mod = mods[l].reshape(8, 1, 6 * d)
        if next_w is not None:
            w_in_l, wb_l, wo_l = next_w[0], next_w[1].reshape(w_branch.shape[1:]), next_w[2]
        else:
            w_in_l, wb_l, wo_l = w_in[l].astype(BF16), w_branch[l].astype(BF16), w_out[l].astype(BF16)
        pw_l = pool_w[l].astype(BF16)
        routed = l % 2 == 1
        mixer = functools.partial(_merge, mod=mod, norm2_g=norm2_g[l], conv_w=conv_w[l], pool_w=pw_l,
                                  pool_scale=pool_scale[l], w_branch=wb_l, w_out=wo_l,
                                  h2_dtype=F32 if routed else BF16)
        if routed:
            wgu, wd = expert_w if expert_w is not None else (moe_w_gu[l // 2].astype(BF16),
                                                             moe_w_down[l // 2].astype(BF16))
        else:
            wgu, wd = ffn_w_gu[l // 2].astype(BF16), ffn_w_down[l // 2].astype(BF16)
        ride = not routed and not last
        ride_gu = moe_w_gu[(l + 1) // 2].reshape(-1, moe_w_gu.shape[-1]) if ride else None
        ride_d = moe_w_down[(l + 1) // 2].reshape(-1, d) if ride else None

        def channel_mix(mixed, *, seq_len, row_fn, final_norm, casts=()):
            xm, h2 = mixed[0], mixed[1]
            if routed:
                return _moe(h2, xm, router_w[l // 2], mod, final_g, wgu, wd, seq_len=seq_len,
                            mod_row_of_tile=row_fn, final_norm=final_norm)
            assert not final_norm
            return _ffn_dense(h2, xm, mod, wgu, wd, seq_len=seq_len, mod_row_of_tile=row_fn, casts=casts)

        if last:
            kvc = _inproj(xc, norm1_g[l], mod, w_in_l, seq_len=lc, mod_row_of_tile=ctx_row, kv_only=True)
        else:
            qc, kvc, mixc, gatec = _inproj(xc, norm1_g[l], mod, w_in_l, seq_len=lc, mod_row_of_tile=ctx_row)
            attn_c = _attention(qc.reshape(bsz, lc, -1), None, kvc.reshape(bsz, lc, -1), sink[l], band=False)
            mixed_c = mixer(attn_c.reshape(bsz * lc, -1), mixc, gatec, xc, seq_len=lc, mod_row_of_tile=ctx_row)
            xc_next = channel_mix(mixed_c, seq_len=lc, row_fn=ctx_row, final_norm=False)
        q, kv, mix, gate = _inproj(xl, norm1_g[l], mod, w_in_l, seq_len=seq, mod_row_of_tile=lat_row,
                                   rope_tabs=rope_tabs)
        attn = _attention(q.reshape(bsz, seq, -1), kv.reshape(bsz, seq, -1), kvc.reshape(bsz, lc, -1), sink[l],
                          band=True, cast=ride_gu)
        merged = mixer((attn[0] if ride else attn).reshape(bsz * seq, -1), mix, gate, xl, seq_len=seq,
                       mod_row_of_tile=lat_row, cast=ride_d)
        if routed or last:
            xl = channel_mix(merged, seq_len=seq, row_fn=lat_row, final_norm=last and routed)
            next_w = None
        else:
            stacked = [w_in, w_branch.reshape(depth, -1, d), w_out]
            xl, *next_w = channel_mix(merged, seq_len=seq, row_fn=lat_row, final_norm=False,
                                      casts=[(w, l + 1) for w in stacked])
        expert_w = (attn[1].reshape(moe_w_gu.shape[1:]), merged[2].reshape(moe_w_down.shape[1:])) if ride else None
        if not last:
            xc = xc_next
    if depth % 2 == 1:
        raise NotImplementedError("final norm is fused into the expert layer; depth must be even")
    return xl.reshape(bsz, seq, d)
```
